```python
import math
import jax, jax.numpy as jnp
from jax import lax
import numpy as np

D_MODEL = 1024
BATCH = 2
SEQ = 8192
DEPTH = 2

REL_BUCKETS = 32
REL_MAX_EXACT = 16
REL_MAX_DIST = 128
ATTN_HEADS = 8

GLA_HEADS = 4
GLA_DK = 64
GLA_DV = 128
GLA_RANK = 16
GLA_TAU = 16.0
GLA_CHUNK = 64

SWA_HEADS = ATTN_HEADS
SWA_KV_HEADS = 2
SWA_DH = 64
SWA_WINDOW = 128

MOBA_HEADS = ATTN_HEADS
MOBA_KV_HEADS = 2
MOBA_DH = 128
MOBA_BLOCK = 256
MOBA_TOPK = 3
MOBA_Q_CHUNK = 32

EPS = 1e-6

A_WIDTH = GLA_HEADS * GLA_DV
B_WIDTH = SWA_HEADS * SWA_DH
C_WIDTH = MOBA_HEADS * MOBA_DH
EVEN_WIDTH = A_WIDTH + B_WIDTH
EVEN_SPLITS = (GLA_HEADS * GLA_DK, GLA_HEADS * GLA_DK, A_WIDTH, GLA_RANK, A_WIDTH,
               B_WIDTH, SWA_KV_HEADS * SWA_DH, SWA_KV_HEADS * SWA_DH, B_WIDTH)
EVEN_COLS = 2832
ODD_SPLITS = (C_WIDTH, MOBA_KV_HEADS * MOBA_DH, MOBA_KV_HEADS * MOBA_DH, C_WIDTH)
ODD_COLS = 2560
N_EVEN = (DEPTH + 1) // 2
N_ODD = DEPTH // 2

kernel_name = "hybrid_gla_swa_moba_gated"


def rmsnorm(x, g):
    xf = x.astype(jnp.float32)
    y = xf * lax.rsqrt(jnp.mean(xf * xf, axis=-1, keepdims=True) + EPS)
    return (y * g.astype(jnp.float32)).astype(x.dtype)


def split_cols(h, sizes):
    idx = np.cumsum(np.array(sizes))[:-1].tolist()
    return jnp.split(h, idx, axis=-1)


def t5_bucket(dist):
    n = jnp.maximum(dist, 0)
    nf = jnp.maximum(n, 1).astype(jnp.float32)
    large = REL_MAX_EXACT + (jnp.log(nf / REL_MAX_EXACT) / math.log(REL_MAX_DIST / REL_MAX_EXACT)
                             * (REL_BUCKETS - REL_MAX_EXACT)).astype(jnp.int32)
    large = jnp.minimum(large, REL_BUCKETS - 1)
    return jnp.where(n < REL_MAX_EXACT, n, large)


def gla_chunked(q, k, v, log_a):
    B_, S_, H_, DK = q.shape
    DV = v.shape[-1]
    N = S_ // GLA_CHUNK

    def chunks(t):
        return t.astype(jnp.float32).reshape(B_, N, GLA_CHUNK, H_, t.shape[-1]).transpose(0, 3, 1, 2, 4)

    qc, kc, vc, gc = chunks(q), chunks(k), chunks(v), chunks(log_a)
    b = jnp.cumsum(gc, axis=3)
    b_last = b[:, :, :, -1:, :]
    q_e = qc * jnp.exp(b) * (DK ** -0.5)
    k_e = kc * jnp.exp(-b)
    causal = jnp.tril(jnp.ones((GLA_CHUNK, GLA_CHUNK), dtype=bool))
    att = jnp.where(causal, jnp.einsum('bhncd,bhnsd->bhncs', q_e, k_e), 0.0)
    o_intra = jnp.einsum('bhncs,bhnse->bhnce', att, vc)
    kv_chunk = jnp.einsum('bhncd,bhnce->bhnde', kc * jnp.exp(b_last - b), vc)
    decay = jnp.exp(b_last[:, :, :, 0, :])

    def step(state, inp):
        dec, kv = inp
        return dec[..., None] * state + kv, state

    s0 = jnp.zeros((B_, H_, DK, DV), jnp.float32)
    _, s_before = lax.scan(step, s0, (jnp.moveaxis(decay, 2, 0), jnp.moveaxis(kv_chunk, 2, 0)))
    s_before = jnp.moveaxis(s_before, 0, 2)
    o = o_intra + jnp.einsum('bhncd,bhnde->bhnce', q_e, s_before)
    return o.transpose(0, 2, 3, 1, 4).reshape(B_, S_, H_, DV).astype(v.dtype)


def swa_attention(q, k, v, sinks, rel_bias):
    B_, S_, H_, DH = q.shape
    KVH = k.shape[2]
    G = H_ // KVH
    W = SWA_WINDOW
    NB = S_ // W
    qb = q.reshape(B_, NB, W, KVH, G, DH).transpose(0, 3, 4, 1, 2, 5)

    def band(t):
        tb = t.reshape(B_, NB, W, KVH, DH).transpose(0, 3, 1, 2, 4)
        prev = jnp.pad(tb[:, :, :-1], ((0, 0), (0, 0), (1, 0), (0, 0), (0, 0)))
        return jnp.concatenate([prev, tb], axis=3)

    kb, vb = band(k), band(v)
    qi = jnp.arange(W)[:, None]
    kj = jnp.arange(2 * W)[None, :]
    dist = qi + W - kj
    in_win = (dist >= 0) & (dist < W)
    blk = jnp.arange(NB)[:, None, None]
    mask = in_win[None] & ((blk > 0) | (kj[None] >= W))
    bias = rel_bias[t5_bucket(dist)].astype(jnp.float32)
    bias = bias.transpose(2, 0, 1).reshape(KVH, G, 1, W, 2 * W)
    logits = jnp.einsum('bkgnqd,bknsd->bkgnqs', qb, kb).astype(jnp.float32) * (DH ** -0.5) + bias
    logits = jnp.where(mask, logits, -jnp.inf)
    sink = jnp.broadcast_to(sinks.astype(jnp.float32).reshape(KVH, G, 1, 1, 1), logits.shape[:-1] + (1,))
    probs = jax.nn.softmax(jnp.concatenate([logits, sink], axis=-1), axis=-1)[..., :-1]
    o = jnp.einsum('bkgnqs,bknsd->bkgnqd', probs.astype(v.dtype), vb)
    return o.transpose(0, 3, 4, 1, 2, 5).reshape(B_, S_, H_ * DH)


def moba_attention(q, k, v, rel_bias):
    B_, S_, H_, DH = q.shape
    G = H_ // k.shape[2]
    S_pad = -(-S_ // MOBA_BLOCK) * MOBA_BLOCK
    pad = ((0, 0), (0, S_pad - S_), (0, 0), (0, 0))
    qh = jnp.pad(q, pad).transpose(0, 2, 1, 3)
    kh = jnp.repeat(jnp.pad(k, pad), G, axis=2).transpose(0, 2, 1, 3)
    vh = jnp.repeat(jnp.pad(v, pad), G, axis=2).transpose(0, 2, 1, 3)
    NBLK = S_pad // MOBA_BLOCK
    kblk = kh.reshape(B_, H_, NBLK, MOBA_BLOCK, DH)
    vblk = vh.reshape(B_, H_, NBLK, MOBA_BLOCK, DH)
    kmean = kblk.mean(axis=3)
    k_sel = min(MOBA_TOPK, NBLK)
    bi = jnp.arange(B_)[:, None, None, None]
    hi = jnp.arange(H_)[None, :, None, None]
    offs = jnp.arange(MOBA_BLOCK)
    blk_ids = jnp.arange(NBLK)
    scale = DH ** -0.5

    def one_chunk(c):
        start = c * MOBA_Q_CHUNK
        qc = lax.dynamic_slice_in_dim(qh, start, MOBA_Q_CHUNK, axis=2)
        tpos = start + jnp.arange(MOBA_Q_CHUNK)
        own = start // MOBA_BLOCK
        gate = jnp.einsum('bhqd,bhnd->bhqn', qc, kmean).astype(jnp.float32)
        gate = jnp.where(blk_ids < own, gate, -jnp.inf)
        _, idx = lax.top_k(gate, k_sel)
        kg = kblk[bi, hi, idx].reshape(B_, H_, MOBA_Q_CHUNK, k_sel * MOBA_BLOCK, DH)
        vg = vblk[bi, hi, idx].reshape(B_, H_, MOBA_Q_CHUNK, k_sel * MOBA_BLOCK, DH)
        kpos = (idx[..., None] * MOBA_BLOCK + offs).reshape(B_, H_, MOBA_Q_CHUNK, k_sel * MOBA_BLOCK)
        valid = jnp.broadcast_to((idx < own)[..., None], idx.shape + (MOBA_BLOCK,)).reshape(kpos.shape)
        bias_sel = rel_bias[t5_bucket(tpos[None, None, :, None] - kpos), hi].astype(jnp.float32)
        logit_sel = jnp.einsum('bhqd,bhqsd->bhqs', qc, kg).astype(jnp.float32) * scale + bias_sel
        logit_sel = jnp.where(valid, logit_sel, -jnp.inf)
        ko = lax.dynamic_slice_in_dim(kh, own * MOBA_BLOCK, MOBA_BLOCK, axis=2)
        vo = lax.dynamic_slice_in_dim(vh, own * MOBA_BLOCK, MOBA_BLOCK, axis=2)
        dist_own = tpos[:, None] - (own * MOBA_BLOCK + offs)[None, :]
        bias_own = rel_bias[t5_bucket(dist_own)].astype(jnp.float32).transpose(2, 0, 1)
        logit_own = jnp.einsum('bhqd,bhsd->bhqs', qc, ko).astype(jnp.float32) * scale + bias_own
        logit_own = jnp.where(dist_own >= 0, logit_own, -jnp.inf)
        probs = jax.nn.softmax(jnp.concatenate([logit_sel, logit_own], axis=-1), axis=-1)
        p_sel = probs[..., :k_sel * MOBA_BLOCK]
        p_own = probs[..., k_sel * MOBA_BLOCK:]
        o = jnp.einsum('bhqs,bhqsd->bhqd', p_sel, vg) + jnp.einsum('bhqs,bhsd->bhqd', p_own, vo)
        return o.astype(v.dtype)

    out = lax.map(one_chunk, jnp.arange(S_pad // MOBA_Q_CHUNK))
    out = out.transpose(1, 0, 3, 2, 4).reshape(B_, S_pad, H_ * DH)
    return out[:, :S_]


def even_layer(hn, w_in, gla_w_up, gla_b_up, gla_gain, sinks, w_out, rel_bias):
    B_, S_, _ = hn.shape
    proj = hn @ w_in
    aq, ak, av, a_down, a_gate, bq, bk, bv, b_gate = split_cols(proj, EVEN_SPLITS)
    log_a = jax.nn.log_sigmoid((a_down @ gla_w_up + gla_b_up).astype(jnp.float32)) / GLA_TAU
    oa = gla_chunked(aq.reshape(B_, S_, GLA_HEADS, GLA_DK), ak.reshape(B_, S_, GLA_HEADS, GLA_DK),
                     av.reshape(B_, S_, GLA_HEADS, GLA_DV), log_a.reshape(B_, S_, GLA_HEADS, GLA_DK))
    oa = rmsnorm(oa, gla_gain).reshape(B_, S_, A_WIDTH) * jax.nn.silu(a_gate)
    ob = swa_attention(bq.reshape(B_, S_, SWA_HEADS, SWA_DH), bk.reshape(B_, S_, SWA_KV_HEADS, SWA_DH),
                       bv.reshape(B_, S_, SWA_KV_HEADS, SWA_DH), sinks, rel_bias) * jax.nn.silu(b_gate)
    return jnp.concatenate([oa, ob], axis=-1) @ w_out


def odd_layer(hn, w_in, w_out, rel_bias):
    B_, S_, _ = hn.shape
    cq, ck, cv, c_gate = split_cols(hn @ w_in, ODD_SPLITS)
    oc = moba_attention(cq.reshape(B_, S_, MOBA_HEADS, MOBA_DH), ck.reshape(B_, S_, MOBA_KV_HEADS, MOBA_DH),
                        cv.reshape(B_, S_, MOBA_KV_HEADS, MOBA_DH), rel_bias)
    return (oc * jax.nn.silu(c_gate)) @ w_out


def setup_inputs(seed: int = 0) -> dict:
    key = jax.random.key(seed)
    ks = jax.random.split(key, 12)
    f32 = jnp.float32
    x = jax.random.normal(ks[0], (BATCH, SEQ, D_MODEL), f32)
    norm_gain = 1.0 + 0.02 * jax.random.normal(ks[1], (DEPTH, D_MODEL), f32)
    final_gain = 1.0 + 0.02 * jax.random.normal(ks[2], (D_MODEL,), f32)
    rel_bias = 0.2 * jax.random.normal(ks[3], (REL_BUCKETS, ATTN_HEADS), f32)
    w_in_even = jax.random.normal(ks[4], (N_EVEN, D_MODEL, EVEN_COLS), f32) * D_MODEL ** -0.5
    gla_w_up = jax.random.normal(ks[5], (N_EVEN, GLA_RANK, GLA_HEADS * GLA_DK), f32) * GLA_RANK ** -0.5
    gla_b_up = 0.1 * jax.random.normal(ks[6], (N_EVEN, GLA_HEADS * GLA_DK), f32)
    gla_norm_gain = 1.0 + 0.02 * jax.random.normal(ks[7], (N_EVEN, GLA_DV), f32)
    swa_sinks = 0.5 * jax.random.normal(ks[8], (N_EVEN, SWA_HEADS), f32)
    w_out_even = jax.random.normal(ks[9], (N_EVEN, EVEN_WIDTH, D_MODEL), f32) * EVEN_WIDTH ** -0.5
    w_in_odd = jax.random.normal(ks[10], (N_ODD, D_MODEL, ODD_COLS), f32) * D_MODEL ** -0.5
    w_out_odd = jax.random.normal(ks[11], (N_ODD, C_WIDTH, D_MODEL), f32) * C_WIDTH ** -0.5
    return {"x": x, "norm_gain": norm_gain, "final_gain": final_gain, "rel_bias": rel_bias,
            "w_in_even": w_in_even, "gla_w_up": gla_w_up, "gla_b_up": gla_b_up,
            "gla_norm_gain": gla_norm_gain, "swa_sinks": swa_sinks, "w_out_even": w_out_even,
            "w_in_odd": w_in_odd, "w_out_odd": w_out_odd}


def reference(x, norm_gain, final_gain, rel_bias, w_in_even, gla_w_up, gla_b_up, gla_norm_gain,
              swa_sinks, w_out_even, w_in_odd, w_out_odd):
    h = x
    for layer in range(DEPTH):
        hn = rmsnorm(h, norm_gain[layer])
        if layer % 2 == 0:
            i = layer // 2
            h = h + even_layer(hn, w_in_even[i], gla_w_up[i], gla_b_up[i], gla_norm_gain[i],
                               swa_sinks[i], w_out_even[i], rel_bias)
        else:
            i = layer // 2
            h = h + odd_layer(hn, w_in_odd[i], w_out_odd[i], rel_bias)
    return rmsnorm(h, final_gain)
```

```python
import functools
import math

import numpy as np
import jax
import jax.numpy as jnp
from jax import lax
from jax.experimental import pallas as pl
from jax.experimental.pallas import tpu as pltpu

D_MODEL = 1024
BATCH = 2
SEQ = 8192
TOKENS = BATCH * SEQ

REL_BUCKETS = 32
REL_MAX_EXACT = 16
REL_MAX_DIST = 128
ATTN_HEADS = 8

GLA_HEADS = 4
GLA_DK = 64
GLA_DV = 128
GLA_RANK = 16
GLA_TAU = 16.0
GLA_CHUNK = 64

SWA_HEADS = 8
SWA_KV_HEADS = 2
SWA_DH = 64
SWA_WINDOW = 128

MOBA_HEADS = 8
MOBA_KV_HEADS = 2
MOBA_DH = 128
MOBA_BLOCK = 256
MOBA_TOPK = 3
MOBA_GROUP = MOBA_HEADS // MOBA_KV_HEADS
MOBA_NBLK = SEQ // MOBA_BLOCK

EPS = 1e-6
LANES = 128
NEG_BIG = -1e30
VMEM_LIMIT = 48 * 1024 * 1024

A_WIDTH = GLA_HEADS * GLA_DV
B_WIDTH = SWA_HEADS * SWA_DH
C_WIDTH = MOBA_HEADS * MOBA_DH
GLA_QK = GLA_HEADS * GLA_DK

ROW_TILE = 512
GLA_TILE = 512
SWA_TILE = 512

F32 = jnp.float32
BF16 = jnp.bfloat16


def _dot(a, b):
    return jnp.dot(a, b, preferred_element_type=F32)


def _dot_nt(a, b):
    return lax.dot_general(a, b, (((1,), (1,)), ((), ())), preferred_element_type=F32)


def _dot_tn(a, b):
    return lax.dot_general(a, b, (((0,), (0,)), ((), ())), preferred_element_type=F32)


def _silu(x):
    return x * (1.0 / (1.0 + jnp.exp(-x)))


def _layer_io_kernel(*refs, n_acc, n_proj, write_h, scales):
    h_ref = refs[0]
    pos = 1
    acc_pairs = []
    for _ in range(n_acc):
        acc_pairs.append((refs[pos], refs[pos + 1]))
        pos += 2
    gain_ref = refs[pos]
    pos += 1
    wp_refs = refs[pos:pos + n_proj]
    pos += n_proj
    out_refs = refs[pos:]

    h = h_ref[...]
    for a_ref, w_ref in acc_pairs:
        h = h + _dot(a_ref[...], w_ref[...])
    o = 0
    if write_h:
        out_refs[0][...] = h
        o = 1
    y = h * lax.rsqrt(jnp.mean(h * h, axis=-1, keepdims=True) + EPS) * gain_ref[...]
    if n_proj == 0:
        out_refs[o][...] = y
        return
    yb = y.astype(BF16)
    for k in range(n_proj):
        r = _dot(yb, wp_refs[k][...])
        if scales[k] != 1.0:
            r = r * scales[k]
        out_refs[o + k][...] = r.astype(out_refs[o + k].dtype)


def _layer_io(h, acc_pairs, gain, proj_ws, proj_dtypes, scales, write_h):
    n_rows = h.shape[0]
    grid = (n_rows // ROW_TILE,)
    row_spec = lambda n: pl.BlockSpec((ROW_TILE, n), lambda i: (i, 0))
    full_spec = lambda a: pl.BlockSpec(a.shape, lambda i: (0,) * a.ndim)

    args = [h]
    in_specs = [row_spec(D_MODEL)]
    for a, w in acc_pairs:
        args += [a, w]
        in_specs += [row_spec(a.shape[1]), full_spec(w)]
    args.append(gain)
    in_specs.append(full_spec(gain))
    for w in proj_ws:
        args.append(w)
        in_specs.append(full_spec(w))

    out_shape, out_specs = [], []
    if write_h:
        out_shape.append(jax.ShapeDtypeStruct((n_rows, D_MODEL), F32))
        out_specs.append(row_spec(D_MODEL))
    if proj_ws:
        for w, dt in zip(proj_ws, proj_dtypes):
            out_shape.append(jax.ShapeDtypeStruct((n_rows, w.shape[1]), dt))
            out_specs.append(row_spec(w.shape[1]))
    else:
        out_shape.append(jax.ShapeDtypeStruct((n_rows, D_MODEL), F32))
        out_specs.append(row_spec(D_MODEL))

    kern = functools.partial(_layer_io_kernel, n_acc=len(acc_pairs), n_proj=len(proj_ws),
                             write_h=write_h, scales=tuple(scales))
    return pl.pallas_call(
        kern, grid=grid, in_specs=in_specs, out_specs=out_specs, out_shape=out_shape,
        compiler_params=pltpu.CompilerParams(dimension_semantics=("arbitrary",),
                                             vmem_limit_bytes=VMEM_LIMIT),
        name="layer_io",
    )(*args)


def _gla_kernel(q_ref, k_ref, v_ref, gate_ref, down_ref, wup_ref, bup_ref, gain_ref, o_ref, st_ref):
    C = GLA_CHUNK

    @pl.when(pl.program_id(1) == 0)
    def _():
        st_ref[...] = jnp.zeros_like(st_ref)

    z = jnp.dot(down_ref[...], wup_ref[...], preferred_element_type=F32,
                precision=lax.Precision.HIGHEST) + bup_ref[...]
    log_a = (jnp.minimum(z, 0.0) - jnp.log1p(jnp.exp(-jnp.abs(z)))) * (1.0 / GLA_TAU)

    r_i = lax.broadcasted_iota(jnp.int32, (C, C), 0)
    c_i = lax.broadcasted_iota(jnp.int32, (C, C), 1)
    tri = (c_i <= r_i).astype(F32)
    lane_qk = lax.broadcasted_iota(jnp.int32, (C, GLA_QK), 1)
    head_masks = [(lane_qk // GLA_DK) == h for h in range(GLA_HEADS)]
    rs = lax.broadcasted_iota(jnp.int32, (GLA_HEADS * C, C), 0)
    cs = lax.broadcasted_iota(jnp.int32, (GLA_HEADS * C, C), 1)
    causal = (rs % C) >= cs
    st_r = lax.broadcasted_iota(jnp.int32, (A_WIDTH, GLA_QK), 0)
    st_c = lax.broadcasted_iota(jnp.int32, (A_WIDTH, GLA_QK), 1)
    same_head = (st_r // GLA_DV) == (st_c // GLA_DK)
    gain = gain_ref[...]

    for c in range(GLA_TILE // C):
        rows = slice(c * C, (c + 1) * C)
        g = log_a[rows]
        b = jnp.dot(tri, g, preferred_element_type=F32, precision=lax.Precision.HIGHEST)
        b_last = b[C - 1:C]
        q = q_ref[rows, :].astype(F32)
        k = k_ref[rows, :].astype(F32)
        v = v_ref[rows, :]
        q_e = q * jnp.exp(b)
        k_e = (k * jnp.exp(-b)).astype(BF16)
        k_l = (k * jnp.exp(b_last - b)).astype(BF16)
        decay = jnp.exp(b_last)

        q_stack = jnp.concatenate([jnp.where(m, q_e, 0.0) for m in head_masks], axis=0).astype(BF16)
        att = jnp.where(causal, _dot_nt(q_stack, k_e), 0.0).astype(BF16)
        o_full = _dot(att, v)
        o_intra = jnp.concatenate(
            [o_full[h * C:(h + 1) * C, h * GLA_DV:(h + 1) * GLA_DV] for h in range(GLA_HEADS)], axis=1)
        st = st_ref[...]
        o = o_intra + _dot_nt(q_e.astype(BF16), st.astype(BF16))
        kv_t = _dot_tn(v, k_l)
        st_ref[...] = st * decay + jnp.where(same_head, kv_t, 0.0)

        outs = []
        for h in range(GLA_HEADS):
            oh = o[:, h * GLA_DV:(h + 1) * GLA_DV]
            outs.append(oh * lax.rsqrt(jnp.mean(oh * oh, axis=-1, keepdims=True) + EPS) * gain)
        on = jnp.concatenate(outs, axis=1)
        o_ref[rows, :] = (on * _silu(gate_ref[rows, :].astype(F32))).astype(o_ref.dtype)


def _gla(aq, ak, av, agate, adown, w_up, b_up, gain):
    nt = SEQ // GLA_TILE
    row = lambda n: pl.BlockSpec((GLA_TILE, n), lambda b, s: (b * nt + s, 0))
    full = lambda a: pl.BlockSpec(a.shape, lambda b, s: (0,) * a.ndim)
    return pl.pallas_call(
        _gla_kernel, grid=(BATCH, nt),
        in_specs=[row(GLA_QK), row(GLA_QK), row(A_WIDTH), row(A_WIDTH), row(LANES),
                  full(w_up), full(b_up), full(gain)],
        out_specs=row(A_WIDTH),
        out_shape=jax.ShapeDtypeStruct((TOKENS, A_WIDTH), BF16),
        scratch_shapes=[pltpu.VMEM((A_WIDTH, GLA_QK), F32)],
        compiler_params=pltpu.CompilerParams(dimension_semantics=("arbitrary", "arbitrary"),
                                             vmem_limit_bytes=VMEM_LIMIT),
        name="gla",
    )(aq, ak, av, agate, adown, w_up, b_up, gain)


def _swap_halves(x):
    half = LANES // 2
    return jnp.concatenate([x[:, half:], x[:, :half]], axis=1)


def _swa_kernel(sink_ref, q_ref, kp_ref, k_ref, vp_ref, v_ref, gate_ref, bias_ref, o_ref):
    W = SWA_WINDOW
    t = pl.program_id(1)
    kcat = jnp.concatenate([kp_ref[...], k_ref[...]], axis=0)
    vcat = jnp.concatenate([vp_ref[...], v_ref[...]], axis=0)
    lo = lax.broadcasted_iota(jnp.int32, kcat.shape, 1) < (LANES // 2)
    zero = jnp.zeros_like(kcat)

    def placed(x):
        xs = _swap_halves(x)
        return [[jnp.where(lo, x, zero), jnp.where(lo, zero, xs)],
                [jnp.where(lo, xs, zero), jnp.where(lo, zero, x)]]

    kpl, vpl = placed(kcat), placed(vcat)
    row_top = lax.broadcasted_iota(jnp.int32, (2 * W, 1), 0) < W
    col_prev = lax.broadcasted_iota(jnp.int32, (2 * W, 2 * W), 1) < W

    for blk in range(SWA_TILE // W):
        rows = slice(blk * W, (blk + 1) * W)
        win = slice(blk * W, blk * W + 2 * W)
        for kh in range(SWA_KV_HEADS):
            j0, j1 = 2 * kh, 2 * kh + 1
            q_st = jnp.concatenate([q_ref[rows, j0 * LANES:(j0 + 1) * LANES],
                                    q_ref[rows, j1 * LANES:(j1 + 1) * LANES]], axis=0)
            o_st = None
            for par in range(2):
                s = _dot_nt(q_st, kpl[kh][par][win]) + bias_ref[kh, par]
                if blk == 0:
                    s = jnp.where(jnp.logical_and(t == 0, col_prev), -jnp.inf, s)
                sink = jnp.where(row_top, sink_ref[4 * kh + par], sink_ref[4 * kh + 2 + par])
                m = jnp.maximum(jnp.max(s, axis=-1, keepdims=True), sink)
                p = jnp.exp(s - m)
                den = jnp.sum(p, axis=-1, keepdims=True) + jnp.exp(sink - m)
                pn = (p * (1.0 / den)).astype(BF16)
                contrib = _dot(pn, vpl[kh][par][win])
                o_st = contrib if o_st is None else o_st + contrib
            for half, j in enumerate((j0, j1)):
                cols = slice(j * LANES, (j + 1) * LANES)
                gate = gate_ref[rows, cols].astype(F32)
                o_ref[rows, cols] = (o_st[half * W:(half + 1) * W] * _silu(gate)).astype(o_ref.dtype)


def _swa(bq, bk, bv, bgate, bias, sinks):
    nt = SEQ // SWA_TILE
    per = SWA_TILE // SWA_WINDOW
    row = lambda n: pl.BlockSpec((SWA_TILE, n), lambda b, t: (b * nt + t, 0))
    prev = pl.BlockSpec((SWA_WINDOW, LANES),
                        lambda b, t: (jnp.maximum((b * nt + t) * per - 1, 0), 0))
    return pl.pallas_call(
        _swa_kernel, grid=(BATCH, nt),
        in_specs=[pl.BlockSpec(memory_space=pltpu.SMEM),
                  row(B_WIDTH), prev, row(LANES), prev, row(LANES), row(B_WIDTH),
                  pl.BlockSpec(bias.shape, lambda b, t: (0, 0, 0, 0))],
        out_specs=row(B_WIDTH),
        out_shape=jax.ShapeDtypeStruct((TOKENS, B_WIDTH), BF16),
        compiler_params=pltpu.CompilerParams(dimension_semantics=("arbitrary", "arbitrary"),
                                             vmem_limit_bytes=VMEM_LIMIT),
        name="swa",
    )(sinks, bq, bk, bk, bv, bv, bgate, bias)


def _moba_kernel(cfar_ref, q_ref, k_ref, v_ref, gate_ref, bown_ref, bnear_ref, o_ref,
                 kaug_ref, kmh_ref, kml_ref, m_ref, l_ref, acc_ref):
    BLK = MOBA_BLOCK
    G = MOBA_GROUP
    R = G * BLK
    kvh = pl.program_id(1)
    i = pl.program_id(2)

    @pl.when(i == 0)
    def _():
        kaug_ref[:, 0:LANES] = k_ref[...]
        lane = lax.broadcasted_iota(jnp.int32, (BLK, LANES), 1)
        for j in range(MOBA_NBLK):
            kaug_ref[j * BLK:(j + 1) * BLK, LANES:2 * LANES] = jnp.where(lane == j, 1.0, 0.0).astype(BF16)
        km = jnp.mean(k_ref[...].astype(F32).reshape(MOBA_NBLK, BLK, MOBA_DH), axis=1)
        km_hi = km.astype(BF16)
        km_lo = (km - km_hi.astype(F32)).astype(BF16)
        kmh_ref[...] = jnp.zeros_like(kmh_ref)
        kml_ref[...] = jnp.zeros_like(kml_ref)
        kmh_ref[0:MOBA_NBLK, :] = km_hi
        kml_ref[0:MOBA_NBLK, :] = km_lo

    q = jnp.concatenate([q_ref[:, g * MOBA_DH:(g + 1) * MOBA_DH] for g in range(G)], axis=0)

    gate = _dot_nt(q, kmh_ref[...]) + _dot_nt(q, kml_ref[...])
    col = lax.broadcasted_iota(jnp.int32, (R, LANES), 1)
    colf = col.astype(F32)
    past = col < i
    g = jnp.where(past, gate, -jnp.inf)
    selected = jnp.zeros((R, LANES), dtype=jnp.bool_)
    for _ in range(MOBA_TOPK):
        mx = jnp.max(g, axis=-1, keepdims=True)
        first = jnp.min(jnp.where(g == mx, colf, 1e9), axis=-1, keepdims=True)
        pick = jnp.logical_and(colf == first, past)
        selected = jnp.logical_or(selected, pick)
        g = jnp.where(pick, -jnp.inf, g)

    row = lax.broadcasted_iota(jnp.int32, (R, 1), 0)
    cfar = jnp.where(row < BLK, cfar_ref[kvh, 0],
                     jnp.where(row < 2 * BLK, cfar_ref[kvh, 1],
                               jnp.where(row < 3 * BLK, cfar_ref[kvh, 2], cfar_ref[kvh, 3])))
    sel_bias = jnp.where(selected, jnp.where(col < i - 1, cfar, 0.0),
                         jnp.where(col == i, 0.0, NEG_BIG))
    q_aug = jnp.concatenate([q, sel_bias.astype(BF16)], axis=1)

    def block_rows(j):
        return pl.ds(pl.multiple_of(j * BLK, BLK), BLK)

    s = _dot_nt(q_aug, kaug_ref[block_rows(i), :]) + bown_ref[...].reshape(R, BLK)
    m0 = jnp.max(s, axis=-1, keepdims=True)
    p = jnp.exp(s - m0)
    m_ref[...] = m0
    l_ref[...] = jnp.sum(p, axis=-1, keepdims=True)
    acc_ref[...] = _dot(p.astype(BF16), v_ref[block_rows(i), :])

    def online(s, j):
        m_old = m_ref[...]
        m_new = jnp.maximum(m_old, jnp.max(s, axis=-1, keepdims=True))
        alpha = jnp.exp(m_old - m_new)
        p = jnp.exp(s - m_new)
        l_ref[...] = alpha * l_ref[...] + jnp.sum(p, axis=-1, keepdims=True)
        acc_ref[...] = alpha * acc_ref[...] + _dot(p.astype(BF16), v_ref[block_rows(j), :])
        m_ref[...] = m_new

    @pl.when(i >= 1)
    def _():
        online(_dot_nt(q_aug, kaug_ref[block_rows(i - 1), :]) + bnear_ref[...].reshape(R, BLK), i - 1)

    def far_body(j, carry):
        online(_dot_nt(q_aug, kaug_ref[block_rows(j), :]), j)
        return carry

    lax.fori_loop(0, jnp.maximum(i - 1, 0), far_body, 0)

    o = acc_ref[...] * (1.0 / l_ref[...])
    for g_ in range(G):
        cols = slice(g_ * MOBA_DH, (g_ + 1) * MOBA_DH)
        gt = gate_ref[:, cols].astype(F32)
        o_ref[:, cols] = (o[g_ * BLK:(g_ + 1) * BLK] * _silu(gt)).astype(o_ref.dtype)


def _moba(cq, ck, cv, cgate, bias_own, bias_near, cfar):
    R = MOBA_GROUP * MOBA_BLOCK
    gw = MOBA_GROUP * MOBA_DH
    qspec = pl.BlockSpec((MOBA_BLOCK, gw), lambda b, h, i: (b * MOBA_NBLK + i, h))
    kvspec = pl.BlockSpec((SEQ, MOBA_DH), lambda b, h, i: (b, h))
    bspec = pl.BlockSpec((None, MOBA_GROUP, MOBA_BLOCK, MOBA_BLOCK), lambda b, h, i: (h, 0, 0, 0))
    return pl.pallas_call(
        _moba_kernel, grid=(BATCH, MOBA_KV_HEADS, MOBA_NBLK),
        in_specs=[pl.BlockSpec(memory_space=pltpu.SMEM), qspec, kvspec, kvspec, qspec, bspec, bspec],
        out_specs=qspec,
        out_shape=jax.ShapeDtypeStruct((TOKENS, C_WIDTH), BF16),
        scratch_shapes=[pltpu.VMEM((SEQ, 2 * LANES), BF16),
                        pltpu.VMEM((LANES, MOBA_DH), BF16),
                        pltpu.VMEM((LANES, MOBA_DH), BF16),
                        pltpu.VMEM((R, 1), F32),
                        pltpu.VMEM((R, 1), F32),
                        pltpu.VMEM((R, MOBA_DH), F32)],
        compiler_params=pltpu.CompilerParams(
            dimension_semantics=("arbitrary", "arbitrary", "arbitrary"), vmem_limit_bytes=VMEM_LIMIT),
        name="moba",
    )(cfar, cq, ck, cv, cgate, bias_own, bias_near)


def _t5_bucket(dist):
    n = jnp.maximum(dist, 0)
    nf = jnp.maximum(n, 1).astype(F32)
    large = REL_MAX_EXACT + (jnp.log(nf / REL_MAX_EXACT) / math.log(REL_MAX_DIST / REL_MAX_EXACT)
                             * (REL_BUCKETS - REL_MAX_EXACT)).astype(jnp.int32)
    large = jnp.minimum(large, REL_BUCKETS - 1)
    return jnp.where(n < REL_MAX_EXACT, n, large)


def _bias_tables(rel_bias):
    max_dist = 2 * MOBA_BLOCK
    by_dist = rel_bias[_t5_bucket(jnp.arange(max_dist, dtype=jnp.int32))].T

    def toeplitz(dist, valid):
        tab = jnp.take(by_dist, jnp.asarray(np.clip(dist, 0, max_dist - 1).reshape(-1)), axis=1)
        tab = tab.reshape((ATTN_HEADS,) + dist.shape)
        return jnp.where(jnp.asarray(valid)[None], tab, -jnp.inf)

    W = SWA_WINDOW
    d_swa = np.arange(W)[:, None] + W - np.arange(2 * W)[None, :]
    swa = toeplitz(d_swa, (d_swa >= 0) & (d_swa < W))
    swa = swa.reshape(SWA_KV_HEADS, 2, 2, W, 2 * W).transpose(0, 2, 1, 3, 4).reshape(SWA_KV_HEADS, 2, 2 * W, 2 * W)

    B = MOBA_BLOCK
    d_own = np.arange(B)[:, None] - np.arange(B)[None, :]
    own = toeplitz(d_own, d_own >= 0).reshape(MOBA_KV_HEADS, MOBA_GROUP, B, B)
    d_near = d_own + B
    near = toeplitz(d_near, d_near >= 0).reshape(MOBA_KV_HEADS, MOBA_GROUP, B, B)
    cfar = rel_bias[REL_BUCKETS - 1].reshape(MOBA_KV_HEADS, MOBA_GROUP)
    return swa, own, near, cfar


def kernel(x, norm_gain, final_gain, rel_bias, w_in_even, gla_w_up, gla_b_up, gla_norm_gain, swa_sinks,
           w_out_even, w_in_odd, w_out_odd):
    assert x.shape == (BATCH, SEQ, D_MODEL)
    h0 = x.reshape(TOKENS, D_MODEL)
    bias_swa, bias_own, bias_near, cfar = _bias_tables(rel_bias)

    we = w_in_even[0]
    splits = np.cumsum([0, GLA_QK, GLA_QK, A_WIDTH, GLA_RANK, A_WIDTH, B_WIDTH, SWA_KV_HEADS * SWA_DH,
                        SWA_KV_HEADS * SWA_DH, B_WIDTH])
    cols = [we[:, splits[k]:splits[k + 1]] for k in range(9)]
    w_aq, w_ak, w_av, w_down, w_agate, w_bq, w_bk, w_bv, w_bgate = cols
    w_down = jnp.pad(w_down, ((0, 0), (0, LANES - GLA_RANK)))
    proj_ws = [w.astype(BF16) for w in (w_aq, w_ak, w_av, w_down, w_agate, w_bq, w_bk, w_bv, w_bgate)]
    proj_dt = [BF16, BF16, BF16, F32, BF16, BF16, BF16, BF16, BF16]
    scales = [GLA_DK ** -0.5, 1.0, 1.0, 1.0, 1.0, SWA_DH ** -0.5, 1.0, 1.0, 1.0]
    aq, ak, av, adown, agate, bq, bk, bv, bgate = _layer_io(
        h0, [], norm_gain[0:1], proj_ws, proj_dt, scales, write_h=False)

    w_up = jnp.pad(gla_w_up[0], ((0, LANES - GLA_RANK), (0, 0)))
    oa = _gla(aq, ak, av, agate, adown, w_up, gla_b_up[0:1], gla_norm_gain[0:1])
    ob = _swa(bq, bk, bv, bgate, bias_swa, swa_sinks[0])

    wo = w_out_even[0].astype(BF16)
    wi = w_in_odd[0]
    osp = np.cumsum([0, C_WIDTH, MOBA_KV_HEADS * MOBA_DH, MOBA_KV_HEADS * MOBA_DH, C_WIDTH])
    odd_ws = [wi[:, osp[k]:osp[k + 1]].astype(BF16) for k in range(4)]
    h1, cq, ck, cv, cgate = _layer_io(
        h0, [(oa, wo[:A_WIDTH]), (ob, wo[A_WIDTH:])], norm_gain[1:2], odd_ws, [BF16] * 4,
        [MOBA_DH ** -0.5, 1.0, 1.0, 1.0], write_h=True)

    oc = _moba(cq, ck, cv, cgate, bias_own, bias_near, cfar)

    (out,) = _layer_io(h1, [(oc, w_out_odd[0].astype(BF16))], final_gain.reshape(1, D_MODEL), [], [], [],
                       write_h=False)
    return out.reshape(BATCH, SEQ, D_MODEL)
```

```python
import functools
import math

import numpy as np
import jax
import jax.numpy as jnp
from jax import lax
from jax.experimental import pallas as pl
from jax.experimental.pallas import tpu as pltpu

D_MODEL = 1024
BATCH = 2
SEQ = 8192
TOKENS = BATCH * SEQ

REL_BUCKETS = 32
REL_MAX_EXACT = 16
REL_MAX_DIST = 128
ATTN_HEADS = 8

GLA_HEADS = 4
GLA_DK = 64
GLA_DV = 128
GLA_RANK = 16
GLA_TAU = 16.0
GLA_CHUNK = 64

SWA_HEADS = 8
SWA_KV_HEADS = 2
SWA_DH = 64
SWA_WINDOW = 128

MOBA_HEADS = 8
MOBA_KV_HEADS = 2
MOBA_DH = 128
MOBA_BLOCK = 256
MOBA_TOPK = 3
MOBA_GROUP = MOBA_HEADS // MOBA_KV_HEADS
MOBA_NBLK = SEQ // MOBA_BLOCK

EPS = 1e-6
LANES = 128
NEG_BIG = -1e30
VMEM_LIMIT = 48 * 1024 * 1024

A_WIDTH = GLA_HEADS * GLA_DV
B_WIDTH = SWA_HEADS * SWA_DH
C_WIDTH = MOBA_HEADS * MOBA_DH
GLA_QK = GLA_HEADS * GLA_DK

ROW_TILE = 512
GLA_TILE = 512
SWA_TILE = 512

F32 = jnp.float32
BF16 = jnp.bfloat16


def _dot(a, b):
    return jnp.dot(a, b, preferred_element_type=F32)


def _dot_nt(a, b):
    return lax.dot_general(a, b, (((1,), (1,)), ((), ())), preferred_element_type=F32)


def _dot_tn(a, b):
    return lax.dot_general(a, b, (((0,), (0,)), ((), ())), preferred_element_type=F32)


def _silu(x):
    return x * (1.0 / (1.0 + jnp.exp(-x)))


def _t5_thresholds():
    d = np.arange(REL_MAX_DIST + 1)
    nf = np.maximum(d, 1).astype(np.float32)
    large = REL_MAX_EXACT + (np.log(nf / np.float32(REL_MAX_EXACT))
                             / np.float32(math.log(REL_MAX_DIST / REL_MAX_EXACT))
                             * np.float32(REL_BUCKETS - REL_MAX_EXACT)).astype(np.int32)
    bucket = np.where(d < REL_MAX_EXACT, d, np.minimum(large, REL_BUCKETS - 1))
    assert np.all(np.diff(bucket) >= 0) and bucket[-1] == REL_BUCKETS - 1
    return [int(np.argmax(bucket >= b)) for b in range(REL_BUCKETS)]


_T5_THRESH = _t5_thresholds()


def _t5_bias(dist, rb_ref, head):
    out = jnp.full(dist.shape, rb_ref[0, head], F32)
    for b in range(1, REL_BUCKETS):
        out = jnp.where(dist >= _T5_THRESH[b], rb_ref[b, head], out)
    return out


def _layer_io_kernel(*refs, n_acc, n_proj, write_h, scales):
    h_ref = refs[0]
    pos = 1
    acc_pairs = []
    for _ in range(n_acc):
        acc_pairs.append((refs[pos], refs[pos + 1]))
        pos += 2
    gain_ref = refs[pos]
    pos += 1
    wp_refs = refs[pos:pos + n_proj]
    pos += n_proj
    out_refs = refs[pos:]

    h = h_ref[...]
    for a_ref, w_ref in acc_pairs:
        h = h + _dot(a_ref[...], w_ref[...])
    o = 0
    if write_h:
        out_refs[0][...] = h
        o = 1
    y = h * lax.rsqrt(jnp.mean(h * h, axis=-1, keepdims=True) + EPS) * gain_ref[...]
    if n_proj == 0:
        out_refs[o][...] = y
        return
    yb = y.astype(BF16)
    for k in range(n_proj):
        r = _dot(yb, wp_refs[k][...])
        if scales[k] != 1.0:
            r = r * scales[k]
        out_refs[o + k][...] = r.astype(out_refs[o + k].dtype)


def _layer_io(h, acc_pairs, gain, proj_ws, proj_dtypes, scales, write_h):
    n_rows = h.shape[0]
    grid = (n_rows // ROW_TILE,)
    row_spec = lambda n: pl.BlockSpec((ROW_TILE, n), lambda i: (i, 0))
    full_spec = lambda a: pl.BlockSpec(a.shape, lambda i: (0,) * a.ndim)

    args = [h]
    in_specs = [row_spec(D_MODEL)]
    for a, w in acc_pairs:
        args += [a, w]
        in_specs += [row_spec(a.shape[1]), full_spec(w)]
    args.append(gain)
    in_specs.append(full_spec(gain))
    for w in proj_ws:
        args.append(w)
        in_specs.append(full_spec(w))

    out_shape, out_specs = [], []
    if write_h:
        out_shape.append(jax.ShapeDtypeStruct((n_rows, D_MODEL), F32))
        out_specs.append(row_spec(D_MODEL))
    if proj_ws:
        for w, dt in zip(proj_ws, proj_dtypes):
            out_shape.append(jax.ShapeDtypeStruct((n_rows, w.shape[1]), dt))
            out_specs.append(row_spec(w.shape[1]))
    else:
        out_shape.append(jax.ShapeDtypeStruct((n_rows, D_MODEL), F32))
        out_specs.append(row_spec(D_MODEL))

    kern = functools.partial(_layer_io_kernel, n_acc=len(acc_pairs), n_proj=len(proj_ws),
                             write_h=write_h, scales=tuple(scales))
    return pl.pallas_call(
        kern, grid=grid, in_specs=in_specs, out_specs=out_specs, out_shape=out_shape,
        compiler_params=pltpu.CompilerParams(dimension_semantics=("arbitrary",),
                                             vmem_limit_bytes=VMEM_LIMIT),
        name="layer_io",
    )(*args)


def _gla_kernel(q_ref, k_ref, v_ref, gate_ref, down_ref, wup_ref, bup_ref, gain_ref, o_ref, st_ref):
    C = GLA_CHUNK

    @pl.when(pl.program_id(1) == 0)
    def _():
        st_ref[...] = jnp.zeros_like(st_ref)

    z = jnp.dot(down_ref[...], wup_ref[...], preferred_element_type=F32,
                precision=lax.Precision.HIGHEST) + bup_ref[...]
    log_a = (jnp.minimum(z, 0.0) - jnp.log1p(jnp.exp(-jnp.abs(z)))) * (1.0 / GLA_TAU)

    r_i = lax.broadcasted_iota(jnp.int32, (C, C), 0)
    c_i = lax.broadcasted_iota(jnp.int32, (C, C), 1)
    tri = (c_i <= r_i).astype(F32)
    lane_qk = lax.broadcasted_iota(jnp.int32, (C, GLA_QK), 1)
    head_masks = [(lane_qk // GLA_DK) == h for h in range(GLA_HEADS)]
    rs = lax.broadcasted_iota(jnp.int32, (GLA_HEADS * C, C), 0)
    cs = lax.broadcasted_iota(jnp.int32, (GLA_HEADS * C, C), 1)
    causal = (rs % C) >= cs
    st_r = lax.broadcasted_iota(jnp.int32, (A_WIDTH, GLA_QK), 0)
    st_c = lax.broadcasted_iota(jnp.int32, (A_WIDTH, GLA_QK), 1)
    same_head = (st_r // GLA_DV) == (st_c // GLA_DK)
    gain = gain_ref[...]

    for c in range(GLA_TILE // C):
        rows = slice(c * C, (c + 1) * C)
        g = log_a[rows]
        b = jnp.dot(tri, g, preferred_element_type=F32, precision=lax.Precision.HIGHEST)
        b_last = b[C - 1:C]
        q = q_ref[rows, :].astype(F32)
        k = k_ref[rows, :].astype(F32)
        v = v_ref[rows, :]
        q_e = q * jnp.exp(b)
        k_e = (k * jnp.exp(-b)).astype(BF16)
        k_l = (k * jnp.exp(b_last - b)).astype(BF16)
        decay = jnp.exp(b_last)

        q_stack = jnp.concatenate([jnp.where(m, q_e, 0.0) for m in head_masks], axis=0).astype(BF16)
        att = jnp.where(causal, _dot_nt(q_stack, k_e), 0.0).astype(BF16)
        o_full = _dot(att, v)
        o_intra = jnp.concatenate(
            [o_full[h * C:(h + 1) * C, h * GLA_DV:(h + 1) * GLA_DV] for h in range(GLA_HEADS)], axis=1)
        st = st_ref[...]
        o = o_intra + _dot_nt(q_e.astype(BF16), st.astype(BF16))
        kv_t = _dot_tn(v, k_l)
        st_ref[...] = st * decay + jnp.where(same_head, kv_t, 0.0)

        outs = []
        for h in range(GLA_HEADS):
            oh = o[:, h * GLA_DV:(h + 1) * GLA_DV]
            outs.append(oh * lax.rsqrt(jnp.mean(oh * oh, axis=-1, keepdims=True) + EPS) * gain)
        on = jnp.concatenate(outs, axis=1)
        o_ref[rows, :] = (on * _silu(gate_ref[rows, :].astype(F32))).astype(o_ref.dtype)


def _gla(aq, ak, av, agate, adown, w_up, b_up, gain):
    nt = SEQ // GLA_TILE
    row = lambda n: pl.BlockSpec((GLA_TILE, n), lambda b, s: (b * nt + s, 0))
    full = lambda a: pl.BlockSpec(a.shape, lambda b, s: (0,) * a.ndim)
    return pl.pallas_call(
        _gla_kernel, grid=(BATCH, nt),
        in_specs=[row(GLA_QK), row(GLA_QK), row(A_WIDTH), row(A_WIDTH), row(LANES),
                  full(w_up), full(b_up), full(gain)],
        out_specs=row(A_WIDTH),
        out_shape=jax.ShapeDtypeStruct((TOKENS, A_WIDTH), BF16),
        scratch_shapes=[pltpu.VMEM((A_WIDTH, GLA_QK), F32)],
        compiler_params=pltpu.CompilerParams(dimension_semantics=("arbitrary", "arbitrary"),
                                             vmem_limit_bytes=VMEM_LIMIT),
        name="gla",
    )(aq, ak, av, agate, adown, w_up, b_up, gain)


def _swap_halves(x):
    half = LANES // 2
    return jnp.concatenate([x[:, half:], x[:, :half]], axis=1)


def _swa_kernel(sink_ref, rb_ref, q_ref, kp_ref, k_ref, vp_ref, v_ref, gate_ref, o_ref, bias_ref):
    W = SWA_WINDOW
    t = pl.program_id(1)

    @pl.when(jnp.logical_and(pl.program_id(0) == 0, t == 0))
    def _():
        qi = lax.broadcasted_iota(jnp.int32, (W, 2 * W), 0)
        kj = lax.broadcasted_iota(jnp.int32, (W, 2 * W), 1)
        dist = qi + W - kj
        in_win = jnp.logical_and(dist >= 0, dist < W)
        for kh in range(SWA_KV_HEADS):
            for par in range(2):
                for half in range(2):
                    b = _t5_bias(dist, rb_ref, 4 * kh + par + 2 * half)
                    bias_ref[kh, par, half * W:(half + 1) * W, :] = jnp.where(in_win, b, -jnp.inf)

    kcat = jnp.concatenate([kp_ref[...], k_ref[...]], axis=0)
    vcat = jnp.concatenate([vp_ref[...], v_ref[...]], axis=0)
    lo = lax.broadcasted_iota(jnp.int32, kcat.shape, 1) < (LANES // 2)
    zero = jnp.zeros_like(kcat)

    def placed(x):
        xs = _swap_halves(x)
        return [[jnp.where(lo, x, zero), jnp.where(lo, zero, xs)],
                [jnp.where(lo, xs, zero), jnp.where(lo, zero, x)]]

    kpl, vpl = placed(kcat), placed(vcat)
    row_top = lax.broadcasted_iota(jnp.int32, (2 * W, 1), 0) < W
    col_prev = lax.broadcasted_iota(jnp.int32, (2 * W, 2 * W), 1) < W

    for blk in range(SWA_TILE // W):
        rows = slice(blk * W, (blk + 1) * W)
        win = slice(blk * W, blk * W + 2 * W)
        for kh in range(SWA_KV_HEADS):
            j0, j1 = 2 * kh, 2 * kh + 1
            q_st = jnp.concatenate([q_ref[rows, j0 * LANES:(j0 + 1) * LANES],
                                    q_ref[rows, j1 * LANES:(j1 + 1) * LANES]], axis=0)
            o_st = None
            for par in range(2):
                s = _dot_nt(q_st, kpl[kh][par][win]) + bias_ref[kh, par]
                if blk == 0:
                    s = jnp.where(jnp.logical_and(t == 0, col_prev), -jnp.inf, s)
                sink = jnp.where(row_top, sink_ref[4 * kh + par], sink_ref[4 * kh + 2 + par])
                m = jnp.maximum(jnp.max(s, axis=-1, keepdims=True), sink)
                p = jnp.exp(s - m)
                den = jnp.sum(p, axis=-1, keepdims=True) + jnp.exp(sink - m)
                pn = (p * (1.0 / den)).astype(BF16)
                contrib = _dot(pn, vpl[kh][par][win])
                o_st = contrib if o_st is None else o_st + contrib
            for half, j in enumerate((j0, j1)):
                cols = slice(j * LANES, (j + 1) * LANES)
                gate = gate_ref[rows, cols].astype(F32)
                o_ref[rows, cols] = (o_st[half * W:(half + 1) * W] * _silu(gate)).astype(o_ref.dtype)


def _swa(bq, bk, bv, bgate, rel_bias, sinks):
    nt = SEQ // SWA_TILE
    per = SWA_TILE // SWA_WINDOW
    row = lambda n: pl.BlockSpec((SWA_TILE, n), lambda b, t: (b * nt + t, 0))
    prev = pl.BlockSpec((SWA_WINDOW, LANES),
                        lambda b, t: (jnp.maximum((b * nt + t) * per - 1, 0), 0))
    smem = pl.BlockSpec(memory_space=pltpu.SMEM)
    return pl.pallas_call(
        _swa_kernel, grid=(BATCH, nt),
        in_specs=[smem, smem, row(B_WIDTH), prev, row(LANES), prev, row(LANES), row(B_WIDTH)],
        out_specs=row(B_WIDTH),
        out_shape=jax.ShapeDtypeStruct((TOKENS, B_WIDTH), BF16),
        scratch_shapes=[pltpu.VMEM((SWA_KV_HEADS, 2, 2 * SWA_WINDOW, 2 * SWA_WINDOW), F32)],
        compiler_params=pltpu.CompilerParams(dimension_semantics=("arbitrary", "arbitrary"),
                                             vmem_limit_bytes=VMEM_LIMIT),
        name="swa",
    )(sinks, rel_bias, bq, bk, bk, bv, bv, bgate)


def _moba_kernel(rb_ref, q_ref, k_ref, v_ref, gate_ref, o_ref,
                 kaug_ref, vt_ref, km_ref, bown_ref, bnear_ref, qat_ref, m_ref, l_ref, acc_ref):
    BLK = MOBA_BLOCK
    G = MOBA_GROUP
    R = G * BLK
    kvh = pl.program_id(1)
    i = pl.program_id(2)

    @pl.when(i == 0)
    def _():
        lane = lax.broadcasted_iota(jnp.int32, (BLK, LANES), 1)

        def prep(j, carry):
            rows = pl.ds(pl.multiple_of(j * BLK, BLK), BLK)
            kb = k_ref[rows, :]
            kaug_ref[j, :, 0:LANES] = kb
            kaug_ref[j, :, LANES:2 * LANES] = jnp.where(lane == j, 1.0, 0.0).astype(BF16)
            km_ref[pl.ds(j, 1), :] = jnp.mean(kb.astype(F32), axis=0, keepdims=True)
            vt_ref[j] = v_ref[rows, :].astype(F32).T.astype(BF16)
            return carry

        lax.fori_loop(0, MOBA_NBLK, prep, 0)
        qat_ref[MOBA_DH + MOBA_NBLK:, :] = jnp.zeros((2 * LANES - MOBA_DH - MOBA_NBLK, R), BF16)

        tk = lax.broadcasted_iota(jnp.int32, (BLK, BLK), 0)
        tq = lax.broadcasted_iota(jnp.int32, (BLK, BLK), 1)
        d_own = tq - tk
        for g in range(G):
            head = kvh * G + g
            cols = slice(g * BLK, (g + 1) * BLK)
            bown_ref[:, cols] = jnp.where(d_own >= 0, _t5_bias(d_own, rb_ref, head), -jnp.inf)
            bnear_ref[:, cols] = _t5_bias(d_own + BLK, rb_ref, head)

    q = jnp.concatenate([q_ref[:, g * MOBA_DH:(g + 1) * MOBA_DH] for g in range(G)], axis=0)
    q_t = q.astype(F32).T.astype(BF16)

    km = km_ref[...]
    km_hi = km.astype(BF16)
    km_lo = (km - km_hi.astype(F32)).astype(BF16)
    gate = _dot(km_hi, q_t) + _dot(km_lo, q_t)
    blk = lax.broadcasted_iota(jnp.int32, (MOBA_NBLK, R), 0)
    blkf = blk.astype(F32)
    past = blk < i
    g_ = jnp.where(past, gate, -jnp.inf)
    selected = jnp.zeros((MOBA_NBLK, R), dtype=jnp.bool_)
    for _ in range(MOBA_TOPK):
        mx = jnp.max(g_, axis=0, keepdims=True)
        first = jnp.min(jnp.where(g_ == mx, blkf, 1e9), axis=0, keepdims=True)
        pick = jnp.logical_and(blkf == first, past)
        selected = jnp.logical_or(selected, pick)
        g_ = jnp.where(pick, -jnp.inf, g_)

    colh = lax.broadcasted_iota(jnp.int32, (1, R), 1) // BLK
    far_row = REL_BUCKETS - 1
    cfar = jnp.where(colh == 0, rb_ref[far_row, kvh * G],
                     jnp.where(colh == 1, rb_ref[far_row, kvh * G + 1],
                               jnp.where(colh == 2, rb_ref[far_row, kvh * G + 2],
                                         rb_ref[far_row, kvh * G + 3])))
    sel_bias = jnp.where(selected, jnp.where(blk < i - 1, cfar, 0.0),
                         jnp.where(blk == i, 0.0, NEG_BIG))
    qat_ref[0:MOBA_DH, :] = q_t
    qat_ref[MOBA_DH:MOBA_DH + MOBA_NBLK, :] = sel_bias.astype(BF16)

    s = _dot(kaug_ref[i], qat_ref[...]) + bown_ref[...]
    m0 = jnp.max(s, axis=0, keepdims=True)
    p = jnp.exp(s - m0)
    m_ref[...] = m0
    l_ref[...] = jnp.sum(p, axis=0, keepdims=True)
    acc_ref[...] = _dot(vt_ref[i], p.astype(BF16))

    def online(s, j):
        m_old = m_ref[...]
        m_new = jnp.maximum(m_old, jnp.max(s, axis=0, keepdims=True))
        alpha = jnp.exp(m_old - m_new)
        p = jnp.exp(s - m_new)
        l_ref[...] = alpha * l_ref[...] + jnp.sum(p, axis=0, keepdims=True)
        acc_ref[...] = alpha * acc_ref[...] + _dot(vt_ref[j], p.astype(BF16))
        m_ref[...] = m_new

    @pl.when(i >= 1)
    def _():
        online(_dot(kaug_ref[i - 1], qat_ref[...]) + bnear_ref[...], i - 1)

    def far_body(j, carry):
        online(_dot(kaug_ref[j], qat_ref[...]), j)
        return carry

    lax.fori_loop(0, jnp.maximum(i - 1, 0), far_body, 0)

    o = (acc_ref[...] * (1.0 / l_ref[...])).T
    for g in range(G):
        cols = slice(g * MOBA_DH, (g + 1) * MOBA_DH)
        gt = gate_ref[:, cols].astype(F32)
        o_ref[:, cols] = (o[g * BLK:(g + 1) * BLK] * _silu(gt)).astype(o_ref.dtype)


def _moba(cq, ck, cv, cgate, rel_bias):
    R = MOBA_GROUP * MOBA_BLOCK
    gw = MOBA_GROUP * MOBA_DH
    qspec = pl.BlockSpec((MOBA_BLOCK, gw), lambda b, h, i: (b * MOBA_NBLK + i, h))
    kvspec = pl.BlockSpec((SEQ, MOBA_DH), lambda b, h, i: (b, h))
    return pl.pallas_call(
        _moba_kernel, grid=(BATCH, MOBA_KV_HEADS, MOBA_NBLK),
        in_specs=[pl.BlockSpec(memory_space=pltpu.SMEM), qspec, kvspec, kvspec, qspec],
        out_specs=qspec,
        out_shape=jax.ShapeDtypeStruct((TOKENS, C_WIDTH), BF16),
        scratch_shapes=[pltpu.VMEM((MOBA_NBLK, MOBA_BLOCK, 2 * LANES), BF16),
                        pltpu.VMEM((MOBA_NBLK, MOBA_DH, MOBA_BLOCK), BF16),
                        pltpu.VMEM((MOBA_NBLK, MOBA_DH), F32),
                        pltpu.VMEM((MOBA_BLOCK, R), F32),
                        pltpu.VMEM((MOBA_BLOCK, R), F32),
                        pltpu.VMEM((2 * LANES, R), BF16),
                        pltpu.VMEM((1, R), F32),
                        pltpu.VMEM((1, R), F32),
                        pltpu.VMEM((MOBA_DH, R), F32)],
        compiler_params=pltpu.CompilerParams(
            dimension_semantics=("arbitrary", "arbitrary", "arbitrary"), vmem_limit_bytes=VMEM_LIMIT),
        name="moba",
    )(rel_bias, cq, ck, cv, cgate)


def kernel(x, norm_gain, final_gain, rel_bias, w_in_even, gla_w_up, gla_b_up, gla_norm_gain, swa_sinks,
           w_out_even, w_in_odd, w_out_odd):
    assert x.shape == (BATCH, SEQ, D_MODEL)
    h0 = x.reshape(TOKENS, D_MODEL)

    we = w_in_even[0]
    splits = np.cumsum([0, GLA_QK, GLA_QK, A_WIDTH, GLA_RANK, A_WIDTH, B_WIDTH, SWA_KV_HEADS * SWA_DH,
                        SWA_KV_HEADS * SWA_DH, B_WIDTH])
    cols = [we[:, splits[k]:splits[k + 1]] for k in range(9)]
    w_aq, w_ak, w_av, w_down, w_agate, w_bq, w_bk, w_bv, w_bgate = cols
    w_down = jnp.pad(w_down, ((0, 0), (0, LANES - GLA_RANK)))
    proj_ws = [w.astype(BF16) for w in (w_aq, w_ak, w_av, w_down, w_agate, w_bq, w_bk, w_bv, w_bgate)]
    proj_dt = [BF16, BF16, BF16, F32, BF16, BF16, BF16, BF16, BF16]
    scales = [GLA_DK ** -0.5, 1.0, 1.0, 1.0, 1.0, SWA_DH ** -0.5, 1.0, 1.0, 1.0]
    aq, ak, av, adown, agate, bq, bk, bv, bgate = _layer_io(
        h0, [], norm_gain[0:1], proj_ws, proj_dt, scales, write_h=False)

    w_up = jnp.pad(gla_w_up[0], ((0, LANES - GLA_RANK), (0, 0)))
    oa = _gla(aq, ak, av, agate, adown, w_up, gla_b_up[0:1], gla_norm_gain[0:1])
    ob = _swa(bq, bk, bv, bgate, rel_bias, swa_sinks[0])

    wo = w_out_even[0].astype(BF16)
    wi = w_in_odd[0]
    osp = np.cumsum([0, C_WIDTH, MOBA_KV_HEADS * MOBA_DH, MOBA_KV_HEADS * MOBA_DH, C_WIDTH])
    odd_ws = [wi[:, osp[k]:osp[k + 1]].astype(BF16) for k in range(4)]
    h1, cq, ck, cv, cgate = _layer_io(
        h0, [(oa, wo[:A_WIDTH]), (ob, wo[A_WIDTH:])], norm_gain[1:2], odd_ws, [BF16] * 4,
        [MOBA_DH ** -0.5, 1.0, 1.0, 1.0], write_h=True)

    oc = _moba(cq, ck, cv, cgate, rel_bias)

    (out,) = _layer_io(h1, [(oc, w_out_odd[0].astype(BF16))], final_gain.reshape(1, D_MODEL), [], [], [],
                       write_h=False)
    return out.reshape(BATCH, SEQ, D_MODEL)
```

```python
import functools
import math

import numpy as np
import jax
import jax.numpy as jnp
from jax import lax
from jax.experimental import pallas as pl
from jax.experimental.pallas import tpu as pltpu

D_MODEL = 1024
BATCH = 2
SEQ = 8192
TOKENS = BATCH * SEQ

REL_BUCKETS = 32
REL_MAX_EXACT = 16
REL_MAX_DIST = 128
ATTN_HEADS = 8

GLA_HEADS = 4
GLA_DK = 64
GLA_DV = 128
GLA_RANK = 16
GLA_TAU = 16.0
GLA_CHUNK = 64

SWA_HEADS = 8
SWA_KV_HEADS = 2
SWA_DH = 64
SWA_WINDOW = 128

MOBA_HEADS = 8
MOBA_KV_HEADS = 2
MOBA_DH = 128
MOBA_BLOCK = 256
MOBA_TOPK = 3
MOBA_GROUP = MOBA_HEADS // MOBA_KV_HEADS
MOBA_NBLK = SEQ // MOBA_BLOCK
MOBA_VT_ROWS = MOBA_DH + 16
LOG2E = math.log2(math.e)

EPS = 1e-6
LANES = 128
NEG_BIG = -1e30
VMEM_LIMIT = 48 * 1024 * 1024

A_WIDTH = GLA_HEADS * GLA_DV
B_WIDTH = SWA_HEADS * SWA_DH
C_WIDTH = MOBA_HEADS * MOBA_DH
GLA_QK = GLA_HEADS * GLA_DK

ROW_TILE = 512
GLA_TILE = 512
SWA_TILE = 512

F32 = jnp.float32
BF16 = jnp.bfloat16


def _dot(a, b):
    return jnp.dot(a, b, preferred_element_type=F32)


def _dot_nt(a, b):
    return lax.dot_general(a, b, (((1,), (1,)), ((), ())), preferred_element_type=F32)


def _dot_tn(a, b):
    return lax.dot_general(a, b, (((0,), (0,)), ((), ())), preferred_element_type=F32)


def _silu(x):
    return x * (1.0 / (1.0 + jnp.exp(-x)))


def _t5_thresholds():
    d = np.arange(REL_MAX_DIST + 1)
    nf = np.maximum(d, 1).astype(np.float32)
    large = REL_MAX_EXACT + (np.log(nf / np.float32(REL_MAX_EXACT))
                             / np.float32(math.log(REL_MAX_DIST / REL_MAX_EXACT))
                             * np.float32(REL_BUCKETS - REL_MAX_EXACT)).astype(np.int32)
    bucket = np.where(d < REL_MAX_EXACT, d, np.minimum(large, REL_BUCKETS - 1))
    assert np.all(np.diff(bucket) >= 0) and bucket[-1] == REL_BUCKETS - 1
    return [int(np.argmax(bucket >= b)) for b in range(REL_BUCKETS)]


_T5_THRESH = _t5_thresholds()


def _t5_bias(dist, rb_ref, head):
    out = jnp.full(dist.shape, rb_ref[0, head], F32)
    for b in range(1, REL_BUCKETS):
        out = jnp.where(dist >= _T5_THRESH[b], rb_ref[b, head], out)
    return out


def _layer_io_kernel(*refs, n_acc, n_proj, write_h, scales):
    h_ref = refs[0]
    pos = 1
    acc_pairs = []
    for _ in range(n_acc):
        acc_pairs.append((refs[pos], refs[pos + 1]))
        pos += 2
    gain_ref = refs[pos]
    pos += 1
    wp_refs = refs[pos:pos + n_proj]
    pos += n_proj
    out_refs = refs[pos:]

    h = h_ref[...]
    for a_ref, w_ref in acc_pairs:
        h = h + _dot(a_ref[...], w_ref[...])
    o = 0
    if write_h:
        out_refs[0][...] = h
        o = 1
    y = h * lax.rsqrt(jnp.mean(h * h, axis=-1, keepdims=True) + EPS) * gain_ref[...]
    if n_proj == 0:
        out_refs[o][...] = y
        return
    yb = y.astype(BF16)
    for k in range(n_proj):
        r = _dot(yb, wp_refs[k][...])
        if scales[k] != 1.0:
            r = r * scales[k]
        out_refs[o + k][...] = r.astype(out_refs[o + k].dtype)


def _layer_io(h, acc_pairs, gain, proj_ws, proj_dtypes, scales, write_h):
    n_rows = h.shape[0]
    grid = (n_rows // ROW_TILE,)
    row_spec = lambda n: pl.BlockSpec((ROW_TILE, n), lambda i: (i, 0))
    full_spec = lambda a: pl.BlockSpec(a.shape, lambda i: (0,) * a.ndim)

    args = [h]
    in_specs = [row_spec(D_MODEL)]
    for a, w in acc_pairs:
        args += [a, w]
        in_specs += [row_spec(a.shape[1]), full_spec(w)]
    args.append(gain)
    in_specs.append(full_spec(gain))
    for w in proj_ws:
        args.append(w)
        in_specs.append(full_spec(w))

    out_shape, out_specs = [], []
    if write_h:
        out_shape.append(jax.ShapeDtypeStruct((n_rows, D_MODEL), F32))
        out_specs.append(row_spec(D_MODEL))
    if proj_ws:
        for w, dt in zip(proj_ws, proj_dtypes):
            out_shape.append(jax.ShapeDtypeStruct((n_rows, w.shape[1]), dt))
            out_specs.append(row_spec(w.shape[1]))
    else:
        out_shape.append(jax.ShapeDtypeStruct((n_rows, D_MODEL), F32))
        out_specs.append(row_spec(D_MODEL))

    kern = functools.partial(_layer_io_kernel, n_acc=len(acc_pairs), n_proj=len(proj_ws),
                             write_h=write_h, scales=tuple(scales))
    return pl.pallas_call(
        kern, grid=grid, in_specs=in_specs, out_specs=out_specs, out_shape=out_shape,
        compiler_params=pltpu.CompilerParams(dimension_semantics=("arbitrary",),
                                             vmem_limit_bytes=VMEM_LIMIT),
        name="layer_io",
    )(*args)


def _gla_kernel(q_ref, k_ref, v_ref, gate_ref, down_ref, wup_ref, bup_ref, gain_ref, o_ref, st_ref):
    C = GLA_CHUNK

    @pl.when(pl.program_id(1) == 0)
    def _():
        st_ref[...] = jnp.zeros_like(st_ref)

    z = jnp.dot(down_ref[...], wup_ref[...], preferred_element_type=F32,
                precision=lax.Precision.HIGHEST) + bup_ref[...]
    log_a = (jnp.minimum(z, 0.0) - jnp.log1p(jnp.exp(-jnp.abs(z)))) * (1.0 / GLA_TAU)

    r_i = lax.broadcasted_iota(jnp.int32, (C, C), 0)
    c_i = lax.broadcasted_iota(jnp.int32, (C, C), 1)
    tri = (c_i <= r_i).astype(F32)
    lane_qk = lax.broadcasted_iota(jnp.int32, (C, GLA_QK), 1)
    head_masks = [(lane_qk // GLA_DK) == h for h in range(GLA_HEADS)]
    rs = lax.broadcasted_iota(jnp.int32, (GLA_HEADS * C, C), 0)
    cs = lax.broadcasted_iota(jnp.int32, (GLA_HEADS * C, C), 1)
    causal = (rs % C) >= cs
    st_r = lax.broadcasted_iota(jnp.int32, (A_WIDTH, GLA_QK), 0)
    st_c = lax.broadcasted_iota(jnp.int32, (A_WIDTH, GLA_QK), 1)
    same_head = (st_r // GLA_DV) == (st_c // GLA_DK)
    gain = gain_ref[...]

    for c in range(GLA_TILE // C):
        rows = slice(c * C, (c + 1) * C)
        g = log_a[rows]
        b = jnp.dot(tri, g, preferred_element_type=F32, precision=lax.Precision.HIGHEST)
        b_last = b[C - 1:C]
        q = q_ref[rows, :].astype(F32)
        k = k_ref[rows, :].astype(F32)
        v = v_ref[rows, :]
        q_e = q * jnp.exp(b)
        k_e = (k * jnp.exp(-b)).astype(BF16)
        k_l = (k * jnp.exp(b_last - b)).astype(BF16)
        decay = jnp.exp(b_last)

        q_stack = jnp.concatenate([jnp.where(m, q_e, 0.0) for m in head_masks], axis=0).astype(BF16)
        att = jnp.where(causal, _dot_nt(q_stack, k_e), 0.0).astype(BF16)
        o_full = _dot(att, v)
        o_intra = jnp.concatenate(
            [o_full[h * C:(h + 1) * C, h * GLA_DV:(h + 1) * GLA_DV] for h in range(GLA_HEADS)], axis=1)
        st = st_ref[...]
        o = o_intra + _dot_nt(q_e.astype(BF16), st.astype(BF16))
        kv_t = _dot_tn(v, k_l)
        st_ref[...] = st * decay + jnp.where(same_head, kv_t, 0.0)

        outs = []
        for h in range(GLA_HEADS):
            oh = o[:, h * GLA_DV:(h + 1) * GLA_DV]
            outs.append(oh * lax.rsqrt(jnp.mean(oh * oh, axis=-1, keepdims=True) + EPS) * gain)
        on = jnp.concatenate(outs, axis=1)
        o_ref[rows, :] = (on * _silu(gate_ref[rows, :].astype(F32))).astype(o_ref.dtype)


def _gla(aq, ak, av, agate, adown, w_up, b_up, gain):
    nt = SEQ // GLA_TILE
    row = lambda n: pl.BlockSpec((GLA_TILE, n), lambda b, s: (b * nt + s, 0))
    full = lambda a: pl.BlockSpec(a.shape, lambda b, s: (0,) * a.ndim)
    return pl.pallas_call(
        _gla_kernel, grid=(BATCH, nt),
        in_specs=[row(GLA_QK), row(GLA_QK), row(A_WIDTH), row(A_WIDTH), row(LANES),
                  full(w_up), full(b_up), full(gain)],
        out_specs=row(A_WIDTH),
        out_shape=jax.ShapeDtypeStruct((TOKENS, A_WIDTH), BF16),
        scratch_shapes=[pltpu.VMEM((A_WIDTH, GLA_QK), F32)],
        compiler_params=pltpu.CompilerParams(dimension_semantics=("arbitrary", "arbitrary"),
                                             vmem_limit_bytes=VMEM_LIMIT),
        name="gla",
    )(aq, ak, av, agate, adown, w_up, b_up, gain)


def _swap_halves(x):
    half = LANES // 2
    return jnp.concatenate([x[:, half:], x[:, :half]], axis=1)


def _swa_kernel(sink_ref, rb_ref, q_ref, kp_ref, k_ref, vp_ref, v_ref, gate_ref, o_ref, bias_ref):
    W = SWA_WINDOW
    t = pl.program_id(1)

    @pl.when(jnp.logical_and(pl.program_id(0) == 0, t == 0))
    def _():
        qi = lax.broadcasted_iota(jnp.int32, (W, 2 * W), 0)
        kj = lax.broadcasted_iota(jnp.int32, (W, 2 * W), 1)
        dist = qi + W - kj
        in_win = jnp.logical_and(dist >= 0, dist < W)
        for kh in range(SWA_KV_HEADS):
            for par in range(2):
                for half in range(2):
                    b = _t5_bias(dist, rb_ref, 4 * kh + par + 2 * half)
                    bias_ref[kh, par, half * W:(half + 1) * W, :] = jnp.where(in_win, b, -jnp.inf)

    kcat = jnp.concatenate([kp_ref[...], k_ref[...]], axis=0)
    vcat = jnp.concatenate([vp_ref[...], v_ref[...]], axis=0)
    lo = lax.broadcasted_iota(jnp.int32, kcat.shape, 1) < (LANES // 2)
    zero = jnp.zeros_like(kcat)

    def placed(x):
        xs = _swap_halves(x)
        return [[jnp.where(lo, x, zero), jnp.where(lo, zero, xs)],
                [jnp.where(lo, xs, zero), jnp.where(lo, zero, x)]]

    kpl, vpl = placed(kcat), placed(vcat)
    row_top = lax.broadcasted_iota(jnp.int32, (2 * W, 1), 0) < W
    col_prev = lax.broadcasted_iota(jnp.int32, (2 * W, 2 * W), 1) < W

    for blk in range(SWA_TILE // W):
        rows = slice(blk * W, (blk + 1) * W)
        win = slice(blk * W, blk * W + 2 * W)
        for kh in range(SWA_KV_HEADS):
            j0, j1 = 2 * kh, 2 * kh + 1
            q_st = jnp.concatenate([q_ref[rows, j0 * LANES:(j0 + 1) * LANES],
                                    q_ref[rows, j1 * LANES:(j1 + 1) * LANES]], axis=0)
            o_st = None
            for par in range(2):
                s = _dot_nt(q_st, kpl[kh][par][win]) + bias_ref[kh, par]
                if blk == 0:
                    s = jnp.where(jnp.logical_and(t == 0, col_prev), -jnp.inf, s)
                sink = jnp.where(row_top, sink_ref[4 * kh + par], sink_ref[4 * kh + 2 + par])
                m = jnp.maximum(jnp.max(s, axis=-1, keepdims=True), sink)
                p = jnp.exp(s - m)
                den = jnp.sum(p, axis=-1, keepdims=True) + jnp.exp(sink - m)
                pn = (p * (1.0 / den)).astype(BF16)
                contrib = _dot(pn, vpl[kh][par][win])
                o_st = contrib if o_st is None else o_st + contrib
            for half, j in enumerate((j0, j1)):
                cols = slice(j * LANES, (j + 1) * LANES)
                gate = gate_ref[rows, cols].astype(F32)
                o_ref[rows, cols] = (o_st[half * W:(half + 1) * W] * _silu(gate)).astype(o_ref.dtype)


def _swa(bq, bk, bv, bgate, rel_bias, sinks):
    nt = SEQ // SWA_TILE
    per = SWA_TILE // SWA_WINDOW
    row = lambda n: pl.BlockSpec((SWA_TILE, n), lambda b, t: (b * nt + t, 0))
    prev = pl.BlockSpec((SWA_WINDOW, LANES),
                        lambda b, t: (jnp.maximum((b * nt + t) * per - 1, 0), 0))
    smem = pl.BlockSpec(memory_space=pltpu.SMEM)
    return pl.pallas_call(
        _swa_kernel, grid=(BATCH, nt),
        in_specs=[smem, smem, row(B_WIDTH), prev, row(LANES), prev, row(LANES), row(B_WIDTH)],
        out_specs=row(B_WIDTH),
        out_shape=jax.ShapeDtypeStruct((TOKENS, B_WIDTH), BF16),
        scratch_shapes=[pltpu.VMEM((SWA_KV_HEADS, 2, 2 * SWA_WINDOW, 2 * SWA_WINDOW), F32)],
        compiler_params=pltpu.CompilerParams(dimension_semantics=("arbitrary", "arbitrary"),
                                             vmem_limit_bytes=VMEM_LIMIT),
        name="swa",
    )(sinks, rel_bias, bq, bk, bk, bv, bv, bgate)


def _moba_kernel(rb_ref, q_ref, k_ref, v_ref, gate_ref, o_ref,
                 kaug_ref, vt_ref, km_ref, bown_ref, bnear_ref, qat_ref, s_ref, m_ref, acc_ref):
    BLK = MOBA_BLOCK
    G = MOBA_GROUP
    R = G * BLK
    DUMMY = MOBA_NBLK
    kvh = pl.program_id(1)
    i = pl.program_id(2)

    @pl.when(i == 0)
    def _():
        lane = lax.broadcasted_iota(jnp.int32, (BLK, LANES), 1)
        ones_rows = jnp.ones((MOBA_VT_ROWS - MOBA_DH, BLK), BF16)

        def prep(j, carry):
            rows = pl.ds(pl.multiple_of(j * BLK, BLK), BLK)
            kb = k_ref[rows, :]
            kaug_ref[j, :, 0:LANES] = kb
            kaug_ref[j, :, LANES:2 * LANES] = jnp.where(lane == j, 1.0, 0.0).astype(BF16)
            km_ref[pl.ds(j, 1), :] = jnp.mean(kb.astype(F32), axis=0, keepdims=True)
            vt_ref[j, 0:MOBA_DH, :] = v_ref[rows, :].astype(F32).T.astype(BF16)
            vt_ref[j, MOBA_DH:, :] = ones_rows
            return carry

        lax.fori_loop(0, MOBA_NBLK, prep, 0)
        kaug_ref[DUMMY, :, 0:LANES] = jnp.zeros((BLK, LANES), BF16)
        kaug_ref[DUMMY, :, LANES:2 * LANES] = jnp.where(lane == DUMMY, 1.0, 0.0).astype(BF16)
        vt_ref[DUMMY] = jnp.zeros((MOBA_VT_ROWS, BLK), BF16)
        qat_ref[MOBA_DH + MOBA_NBLK:, :] = jnp.full((2 * LANES - MOBA_DH - MOBA_NBLK, R), NEG_BIG, BF16)

        tk = lax.broadcasted_iota(jnp.int32, (BLK, BLK), 0)
        tq = lax.broadcasted_iota(jnp.int32, (BLK, BLK), 1)
        d_own = tq - tk
        for g in range(G):
            head = kvh * G + g
            cols = slice(g * BLK, (g + 1) * BLK)
            bown_ref[:, cols] = jnp.where(d_own >= 0, _t5_bias(d_own, rb_ref, head) * LOG2E, -jnp.inf)
            bnear_ref[:, cols] = _t5_bias(d_own + BLK, rb_ref, head) * LOG2E

    q = jnp.concatenate([q_ref[:, g * MOBA_DH:(g + 1) * MOBA_DH] for g in range(G)], axis=0)
    q_t = q.astype(F32).T.astype(BF16)

    km = km_ref[...]
    km_hi = km.astype(BF16)
    km_lo = (km - km_hi.astype(F32)).astype(BF16)
    gate = _dot(km_hi, q_t) + _dot(km_lo, q_t)
    blk = lax.broadcasted_iota(jnp.int32, (MOBA_NBLK, R), 0)
    blkf = blk.astype(F32)
    past = blk < i
    g_ = jnp.where(past, gate, -jnp.inf)
    selected = jnp.zeros((MOBA_NBLK, R), dtype=jnp.bool_)
    for _ in range(MOBA_TOPK):
        mx = jnp.max(g_, axis=0, keepdims=True)
        first = jnp.min(jnp.where(g_ == mx, blkf, 1e9), axis=0, keepdims=True)
        pick = jnp.logical_and(blkf == first, past)
        selected = jnp.logical_or(selected, pick)
        g_ = jnp.where(pick, -jnp.inf, g_)

    colh = lax.broadcasted_iota(jnp.int32, (1, R), 1) // BLK
    far_row = REL_BUCKETS - 1
    cfar = jnp.where(colh == 0, rb_ref[far_row, kvh * G],
                     jnp.where(colh == 1, rb_ref[far_row, kvh * G + 1],
                               jnp.where(colh == 2, rb_ref[far_row, kvh * G + 2],
                                         rb_ref[far_row, kvh * G + 3])))
    sel_bias = jnp.where(selected, jnp.where(blk < i - 1, cfar * LOG2E, 0.0),
                         jnp.where(blk == i, 0.0, NEG_BIG))
    qat_ref[0:MOBA_DH, :] = q_t
    qat_ref[MOBA_DH:MOBA_DH + MOBA_NBLK, :] = sel_bias.astype(BF16)

    m_ref[...] = jnp.full(m_ref.shape, NEG_BIG, F32)
    acc_ref[...] = jnp.zeros_like(acc_ref)

    def logits(j):
        return _dot(kaug_ref[j], qat_ref[...])

    def consume(slot, j):
        s = s_ref[slot]
        m_old = m_ref[...]
        m_new = jnp.maximum(m_old, jnp.max(s, axis=0, keepdims=True))
        alpha = jnp.exp2(m_old - m_new)
        p = jnp.exp2(s - m_new).astype(BF16)
        acc_ref[...] = alpha * acc_ref[...] + _dot(vt_ref[j], p)
        m_ref[...] = m_new

    n_far = jnp.maximum(i - 1, 0)

    def far_idx(n):
        return jnp.where(n < n_far, n, DUMMY)

    s_ref[0] = logits(far_idx(0))

    def far_pair(t, carry):
        j0, j1, j2 = far_idx(2 * t), far_idx(2 * t + 1), far_idx(2 * t + 2)
        s_ref[1] = logits(j1)
        consume(0, j0)
        s_ref[0] = logits(j2)
        consume(1, j1)
        return carry

    lax.fori_loop(0, (n_far + 1) // 2, far_pair, 0)

    near = jnp.where(i >= 1, i - 1, DUMMY)
    s_ref[1] = logits(near) + bnear_ref[...]
    s_ref[0] = logits(i) + bown_ref[...]
    consume(1, near)
    consume(0, i)

    acc = acc_ref[...]
    o = (acc[0:MOBA_DH] * (1.0 / acc[MOBA_DH:MOBA_DH + 1])).T
    for g in range(G):
        cols = slice(g * MOBA_DH, (g + 1) * MOBA_DH)
        gt = gate_ref[:, cols].astype(F32)
        o_ref[:, cols] = (o[g * BLK:(g + 1) * BLK] * _silu(gt)).astype(o_ref.dtype)


def _moba(cq, ck, cv, cgate, rel_bias):
    R = MOBA_GROUP * MOBA_BLOCK
    gw = MOBA_GROUP * MOBA_DH
    qspec = pl.BlockSpec((MOBA_BLOCK, gw), lambda b, h, i: (b * MOBA_NBLK + i, h))
    kvspec = pl.BlockSpec((SEQ, MOBA_DH), lambda b, h, i: (b, h))
    return pl.pallas_call(
        _moba_kernel, grid=(BATCH, MOBA_KV_HEADS, MOBA_NBLK),
        in_specs=[pl.BlockSpec(memory_space=pltpu.SMEM), qspec, kvspec, kvspec, qspec],
        out_specs=qspec,
        out_shape=jax.ShapeDtypeStruct((TOKENS, C_WIDTH), BF16),
        scratch_shapes=[pltpu.VMEM((MOBA_NBLK + 1, MOBA_BLOCK, 2 * LANES), BF16),
                        pltpu.VMEM((MOBA_NBLK + 1, MOBA_VT_ROWS, MOBA_BLOCK), BF16),
                        pltpu.VMEM((MOBA_NBLK, MOBA_DH), F32),
                        pltpu.VMEM((MOBA_BLOCK, R), F32),
                        pltpu.VMEM((MOBA_BLOCK, R), F32),
                        pltpu.VMEM((2 * LANES, R), BF16),
                        pltpu.VMEM((2, MOBA_BLOCK, R), F32),
                        pltpu.VMEM((1, R), F32),
                        pltpu.VMEM((MOBA_VT_ROWS, R), F32)],
        compiler_params=pltpu.CompilerParams(
            dimension_semantics=("arbitrary", "arbitrary", "arbitrary"), vmem_limit_bytes=VMEM_LIMIT),
        name="moba",
    )(rel_bias, cq, ck, cv, cgate)


def kernel(x, norm_gain, final_gain, rel_bias, w_in_even, gla_w_up, gla_b_up, gla_norm_gain, swa_sinks,
           w_out_even, w_in_odd, w_out_odd):
    assert x.shape == (BATCH, SEQ, D_MODEL)
    h0 = x.reshape(TOKENS, D_MODEL)

    we = w_in_even[0]
    splits = np.cumsum([0, GLA_QK, GLA_QK, A_WIDTH, GLA_RANK, A_WIDTH, B_WIDTH, SWA_KV_HEADS * SWA_DH,
                        SWA_KV_HEADS * SWA_DH, B_WIDTH])
    cols = [we[:, splits[k]:splits[k + 1]] for k in range(9)]
    w_aq, w_ak, w_av, w_down, w_agate, w_bq, w_bk, w_bv, w_bgate = cols
    w_down = jnp.pad(w_down, ((0, 0), (0, LANES - GLA_RANK)))
    proj_ws = [w.astype(BF16) for w in (w_aq, w_ak, w_av, w_down, w_agate, w_bq, w_bk, w_bv, w_bgate)]
    proj_dt = [BF16, BF16, BF16, F32, BF16, BF16, BF16, BF16, BF16]
    scales = [GLA_DK ** -0.5, 1.0, 1.0, 1.0, 1.0, SWA_DH ** -0.5, 1.0, 1.0, 1.0]
    aq, ak, av, adown, agate, bq, bk, bv, bgate = _layer_io(
        h0, [], norm_gain[0:1], proj_ws, proj_dt, scales, write_h=False)

    w_up = jnp.pad(gla_w_up[0], ((0, LANES - GLA_RANK), (0, 0)))
    oa = _gla(aq, ak, av, agate, adown, w_up, gla_b_up[0:1], gla_norm_gain[0:1])
    ob = _swa(bq, bk, bv, bgate, rel_bias, swa_sinks[0])

    wo = w_out_even[0].astype(BF16)
    wi = w_in_odd[0]
    osp = np.cumsum([0, C_WIDTH, MOBA_KV_HEADS * MOBA_DH, MOBA_KV_HEADS * MOBA_DH, C_WIDTH])
    odd_ws = [wi[:, osp[k]:osp[k + 1]].astype(BF16) for k in range(4)]
    h1, cq, ck, cv, cgate = _layer_io(
        h0, [(oa, wo[:A_WIDTH]), (ob, wo[A_WIDTH:])], norm_gain[1:2], odd_ws, [BF16] * 4,
        [MOBA_DH ** -0.5 * LOG2E, 1.0, 1.0, 1.0], write_h=True)

    oc = _moba(cq, ck, cv, cgate, rel_bias)

    (out,) = _layer_io(h1, [(oc, w_out_odd[0].astype(BF16))], final_gain.reshape(1, D_MODEL), [], [], [],
                       write_h=False)
    return out.reshape(BATCH, SEQ, D_MODEL)
```

```python
import functools
import math

import numpy as np
import jax
import jax.numpy as jnp
from jax import lax
from jax.experimental import pallas as pl
from jax.experimental.pallas import tpu as pltpu

D_MODEL = 1024
BATCH = 2
SEQ = 8192
TOKENS = BATCH * SEQ

REL_BUCKETS = 32
REL_MAX_EXACT = 16
REL_MAX_DIST = 128
ATTN_HEADS = 8

GLA_HEADS = 4
GLA_DK = 64
GLA_DV = 128
GLA_RANK = 16
GLA_TAU = 16.0
GLA_CHUNK = 64

SWA_HEADS = 8
SWA_KV_HEADS = 2
SWA_DH = 64
SWA_WINDOW = 128

MOBA_HEADS = 8
MOBA_KV_HEADS = 2
MOBA_DH = 128
MOBA_BLOCK = 256
MOBA_TOPK = 3
MOBA_GROUP = MOBA_HEADS // MOBA_KV_HEADS
MOBA_NBLK = SEQ // MOBA_BLOCK
MOBA_VT_ROWS = MOBA_DH + 16
LOG2E = math.log2(math.e)

EPS = 1e-6
LANES = 128
NEG_BIG = -1e30
VMEM_LIMIT = 48 * 1024 * 1024

A_WIDTH = GLA_HEADS * GLA_DV
B_WIDTH = SWA_HEADS * SWA_DH
C_WIDTH = MOBA_HEADS * MOBA_DH
GLA_QK = GLA_HEADS * GLA_DK

ROW_TILE = 512
GLA_TILE = 512
SWA_TILE = 512

F32 = jnp.float32
BF16 = jnp.bfloat16


def _dot(a, b):
    return jnp.dot(a, b, preferred_element_type=F32)


def _dot_nt(a, b):
    return lax.dot_general(a, b, (((1,), (1,)), ((), ())), preferred_element_type=F32)


def _dot_tn(a, b):
    return lax.dot_general(a, b, (((0,), (0,)), ((), ())), preferred_element_type=F32)


def _silu(x):
    return x * (1.0 / (1.0 + jnp.exp(-x)))


def _t5_thresholds():
    d = np.arange(REL_MAX_DIST + 1)
    nf = np.maximum(d, 1).astype(np.float32)
    large = REL_MAX_EXACT + (np.log(nf / np.float32(REL_MAX_EXACT))
                             / np.float32(math.log(REL_MAX_DIST / REL_MAX_EXACT))
                             * np.float32(REL_BUCKETS - REL_MAX_EXACT)).astype(np.int32)
    bucket = np.where(d < REL_MAX_EXACT, d, np.minimum(large, REL_BUCKETS - 1))
    assert np.all(np.diff(bucket) >= 0) and bucket[-1] == REL_BUCKETS - 1
    return [int(np.argmax(bucket >= b)) for b in range(REL_BUCKETS)]


_T5_THRESH = _t5_thresholds()


def _t5_bias(dist, rb_ref, head):
    out = jnp.full(dist.shape, rb_ref[0, head], F32)
    for b in range(1, REL_BUCKETS):
        out = jnp.where(dist >= _T5_THRESH[b], rb_ref[b, head], out)
    return out


def _layer_io_kernel(*refs, n_acc, n_proj, write_h, scales):
    h_ref = refs[0]
    pos = 1
    acc_pairs = []
    for _ in range(n_acc):
        acc_pairs.append((refs[pos], refs[pos + 1]))
        pos += 2
    gain_ref = refs[pos]
    pos += 1
    wp_refs = refs[pos:pos + n_proj]
    pos += n_proj
    out_refs = refs[pos:]

    h = h_ref[...]
    for a_ref, w_ref in acc_pairs:
        h = h + _dot(a_ref[...], w_ref[...])
    o = 0
    if write_h:
        out_refs[0][...] = h
        o = 1
    y = h * lax.rsqrt(jnp.mean(h * h, axis=-1, keepdims=True) + EPS) * gain_ref[...]
    if n_proj == 0:
        out_refs[o][...] = y
        return
    yb = y.astype(BF16)
    for k in range(n_proj):
        r = _dot(yb, wp_refs[k][...])
        if scales[k] != 1.0:
            r = r * scales[k]
        out_refs[o + k][...] = r.astype(out_refs[o + k].dtype)


def _layer_io(h, acc_pairs, gain, proj_ws, proj_dtypes, scales, write_h):
    n_rows = h.shape[0]
    grid = (n_rows // ROW_TILE,)
    row_spec = lambda n: pl.BlockSpec((ROW_TILE, n), lambda i: (i, 0))
    full_spec = lambda a: pl.BlockSpec(a.shape, lambda i: (0,) * a.ndim)

    args = [h]
    in_specs = [row_spec(D_MODEL)]
    for a, w in acc_pairs:
        args += [a, w]
        in_specs += [row_spec(a.shape[1]), full_spec(w)]
    args.append(gain)
    in_specs.append(full_spec(gain))
    for w in proj_ws:
        args.append(w)
        in_specs.append(full_spec(w))

    out_shape, out_specs = [], []
    if write_h:
        out_shape.append(jax.ShapeDtypeStruct((n_rows, D_MODEL), F32))
        out_specs.append(row_spec(D_MODEL))
    if proj_ws:
        for w, dt in zip(proj_ws, proj_dtypes):
            out_shape.append(jax.ShapeDtypeStruct((n_rows, w.shape[1]), dt))
            out_specs.append(row_spec(w.shape[1]))
    else:
        out_shape.append(jax.ShapeDtypeStruct((n_rows, D_MODEL), F32))
        out_specs.append(row_spec(D_MODEL))

    kern = functools.partial(_layer_io_kernel, n_acc=len(acc_pairs), n_proj=len(proj_ws),
                             write_h=write_h, scales=tuple(scales))
    return pl.pallas_call(
        kern, grid=grid, in_specs=in_specs, out_specs=out_specs, out_shape=out_shape,
        compiler_params=pltpu.CompilerParams(dimension_semantics=("arbitrary",),
                                             vmem_limit_bytes=VMEM_LIMIT),
        name="layer_io",
    )(*args)


def _gla_kernel(q_ref, k_ref, v_ref, gate_ref, down_ref, wup_ref, bup_ref, gain_ref, o_ref, st_ref):
    C = GLA_CHUNK

    @pl.when(pl.program_id(1) == 0)
    def _():
        st_ref[...] = jnp.zeros_like(st_ref)

    z = jnp.dot(down_ref[...], wup_ref[...], preferred_element_type=F32,
                precision=lax.Precision.HIGHEST) + bup_ref[...]
    log_a = (jnp.minimum(z, 0.0) - jnp.log1p(jnp.exp(-jnp.abs(z)))) * (1.0 / GLA_TAU)

    r_i = lax.broadcasted_iota(jnp.int32, (C, C), 0)
    c_i = lax.broadcasted_iota(jnp.int32, (C, C), 1)
    tri = (c_i <= r_i).astype(F32)
    lane_qk = lax.broadcasted_iota(jnp.int32, (C, GLA_QK), 1)
    head_masks = [(lane_qk // GLA_DK) == h for h in range(GLA_HEADS)]
    rs = lax.broadcasted_iota(jnp.int32, (GLA_HEADS * C, C), 0)
    cs = lax.broadcasted_iota(jnp.int32, (GLA_HEADS * C, C), 1)
    causal = (rs % C) >= cs
    st_r = lax.broadcasted_iota(jnp.int32, (A_WIDTH, GLA_QK), 0)
    st_c = lax.broadcasted_iota(jnp.int32, (A_WIDTH, GLA_QK), 1)
    same_head = (st_r // GLA_DV) == (st_c // GLA_DK)
    gain = gain_ref[...]

    for c in range(GLA_TILE // C):
        rows = slice(c * C, (c + 1) * C)
        g = log_a[rows]
        b = jnp.dot(tri, g, preferred_element_type=F32, precision=lax.Precision.HIGHEST)
        b_last = b[C - 1:C]
        q = q_ref[rows, :].astype(F32)
        k = k_ref[rows, :].astype(F32)
        v = v_ref[rows, :]
        q_e = q * jnp.exp(b)
        k_e = (k * jnp.exp(-b)).astype(BF16)
        k_l = (k * jnp.exp(b_last - b)).astype(BF16)
        decay = jnp.exp(b_last)

        q_stack = jnp.concatenate([jnp.where(m, q_e, 0.0) for m in head_masks], axis=0).astype(BF16)
        att = jnp.where(causal, _dot_nt(q_stack, k_e), 0.0).astype(BF16)
        o_full = _dot(att, v)
        o_intra = jnp.concatenate(
            [o_full[h * C:(h + 1) * C, h * GLA_DV:(h + 1) * GLA_DV] for h in range(GLA_HEADS)], axis=1)
        st = st_ref[...]
        o = o_intra + _dot_nt(q_e.astype(BF16), st.astype(BF16))
        kv_t = _dot_tn(v, k_l)
        st_ref[...] = st * decay + jnp.where(same_head, kv_t, 0.0)

        outs = []
        for h in range(GLA_HEADS):
            oh = o[:, h * GLA_DV:(h + 1) * GLA_DV]
            outs.append(oh * lax.rsqrt(jnp.mean(oh * oh, axis=-1, keepdims=True) + EPS) * gain)
        on = jnp.concatenate(outs, axis=1)
        o_ref[rows, :] = (on * _silu(gate_ref[rows, :].astype(F32))).astype(o_ref.dtype)


def _gla(aq, ak, av, agate, adown, w_up, b_up, gain):
    nt = SEQ // GLA_TILE
    row = lambda n: pl.BlockSpec((GLA_TILE, n), lambda b, s: (b * nt + s, 0))
    full = lambda a: pl.BlockSpec(a.shape, lambda b, s: (0,) * a.ndim)
    return pl.pallas_call(
        _gla_kernel, grid=(BATCH, nt),
        in_specs=[row(GLA_QK), row(GLA_QK), row(A_WIDTH), row(A_WIDTH), row(LANES),
                  full(w_up), full(b_up), full(gain)],
        out_specs=row(A_WIDTH),
        out_shape=jax.ShapeDtypeStruct((TOKENS, A_WIDTH), BF16),
        scratch_shapes=[pltpu.VMEM((A_WIDTH, GLA_QK), F32)],
        compiler_params=pltpu.CompilerParams(dimension_semantics=("arbitrary", "arbitrary"),
                                             vmem_limit_bytes=VMEM_LIMIT),
        name="gla",
    )(aq, ak, av, agate, adown, w_up, b_up, gain)


def _swap_halves(x):
    half = LANES // 2
    return jnp.concatenate([x[:, half:], x[:, :half]], axis=1)


def _swa_kernel(sink_ref, rb_ref, q_ref, kp_ref, k_ref, vp_ref, v_ref, gate_ref, o_ref, bias_ref):
    W = SWA_WINDOW
    t = pl.program_id(1)

    @pl.when(jnp.logical_and(pl.program_id(0) == 0, t == 0))
    def _():
        qi = lax.broadcasted_iota(jnp.int32, (W, 2 * W), 0)
        kj = lax.broadcasted_iota(jnp.int32, (W, 2 * W), 1)
        dist = qi + W - kj
        in_win = jnp.logical_and(dist >= 0, dist < W)
        for kh in range(SWA_KV_HEADS):
            for par in range(2):
                for half in range(2):
                    b = _t5_bias(dist, rb_ref, 4 * kh + par + 2 * half)
                    bias_ref[kh, par, half * W:(half + 1) * W, :] = jnp.where(in_win, b, -jnp.inf)

    kcat = jnp.concatenate([kp_ref[...], k_ref[...]], axis=0)
    vcat = jnp.concatenate([vp_ref[...], v_ref[...]], axis=0)
    lo = lax.broadcasted_iota(jnp.int32, kcat.shape, 1) < (LANES // 2)
    zero = jnp.zeros_like(kcat)

    def placed(x):
        xs = _swap_halves(x)
        return [[jnp.where(lo, x, zero), jnp.where(lo, zero, xs)],
                [jnp.where(lo, xs, zero), jnp.where(lo, zero, x)]]

    kpl, vpl = placed(kcat), placed(vcat)
    row_top = lax.broadcasted_iota(jnp.int32, (2 * W, 1), 0) < W
    col_prev = lax.broadcasted_iota(jnp.int32, (2 * W, 2 * W), 1) < W

    for blk in range(SWA_TILE // W):
        rows = slice(blk * W, (blk + 1) * W)
        win = slice(blk * W, blk * W + 2 * W)
        for kh in range(SWA_KV_HEADS):
            j0, j1 = 2 * kh, 2 * kh + 1
            q_st = jnp.concatenate([q_ref[rows, j0 * LANES:(j0 + 1) * LANES],
                                    q_ref[rows, j1 * LANES:(j1 + 1) * LANES]], axis=0)
            o_st = None
            for par in range(2):
                s = _dot_nt(q_st, kpl[kh][par][win]) + bias_ref[kh, par]
                if blk == 0:
                    s = jnp.where(jnp.logical_and(t == 0, col_prev), -jnp.inf, s)
                sink = jnp.where(row_top, sink_ref[4 * kh + par], sink_ref[4 * kh + 2 + par])
                m = jnp.maximum(jnp.max(s, axis=-1, keepdims=True), sink)
                p = jnp.exp(s - m)
                den = jnp.sum(p, axis=-1, keepdims=True) + jnp.exp(sink - m)
                pn = (p * (1.0 / den)).astype(BF16)
                contrib = _dot(pn, vpl[kh][par][win])
                o_st = contrib if o_st is None else o_st + contrib
            for half, j in enumerate((j0, j1)):
                cols = slice(j * LANES, (j + 1) * LANES)
                gate = gate_ref[rows, cols].astype(F32)
                o_ref[rows, cols] = (o_st[half * W:(half + 1) * W] * _silu(gate)).astype(o_ref.dtype)


def _swa(bq, bk, bv, bgate, rel_bias, sinks):
    nt = SEQ // SWA_TILE
    per = SWA_TILE // SWA_WINDOW
    row = lambda n: pl.BlockSpec((SWA_TILE, n), lambda b, t: (b * nt + t, 0))
    prev = pl.BlockSpec((SWA_WINDOW, LANES),
                        lambda b, t: (jnp.maximum((b * nt + t) * per - 1, 0), 0))
    smem = pl.BlockSpec(memory_space=pltpu.SMEM)
    return pl.pallas_call(
        _swa_kernel, grid=(BATCH, nt),
        in_specs=[smem, smem, row(B_WIDTH), prev, row(LANES), prev, row(LANES), row(B_WIDTH)],
        out_specs=row(B_WIDTH),
        out_shape=jax.ShapeDtypeStruct((TOKENS, B_WIDTH), BF16),
        scratch_shapes=[pltpu.VMEM((SWA_KV_HEADS, 2, 2 * SWA_WINDOW, 2 * SWA_WINDOW), F32)],
        compiler_params=pltpu.CompilerParams(dimension_semantics=("arbitrary", "arbitrary"),
                                             vmem_limit_bytes=VMEM_LIMIT),
        name="swa",
    )(sinks, rel_bias, bq, bk, bk, bv, bv, bgate)


def _moba_kernel(rb_ref, q_ref, k_ref, v_ref, gate_ref, o_ref,
                 kaug_ref, vt_ref, km_ref, bown_ref, bnear_ref, qat_ref, s_ref, m_ref, acc_ref):
    BLK = MOBA_BLOCK
    G = MOBA_GROUP
    R = G * BLK
    DUMMY = MOBA_NBLK
    kvh = pl.program_id(1)
    i = pl.program_id(2)

    @pl.when(i == 0)
    def _():
        lane = lax.broadcasted_iota(jnp.int32, (BLK, LANES), 1)
        ones_rows = jnp.ones((MOBA_VT_ROWS - MOBA_DH, BLK), BF16)

        def prep(j, carry):
            rows = pl.ds(pl.multiple_of(j * BLK, BLK), BLK)
            kb = k_ref[rows, :]
            kaug_ref[j, :, 0:LANES] = kb
            kaug_ref[j, :, LANES:2 * LANES] = jnp.where(lane == j, 1.0, 0.0).astype(BF16)
            km_ref[pl.ds(j, 1), :] = jnp.mean(kb.astype(F32), axis=0, keepdims=True)
            vt_ref[j, 0:MOBA_DH, :] = v_ref[rows, :].astype(F32).T.astype(BF16)
            vt_ref[j, MOBA_DH:, :] = ones_rows
            return carry

        lax.fori_loop(0, MOBA_NBLK, prep, 0)
        kaug_ref[DUMMY, :, 0:LANES] = jnp.zeros((BLK, LANES), BF16)
        kaug_ref[DUMMY, :, LANES:2 * LANES] = jnp.where(lane == DUMMY, 1.0, 0.0).astype(BF16)
        vt_ref[DUMMY] = jnp.zeros((MOBA_VT_ROWS, BLK), BF16)
        qat_ref[MOBA_DH + MOBA_NBLK:, :] = jnp.full((2 * LANES - MOBA_DH - MOBA_NBLK, R), NEG_BIG, BF16)

        tk = lax.broadcasted_iota(jnp.int32, (BLK, BLK), 0)
        tq = lax.broadcasted_iota(jnp.int32, (BLK, BLK), 1)
        d_own = tq - tk
        for g in range(G):
            head = kvh * G + g
            cols = slice(g * BLK, (g + 1) * BLK)
            bown_ref[:, cols] = jnp.where(d_own >= 0, _t5_bias(d_own, rb_ref, head) * LOG2E, -jnp.inf)
            bnear_ref[:, cols] = _t5_bias(d_own + BLK, rb_ref, head) * LOG2E

    q = jnp.concatenate([q_ref[:, g * MOBA_DH:(g + 1) * MOBA_DH] for g in range(G)], axis=0)
    q_t = q.astype(F32).T.astype(BF16)
    qat_ref[0:MOBA_DH, :] = q_t

    s_ref[0] = _dot(kaug_ref[i, :, 0:MOBA_DH], q_t) + bown_ref[...]

    km = km_ref[...]
    km_hi = km.astype(BF16)
    km_lo = (km - km_hi.astype(F32)).astype(BF16)
    gate = _dot(km_hi, q_t) + _dot(km_lo, q_t)
    blk = lax.broadcasted_iota(jnp.int32, (MOBA_NBLK, R), 0)
    blkf = blk.astype(F32)
    past = blk < i
    g_ = jnp.where(past, gate, -jnp.inf)
    selected = jnp.zeros((MOBA_NBLK, R), dtype=jnp.bool_)
    for _ in range(MOBA_TOPK):
        mx = jnp.max(g_, axis=0, keepdims=True)
        first = jnp.min(jnp.where(g_ == mx, blkf, 1e9), axis=0, keepdims=True)
        pick = jnp.logical_and(blkf == first, past)
        selected = jnp.logical_or(selected, pick)
        g_ = jnp.where(pick, -jnp.inf, g_)

    colh = lax.broadcasted_iota(jnp.int32, (1, R), 1) // BLK
    far_row = REL_BUCKETS - 1
    cfar = jnp.where(colh == 0, rb_ref[far_row, kvh * G],
                     jnp.where(colh == 1, rb_ref[far_row, kvh * G + 1],
                               jnp.where(colh == 2, rb_ref[far_row, kvh * G + 2],
                                         rb_ref[far_row, kvh * G + 3])))
    sel_bias = jnp.where(selected, jnp.where(blk < i - 1, cfar * LOG2E, 0.0), NEG_BIG)
    qat_ref[MOBA_DH:MOBA_DH + MOBA_NBLK, :] = sel_bias.astype(BF16)

    def logits(j):
        return _dot(kaug_ref[j], qat_ref[...])

    def consume(slot, j, first=False):
        s = s_ref[slot]
        if first:
            m_new = jnp.max(s, axis=0, keepdims=True)
        else:
            m_old = m_ref[...]
            m_new = jnp.maximum(m_old, jnp.max(s, axis=0, keepdims=True))
        p = jnp.exp2(s - m_new).astype(BF16)
        pv = _dot(vt_ref[j], p)
        acc_ref[...] = pv if first else jnp.exp2(m_old - m_new) * acc_ref[...] + pv
        m_ref[...] = m_new

    n_far = jnp.maximum(i - 1, 0)
    n_pairs = (n_far + 1) // 2

    def far_idx(n):
        return jnp.where(n < n_far, n, DUMMY)

    def far_step(n, slot):
        s_ref[1 - slot] = logits(far_idx(n + 1))
        consume(slot, far_idx(n))

    near = jnp.where(i >= 1, i - 1, DUMMY)
    s_ref[1] = logits(near) + bnear_ref[...]
    consume(0, i, first=True)
    s_ref[0] = logits(far_idx(0))
    consume(1, near)

    def far_quad(u, carry):
        for d in range(4):
            far_step(4 * u + d, d % 2)
        return carry

    lax.fori_loop(0, n_pairs // 2, far_quad, 0)

    @pl.when(n_pairs % 2 == 1)
    def _():
        for d in range(2):
            far_step(2 * (n_pairs - 1) + d, d % 2)

    acc = acc_ref[...]
    o = (acc[0:MOBA_DH] * (1.0 / acc[MOBA_DH:MOBA_DH + 1])).T
    for g in range(G):
        cols = slice(g * MOBA_DH, (g + 1) * MOBA_DH)
        gt = gate_ref[:, cols].astype(F32)
        o_ref[:, cols] = (o[g * BLK:(g + 1) * BLK] * _silu(gt)).astype(o_ref.dtype)


def _moba(cq, ck, cv, cgate, rel_bias):
    R = MOBA_GROUP * MOBA_BLOCK
    gw = MOBA_GROUP * MOBA_DH
    qspec = pl.BlockSpec((MOBA_BLOCK, gw), lambda b, h, i: (b * MOBA_NBLK + i, h))
    kvspec = pl.BlockSpec((SEQ, MOBA_DH), lambda b, h, i: (b, h))
    return pl.pallas_call(
        _moba_kernel, grid=(BATCH, MOBA_KV_HEADS, MOBA_NBLK),
        in_specs=[pl.BlockSpec(memory_space=pltpu.SMEM), qspec, kvspec, kvspec, qspec],
        out_specs=qspec,
        out_shape=jax.ShapeDtypeStruct((TOKENS, C_WIDTH), BF16),
        scratch_shapes=[pltpu.VMEM((MOBA_NBLK + 1, MOBA_BLOCK, 2 * LANES), BF16),
                        pltpu.VMEM((MOBA_NBLK + 1, MOBA_VT_ROWS, MOBA_BLOCK), BF16),
                        pltpu.VMEM((MOBA_NBLK, MOBA_DH), F32),
                        pltpu.VMEM((MOBA_BLOCK, R), F32),
                        pltpu.VMEM((MOBA_BLOCK, R), F32),
                        pltpu.VMEM((2 * LANES, R), BF16),
                        pltpu.VMEM((2, MOBA_BLOCK, R), F32),
                        pltpu.VMEM((1, R), F32),
                        pltpu.VMEM((MOBA_VT_ROWS, R), F32)],
        compiler_params=pltpu.CompilerParams(
            dimension_semantics=("arbitrary", "arbitrary", "arbitrary"), vmem_limit_bytes=VMEM_LIMIT),
        name="moba",
    )(rel_bias, cq, ck, cv, cgate)


def kernel(x, norm_gain, final_gain, rel_bias, w_in_even, gla_w_up, gla_b_up, gla_norm_gain, swa_sinks,
           w_out_even, w_in_odd, w_out_odd):
    assert x.shape == (BATCH, SEQ, D_MODEL)
    h0 = x.reshape(TOKENS, D_MODEL)

    we = w_in_even[0]
    splits = np.cumsum([0, GLA_QK, GLA_QK, A_WIDTH, GLA_RANK, A_WIDTH, B_WIDTH, SWA_KV_HEADS * SWA_DH,
                        SWA_KV_HEADS * SWA_DH, B_WIDTH])
    cols = [we[:, splits[k]:splits[k + 1]] for k in range(9)]
    w_aq, w_ak, w_av, w_down, w_agate, w_bq, w_bk, w_bv, w_bgate = cols
    w_down = jnp.pad(w_down, ((0, 0), (0, LANES - GLA_RANK)))
    proj_ws = [w.astype(BF16) for w in (w_aq, w_ak, w_av, w_down, w_agate, w_bq, w_bk, w_bv, w_bgate)]
    proj_dt = [BF16, BF16, BF16, F32, BF16, BF16, BF16, BF16, BF16]
    scales = [GLA_DK ** -0.5, 1.0, 1.0, 1.0, 1.0, SWA_DH ** -0.5, 1.0, 1.0, 1.0]
    aq, ak, av, adown, agate, bq, bk, bv, bgate = _layer_io(
        h0, [], norm_gain[0:1], proj_ws, proj_dt, scales, write_h=False)

    w_up = jnp.pad(gla_w_up[0], ((0, LANES - GLA_RANK), (0, 0)))
    oa = _gla(aq, ak, av, agate, adown, w_up, gla_b_up[0:1], gla_norm_gain[0:1])
    ob = _swa(bq, bk, bv, bgate, rel_bias, swa_sinks[0])

    wo = w_out_even[0].astype(BF16)
    wi = w_in_odd[0]
    osp = np.cumsum([0, C_WIDTH, MOBA_KV_HEADS * MOBA_DH, MOBA_KV_HEADS * MOBA_DH, C_WIDTH])
    odd_ws = [wi[:, osp[k]:osp[k + 1]].astype(BF16) for k in range(4)]
    h1, cq, ck, cv, cgate = _layer_io(
        h0, [(oa, wo[:A_WIDTH]), (ob, wo[A_WIDTH:])], norm_gain[1:2], odd_ws, [BF16] * 4,
        [MOBA_DH ** -0.5 * LOG2E, 1.0, 1.0, 1.0], write_h=True)

    oc = _moba(cq, ck, cv, cgate, rel_bias)

    (out,) = _layer_io(h1, [(oc, w_out_odd[0].astype(BF16))], final_gain.reshape(1, D_MODEL), [], [], [],
                       write_h=False)
    return out.reshape(BATCH, SEQ, D_MODEL)
```

```python
import functools
import math

import numpy as np
import jax
import jax.numpy as jnp
from jax import lax
from jax.experimental import pallas as pl
from jax.experimental.pallas import tpu as pltpu

D_MODEL = 1024
BATCH = 2
SEQ = 8192
TOKENS = BATCH * SEQ

REL_BUCKETS = 32
REL_MAX_EXACT = 16
REL_MAX_DIST = 128
ATTN_HEADS = 8

GLA_HEADS = 4
GLA_DK = 64
GLA_DV = 128
GLA_RANK = 16
GLA_TAU = 16.0
GLA_CHUNK = 64

SWA_HEADS = 8
SWA_KV_HEADS = 2
SWA_DH = 64
SWA_WINDOW = 128

MOBA_HEADS = 8
MOBA_KV_HEADS = 2
MOBA_DH = 128
MOBA_BLOCK = 256
MOBA_TOPK = 3
MOBA_GROUP = MOBA_HEADS // MOBA_KV_HEADS
MOBA_NBLK = SEQ // MOBA_BLOCK
MOBA_VT_ROWS = MOBA_DH + 16
LOG2E = math.log2(math.e)
SWA_VT_ROWS = SWA_KV_HEADS * SWA_DH + 16

EPS = 1e-6
LANES = 128
NEG_BIG = -1e30
VMEM_LIMIT = 48 * 1024 * 1024

A_WIDTH = GLA_HEADS * GLA_DV
B_WIDTH = SWA_HEADS * SWA_DH
C_WIDTH = MOBA_HEADS * MOBA_DH
GLA_QK = GLA_HEADS * GLA_DK

ROW_TILE = 512
GLA_TILE = 512
SWA_TILE = 512

F32 = jnp.float32
BF16 = jnp.bfloat16


def _dot(a, b):
    return jnp.dot(a, b, preferred_element_type=F32)


def _dot_nt(a, b):
    return lax.dot_general(a, b, (((1,), (1,)), ((), ())), preferred_element_type=F32)


def _dot_tn(a, b):
    return lax.dot_general(a, b, (((0,), (0,)), ((), ())), preferred_element_type=F32)


def _silu(x):
    return x * (1.0 / (1.0 + jnp.exp(-x)))


def _t5_thresholds():
    d = np.arange(REL_MAX_DIST + 1)
    nf = np.maximum(d, 1).astype(np.float32)
    large = REL_MAX_EXACT + (np.log(nf / np.float32(REL_MAX_EXACT))
                             / np.float32(math.log(REL_MAX_DIST / REL_MAX_EXACT))
                             * np.float32(REL_BUCKETS - REL_MAX_EXACT)).astype(np.int32)
    bucket = np.where(d < REL_MAX_EXACT, d, np.minimum(large, REL_BUCKETS - 1))
    assert np.all(np.diff(bucket) >= 0) and bucket[-1] == REL_BUCKETS - 1
    return [int(np.argmax(bucket >= b)) for b in range(REL_BUCKETS)]


_T5_THRESH = _t5_thresholds()


def _t5_bias(dist, rb_ref, head):
    out = jnp.full(dist.shape, rb_ref[0, head], F32)
    for b in range(1, REL_BUCKETS):
        out = jnp.where(dist >= _T5_THRESH[b], rb_ref[b, head], out)
    return out


def _layer_io_kernel(*refs, n_acc, n_proj, write_h, scales):
    h_ref = refs[0]
    pos = 1
    acc_pairs = []
    for _ in range(n_acc):
        acc_pairs.append((refs[pos], refs[pos + 1]))
        pos += 2
    gain_ref = refs[pos]
    pos += 1
    wp_refs = refs[pos:pos + n_proj]
    pos += n_proj
    out_refs = refs[pos:]

    h = h_ref[...]
    for a_ref, w_ref in acc_pairs:
        h = h + _dot(a_ref[...], w_ref[...])
    o = 0
    if write_h:
        out_refs[0][...] = h
        o = 1
    y = h * lax.rsqrt(jnp.mean(h * h, axis=-1, keepdims=True) + EPS) * gain_ref[...]
    if n_proj == 0:
        out_refs[o][...] = y
        return
    yb = y.astype(BF16)
    for k in range(n_proj):
        r = _dot(yb, wp_refs[k][...])
        if scales[k] != 1.0:
            r = r * scales[k]
        out_refs[o + k][...] = r.astype(out_refs[o + k].dtype)


def _layer_io(h, acc_pairs, gain, proj_ws, proj_dtypes, scales, write_h):
    n_rows = h.shape[0]
    grid = (n_rows // ROW_TILE,)
    row_spec = lambda n: pl.BlockSpec((ROW_TILE, n), lambda i: (i, 0))
    full_spec = lambda a: pl.BlockSpec(a.shape, lambda i: (0,) * a.ndim)

    args = [h]
    in_specs = [row_spec(D_MODEL)]
    for a, w in acc_pairs:
        args += [a, w]
        in_specs += [row_spec(a.shape[1]), full_spec(w)]
    args.append(gain)
    in_specs.append(full_spec(gain))
    for w in proj_ws:
        args.append(w)
        in_specs.append(full_spec(w))

    out_shape, out_specs = [], []
    if write_h:
        out_shape.append(jax.ShapeDtypeStruct((n_rows, D_MODEL), F32))
        out_specs.append(row_spec(D_MODEL))
    if proj_ws:
        for w, dt in zip(proj_ws, proj_dtypes):
            out_shape.append(jax.ShapeDtypeStruct((n_rows, w.shape[1]), dt))
            out_specs.append(row_spec(w.shape[1]))
    else:
        out_shape.append(jax.ShapeDtypeStruct((n_rows, D_MODEL), F32))
        out_specs.append(row_spec(D_MODEL))

    kern = functools.partial(_layer_io_kernel, n_acc=len(acc_pairs), n_proj=len(proj_ws),
                             write_h=write_h, scales=tuple(scales))
    return pl.pallas_call(
        kern, grid=grid, in_specs=in_specs, out_specs=out_specs, out_shape=out_shape,
        compiler_params=pltpu.CompilerParams(dimension_semantics=("arbitrary",),
                                             vmem_limit_bytes=VMEM_LIMIT),
        name="layer_io",
    )(*args)


def _gla_kernel(q_ref, k_ref, v_ref, gate_ref, down_ref, wup_ref, bup_ref, gain_ref, o_ref, st_ref):
    C = GLA_CHUNK

    @pl.when(pl.program_id(1) == 0)
    def _():
        st_ref[...] = jnp.zeros_like(st_ref)

    a = down_ref[...]
    a_hi = a.astype(BF16)
    a_lo = (a - a_hi.astype(F32)).astype(BF16)
    lane_a = lax.broadcasted_iota(jnp.int32, a.shape, 1) // GLA_RANK
    z = _dot(jnp.where(lane_a == 1, a_lo, a_hi), wup_ref[...]) + bup_ref[...]
    log_a = (jnp.minimum(z, 0.0) - jnp.log1p(jnp.exp(-jnp.abs(z)))) * (1.0 / GLA_TAU)

    r_i = lax.broadcasted_iota(jnp.int32, (C, C), 0)
    c_i = lax.broadcasted_iota(jnp.int32, (C, C), 1)
    tri = (c_i <= r_i).astype(BF16)
    lane_qk = lax.broadcasted_iota(jnp.int32, (C, GLA_QK), 1)
    head_masks = [(lane_qk // GLA_DK) == h for h in range(GLA_HEADS)]
    rs = lax.broadcasted_iota(jnp.int32, (GLA_HEADS * C, C), 0)
    cs = lax.broadcasted_iota(jnp.int32, (GLA_HEADS * C, C), 1)
    causal = (rs % C) >= cs
    st_r = lax.broadcasted_iota(jnp.int32, (A_WIDTH, GLA_QK), 0)
    st_c = lax.broadcasted_iota(jnp.int32, (A_WIDTH, GLA_QK), 1)
    same_head = (st_r // GLA_DV) == (st_c // GLA_DK)
    gain = gain_ref[...]

    for c in range(GLA_TILE // C):
        rows = slice(c * C, (c + 1) * C)
        g = log_a[rows]
        g_hi = g.astype(BF16)
        g_r = g - g_hi.astype(F32)
        g_mid = g_r.astype(BF16)
        g_lo = (g_r - g_mid.astype(F32)).astype(BF16)
        b = _dot(tri, g_hi) + _dot(tri, g_mid) + _dot(tri, g_lo)
        b_last = b[C - 1:C]
        q = q_ref[rows, :].astype(F32)
        k = k_ref[rows, :].astype(F32)
        v = v_ref[rows, :]
        q_e = q * jnp.exp(b)
        k_e = (k * jnp.exp(-b)).astype(BF16)
        k_l = (k * jnp.exp(b_last - b)).astype(BF16)
        decay = jnp.exp(b_last)

        q_stack = jnp.concatenate([jnp.where(m, q_e, 0.0) for m in head_masks], axis=0).astype(BF16)
        att = jnp.where(causal, _dot_nt(q_stack, k_e), 0.0).astype(BF16)
        o_full = _dot(att, v)
        o_intra = jnp.concatenate(
            [o_full[h * C:(h + 1) * C, h * GLA_DV:(h + 1) * GLA_DV] for h in range(GLA_HEADS)], axis=1)
        st = st_ref[...]
        o = o_intra + _dot_nt(q_e.astype(BF16), st.astype(BF16))
        kv_t = _dot_tn(v, k_l)
        st_ref[...] = st * decay + jnp.where(same_head, kv_t, 0.0)

        outs = []
        for h in range(GLA_HEADS):
            oh = o[:, h * GLA_DV:(h + 1) * GLA_DV]
            outs.append(oh * lax.rsqrt(jnp.mean(oh * oh, axis=-1, keepdims=True) + EPS) * gain)
        on = jnp.concatenate(outs, axis=1)
        o_ref[rows, :] = (on * _silu(gate_ref[rows, :].astype(F32))).astype(o_ref.dtype)


def _gla(aq, ak, av, agate, adown, w_up, b_up, gain):
    nt = SEQ // GLA_TILE
    row = lambda n: pl.BlockSpec((GLA_TILE, n), lambda b, s: (b * nt + s, 0))
    full = lambda a: pl.BlockSpec(a.shape, lambda b, s: (0,) * a.ndim)
    return pl.pallas_call(
        _gla_kernel, grid=(BATCH, nt),
        in_specs=[row(GLA_QK), row(GLA_QK), row(A_WIDTH), row(A_WIDTH), row(LANES),
                  full(w_up), full(b_up), full(gain)],
        out_specs=row(A_WIDTH),
        out_shape=jax.ShapeDtypeStruct((TOKENS, A_WIDTH), BF16),
        scratch_shapes=[pltpu.VMEM((A_WIDTH, GLA_QK), F32)],
        compiler_params=pltpu.CompilerParams(dimension_semantics=("arbitrary", "arbitrary"),
                                             vmem_limit_bytes=VMEM_LIMIT),
        name="gla",
    )(aq, ak, av, agate, adown, w_up, b_up, gain)


def _swa_kernel(sink_ref, rb_ref, q_ref, kp_ref, k_ref, vp_ref, v_ref, gate_ref, o_ref, bias_ref):
    W = SWA_WINDOW
    H = SWA_HEADS
    DH = SWA_DH
    G = SWA_HEADS // SWA_KV_HEADS
    t = pl.program_id(1)

    @pl.when(jnp.logical_and(pl.program_id(0) == 0, t == 0))
    def _():
        kj = lax.broadcasted_iota(jnp.int32, (2 * W, W), 0)
        qi = lax.broadcasted_iota(jnp.int32, (2 * W, W), 1)
        dist = qi + W - kj
        in_win = jnp.logical_and(dist >= 0, dist < W)
        for h in range(H):
            bias_ref[:, h * W:(h + 1) * W] = jnp.where(in_win, _t5_bias(dist, rb_ref, h) * LOG2E, -jnp.inf)

    kcat = jnp.concatenate([kp_ref[...], k_ref[...]], axis=0)
    vcat = jnp.concatenate([vp_ref[...], v_ref[...]], axis=0)
    vt = vcat.astype(F32).T.astype(BF16)
    ones_rows = jnp.ones((SWA_VT_ROWS - LANES, 2 * W), BF16)
    q_t = q_ref[...].astype(F32).T.astype(BF16)
    zeros = jnp.zeros((DH, W), BF16)
    colh = lax.broadcasted_iota(jnp.int32, (1, H * W), 1) // W
    sink = jnp.full((1, H * W), sink_ref[H - 1], F32)
    for h in range(H - 2, -1, -1):
        sink = jnp.where(colh == h, sink_ref[h], sink)
    sink = sink * LOG2E
    key_prev = lax.broadcasted_iota(jnp.int32, (2 * W, H * W), 0) < W

    for blk in range(SWA_TILE // W):
        tok = slice(blk * W, (blk + 1) * W)
        win = slice(blk * W, blk * W + 2 * W)
        pieces = []
        for h in range(H):
            piece = q_t[h * DH:(h + 1) * DH, tok]
            pieces.append(jnp.concatenate([piece, zeros] if h < G else [zeros, piece], axis=0))
        q_pad = jnp.concatenate(pieces, axis=1)
        s = _dot(kcat[win], q_pad) + bias_ref[...]
        if blk == 0:
            s = jnp.where(jnp.logical_and(t == 0, key_prev), -jnp.inf, s)
        m = jnp.maximum(jnp.max(s, axis=0, keepdims=True), sink)
        p = jnp.exp2(s - m).astype(BF16)
        pv = _dot(jnp.concatenate([vt[:, win], ones_rows], axis=0), p)
        inv = 1.0 / (pv[LANES:LANES + 1] + jnp.exp2(sink - m))
        o_t = jnp.concatenate(
            [pv[(h // G) * DH:(h // G + 1) * DH, h * W:(h + 1) * W] * inv[:, h * W:(h + 1) * W]
             for h in range(H)], axis=0)
        gate = gate_ref[tok, :].astype(F32)
        o_ref[tok, :] = (o_t.T * _silu(gate)).astype(o_ref.dtype)


def _swa(bq, bk, bv, bgate, rel_bias, sinks):
    nt = SEQ // SWA_TILE
    per = SWA_TILE // SWA_WINDOW
    row = lambda n: pl.BlockSpec((SWA_TILE, n), lambda b, t: (b * nt + t, 0))
    prev = pl.BlockSpec((SWA_WINDOW, LANES),
                        lambda b, t: (jnp.maximum((b * nt + t) * per - 1, 0), 0))
    smem = pl.BlockSpec(memory_space=pltpu.SMEM)
    return pl.pallas_call(
        _swa_kernel, grid=(BATCH, nt),
        in_specs=[smem, smem, row(B_WIDTH), prev, row(LANES), prev, row(LANES), row(B_WIDTH)],
        out_specs=row(B_WIDTH),
        out_shape=jax.ShapeDtypeStruct((TOKENS, B_WIDTH), BF16),
        scratch_shapes=[pltpu.VMEM((2 * SWA_WINDOW, SWA_HEADS * SWA_WINDOW), F32)],
        compiler_params=pltpu.CompilerParams(dimension_semantics=("arbitrary", "arbitrary"),
                                             vmem_limit_bytes=VMEM_LIMIT),
        name="swa",
    )(sinks, rel_bias, bq, bk, bk, bv, bv, bgate)


def _moba_kernel(rb_ref, q_ref, k_ref, v_ref, gate_ref, o_ref,
                 kaug_ref, vt_ref, km_ref, bown_ref, bnear_ref, qat_ref, s_ref, m_ref, acc_ref):
    BLK = MOBA_BLOCK
    G = MOBA_GROUP
    R = G * BLK
    DUMMY = MOBA_NBLK
    kvh = pl.program_id(1)
    i = pl.program_id(2)

    @pl.when(i == 0)
    def _():
        lane = lax.broadcasted_iota(jnp.int32, (BLK, LANES), 1)
        ones_rows = jnp.ones((MOBA_VT_ROWS - MOBA_DH, BLK), BF16)

        def prep(j, carry):
            rows = pl.ds(pl.multiple_of(j * BLK, BLK), BLK)
            kb = k_ref[rows, :]
            kaug_ref[j, :, 0:LANES] = kb
            kaug_ref[j, :, LANES:2 * LANES] = jnp.where(lane == j, 1.0, 0.0).astype(BF16)
            km_ref[pl.ds(j, 1), :] = jnp.mean(kb.astype(F32), axis=0, keepdims=True)
            vt_ref[j, 0:MOBA_DH, :] = v_ref[rows, :].astype(F32).T.astype(BF16)
            vt_ref[j, MOBA_DH:, :] = ones_rows
            return carry

        lax.fori_loop(0, MOBA_NBLK, prep, 0)
        kaug_ref[DUMMY, :, 0:LANES] = jnp.zeros((BLK, LANES), BF16)
        kaug_ref[DUMMY, :, LANES:2 * LANES] = jnp.where(lane == DUMMY, 1.0, 0.0).astype(BF16)
        vt_ref[DUMMY] = jnp.zeros((MOBA_VT_ROWS, BLK), BF16)
        qat_ref[MOBA_DH + MOBA_NBLK:, :] = jnp.full((2 * LANES - MOBA_DH - MOBA_NBLK, R), NEG_BIG, BF16)

        tk = lax.broadcasted_iota(jnp.int32, (BLK, BLK), 0)
        tq = lax.broadcasted_iota(jnp.int32, (BLK, BLK), 1)
        d_own = tq - tk
        for g in range(G):
            head = kvh * G + g
            cols = slice(g * BLK, (g + 1) * BLK)
            bown_ref[:, cols] = jnp.where(d_own >= 0, _t5_bias(d_own, rb_ref, head) * LOG2E, -jnp.inf)
            bnear_ref[:, cols] = _t5_bias(d_own + BLK, rb_ref, head) * LOG2E

    q = jnp.concatenate([q_ref[:, g * MOBA_DH:(g + 1) * MOBA_DH] for g in range(G)], axis=0)
    q_t = q.astype(F32).T.astype(BF16)
    qat_ref[0:MOBA_DH, :] = q_t

    s_ref[0] = _dot(kaug_ref[i, :, 0:MOBA_DH], q_t) + bown_ref[...]

    km = km_ref[...]
    km_hi = km.astype(BF16)
    km_lo = (km - km_hi.astype(F32)).astype(BF16)
    gate = _dot(km_hi, q_t) + _dot(km_lo, q_t)
    blk = lax.broadcasted_iota(jnp.int32, (MOBA_NBLK, R), 0)
    blkf = blk.astype(F32)
    past = blk < i
    g_ = jnp.where(past, gate, -jnp.inf)
    selected = jnp.zeros((MOBA_NBLK, R), dtype=jnp.bool_)
    for _ in range(MOBA_TOPK):
        mx = jnp.max(g_, axis=0, keepdims=True)
        first = jnp.min(jnp.where(g_ == mx, blkf, 1e9), axis=0, keepdims=True)
        pick = jnp.logical_and(blkf == first, past)
        selected = jnp.logical_or(selected, pick)
        g_ = jnp.where(pick, -jnp.inf, g_)

    colh = lax.broadcasted_iota(jnp.int32, (1, R), 1) // BLK
    far_row = REL_BUCKETS - 1
    cfar = jnp.where(colh == 0, rb_ref[far_row, kvh * G],
                     jnp.where(colh == 1, rb_ref[far_row, kvh * G + 1],
                               jnp.where(colh == 2, rb_ref[far_row, kvh * G + 2],
                                         rb_ref[far_row, kvh * G + 3])))
    sel_bias = jnp.where(selected, jnp.where(blk < i - 1, cfar * LOG2E, 0.0), NEG_BIG)
    qat_ref[MOBA_DH:MOBA_DH + MOBA_NBLK, :] = sel_bias.astype(BF16)

    def logits(j):
        return _dot(kaug_ref[j], qat_ref[...])

    def consume(slot, j, first=False):
        s = s_ref[slot]
        if first:
            m_new = jnp.max(s, axis=0, keepdims=True)
        else:
            m_old = m_ref[...]
            m_new = jnp.maximum(m_old, jnp.max(s, axis=0, keepdims=True))
        p = jnp.exp2(s - m_new).astype(BF16)
        pv = _dot(vt_ref[j], p)
        acc_ref[...] = pv if first else jnp.exp2(m_old - m_new) * acc_ref[...] + pv
        m_ref[...] = m_new

    n_far = jnp.maximum(i - 1, 0)
    n_pairs = (n_far + 1) // 2

    def far_idx(n):
        return jnp.where(n < n_far, n, DUMMY)

    def far_step(n, slot):
        s_ref[1 - slot] = logits(far_idx(n + 1))
        consume(slot, far_idx(n))

    near = jnp.where(i >= 1, i - 1, DUMMY)
    s_ref[1] = logits(near) + bnear_ref[...]
    consume(0, i, first=True)
    s_ref[0] = logits(far_idx(0))
    consume(1, near)

    def far_quad(u, carry):
        for d in range(4):
            far_step(4 * u + d, d % 2)
        return carry

    lax.fori_loop(0, n_pairs // 2, far_quad, 0)

    @pl.when(n_pairs % 2 == 1)
    def _():
        for d in range(2):
            far_step(2 * (n_pairs - 1) + d, d % 2)

    acc = acc_ref[...]
    o = (acc[0:MOBA_DH] * (1.0 / acc[MOBA_DH:MOBA_DH + 1])).T
    for g in range(G):
        cols = slice(g * MOBA_DH, (g + 1) * MOBA_DH)
        gt = gate_ref[:, cols].astype(F32)
        o_ref[:, cols] = (o[g * BLK:(g + 1) * BLK] * _silu(gt)).astype(o_ref.dtype)


def _moba(cq, ck, cv, cgate, rel_bias):
    R = MOBA_GROUP * MOBA_BLOCK
    gw = MOBA_GROUP * MOBA_DH
    qspec = pl.BlockSpec((MOBA_BLOCK, gw), lambda b, h, i: (b * MOBA_NBLK + i, h))
    kvspec = pl.BlockSpec((SEQ, MOBA_DH), lambda b, h, i: (b, h))
    return pl.pallas_call(
        _moba_kernel, grid=(BATCH, MOBA_KV_HEADS, MOBA_NBLK),
        in_specs=[pl.BlockSpec(memory_space=pltpu.SMEM), qspec, kvspec, kvspec, qspec],
        out_specs=qspec,
        out_shape=jax.ShapeDtypeStruct((TOKENS, C_WIDTH), BF16),
        scratch_shapes=[pltpu.VMEM((MOBA_NBLK + 1, MOBA_BLOCK, 2 * LANES), BF16),
                        pltpu.VMEM((MOBA_NBLK + 1, MOBA_VT_ROWS, MOBA_BLOCK), BF16),
                        pltpu.VMEM((MOBA_NBLK, MOBA_DH), F32),
                        pltpu.VMEM((MOBA_BLOCK, R), F32),
                        pltpu.VMEM((MOBA_BLOCK, R), F32),
                        pltpu.VMEM((2 * LANES, R), BF16),
                        pltpu.VMEM((2, MOBA_BLOCK, R), F32),
                        pltpu.VMEM((1, R), F32),
                        pltpu.VMEM((MOBA_VT_ROWS, R), F32)],
        compiler_params=pltpu.CompilerParams(
            dimension_semantics=("arbitrary", "arbitrary", "arbitrary"), vmem_limit_bytes=VMEM_LIMIT),
        name="moba",
    )(rel_bias, cq, ck, cv, cgate)


def kernel(x, norm_gain, final_gain, rel_bias, w_in_even, gla_w_up, gla_b_up, gla_norm_gain, swa_sinks,
           w_out_even, w_in_odd, w_out_odd):
    assert x.shape == (BATCH, SEQ, D_MODEL)
    h0 = x.reshape(TOKENS, D_MODEL)

    we = w_in_even[0]
    splits = np.cumsum([0, GLA_QK, GLA_QK, A_WIDTH, GLA_RANK, A_WIDTH, B_WIDTH, SWA_KV_HEADS * SWA_DH,
                        SWA_KV_HEADS * SWA_DH, B_WIDTH])
    cols = [we[:, splits[k]:splits[k + 1]] for k in range(9)]
    w_aq, w_ak, w_av, w_down, w_agate, w_bq, w_bk, w_bv, w_bgate = cols
    w_down = jnp.pad(jnp.concatenate([w_down] * 3, axis=1), ((0, 0), (0, LANES - 3 * GLA_RANK)))
    proj_ws = [w.astype(BF16) for w in (w_aq, w_ak, w_av, w_down, w_agate, w_bq, w_bk, w_bv, w_bgate)]
    proj_dt = [BF16, BF16, BF16, F32, BF16, BF16, BF16, BF16, BF16]
    scales = [GLA_DK ** -0.5, 1.0, 1.0, 1.0, 1.0, SWA_DH ** -0.5 * LOG2E, 1.0, 1.0, 1.0]
    aq, ak, av, adown, agate, bq, bk, bv, bgate = _layer_io(
        h0, [], norm_gain[0:1], proj_ws, proj_dt, scales, write_h=False)

    w_up_hi = gla_w_up[0].astype(BF16)
    w_up_lo = (gla_w_up[0] - w_up_hi.astype(F32)).astype(BF16)
    w_up = jnp.pad(jnp.concatenate([w_up_hi, w_up_hi, w_up_lo], axis=0), ((0, LANES - 3 * GLA_RANK), (0, 0)))
    oa = _gla(aq, ak, av, agate, adown, w_up, gla_b_up[0:1], gla_norm_gain[0:1])
    ob = _swa(bq, bk, bv, bgate, rel_bias, swa_sinks[0])

    wo = w_out_even[0].astype(BF16)
    wi = w_in_odd[0]
    osp = np.cumsum([0, C_WIDTH, MOBA_KV_HEADS * MOBA_DH, MOBA_KV_HEADS * MOBA_DH, C_WIDTH])
    odd_ws = [wi[:, osp[k]:osp[k + 1]].astype(BF16) for k in range(4)]
    h1, cq, ck, cv, cgate = _layer_io(
        h0, [(oa, wo[:A_WIDTH]), (ob, wo[A_WIDTH:])], norm_gain[1:2], odd_ws, [BF16] * 4,
        [MOBA_DH ** -0.5 * LOG2E, 1.0, 1.0, 1.0], write_h=True)

    oc = _moba(cq, ck, cv, cgate, rel_bias)

    (out,) = _layer_io(h1, [(oc, w_out_odd[0].astype(BF16))], final_gain.reshape(1, D_MODEL), [], [], [],
                       write_h=False)
    return out.reshape(BATCH, SEQ, D_MODEL)
```

```python
import functools
import math

import numpy as np
import jax
import jax.numpy as jnp
from jax import lax
from jax.experimental import pallas as pl
from jax.experimental.pallas import tpu as pltpu

D_MODEL = 1024
BATCH = 2
SEQ = 8192
TOKENS = BATCH * SEQ

REL_BUCKETS = 32
REL_MAX_EXACT = 16
REL_MAX_DIST = 128
ATTN_HEADS = 8

GLA_HEADS = 4
GLA_DK = 64
GLA_DV = 128
GLA_RANK = 16
GLA_TAU = 16.0
GLA_CHUNK = 64

SWA_HEADS = 8
SWA_KV_HEADS = 2
SWA_DH = 64
SWA_WINDOW = 128

MOBA_HEADS = 8
MOBA_KV_HEADS = 2
MOBA_DH = 128
MOBA_BLOCK = 256
MOBA_TOPK = 3
MOBA_GROUP = MOBA_HEADS // MOBA_KV_HEADS
MOBA_NBLK = SEQ // MOBA_BLOCK
MOBA_VT_ROWS = MOBA_DH + 16
LOG2E = math.log2(math.e)
SWA_VT_ROWS = SWA_KV_HEADS * SWA_DH + 16

EPS = 1e-6
LANES = 128
NEG_BIG = -1e30
VMEM_LIMIT = 48 * 1024 * 1024
MOBA_VMEM_LIMIT = 56 * 1024 * 1024

A_WIDTH = GLA_HEADS * GLA_DV
B_WIDTH = SWA_HEADS * SWA_DH
C_WIDTH = MOBA_HEADS * MOBA_DH
GLA_QK = GLA_HEADS * GLA_DK

ROW_TILE = 512
GLA_TILE = 512
SWA_TILE = 512

F32 = jnp.float32
BF16 = jnp.bfloat16


def _dot(a, b):
    return jnp.dot(a, b, preferred_element_type=F32)


def _dot_nt(a, b):
    return lax.dot_general(a, b, (((1,), (1,)), ((), ())), preferred_element_type=F32)


def _dot_tn(a, b):
    return lax.dot_general(a, b, (((0,), (0,)), ((), ())), preferred_element_type=F32)


def _silu(x):
    return x * (1.0 / (1.0 + jnp.exp(-x)))


def _t5_thresholds():
    d = np.arange(REL_MAX_DIST + 1)
    nf = np.maximum(d, 1).astype(np.float32)
    large = REL_MAX_EXACT + (np.log(nf / np.float32(REL_MAX_EXACT))
                             / np.float32(math.log(REL_MAX_DIST / REL_MAX_EXACT))
                             * np.float32(REL_BUCKETS - REL_MAX_EXACT)).astype(np.int32)
    bucket = np.where(d < REL_MAX_EXACT, d, np.minimum(large, REL_BUCKETS - 1))
    assert np.all(np.diff(bucket) >= 0) and bucket[-1] == REL_BUCKETS - 1
    return [int(np.argmax(bucket >= b)) for b in range(REL_BUCKETS)]


_T5_THRESH = _t5_thresholds()


def _t5_bias(dist, rb_ref, head):
    out = jnp.full(dist.shape, rb_ref[0, head], F32)
    for b in range(1, REL_BUCKETS):
        out = jnp.where(dist >= _T5_THRESH[b], rb_ref[b, head], out)
    return out


def _layer_io_kernel(*refs, n_acc, n_proj, write_h, scales):
    h_ref = refs[0]
    pos = 1
    acc_pairs = []
    for _ in range(n_acc):
        acc_pairs.append((refs[pos], refs[pos + 1]))
        pos += 2
    gain_ref = refs[pos]
    pos += 1
    wp_refs = refs[pos:pos + n_proj]
    pos += n_proj
    out_refs = refs[pos:]

    h = h_ref[...]
    for a_ref, w_ref in acc_pairs:
        h = h + _dot(a_ref[...], w_ref[...])
    o = 0
    if write_h:
        out_refs[0][...] = h
        o = 1
    y = h * lax.rsqrt(jnp.mean(h * h, axis=-1, keepdims=True) + EPS) * gain_ref[...]
    if n_proj == 0:
        out_refs[o][...] = y
        return
    yb = y.astype(BF16)
    for k in range(n_proj):
        r = _dot(yb, wp_refs[k][...])
        if scales[k] != 1.0:
            r = r * scales[k]
        out_refs[o + k][...] = r.astype(out_refs[o + k].dtype)


def _layer_io(h, acc_pairs, gain, proj_ws, proj_dtypes, scales, write_h):
    n_rows = h.shape[0]
    grid = (n_rows // ROW_TILE,)
    row_spec = lambda n: pl.BlockSpec((ROW_TILE, n), lambda i: (i, 0))
    full_spec = lambda a: pl.BlockSpec(a.shape, lambda i: (0,) * a.ndim)

    args = [h]
    in_specs = [row_spec(D_MODEL)]
    for a, w in acc_pairs:
        args += [a, w]
        in_specs += [row_spec(a.shape[1]), full_spec(w)]
    args.append(gain)
    in_specs.append(full_spec(gain))
    for w in proj_ws:
        args.append(w)
        in_specs.append(full_spec(w))

    out_shape, out_specs = [], []
    if write_h:
        out_shape.append(jax.ShapeDtypeStruct((n_rows, D_MODEL), F32))
        out_specs.append(row_spec(D_MODEL))
    if proj_ws:
        for w, dt in zip(proj_ws, proj_dtypes):
            out_shape.append(jax.ShapeDtypeStruct((n_rows, w.shape[1]), dt))
            out_specs.append(row_spec(w.shape[1]))
    else:
        out_shape.append(jax.ShapeDtypeStruct((n_rows, D_MODEL), F32))
        out_specs.append(row_spec(D_MODEL))

    kern = functools.partial(_layer_io_kernel, n_acc=len(acc_pairs), n_proj=len(proj_ws),
                             write_h=write_h, scales=tuple(scales))
    return pl.pallas_call(
        kern, grid=grid, in_specs=in_specs, out_specs=out_specs, out_shape=out_shape,
        compiler_params=pltpu.CompilerParams(dimension_semantics=("arbitrary",),
                                             vmem_limit_bytes=VMEM_LIMIT),
        name="layer_io",
    )(*args)


def _gla_kernel(q_ref, k_ref, v_ref, gate_ref, down_ref, wup_ref, bup_ref, gain_ref, o_ref, st_ref):
    C = GLA_CHUNK

    @pl.when(pl.program_id(1) == 0)
    def _():
        st_ref[...] = jnp.zeros_like(st_ref)

    a = down_ref[...]
    a_hi = a.astype(BF16)
    a_lo = (a - a_hi.astype(F32)).astype(BF16)
    lane_a = lax.broadcasted_iota(jnp.int32, a.shape, 1) // GLA_RANK
    z = _dot(jnp.where(lane_a == 1, a_lo, a_hi), wup_ref[...]) + bup_ref[...]
    log_a = (jnp.minimum(z, 0.0) - jnp.log1p(jnp.exp(-jnp.abs(z)))) * (1.0 / GLA_TAU)

    r_i = lax.broadcasted_iota(jnp.int32, (C, C), 0)
    c_i = lax.broadcasted_iota(jnp.int32, (C, C), 1)
    tri = (c_i <= r_i).astype(BF16)
    lane_qk = lax.broadcasted_iota(jnp.int32, (C, GLA_QK), 1)
    head_masks = [(lane_qk // GLA_DK) == h for h in range(GLA_HEADS)]
    rs = lax.broadcasted_iota(jnp.int32, (GLA_HEADS * C, C), 0)
    cs = lax.broadcasted_iota(jnp.int32, (GLA_HEADS * C, C), 1)
    causal = (rs % C) >= cs
    st_r = lax.broadcasted_iota(jnp.int32, (A_WIDTH, GLA_QK), 0)
    st_c = lax.broadcasted_iota(jnp.int32, (A_WIDTH, GLA_QK), 1)
    same_head = (st_r // GLA_DV) == (st_c // GLA_DK)
    gain = gain_ref[...]

    for c in range(GLA_TILE // C):
        rows = slice(c * C, (c + 1) * C)
        g = log_a[rows]
        g_hi = g.astype(BF16)
        g_r = g - g_hi.astype(F32)
        g_mid = g_r.astype(BF16)
        g_lo = (g_r - g_mid.astype(F32)).astype(BF16)
        b = _dot(tri, g_hi) + _dot(tri, g_mid) + _dot(tri, g_lo)
        b_last = b[C - 1:C]
        q = q_ref[rows, :].astype(F32)
        k = k_ref[rows, :].astype(F32)
        v = v_ref[rows, :]
        q_e = q * jnp.exp(b)
        k_e = (k * jnp.exp(-b)).astype(BF16)
        k_l = (k * jnp.exp(b_last - b)).astype(BF16)
        decay = jnp.exp(b_last)

        q_stack = jnp.concatenate([jnp.where(m, q_e, 0.0) for m in head_masks], axis=0).astype(BF16)
        att = jnp.where(causal, _dot_nt(q_stack, k_e), 0.0).astype(BF16)
        o_full = _dot(att, v)
        o_intra = jnp.concatenate(
            [o_full[h * C:(h + 1) * C, h * GLA_DV:(h + 1) * GLA_DV] for h in range(GLA_HEADS)], axis=1)
        st = st_ref[...]
        o = o_intra + _dot_nt(q_e.astype(BF16), st.astype(BF16))
        kv_t = _dot_tn(v, k_l)
        st_ref[...] = st * decay + jnp.where(same_head, kv_t, 0.0)

        outs = []
        for h in range(GLA_HEADS):
            oh = o[:, h * GLA_DV:(h + 1) * GLA_DV]
            outs.append(oh * lax.rsqrt(jnp.mean(oh * oh, axis=-1, keepdims=True) + EPS) * gain)
        on = jnp.concatenate(outs, axis=1)
        o_ref[rows, :] = (on * _silu(gate_ref[rows, :].astype(F32))).astype(o_ref.dtype)


def _gla(aq, ak, av, agate, adown, w_up, b_up, gain):
    nt = SEQ // GLA_TILE
    row = lambda n: pl.BlockSpec((GLA_TILE, n), lambda b, s: (b * nt + s, 0))
    full = lambda a: pl.BlockSpec(a.shape, lambda b, s: (0,) * a.ndim)
    return pl.pallas_call(
        _gla_kernel, grid=(BATCH, nt),
        in_specs=[row(GLA_QK), row(GLA_QK), row(A_WIDTH), row(A_WIDTH), row(LANES),
                  full(w_up), full(b_up), full(gain)],
        out_specs=row(A_WIDTH),
        out_shape=jax.ShapeDtypeStruct((TOKENS, A_WIDTH), BF16),
        scratch_shapes=[pltpu.VMEM((A_WIDTH, GLA_QK), F32)],
        compiler_params=pltpu.CompilerParams(dimension_semantics=("arbitrary", "arbitrary"),
                                             vmem_limit_bytes=VMEM_LIMIT),
        name="gla",
    )(aq, ak, av, agate, adown, w_up, b_up, gain)


def _swa_kernel(sink_ref, rb_ref, q_ref, kp_ref, k_ref, vp_ref, v_ref, gate_ref, o_ref, bias_ref):
    W = SWA_WINDOW
    H = SWA_HEADS
    DH = SWA_DH
    G = SWA_HEADS // SWA_KV_HEADS
    t = pl.program_id(1)

    @pl.when(jnp.logical_and(pl.program_id(0) == 0, t == 0))
    def _():
        kj = lax.broadcasted_iota(jnp.int32, (2 * W, W), 0)
        qi = lax.broadcasted_iota(jnp.int32, (2 * W, W), 1)
        dist = qi + W - kj
        in_win = jnp.logical_and(dist >= 0, dist < W)
        for h in range(H):
            bias_ref[:, h * W:(h + 1) * W] = jnp.where(in_win, _t5_bias(dist, rb_ref, h) * LOG2E, -jnp.inf)

    kcat = jnp.concatenate([kp_ref[...], k_ref[...]], axis=0)
    vcat = jnp.concatenate([vp_ref[...], v_ref[...]], axis=0)
    vt = vcat.astype(F32).T.astype(BF16)
    ones_rows = jnp.ones((SWA_VT_ROWS - LANES, 2 * W), BF16)
    q_t = q_ref[...].astype(F32).T.astype(BF16)
    zeros = jnp.zeros((DH, W), BF16)
    colh = lax.broadcasted_iota(jnp.int32, (1, H * W), 1) // W
    sink = jnp.full((1, H * W), sink_ref[H - 1], F32)
    for h in range(H - 2, -1, -1):
        sink = jnp.where(colh == h, sink_ref[h], sink)
    sink = sink * LOG2E
    key_prev = lax.broadcasted_iota(jnp.int32, (2 * W, H * W), 0) < W

    for blk in range(SWA_TILE // W):
        tok = slice(blk * W, (blk + 1) * W)
        win = slice(blk * W, blk * W + 2 * W)
        pieces = []
        for h in range(H):
            piece = q_t[h * DH:(h + 1) * DH, tok]
            pieces.append(jnp.concatenate([piece, zeros] if h < G else [zeros, piece], axis=0))
        q_pad = jnp.concatenate(pieces, axis=1)
        s = _dot(kcat[win], q_pad) + bias_ref[...]
        if blk == 0:
            s = jnp.where(jnp.logical_and(t == 0, key_prev), -jnp.inf, s)
        m = jnp.maximum(jnp.max(s, axis=0, keepdims=True), sink)
        p = jnp.exp2(s - m).astype(BF16)
        pv = _dot(jnp.concatenate([vt[:, win], ones_rows], axis=0), p)
        inv = 1.0 / (pv[LANES:LANES + 1] + jnp.exp2(sink - m))
        o_t = jnp.concatenate(
            [pv[(h // G) * DH:(h // G + 1) * DH, h * W:(h + 1) * W] * inv[:, h * W:(h + 1) * W]
             for h in range(H)], axis=0)
        gate = gate_ref[tok, :].astype(F32)
        o_ref[tok, :] = (o_t.T * _silu(gate)).astype(o_ref.dtype)


def _swa(bq, bk, bv, bgate, rel_bias, sinks):
    nt = SEQ // SWA_TILE
    per = SWA_TILE // SWA_WINDOW
    row = lambda n: pl.BlockSpec((SWA_TILE, n), lambda b, t: (b * nt + t, 0))
    prev = pl.BlockSpec((SWA_WINDOW, LANES),
                        lambda b, t: (jnp.maximum((b * nt + t) * per - 1, 0), 0))
    smem = pl.BlockSpec(memory_space=pltpu.SMEM)
    return pl.pallas_call(
        _swa_kernel, grid=(BATCH, nt),
        in_specs=[smem, smem, row(B_WIDTH), prev, row(LANES), prev, row(LANES), row(B_WIDTH)],
        out_specs=row(B_WIDTH),
        out_shape=jax.ShapeDtypeStruct((TOKENS, B_WIDTH), BF16),
        scratch_shapes=[pltpu.VMEM((2 * SWA_WINDOW, SWA_HEADS * SWA_WINDOW), F32)],
        compiler_params=pltpu.CompilerParams(dimension_semantics=("arbitrary", "arbitrary"),
                                             vmem_limit_bytes=VMEM_LIMIT),
        name="swa",
    )(sinks, rel_bias, bq, bk, bk, bv, bv, bgate)


def _moba_kernel(rb_ref, q_ref, k_ref, v_ref, gate_ref, o_ref,
                 kaug_ref, vt_ref, km_ref, bown_ref, bnear_ref, qat_ref, s_ref, m_ref, acc_ref):
    BLK = MOBA_BLOCK
    G = MOBA_GROUP
    KVH = MOBA_KV_HEADS
    R = G * BLK
    NB1 = MOBA_NBLK + 1
    DUMMY = MOBA_NBLK
    i = pl.program_id(1)

    @pl.when(i == 0)
    def _():
        lane = lax.broadcasted_iota(jnp.int32, (BLK, LANES), 1)
        ones_rows = jnp.ones((MOBA_VT_ROWS - MOBA_DH, BLK), BF16)

        def prep(j, carry):
            rows = pl.ds(pl.multiple_of(j * BLK, BLK), BLK)
            onehot = jnp.where(lane == j, 1.0, 0.0).astype(BF16)
            for kh in range(KVH):
                kb = k_ref[rows, kh * MOBA_DH:(kh + 1) * MOBA_DH]
                kaug_ref[kh * NB1 + j, :, 0:LANES] = kb
                kaug_ref[kh * NB1 + j, :, LANES:2 * LANES] = onehot
                km_ref[pl.ds(kh * MOBA_NBLK + j, 1), :] = jnp.mean(kb.astype(F32), axis=0, keepdims=True)
                vb = v_ref[rows, kh * MOBA_DH:(kh + 1) * MOBA_DH]
                vt_ref[kh * NB1 + j, 0:MOBA_DH, :] = vb.astype(F32).T.astype(BF16)
                vt_ref[kh * NB1 + j, MOBA_DH:, :] = ones_rows
            return carry

        lax.fori_loop(0, MOBA_NBLK, prep, 0)
        for kh in range(KVH):
            kaug_ref[kh * NB1 + DUMMY, :, 0:LANES] = jnp.zeros((BLK, LANES), BF16)
            kaug_ref[kh * NB1 + DUMMY, :, LANES:2 * LANES] = jnp.where(lane == DUMMY, 1.0, 0.0).astype(BF16)
            vt_ref[kh * NB1 + DUMMY] = jnp.zeros((MOBA_VT_ROWS, BLK), BF16)
            qat_ref[kh, MOBA_DH + MOBA_NBLK:, :] = jnp.full((2 * LANES - MOBA_DH - MOBA_NBLK, R), NEG_BIG, BF16)

    @pl.when(jnp.logical_and(pl.program_id(0) == 0, i == 0))
    def _():
        tk = lax.broadcasted_iota(jnp.int32, (BLK, BLK), 0)
        tq = lax.broadcasted_iota(jnp.int32, (BLK, BLK), 1)
        d_own = tq - tk
        for head in range(MOBA_HEADS):
            cols = slice(head * BLK, (head + 1) * BLK)
            bown_ref[:, cols] = jnp.where(d_own >= 0, _t5_bias(d_own, rb_ref, head) * LOG2E, -jnp.inf)
            bnear_ref[:, cols] = _t5_bias(d_own + BLK, rb_ref, head) * LOG2E

    blk = lax.broadcasted_iota(jnp.int32, (MOBA_NBLK, R), 0)
    blkf = blk.astype(F32)
    past = blk < i
    colh = lax.broadcasted_iota(jnp.int32, (1, R), 1) // BLK
    far_row = REL_BUCKETS - 1

    for kh in range(KVH):
        q = jnp.concatenate([q_ref[:, (kh * G + g) * MOBA_DH:(kh * G + g + 1) * MOBA_DH] for g in range(G)],
                            axis=0)
        q_t = q.astype(F32).T.astype(BF16)
        qat_ref[kh, 0:MOBA_DH, :] = q_t
        s_ref[kh] = _dot(kaug_ref[kh * NB1 + i, :, 0:MOBA_DH], q_t) + bown_ref[:, kh * R:(kh + 1) * R]

        km = km_ref[kh * MOBA_NBLK:(kh + 1) * MOBA_NBLK, :]
        km_hi = km.astype(BF16)
        km_lo = (km - km_hi.astype(F32)).astype(BF16)
        gate = _dot(km_hi, q_t) + _dot(km_lo, q_t)
        g_ = jnp.where(past, gate, -jnp.inf)
        selected = jnp.zeros((MOBA_NBLK, R), dtype=jnp.bool_)
        for _ in range(MOBA_TOPK):
            mx = jnp.max(g_, axis=0, keepdims=True)
            first = jnp.min(jnp.where(g_ == mx, blkf, 1e9), axis=0, keepdims=True)
            pick = jnp.logical_and(blkf == first, past)
            selected = jnp.logical_or(selected, pick)
            g_ = jnp.where(pick, -jnp.inf, g_)
        cfar = jnp.where(colh == 0, rb_ref[far_row, kh * G],
                         jnp.where(colh == 1, rb_ref[far_row, kh * G + 1],
                                   jnp.where(colh == 2, rb_ref[far_row, kh * G + 2],
                                             rb_ref[far_row, kh * G + 3])))
        sel_bias = jnp.where(selected, jnp.where(blk < i - 1, cfar * LOG2E, 0.0), NEG_BIG)
        qat_ref[kh, MOBA_DH:MOBA_DH + MOBA_NBLK, :] = sel_bias.astype(BF16)

    def logits(kh, j):
        return _dot(kaug_ref[kh * NB1 + j], qat_ref[kh])

    def consume(slot, kh, j, first=False):
        s = s_ref[slot * KVH + kh]
        if first:
            m_new = jnp.max(s, axis=0, keepdims=True)
        else:
            m_old = m_ref[kh]
            m_new = jnp.maximum(m_old, jnp.max(s, axis=0, keepdims=True))
        p = jnp.exp2(s - m_new).astype(BF16)
        pv = _dot(vt_ref[kh * NB1 + j], p)
        acc_ref[kh] = pv if first else jnp.exp2(m_old - m_new) * acc_ref[kh] + pv
        m_ref[kh] = m_new

    n_far = jnp.maximum(i - 1, 0)
    n_pairs = (n_far + 1) // 2

    def far_idx(n):
        return jnp.where(n < n_far, n, DUMMY)

    def far_step(n, slot):
        for kh in range(KVH):
            s_ref[(1 - slot) * KVH + kh] = logits(kh, far_idx(n + 1))
        for kh in range(KVH):
            consume(slot, kh, far_idx(n))

    near = jnp.where(i >= 1, i - 1, DUMMY)
    for kh in range(KVH):
        s_ref[KVH + kh] = logits(kh, near) + bnear_ref[:, kh * R:(kh + 1) * R]
    for kh in range(KVH):
        consume(0, kh, i, first=True)
    for kh in range(KVH):
        s_ref[kh] = logits(kh, far_idx(0))
    for kh in range(KVH):
        consume(1, kh, near)

    def far_quad(u, carry):
        for d in range(4):
            far_step(4 * u + d, d % 2)
        return carry

    lax.fori_loop(0, n_pairs // 2, far_quad, 0)

    @pl.when(n_pairs % 2 == 1)
    def _():
        for d in range(2):
            far_step(2 * (n_pairs - 1) + d, d % 2)

    for kh in range(KVH):
        acc = acc_ref[kh]
        o = (acc[0:MOBA_DH] * (1.0 / acc[MOBA_DH:MOBA_DH + 1])).T
        for g in range(G):
            cols = slice((kh * G + g) * MOBA_DH, (kh * G + g + 1) * MOBA_DH)
            gt = gate_ref[:, cols].astype(F32)
            o_ref[:, cols] = (o[g * BLK:(g + 1) * BLK] * _silu(gt)).astype(o_ref.dtype)


def _moba(cq, ck, cv, cgate, rel_bias):
    R = MOBA_GROUP * MOBA_BLOCK
    KVH = MOBA_KV_HEADS
    NB1 = MOBA_NBLK + 1
    qspec = pl.BlockSpec((MOBA_BLOCK, C_WIDTH), lambda b, i: (b * MOBA_NBLK + i, 0))
    kvspec = pl.BlockSpec((SEQ, KVH * MOBA_DH), lambda b, i: (b, 0))
    return pl.pallas_call(
        _moba_kernel, grid=(BATCH, MOBA_NBLK),
        in_specs=[pl.BlockSpec(memory_space=pltpu.SMEM), qspec, kvspec, kvspec, qspec],
        out_specs=qspec,
        out_shape=jax.ShapeDtypeStruct((TOKENS, C_WIDTH), BF16),
        scratch_shapes=[pltpu.VMEM((KVH * NB1, MOBA_BLOCK, 2 * LANES), BF16),
                        pltpu.VMEM((KVH * NB1, MOBA_VT_ROWS, MOBA_BLOCK), BF16),
                        pltpu.VMEM((KVH * MOBA_NBLK, MOBA_DH), F32),
                        pltpu.VMEM((MOBA_BLOCK, KVH * R), F32),
                        pltpu.VMEM((MOBA_BLOCK, KVH * R), F32),
                        pltpu.VMEM((KVH, 2 * LANES, R), BF16),
                        pltpu.VMEM((2 * KVH, MOBA_BLOCK, R), F32),
                        pltpu.VMEM((KVH, 1, R), F32),
                        pltpu.VMEM((KVH, MOBA_VT_ROWS, R), F32)],
        compiler_params=pltpu.CompilerParams(
            dimension_semantics=("arbitrary", "arbitrary"), vmem_limit_bytes=MOBA_VMEM_LIMIT),
        name="moba",
    )(rel_bias, cq, ck, cv, cgate)


def kernel(x, norm_gain, final_gain, rel_bias, w_in_even, gla_w_up, gla_b_up, gla_norm_gain, swa_sinks,
           w_out_even, w_in_odd, w_out_odd):
    assert x.shape == (BATCH, SEQ, D_MODEL)
    h0 = x.reshape(TOKENS, D_MODEL)

    we = w_in_even[0]
    splits = np.cumsum([0, GLA_QK, GLA_QK, A_WIDTH, GLA_RANK, A_WIDTH, B_WIDTH, SWA_KV_HEADS * SWA_DH,
                        SWA_KV_HEADS * SWA_DH, B_WIDTH])
    cols = [we[:, splits[k]:splits[k + 1]] for k in range(9)]
    w_aq, w_ak, w_av, w_down, w_agate, w_bq, w_bk, w_bv, w_bgate = cols
    w_down = jnp.pad(jnp.concatenate([w_down] * 3, axis=1), ((0, 0), (0, LANES - 3 * GLA_RANK)))
    proj_ws = [w.astype(BF16) for w in (w_aq, w_ak, w_av, w_down, w_agate, w_bq, w_bk, w_bv, w_bgate)]
    proj_dt = [BF16, BF16, BF16, F32, BF16, BF16, BF16, BF16, BF16]
    scales = [GLA_DK ** -0.5, 1.0, 1.0, 1.0, 1.0, SWA_DH ** -0.5 * LOG2E, 1.0, 1.0, 1.0]
    aq, ak, av, adown, agate, bq, bk, bv, bgate = _layer_io(
        h0, [], norm_gain[0:1], proj_ws, proj_dt, scales, write_h=False)

    w_up_hi = gla_w_up[0].astype(BF16)
    w_up_lo = (gla_w_up[0] - w_up_hi.astype(F32)).astype(BF16)
    w_up = jnp.pad(jnp.concatenate([w_up_hi, w_up_hi, w_up_lo], axis=0), ((0, LANES - 3 * GLA_RANK), (0, 0)))
    oa = _gla(aq, ak, av, agate, adown, w_up, gla_b_up[0:1], gla_norm_gain[0:1])
    ob = _swa(bq, bk, bv, bgate, rel_bias, swa_sinks[0])

    wo = w_out_even[0].astype(BF16)
    wi = w_in_odd[0]
    osp = np.cumsum([0, C_WIDTH, MOBA_KV_HEADS * MOBA_DH, MOBA_KV_HEADS * MOBA_DH, C_WIDTH])
    odd_ws = [wi[:, osp[k]:osp[k + 1]].astype(BF16) for k in range(4)]
    h1, cq, ck, cv, cgate = _layer_io(
        h0, [(oa, wo[:A_WIDTH]), (ob, wo[A_WIDTH:])], norm_gain[1:2], odd_ws, [BF16] * 4,
        [MOBA_DH ** -0.5 * LOG2E, 1.0, 1.0, 1.0], write_h=True)

    oc = _moba(cq, ck, cv, cgate, rel_bias)

    (out,) = _layer_io(h1, [(oc, w_out_odd[0].astype(BF16))], final_gain.reshape(1, D_MODEL), [], [], [],
                       write_h=False)
    return out.reshape(BATCH, SEQ, D_MODEL)
```

```python
import functools
import math

import numpy as np
import jax
import jax.numpy as jnp
from jax import lax
from jax.experimental import pallas as pl
from jax.experimental.pallas import tpu as pltpu

D_MODEL = 1024
BATCH = 2
SEQ = 8192
TOKENS = BATCH * SEQ

REL_BUCKETS = 32
REL_MAX_EXACT = 16
REL_MAX_DIST = 128
ATTN_HEADS = 8

GLA_HEADS = 4
GLA_DK = 64
GLA_DV = 128
GLA_RANK = 16
GLA_TAU = 16.0
GLA_CHUNK = 64

SWA_HEADS = 8
SWA_KV_HEADS = 2
SWA_DH = 64
SWA_WINDOW = 128

MOBA_HEADS = 8
MOBA_KV_HEADS = 2
MOBA_DH = 128
MOBA_BLOCK = 256
MOBA_TOPK = 3
MOBA_GROUP = MOBA_HEADS // MOBA_KV_HEADS
MOBA_NBLK = SEQ // MOBA_BLOCK
MOBA_VT_ROWS = MOBA_DH + 16
LOG2E = math.log2(math.e)
SWA_VT_ROWS = SWA_KV_HEADS * SWA_DH + 16

EPS = 1e-6
LANES = 128
NEG_BIG = -1e30
VMEM_LIMIT = 48 * 1024 * 1024
MOBA_VMEM_LIMIT = 56 * 1024 * 1024

A_WIDTH = GLA_HEADS * GLA_DV
B_WIDTH = SWA_HEADS * SWA_DH
C_WIDTH = MOBA_HEADS * MOBA_DH
GLA_QK = GLA_HEADS * GLA_DK

ROW_TILE = 1024
GLA_TILE = 512
SWA_TILE = 512

F32 = jnp.float32
BF16 = jnp.bfloat16


def _dot(a, b):
    return jnp.dot(a, b, preferred_element_type=F32)


def _dot_nt(a, b):
    return lax.dot_general(a, b, (((1,), (1,)), ((), ())), preferred_element_type=F32)


def _dot_tn(a, b):
    return lax.dot_general(a, b, (((0,), (0,)), ((), ())), preferred_element_type=F32)


def _silu(x):
    return x * (1.0 / (1.0 + jnp.exp(-x)))


def _t5_thresholds():
    d = np.arange(REL_MAX_DIST + 1)
    nf = np.maximum(d, 1).astype(np.float32)
    large = REL_MAX_EXACT + (np.log(nf / np.float32(REL_MAX_EXACT))
                             / np.float32(math.log(REL_MAX_DIST / REL_MAX_EXACT))
                             * np.float32(REL_BUCKETS - REL_MAX_EXACT)).astype(np.int32)
    bucket = np.where(d < REL_MAX_EXACT, d, np.minimum(large, REL_BUCKETS - 1))
    assert np.all(np.diff(bucket) >= 0) and bucket[-1] == REL_BUCKETS - 1
    return [int(np.argmax(bucket >= b)) for b in range(REL_BUCKETS)]


_T5_THRESH = _t5_thresholds()


def _t5_bias(dist, rb_ref, head):
    out = jnp.full(dist.shape, rb_ref[0, head], F32)
    for b in range(1, REL_BUCKETS):
        out = jnp.where(dist >= _T5_THRESH[b], rb_ref[b, head], out)
    return out


def _layer_io_kernel(*refs, n_acc, n_proj, write_h, scales):
    h_ref = refs[0]
    pos = 1
    acc_pairs = []
    for _ in range(n_acc):
        acc_pairs.append((refs[pos], refs[pos + 1]))
        pos += 2
    gain_ref = refs[pos]
    pos += 1
    wp_refs = refs[pos:pos + n_proj]
    pos += n_proj
    out_refs = refs[pos:]

    h = h_ref[...]
    for a_ref, w_ref in acc_pairs:
        h = h + _dot(a_ref[...], w_ref[...])
    o = 0
    if write_h:
        out_refs[0][...] = h
        o = 1
    y = h * lax.rsqrt(jnp.mean(h * h, axis=-1, keepdims=True) + EPS) * gain_ref[...]
    if n_proj == 0:
        out_refs[o][...] = y
        return
    yb = y.astype(BF16)
    for k in range(n_proj):
        r = _dot(yb, wp_refs[k][...])
        if scales[k] != 1.0:
            r = r * scales[k]
        out_refs[o + k][...] = r.astype(out_refs[o + k].dtype)


def _layer_io(h, acc_pairs, gain, proj_ws, proj_dtypes, scales, write_h):
    n_rows = h.shape[0]
    grid = (n_rows // ROW_TILE,)
    row_spec = lambda n: pl.BlockSpec((ROW_TILE, n), lambda i: (i, 0))
    full_spec = lambda a: pl.BlockSpec(a.shape, lambda i: (0,) * a.ndim, pipeline_mode=pl.Buffered(1))

    args = [h]
    in_specs = [row_spec(D_MODEL)]
    for a, w in acc_pairs:
        args += [a, w]
        in_specs += [row_spec(a.shape[1]), full_spec(w)]
    args.append(gain)
    in_specs.append(full_spec(gain))
    for w in proj_ws:
        args.append(w)
        in_specs.append(full_spec(w))

    out_shape, out_specs = [], []
    if write_h:
        out_shape.append(jax.ShapeDtypeStruct((n_rows, D_MODEL), F32))
        out_specs.append(row_spec(D_MODEL))
    if proj_ws:
        for w, dt in zip(proj_ws, proj_dtypes):
            out_shape.append(jax.ShapeDtypeStruct((n_rows, w.shape[1]), dt))
            out_specs.append(row_spec(w.shape[1]))
    else:
        out_shape.append(jax.ShapeDtypeStruct((n_rows, D_MODEL), F32))
        out_specs.append(row_spec(D_MODEL))

    kern = functools.partial(_layer_io_kernel, n_acc=len(acc_pairs), n_proj=len(proj_ws),
                             write_h=write_h, scales=tuple(scales))
    return pl.pallas_call(
        kern, grid=grid, in_specs=in_specs, out_specs=out_specs, out_shape=out_shape,
        compiler_params=pltpu.CompilerParams(dimension_semantics=("arbitrary",),
                                             vmem_limit_bytes=VMEM_LIMIT),
        name="layer_io",
    )(*args)


def _gla_kernel(q_ref, k_ref, v_ref, gate_ref, down_ref, wup_ref, bup_ref, gain_ref, o_ref, st_ref):
    C = GLA_CHUNK

    @pl.when(pl.program_id(1) == 0)
    def _():
        st_ref[...] = jnp.zeros_like(st_ref)

    a = down_ref[...]
    a_hi = a.astype(BF16)
    a_lo = (a - a_hi.astype(F32)).astype(BF16)
    lane_a = lax.broadcasted_iota(jnp.int32, a.shape, 1) // GLA_RANK
    z = _dot(jnp.where(lane_a == 1, a_lo, a_hi), wup_ref[...]) + bup_ref[...]
    log_a = (jnp.minimum(z, 0.0) - jnp.log1p(jnp.exp(-jnp.abs(z)))) * (1.0 / GLA_TAU)

    r_i = lax.broadcasted_iota(jnp.int32, (C, C), 0)
    c_i = lax.broadcasted_iota(jnp.int32, (C, C), 1)
    tri = (c_i <= r_i).astype(BF16)
    lane_qk = lax.broadcasted_iota(jnp.int32, (C, GLA_QK), 1)
    head_masks = [(lane_qk // GLA_DK) == h for h in range(GLA_HEADS)]
    rs = lax.broadcasted_iota(jnp.int32, (GLA_HEADS * C, C), 0)
    cs = lax.broadcasted_iota(jnp.int32, (GLA_HEADS * C, C), 1)
    causal = (rs % C) >= cs
    st_r = lax.broadcasted_iota(jnp.int32, (A_WIDTH, GLA_QK), 0)
    st_c = lax.broadcasted_iota(jnp.int32, (A_WIDTH, GLA_QK), 1)
    same_head = (st_r // GLA_DV) == (st_c // GLA_DK)
    gain = gain_ref[...]

    for c in range(GLA_TILE // C):
        rows = slice(c * C, (c + 1) * C)
        g = log_a[rows]
        g_hi = g.astype(BF16)
        g_r = g - g_hi.astype(F32)
        g_mid = g_r.astype(BF16)
        g_lo = (g_r - g_mid.astype(F32)).astype(BF16)
        b = _dot(tri, g_hi) + _dot(tri, g_mid) + _dot(tri, g_lo)
        b_last = b[C - 1:C]
        q = q_ref[rows, :].astype(F32)
        k = k_ref[rows, :].astype(F32)
        v = v_ref[rows, :]
        q_e = q * jnp.exp(b)
        k_e = (k * jnp.exp(-b)).astype(BF16)
        k_l = (k * jnp.exp(b_last - b)).astype(BF16)
        decay = jnp.exp(b_last)

        q_stack = jnp.concatenate([jnp.where(m, q_e, 0.0) for m in head_masks], axis=0).astype(BF16)
        att = jnp.where(causal, _dot_nt(q_stack, k_e), 0.0).astype(BF16)
        o_full = _dot(att, v)
        o_intra = jnp.concatenate(
            [o_full[h * C:(h + 1) * C, h * GLA_DV:(h + 1) * GLA_DV] for h in range(GLA_HEADS)], axis=1)
        st = st_ref[...]
        o = o_intra + _dot_nt(q_e.astype(BF16), st.astype(BF16))
        kv_t = _dot_tn(v, k_l)
        st_ref[...] = st * decay + jnp.where(same_head, kv_t, 0.0)

        outs = []
        for h in range(GLA_HEADS):
            oh = o[:, h * GLA_DV:(h + 1) * GLA_DV]
            outs.append(oh * lax.rsqrt(jnp.mean(oh * oh, axis=-1, keepdims=True) + EPS) * gain)
        on = jnp.concatenate(outs, axis=1)
        o_ref[rows, :] = (on * _silu(gate_ref[rows, :].astype(F32))).astype(o_ref.dtype)


def _gla(aq, ak, av, agate, adown, w_up, b_up, gain):
    nt = SEQ // GLA_TILE
    row = lambda n: pl.BlockSpec((GLA_TILE, n), lambda b, s: (b * nt + s, 0))
    full = lambda a: pl.BlockSpec(a.shape, lambda b, s: (0,) * a.ndim)
    return pl.pallas_call(
        _gla_kernel, grid=(BATCH, nt),
        in_specs=[row(GLA_QK), row(GLA_QK), row(A_WIDTH), row(A_WIDTH), row(LANES),
                  full(w_up), full(b_up), full(gain)],
        out_specs=row(A_WIDTH),
        out_shape=jax.ShapeDtypeStruct((TOKENS, A_WIDTH), BF16),
        scratch_shapes=[pltpu.VMEM((A_WIDTH, GLA_QK), F32)],
        compiler_params=pltpu.CompilerParams(dimension_semantics=("arbitrary", "arbitrary"),
                                             vmem_limit_bytes=VMEM_LIMIT),
        name="gla",
    )(aq, ak, av, agate, adown, w_up, b_up, gain)


def _swa_kernel(sink_ref, rb_ref, q_ref, kp_ref, k_ref, vp_ref, v_ref, gate_ref, o_ref, bias_ref):
    W = SWA_WINDOW
    H = SWA_HEADS
    DH = SWA_DH
    G = SWA_HEADS // SWA_KV_HEADS
    t = pl.program_id(1)

    @pl.when(jnp.logical_and(pl.program_id(0) == 0, t == 0))
    def _():
        kj = lax.broadcasted_iota(jnp.int32, (2 * W, W), 0)
        qi = lax.broadcasted_iota(jnp.int32, (2 * W, W), 1)
        dist = qi + W - kj
        in_win = jnp.logical_and(dist >= 0, dist < W)
        for h in range(H):
            bias_ref[:, h * W:(h + 1) * W] = jnp.where(in_win, _t5_bias(dist, rb_ref, h) * LOG2E, -jnp.inf)

    kcat = jnp.concatenate([kp_ref[...], k_ref[...]], axis=0)
    vcat = jnp.concatenate([vp_ref[...], v_ref[...]], axis=0)
    vt = vcat.astype(F32).T.astype(BF16)
    ones_rows = jnp.ones((SWA_VT_ROWS - LANES, 2 * W), BF16)
    q_t = q_ref[...].astype(F32).T.astype(BF16)
    zeros = jnp.zeros((DH, W), BF16)
    colh = lax.broadcasted_iota(jnp.int32, (1, H * W), 1) // W
    sink = jnp.full((1, H * W), sink_ref[H - 1], F32)
    for h in range(H - 2, -1, -1):
        sink = jnp.where(colh == h, sink_ref[h], sink)
    sink = sink * LOG2E
    key_prev = lax.broadcasted_iota(jnp.int32, (2 * W, H * W), 0) < W

    for blk in range(SWA_TILE // W):
        tok = slice(blk * W, (blk + 1) * W)
        win = slice(blk * W, blk * W + 2 * W)
        pieces = []
        for h in range(H):
            piece = q_t[h * DH:(h + 1) * DH, tok]
            pieces.append(jnp.concatenate([piece, zeros] if h < G else [zeros, piece], axis=0))
        q_pad = jnp.concatenate(pieces, axis=1)
        s = _dot(kcat[win], q_pad) + bias_ref[...]
        if blk == 0:
            s = jnp.where(jnp.logical_and(t == 0, key_prev), -jnp.inf, s)
        m = jnp.maximum(jnp.max(s, axis=0, keepdims=True), sink)
        p = jnp.exp2(s - m).astype(BF16)
        pv = _dot(jnp.concatenate([vt[:, win], ones_rows], axis=0), p)
        inv = 1.0 / (pv[LANES:LANES + 1] + jnp.exp2(sink - m))
        o_t = jnp.concatenate(
            [pv[(h // G) * DH:(h // G + 1) * DH, h * W:(h + 1) * W] * inv[:, h * W:(h + 1) * W]
             for h in range(H)], axis=0)
        gate = gate_ref[tok, :].astype(F32)
        o_ref[tok, :] = (o_t.T * _silu(gate)).astype(o_ref.dtype)


def _swa(bq, bkv, bgate, rel_bias, sinks):
    nt = SEQ // SWA_TILE
    per = SWA_TILE // SWA_WINDOW
    row = lambda n: pl.BlockSpec((SWA_TILE, n), lambda b, t: (b * nt + t, 0))
    own = lambda c: pl.BlockSpec((SWA_TILE, LANES), lambda b, t: (b * nt + t, c))
    prev = lambda c: pl.BlockSpec((SWA_WINDOW, LANES),
                                  lambda b, t: (jnp.maximum((b * nt + t) * per - 1, 0), c))
    smem = pl.BlockSpec(memory_space=pltpu.SMEM)
    return pl.pallas_call(
        _swa_kernel, grid=(BATCH, nt),
        in_specs=[smem, smem, row(B_WIDTH), prev(0), own(0), prev(1), own(1), row(B_WIDTH)],
        out_specs=row(B_WIDTH),
        out_shape=jax.ShapeDtypeStruct((TOKENS, B_WIDTH), BF16),
        scratch_shapes=[pltpu.VMEM((2 * SWA_WINDOW, SWA_HEADS * SWA_WINDOW), F32)],
        compiler_params=pltpu.CompilerParams(dimension_semantics=("arbitrary", "arbitrary"),
                                             vmem_limit_bytes=VMEM_LIMIT),
        name="swa",
    )(sinks, rel_bias, bq, bkv, bkv, bkv, bkv, bgate)


def _moba_kernel(rb_ref, q_ref, k_ref, v_ref, gate_ref, o_ref,
                 kaug_ref, vt_ref, km_ref, bown_ref, bnear_ref, qat_ref, s_ref, m_ref, acc_ref):
    BLK = MOBA_BLOCK
    G = MOBA_GROUP
    KVH = MOBA_KV_HEADS
    R = G * BLK
    NB1 = MOBA_NBLK + 1
    DUMMY = MOBA_NBLK
    i = pl.program_id(1)

    @pl.when(i == 0)
    def _():
        lane = lax.broadcasted_iota(jnp.int32, (BLK, LANES), 1)
        ones_rows = jnp.ones((MOBA_VT_ROWS - MOBA_DH, BLK), BF16)

        def prep(j, carry):
            rows = pl.ds(pl.multiple_of(j * BLK, BLK), BLK)
            onehot = jnp.where(lane == j, 1.0, 0.0).astype(BF16)
            for kh in range(KVH):
                kb = k_ref[rows, kh * MOBA_DH:(kh + 1) * MOBA_DH]
                kaug_ref[kh * NB1 + j, :, 0:LANES] = kb
                kaug_ref[kh * NB1 + j, :, LANES:2 * LANES] = onehot
                km_ref[pl.ds(kh * MOBA_NBLK + j, 1), :] = jnp.mean(kb.astype(F32), axis=0, keepdims=True)
                vb = v_ref[rows, kh * MOBA_DH:(kh + 1) * MOBA_DH]
                vt_ref[kh * NB1 + j, 0:MOBA_DH, :] = vb.astype(F32).T.astype(BF16)
                vt_ref[kh * NB1 + j, MOBA_DH:, :] = ones_rows
            return carry

        lax.fori_loop(0, MOBA_NBLK, prep, 0)
        for kh in range(KVH):
            kaug_ref[kh * NB1 + DUMMY, :, 0:LANES] = jnp.zeros((BLK, LANES), BF16)
            kaug_ref[kh * NB1 + DUMMY, :, LANES:2 * LANES] = jnp.where(lane == DUMMY, 1.0, 0.0).astype(BF16)
            vt_ref[kh * NB1 + DUMMY] = jnp.zeros((MOBA_VT_ROWS, BLK), BF16)
            qat_ref[kh, MOBA_DH + MOBA_NBLK:, :] = jnp.full((2 * LANES - MOBA_DH - MOBA_NBLK, R), NEG_BIG, BF16)

    @pl.when(jnp.logical_and(pl.program_id(0) == 0, i == 0))
    def _():
        tk = lax.broadcasted_iota(jnp.int32, (BLK, BLK), 0)
        tq = lax.broadcasted_iota(jnp.int32, (BLK, BLK), 1)
        d_own = tq - tk
        for head in range(MOBA_HEADS):
            cols = slice(head * BLK, (head + 1) * BLK)
            bown_ref[:, cols] = jnp.where(d_own >= 0, _t5_bias(d_own, rb_ref, head) * LOG2E, -jnp.inf)
            bnear_ref[:, cols] = _t5_bias(d_own + BLK, rb_ref, head) * LOG2E

    blk = lax.broadcasted_iota(jnp.int32, (MOBA_NBLK, R), 0)
    blkf = blk.astype(F32)
    past = blk < i
    colh = lax.broadcasted_iota(jnp.int32, (1, R), 1) // BLK
    far_row = REL_BUCKETS - 1

    for kh in range(KVH):
        q = jnp.concatenate([q_ref[:, (kh * G + g) * MOBA_DH:(kh * G + g + 1) * MOBA_DH] for g in range(G)],
                            axis=0)
        q_t = q.astype(F32).T.astype(BF16)
        qat_ref[kh, 0:MOBA_DH, :] = q_t
        s_ref[kh] = _dot(kaug_ref[kh * NB1 + i, :, 0:MOBA_DH], q_t) + bown_ref[:, kh * R:(kh + 1) * R]

        km = km_ref[kh * MOBA_NBLK:(kh + 1) * MOBA_NBLK, :]
        km_hi = km.astype(BF16)
        km_lo = (km - km_hi.astype(F32)).astype(BF16)
        gate = _dot(km_hi, q_t) + _dot(km_lo, q_t)
        g_ = jnp.where(past, gate, -jnp.inf)
        selected = jnp.zeros((MOBA_NBLK, R), dtype=jnp.bool_)
        for _ in range(MOBA_TOPK):
            mx = jnp.max(g_, axis=0, keepdims=True)
            first = jnp.min(jnp.where(g_ == mx, blkf, 1e9), axis=0, keepdims=True)
            pick = jnp.logical_and(blkf == first, past)
            selected = jnp.logical_or(selected, pick)
            g_ = jnp.where(pick, -jnp.inf, g_)
        cfar = jnp.where(colh == 0, rb_ref[far_row, kh * G],
                         jnp.where(colh == 1, rb_ref[far_row, kh * G + 1],
                                   jnp.where(colh == 2, rb_ref[far_row, kh * G + 2],
                                             rb_ref[far_row, kh * G + 3])))
        sel_bias = jnp.where(selected, jnp.where(blk < i - 1, cfar * LOG2E, 0.0), NEG_BIG)
        qat_ref[kh, MOBA_DH:MOBA_DH + MOBA_NBLK, :] = sel_bias.astype(BF16)

    def logits(kh, j):
        return _dot(kaug_ref[kh * NB1 + j], qat_ref[kh])

    def consume(slot, kh, j, first=False):
        s = s_ref[slot * KVH + kh]
        if first:
            m_new = jnp.max(s, axis=0, keepdims=True)
        else:
            m_old = m_ref[kh]
            m_new = jnp.maximum(m_old, jnp.max(s, axis=0, keepdims=True))
        p = jnp.exp2(s - m_new).astype(BF16)
        pv = _dot(vt_ref[kh * NB1 + j], p)
        acc_ref[kh] = pv if first else jnp.exp2(m_old - m_new) * acc_ref[kh] + pv
        m_ref[kh] = m_new

    n_far = jnp.maximum(i - 1, 0)
    n_pairs = (n_far + 1) // 2

    def far_idx(n):
        return jnp.where(n < n_far, n, DUMMY)

    def far_step(n, slot):
        for kh in range(KVH):
            s_ref[(1 - slot) * KVH + kh] = logits(kh, far_idx(n + 1))
        for kh in range(KVH):
            consume(slot, kh, far_idx(n))

    near = jnp.where(i >= 1, i - 1, DUMMY)
    for kh in range(KVH):
        s_ref[KVH + kh] = logits(kh, near) + bnear_ref[:, kh * R:(kh + 1) * R]
    for kh in range(KVH):
        consume(0, kh, i, first=True)
    for kh in range(KVH):
        s_ref[kh] = logits(kh, far_idx(0))
    for kh in range(KVH):
        consume(1, kh, near)

    def far_quad(u, carry):
        for d in range(4):
            far_step(4 * u + d, d % 2)
        return carry

    lax.fori_loop(0, n_pairs // 2, far_quad, 0)

    @pl.when(n_pairs % 2 == 1)
    def _():
        for d in range(2):
            far_step(2 * (n_pairs - 1) + d, d % 2)

    for kh in range(KVH):
        acc = acc_ref[kh]
        o = (acc[0:MOBA_DH] * (1.0 / acc[MOBA_DH:MOBA_DH + 1])).T
        for g in range(G):
            cols = slice((kh * G + g) * MOBA_DH, (kh * G + g + 1) * MOBA_DH)
            gt = gate_ref[:, cols].astype(F32)
            o_ref[:, cols] = (o[g * BLK:(g + 1) * BLK] * _silu(gt)).astype(o_ref.dtype)


def _moba(cq, ck, cv, cgate, rel_bias):
    R = MOBA_GROUP * MOBA_BLOCK
    KVH = MOBA_KV_HEADS
    NB1 = MOBA_NBLK + 1
    qspec = pl.BlockSpec((MOBA_BLOCK, C_WIDTH), lambda b, i: (b * MOBA_NBLK + i, 0))
    kvspec = pl.BlockSpec((SEQ, KVH * MOBA_DH), lambda b, i: (b, 0), pipeline_mode=pl.Buffered(1))
    return pl.pallas_call(
        _moba_kernel, grid=(BATCH, MOBA_NBLK),
        in_specs=[pl.BlockSpec(memory_space=pltpu.SMEM), qspec, kvspec, kvspec, qspec],
        out_specs=qspec,
        out_shape=jax.ShapeDtypeStruct((TOKENS, C_WIDTH), BF16),
        scratch_shapes=[pltpu.VMEM((KVH * NB1, MOBA_BLOCK, 2 * LANES), BF16),
                        pltpu.VMEM((KVH * NB1, MOBA_VT_ROWS, MOBA_BLOCK), BF16),
                        pltpu.VMEM((KVH * MOBA_NBLK, MOBA_DH), F32),
                        pltpu.VMEM((MOBA_BLOCK, KVH * R), F32),
                        pltpu.VMEM((MOBA_BLOCK, KVH * R), F32),
                        pltpu.VMEM((KVH, 2 * LANES, R), BF16),
                        pltpu.VMEM((2 * KVH, MOBA_BLOCK, R), F32),
                        pltpu.VMEM((KVH, 1, R), F32),
                        pltpu.VMEM((KVH, MOBA_VT_ROWS, R), F32)],
        compiler_params=pltpu.CompilerParams(
            dimension_semantics=("arbitrary", "arbitrary"), vmem_limit_bytes=MOBA_VMEM_LIMIT),
        name="moba",
    )(rel_bias, cq, ck, cv, cgate)


def kernel(x, norm_gain, final_gain, rel_bias, w_in_even, gla_w_up, gla_b_up, gla_norm_gain, swa_sinks,
           w_out_even, w_in_odd, w_out_odd):
    assert x.shape == (BATCH, SEQ, D_MODEL)
    h0 = x.reshape(TOKENS, D_MODEL)

    we = w_in_even[0]
    splits = np.cumsum([0, GLA_QK, GLA_QK, A_WIDTH, GLA_RANK, A_WIDTH, B_WIDTH, SWA_KV_HEADS * SWA_DH,
                        SWA_KV_HEADS * SWA_DH, B_WIDTH])
    w_aq, w_ak, w_av, w_down, w_agate, w_bq = [we[:, splits[k]:splits[k + 1]] for k in range(6)]
    w_bkv = we[:, splits[6]:splits[8]]
    w_bgate = we[:, splits[8]:splits[9]]
    w_down = jnp.pad(jnp.concatenate([w_down] * 3, axis=1), ((0, 0), (0, LANES - 3 * GLA_RANK)))
    proj_ws = [w.astype(BF16) for w in (w_aq, w_ak, w_av, w_down, w_agate, w_bq, w_bkv, w_bgate)]
    proj_dt = [BF16, BF16, BF16, F32, BF16, BF16, BF16, BF16]
    scales = [GLA_DK ** -0.5, 1.0, 1.0, 1.0, 1.0, SWA_DH ** -0.5 * LOG2E, 1.0, 1.0]
    aq, ak, av, adown, agate, bq, bkv, bgate = _layer_io(
        h0, [], norm_gain[0:1], proj_ws, proj_dt, scales, write_h=False)

    w_up_hi = gla_w_up[0].astype(BF16)
    w_up_lo = (gla_w_up[0] - w_up_hi.astype(F32)).astype(BF16)
    w_up = jnp.pad(jnp.concatenate([w_up_hi, w_up_hi, w_up_lo], axis=0), ((0, LANES - 3 * GLA_RANK), (0, 0)))
    oa = _gla(aq, ak, av, agate, adown, w_up, gla_b_up[0:1], gla_norm_gain[0:1])
    ob = _swa(bq, bkv, bgate, rel_bias, swa_sinks[0])

    wo = w_out_even[0].astype(BF16)
    wi = w_in_odd[0]
    osp = np.cumsum([0, C_WIDTH, MOBA_KV_HEADS * MOBA_DH, MOBA_KV_HEADS * MOBA_DH, C_WIDTH])
    odd_ws = [wi[:, osp[k]:osp[k + 1]].astype(BF16) for k in range(4)]
    h1, cq, ck, cv, cgate = _layer_io(
        h0, [(oa, wo[:A_WIDTH]), (ob, wo[A_WIDTH:])], norm_gain[1:2], odd_ws, [BF16] * 4,
        [MOBA_DH ** -0.5 * LOG2E, 1.0, 1.0, 1.0], write_h=True)

    oc = _moba(cq, ck, cv, cgate, rel_bias)

    (out,) = _layer_io(h1, [(oc, w_out_odd[0].astype(BF16))], final_gain.reshape(1, D_MODEL), [], [], [],
                       write_h=False)
    return out.reshape(BATCH, SEQ, D_MODEL)
```

```python
import functools
import math

import numpy as np
import jax
import jax.numpy as jnp
from jax import lax
from jax.experimental import pallas as pl
from jax.experimental.pallas import tpu as pltpu

D_MODEL = 1024
BATCH = 2
SEQ = 8192
TOKENS = BATCH * SEQ

REL_BUCKETS = 32
REL_MAX_EXACT = 16
REL_MAX_DIST = 128
ATTN_HEADS = 8

GLA_HEADS = 4
GLA_DK = 64
GLA_DV = 128
GLA_RANK = 16
GLA_TAU = 16.0
GLA_CHUNK = 64

SWA_HEADS = 8
SWA_KV_HEADS = 2
SWA_DH = 64
SWA_WINDOW = 128

MOBA_HEADS = 8
MOBA_KV_HEADS = 2
MOBA_DH = 128
MOBA_BLOCK = 256
MOBA_TOPK = 3
MOBA_GROUP = MOBA_HEADS // MOBA_KV_HEADS
MOBA_NBLK = SEQ // MOBA_BLOCK
MOBA_VT_ROWS = MOBA_DH + 16
LOG2E = math.log2(math.e)
SWA_VT_ROWS = SWA_KV_HEADS * SWA_DH + 16

EPS = 1e-6
LANES = 128
NEG_BIG = -1e30
VMEM_LIMIT = 48 * 1024 * 1024
MOBA_VMEM_LIMIT = 56 * 1024 * 1024

A_WIDTH = GLA_HEADS * GLA_DV
B_WIDTH = SWA_HEADS * SWA_DH
C_WIDTH = MOBA_HEADS * MOBA_DH
GLA_QK = GLA_HEADS * GLA_DK

ROW_TILE = 1024
GLA_TILE = 1024
SWA_TILE = 1024

F32 = jnp.float32
BF16 = jnp.bfloat16


def _dot(a, b):
    return jnp.dot(a, b, preferred_element_type=F32)


def _dot_nt(a, b):
    return lax.dot_general(a, b, (((1,), (1,)), ((), ())), preferred_element_type=F32)


def _dot_tn(a, b):
    return lax.dot_general(a, b, (((0,), (0,)), ((), ())), preferred_element_type=F32)


def _silu(x):
    return x * (1.0 / (1.0 + jnp.exp(-x)))


def _t5_thresholds():
    d = np.arange(REL_MAX_DIST + 1)
    nf = np.maximum(d, 1).astype(np.float32)
    large = REL_MAX_EXACT + (np.log(nf / np.float32(REL_MAX_EXACT))
                             / np.float32(math.log(REL_MAX_DIST / REL_MAX_EXACT))
                             * np.float32(REL_BUCKETS - REL_MAX_EXACT)).astype(np.int32)
    bucket = np.where(d < REL_MAX_EXACT, d, np.minimum(large, REL_BUCKETS - 1))
    assert np.all(np.diff(bucket) >= 0) and bucket[-1] == REL_BUCKETS - 1
    return [int(np.argmax(bucket >= b)) for b in range(REL_BUCKETS)]


_T5_THRESH = _t5_thresholds()


def _t5_bias(dist, rb_ref, head):
    out = jnp.full(dist.shape, rb_ref[0, head], F32)
    for b in range(1, REL_BUCKETS):
        out = jnp.where(dist >= _T5_THRESH[b], rb_ref[b, head], out)
    return out


def _layer_io_kernel(*refs, n_acc, n_proj, write_h, scales):
    h_ref = refs[0]
    pos = 1
    acc_pairs = []
    for _ in range(n_acc):
        acc_pairs.append((refs[pos], refs[pos + 1]))
        pos += 2
    gain_ref = refs[pos]
    pos += 1
    wp_refs = refs[pos:pos + n_proj]
    pos += n_proj
    out_refs = refs[pos:]

    h = h_ref[...]
    for a_ref, w_ref in acc_pairs:
        h = h + _dot(a_ref[...], w_ref[...])
    o = 0
    if write_h:
        out_refs[0][...] = h
        o = 1
    y = h * lax.rsqrt(jnp.mean(h * h, axis=-1, keepdims=True) + EPS) * gain_ref[...]
    if n_proj == 0:
        out_refs[o][...] = y
        return
    yb = y.astype(BF16)
    for k in range(n_proj):
        r = _dot(yb, wp_refs[k][...])
        if scales[k] != 1.0:
            r = r * scales[k]
        out_refs[o + k][...] = r.astype(out_refs[o + k].dtype)


def _layer_io(h, acc_pairs, gain, proj_ws, proj_dtypes, scales, write_h):
    n_rows = h.shape[0]
    grid = (n_rows // ROW_TILE,)
    row_spec = lambda n: pl.BlockSpec((ROW_TILE, n), lambda i: (i, 0))
    full_spec = lambda a: pl.BlockSpec(a.shape, lambda i: (0,) * a.ndim, pipeline_mode=pl.Buffered(1))

    args = [h]
    in_specs = [row_spec(D_MODEL)]
    for a, w in acc_pairs:
        args += [a, w]
        in_specs += [row_spec(a.shape[1]), full_spec(w)]
    args.append(gain)
    in_specs.append(full_spec(gain))
    for w in proj_ws:
        args.append(w)
        in_specs.append(full_spec(w))

    out_shape, out_specs = [], []
    if write_h:
        out_shape.append(jax.ShapeDtypeStruct((n_rows, D_MODEL), F32))
        out_specs.append(row_spec(D_MODEL))
    if proj_ws:
        for w, dt in zip(proj_ws, proj_dtypes):
            out_shape.append(jax.ShapeDtypeStruct((n_rows, w.shape[1]), dt))
            out_specs.append(row_spec(w.shape[1]))
    else:
        out_shape.append(jax.ShapeDtypeStruct((n_rows, D_MODEL), F32))
        out_specs.append(row_spec(D_MODEL))

    kern = functools.partial(_layer_io_kernel, n_acc=len(acc_pairs), n_proj=len(proj_ws),
                             write_h=write_h, scales=tuple(scales))
    return pl.pallas_call(
        kern, grid=grid, in_specs=in_specs, out_specs=out_specs, out_shape=out_shape,
        compiler_params=pltpu.CompilerParams(dimension_semantics=("arbitrary",),
                                             vmem_limit_bytes=VMEM_LIMIT),
        name="layer_io",
    )(*args)


def _gla_kernel(q_ref, k_ref, v_ref, gate_ref, down_ref, wup_ref, bup_ref, gain_ref, o_ref, st_ref):
    C = GLA_CHUNK

    @pl.when(pl.program_id(1) == 0)
    def _():
        st_ref[...] = jnp.zeros_like(st_ref)

    a = down_ref[...]
    a_hi = a.astype(BF16)
    a_lo = (a - a_hi.astype(F32)).astype(BF16)
    lane_a = lax.broadcasted_iota(jnp.int32, a.shape, 1) // GLA_RANK
    z = _dot(jnp.where(lane_a == 1, a_lo, a_hi), wup_ref[...]) + bup_ref[...]
    log_a = (jnp.minimum(z, 0.0) - jnp.log1p(jnp.exp(-jnp.abs(z)))) * (1.0 / GLA_TAU)

    r_i = lax.broadcasted_iota(jnp.int32, (C, C), 0)
    c_i = lax.broadcasted_iota(jnp.int32, (C, C), 1)
    tri = (c_i <= r_i).astype(BF16)
    lane_qk = lax.broadcasted_iota(jnp.int32, (C, GLA_QK), 1)
    head_masks = [(lane_qk // GLA_DK) == h for h in range(GLA_HEADS)]
    rs = lax.broadcasted_iota(jnp.int32, (GLA_HEADS * C, C), 0)
    cs = lax.broadcasted_iota(jnp.int32, (GLA_HEADS * C, C), 1)
    causal = (rs % C) >= cs
    st_r = lax.broadcasted_iota(jnp.int32, (A_WIDTH, GLA_QK), 0)
    st_c = lax.broadcasted_iota(jnp.int32, (A_WIDTH, GLA_QK), 1)
    same_head = (st_r // GLA_DV) == (st_c // GLA_DK)
    gain = gain_ref[...]

    for c in range(GLA_TILE // C):
        rows = slice(c * C, (c + 1) * C)
        g = log_a[rows]
        g_hi = g.astype(BF16)
        g_r = g - g_hi.astype(F32)
        g_mid = g_r.astype(BF16)
        g_lo = (g_r - g_mid.astype(F32)).astype(BF16)
        b = _dot(tri, g_hi) + _dot(tri, g_mid) + _dot(tri, g_lo)
        b_last = b[C - 1:C]
        q = q_ref[rows, :].astype(F32)
        k = k_ref[rows, :].astype(F32)
        v = v_ref[rows, :]
        q_e = q * jnp.exp(b)
        k_e = (k * jnp.exp(-b)).astype(BF16)
        k_l = (k * jnp.exp(b_last - b)).astype(BF16)
        decay = jnp.exp(b_last)

        q_stack = jnp.concatenate([jnp.where(m, q_e, 0.0) for m in head_masks], axis=0).astype(BF16)
        att = jnp.where(causal, _dot_nt(q_stack, k_e), 0.0).astype(BF16)
        o_full = _dot(att, v)
        o_intra = jnp.concatenate(
            [o_full[h * C:(h + 1) * C, h * GLA_DV:(h + 1) * GLA_DV] for h in range(GLA_HEADS)], axis=1)
        st = st_ref[...]
        o = o_intra + _dot_nt(q_e.astype(BF16), st.astype(BF16))
        kv_t = _dot_tn(v, k_l)
        st_ref[...] = st * decay + jnp.where(same_head, kv_t, 0.0)

        outs = []
        for h in range(GLA_HEADS):
            oh = o[:, h * GLA_DV:(h + 1) * GLA_DV]
            outs.append(oh * lax.rsqrt(jnp.mean(oh * oh, axis=-1, keepdims=True) + EPS) * gain)
        on = jnp.concatenate(outs, axis=1)
        o_ref[rows, :] = (on * _silu(gate_ref[rows, :].astype(F32))).astype(o_ref.dtype)


def _gla(aq, ak, av, agate, adown, w_up, b_up, gain):
    nt = SEQ // GLA_TILE
    row = lambda n: pl.BlockSpec((GLA_TILE, n), lambda b, s: (b * nt + s, 0))
    full = lambda a: pl.BlockSpec(a.shape, lambda b, s: (0,) * a.ndim)
    return pl.pallas_call(
        _gla_kernel, grid=(BATCH, nt),
        in_specs=[row(GLA_QK), row(GLA_QK), row(A_WIDTH), row(A_WIDTH), row(LANES),
                  full(w_up), full(b_up), full(gain)],
        out_specs=row(A_WIDTH),
        out_shape=jax.ShapeDtypeStruct((TOKENS, A_WIDTH), BF16),
        scratch_shapes=[pltpu.VMEM((A_WIDTH, GLA_QK), F32)],
        compiler_params=pltpu.CompilerParams(dimension_semantics=("arbitrary", "arbitrary"),
                                             vmem_limit_bytes=VMEM_LIMIT),
        name="gla",
    )(aq, ak, av, agate, adown, w_up, b_up, gain)


def _swa_kernel(sink_ref, rb_ref, q_ref, kp_ref, k_ref, vp_ref, v_ref, gate_ref, o_ref, bias_ref):
    W = SWA_WINDOW
    H = SWA_HEADS
    DH = SWA_DH
    G = SWA_HEADS // SWA_KV_HEADS
    t = pl.program_id(1)

    @pl.when(jnp.logical_and(pl.program_id(0) == 0, t == 0))
    def _():
        kj = lax.broadcasted_iota(jnp.int32, (2 * W, W), 0)
        qi = lax.broadcasted_iota(jnp.int32, (2 * W, W), 1)
        dist = qi + W - kj
        in_win = jnp.logical_and(dist >= 0, dist < W)
        for h in range(H):
            bias_ref[:, h * W:(h + 1) * W] = jnp.where(in_win, _t5_bias(dist, rb_ref, h) * LOG2E, -jnp.inf)

    kcat = jnp.concatenate([kp_ref[...], k_ref[...]], axis=0)
    vcat = jnp.concatenate([vp_ref[...], v_ref[...]], axis=0)
    vt = vcat.astype(F32).T.astype(BF16)
    ones_rows = jnp.ones((SWA_VT_ROWS - LANES, 2 * W), BF16)
    q_t = q_ref[...].astype(F32).T.astype(BF16)
    zeros = jnp.zeros((DH, W), BF16)
    colh = lax.broadcasted_iota(jnp.int32, (1, H * W), 1) // W
    sink = jnp.full((1, H * W), sink_ref[H - 1], F32)
    for h in range(H - 2, -1, -1):
        sink = jnp.where(colh == h, sink_ref[h], sink)
    sink = sink * LOG2E
    key_prev = lax.broadcasted_iota(jnp.int32, (2 * W, H * W), 0) < W

    for blk in range(SWA_TILE // W):
        tok = slice(blk * W, (blk + 1) * W)
        win = slice(blk * W, blk * W + 2 * W)
        pieces = []
        for h in range(H):
            piece = q_t[h * DH:(h + 1) * DH, tok]
            pieces.append(jnp.concatenate([piece, zeros] if h < G else [zeros, piece], axis=0))
        q_pad = jnp.concatenate(pieces, axis=1)
        s = _dot(kcat[win], q_pad) + bias_ref[...]
        if blk == 0:
            s = jnp.where(jnp.logical_and(t == 0, key_prev), -jnp.inf, s)
        m = jnp.maximum(jnp.max(s, axis=0, keepdims=True), sink)
        p = jnp.exp2(s - m).astype(BF16)
        pv = _dot(jnp.concatenate([vt[:, win], ones_rows], axis=0), p)
        inv = 1.0 / (pv[LANES:LANES + 1] + jnp.exp2(sink - m))
        o_t = jnp.concatenate(
            [pv[(h // G) * DH:(h // G + 1) * DH, h * W:(h + 1) * W] * inv[:, h * W:(h + 1) * W]
             for h in range(H)], axis=0)
        gate = gate_ref[tok, :].astype(F32)
        o_ref[tok, :] = (o_t.T * _silu(gate)).astype(o_ref.dtype)


def _swa(bq, bkv, bgate, rel_bias, sinks):
    nt = SEQ // SWA_TILE
    per = SWA_TILE // SWA_WINDOW
    row = lambda n: pl.BlockSpec((SWA_TILE, n), lambda b, t: (b * nt + t, 0))
    own = lambda c: pl.BlockSpec((SWA_TILE, LANES), lambda b, t: (b * nt + t, c))
    prev = lambda c: pl.BlockSpec((SWA_WINDOW, LANES),
                                  lambda b, t: (jnp.maximum((b * nt + t) * per - 1, 0), c))
    smem = pl.BlockSpec(memory_space=pltpu.SMEM)
    return pl.pallas_call(
        _swa_kernel, grid=(BATCH, nt),
        in_specs=[smem, smem, row(B_WIDTH), prev(0), own(0), prev(1), own(1), row(B_WIDTH)],
        out_specs=row(B_WIDTH),
        out_shape=jax.ShapeDtypeStruct((TOKENS, B_WIDTH), BF16),
        scratch_shapes=[pltpu.VMEM((2 * SWA_WINDOW, SWA_HEADS * SWA_WINDOW), F32)],
        compiler_params=pltpu.CompilerParams(dimension_semantics=("arbitrary", "arbitrary"),
                                             vmem_limit_bytes=VMEM_LIMIT),
        name="swa",
    )(sinks, rel_bias, bq, bkv, bkv, bkv, bkv, bgate)


def _moba_kernel(rb_ref, q_ref, k_ref, v_ref, gate_ref, o_ref,
                 kaug_ref, vt_ref, km_ref, bown_ref, bnear_ref, qat_ref, s_ref, smax_ref, m_ref, acc_ref):
    BLK = MOBA_BLOCK
    G = MOBA_GROUP
    KVH = MOBA_KV_HEADS
    R = G * BLK
    NB1 = MOBA_NBLK + 1
    DUMMY = MOBA_NBLK
    i = pl.program_id(1)

    @pl.when(i == 0)
    def _():
        lane = lax.broadcasted_iota(jnp.int32, (BLK, LANES), 1)
        ones_rows = jnp.ones((MOBA_VT_ROWS - MOBA_DH, BLK), BF16)

        def prep(j, carry):
            rows = pl.ds(pl.multiple_of(j * BLK, BLK), BLK)
            onehot = jnp.where(lane == j, 1.0, 0.0).astype(BF16)
            for kh in range(KVH):
                kb = k_ref[rows, kh * MOBA_DH:(kh + 1) * MOBA_DH]
                kaug_ref[kh * NB1 + j, :, 0:LANES] = kb
                kaug_ref[kh * NB1 + j, :, LANES:2 * LANES] = onehot
                km_ref[pl.ds(kh * MOBA_NBLK + j, 1), :] = jnp.mean(kb.astype(F32), axis=0, keepdims=True)
                vb = v_ref[rows, kh * MOBA_DH:(kh + 1) * MOBA_DH]
                vt_ref[kh * NB1 + j, 0:MOBA_DH, :] = vb.astype(F32).T.astype(BF16)
                vt_ref[kh * NB1 + j, MOBA_DH:, :] = ones_rows
            return carry

        lax.fori_loop(0, MOBA_NBLK, prep, 0)
        for kh in range(KVH):
            kaug_ref[kh * NB1 + DUMMY, :, 0:LANES] = jnp.zeros((BLK, LANES), BF16)
            kaug_ref[kh * NB1 + DUMMY, :, LANES:2 * LANES] = jnp.where(lane == DUMMY, 1.0, 0.0).astype(BF16)
            vt_ref[kh * NB1 + DUMMY] = jnp.zeros((MOBA_VT_ROWS, BLK), BF16)
            qat_ref[kh, MOBA_DH + MOBA_NBLK:, :] = jnp.full((2 * LANES - MOBA_DH - MOBA_NBLK, R), NEG_BIG, BF16)

    @pl.when(jnp.logical_and(pl.program_id(0) == 0, i == 0))
    def _():
        tk = lax.broadcasted_iota(jnp.int32, (BLK, BLK), 0)
        tq = lax.broadcasted_iota(jnp.int32, (BLK, BLK), 1)
        d_own = tq - tk
        for head in range(MOBA_HEADS):
            cols = slice(head * BLK, (head + 1) * BLK)
            bown_ref[:, cols] = jnp.where(d_own >= 0, _t5_bias(d_own, rb_ref, head) * LOG2E, -jnp.inf)
            bnear_ref[:, cols] = _t5_bias(d_own + BLK, rb_ref, head) * LOG2E

    blk = lax.broadcasted_iota(jnp.int32, (MOBA_NBLK, R), 0)
    blkf = blk.astype(F32)
    past = blk < i
    colh = lax.broadcasted_iota(jnp.int32, (1, R), 1) // BLK
    far_row = REL_BUCKETS - 1

    def produce(slot, kh, s):
        s_ref[slot * KVH + kh] = s
        smax_ref[slot * KVH + kh] = jnp.max(s, axis=0, keepdims=True)

    for kh in range(KVH):
        q = jnp.concatenate([q_ref[:, (kh * G + g) * MOBA_DH:(kh * G + g + 1) * MOBA_DH] for g in range(G)],
                            axis=0)
        q_t = q.astype(F32).T.astype(BF16)
        qat_ref[kh, 0:MOBA_DH, :] = q_t
        produce(0, kh, _dot(kaug_ref[kh * NB1 + i, :, 0:MOBA_DH], q_t) + bown_ref[:, kh * R:(kh + 1) * R])

        km = km_ref[kh * MOBA_NBLK:(kh + 1) * MOBA_NBLK, :]
        km_hi = km.astype(BF16)
        km_lo = (km - km_hi.astype(F32)).astype(BF16)
        gate = _dot(km_hi, q_t) + _dot(km_lo, q_t)
        g_ = jnp.where(past, gate, -jnp.inf)
        selected = jnp.zeros((MOBA_NBLK, R), dtype=jnp.bool_)
        for _ in range(MOBA_TOPK):
            mx = jnp.max(g_, axis=0, keepdims=True)
            first = jnp.min(jnp.where(g_ == mx, blkf, 1e9), axis=0, keepdims=True)
            pick = jnp.logical_and(blkf == first, past)
            selected = jnp.logical_or(selected, pick)
            g_ = jnp.where(pick, -jnp.inf, g_)
        cfar = jnp.where(colh == 0, rb_ref[far_row, kh * G],
                         jnp.where(colh == 1, rb_ref[far_row, kh * G + 1],
                                   jnp.where(colh == 2, rb_ref[far_row, kh * G + 2],
                                             rb_ref[far_row, kh * G + 3])))
        sel_bias = jnp.where(selected, jnp.where(blk < i - 1, cfar * LOG2E, 0.0), NEG_BIG)
        qat_ref[kh, MOBA_DH:MOBA_DH + MOBA_NBLK, :] = sel_bias.astype(BF16)

    def logits(kh, j):
        return _dot(kaug_ref[kh * NB1 + j], qat_ref[kh])

    def consume(slot, kh, j, first=False):
        m_new = smax_ref[slot * KVH + kh]
        if not first:
            m_old = m_ref[kh]
            m_new = jnp.maximum(m_old, m_new)
        p = jnp.exp2(s_ref[slot * KVH + kh] - m_new).astype(BF16)
        pv = _dot(vt_ref[kh * NB1 + j], p)
        acc_ref[kh] = pv if first else jnp.exp2(m_old - m_new) * acc_ref[kh] + pv
        m_ref[kh] = m_new

    n_far = jnp.maximum(i - 1, 0)
    n_pairs = (n_far + 1) // 2

    def far_idx(n):
        return jnp.where(n < n_far, n, DUMMY)

    def far_step(n, slot):
        for kh in range(KVH):
            produce(1 - slot, kh, logits(kh, far_idx(n + 1)))
        for kh in range(KVH):
            consume(slot, kh, far_idx(n))

    near = jnp.where(i >= 1, i - 1, DUMMY)
    for kh in range(KVH):
        produce(1, kh, logits(kh, near) + bnear_ref[:, kh * R:(kh + 1) * R])
    for kh in range(KVH):
        consume(0, kh, i, first=True)
    for kh in range(KVH):
        produce(0, kh, logits(kh, far_idx(0)))
    for kh in range(KVH):
        consume(1, kh, near)

    def far_quad(u, carry):
        for d in range(4):
            far_step(4 * u + d, d % 2)
        return carry

    lax.fori_loop(0, n_pairs // 2, far_quad, 0)

    @pl.when(n_pairs % 2 == 1)
    def _():
        for d in range(2):
            far_step(2 * (n_pairs - 1) + d, d % 2)

    for kh in range(KVH):
        acc = acc_ref[kh]
        o = (acc[0:MOBA_DH] * (1.0 / acc[MOBA_DH:MOBA_DH + 1])).T
        for g in range(G):
            cols = slice((kh * G + g) * MOBA_DH, (kh * G + g + 1) * MOBA_DH)
            gt = gate_ref[:, cols].astype(F32)
            o_ref[:, cols] = (o[g * BLK:(g + 1) * BLK] * _silu(gt)).astype(o_ref.dtype)


def _moba(cq, ck, cv, cgate, rel_bias):
    R = MOBA_GROUP * MOBA_BLOCK
    KVH = MOBA_KV_HEADS
    NB1 = MOBA_NBLK + 1
    qspec = pl.BlockSpec((MOBA_BLOCK, C_WIDTH), lambda b, i: (b * MOBA_NBLK + i, 0))
    kvspec = pl.BlockSpec((SEQ, KVH * MOBA_DH), lambda b, i: (b, 0), pipeline_mode=pl.Buffered(1))
    return pl.pallas_call(
        _moba_kernel, grid=(BATCH, MOBA_NBLK),
        in_specs=[pl.BlockSpec(memory_space=pltpu.SMEM), qspec, kvspec, kvspec, qspec],
        out_specs=qspec,
        out_shape=jax.ShapeDtypeStruct((TOKENS, C_WIDTH), BF16),
        scratch_shapes=[pltpu.VMEM((KVH * NB1, MOBA_BLOCK, 2 * LANES), BF16),
                        pltpu.VMEM((KVH * NB1, MOBA_VT_ROWS, MOBA_BLOCK), BF16),
                        pltpu.VMEM((KVH * MOBA_NBLK, MOBA_DH), F32),
                        pltpu.VMEM((MOBA_BLOCK, KVH * R), F32),
                        pltpu.VMEM((MOBA_BLOCK, KVH * R), F32),
                        pltpu.VMEM((KVH, 2 * LANES, R), BF16),
                        pltpu.VMEM((2 * KVH, MOBA_BLOCK, R), F32),
                        pltpu.VMEM((2 * KVH, 1, R), F32),
                        pltpu.VMEM((KVH, 1, R), F32),
                        pltpu.VMEM((KVH, MOBA_VT_ROWS, R), F32)],
        compiler_params=pltpu.CompilerParams(
            dimension_semantics=("arbitrary", "arbitrary"), vmem_limit_bytes=MOBA_VMEM_LIMIT),
        name="moba",
    )(rel_bias, cq, ck, cv, cgate)


def kernel(x, norm_gain, final_gain, rel_bias, w_in_even, gla_w_up, gla_b_up, gla_norm_gain, swa_sinks,
           w_out_even, w_in_odd, w_out_odd):
    assert x.shape == (BATCH, SEQ, D_MODEL)
    h0 = x.reshape(TOKENS, D_MODEL)

    we = w_in_even[0]
    splits = np.cumsum([0, GLA_QK, GLA_QK, A_WIDTH, GLA_RANK, A_WIDTH, B_WIDTH, SWA_KV_HEADS * SWA_DH,
                        SWA_KV_HEADS * SWA_DH, B_WIDTH])
    w_aq, w_ak, w_av, w_down, w_agate, w_bq = [we[:, splits[k]:splits[k + 1]] for k in range(6)]
    w_bkv = we[:, splits[6]:splits[8]]
    w_bgate = we[:, splits[8]:splits[9]]
    w_down = jnp.pad(jnp.concatenate([w_down] * 3, axis=1), ((0, 0), (0, LANES - 3 * GLA_RANK)))
    proj_ws = [w.astype(BF16) for w in (w_aq, w_ak, w_av, w_down, w_agate, w_bq, w_bkv, w_bgate)]
    proj_dt = [BF16, BF16, BF16, F32, BF16, BF16, BF16, BF16]
    scales = [GLA_DK ** -0.5, 1.0, 1.0, 1.0, 1.0, SWA_DH ** -0.5 * LOG2E, 1.0, 1.0]
    aq, ak, av, adown, agate, bq, bkv, bgate = _layer_io(
        h0, [], norm_gain[0:1], proj_ws, proj_dt, scales, write_h=False)

    w_up_hi = gla_w_up[0].astype(BF16)
    w_up_lo = (gla_w_up[0] - w_up_hi.astype(F32)).astype(BF16)
    w_up = jnp.pad(jnp.concatenate([w_up_hi, w_up_hi, w_up_lo], axis=0), ((0, LANES - 3 * GLA_RANK), (0, 0)))
    oa = _gla(aq, ak, av, agate, adown, w_up, gla_b_up[0:1], gla_norm_gain[0:1])
    ob = _swa(bq, bkv, bgate, rel_bias, swa_sinks[0])

    wo = w_out_even[0].astype(BF16)
    wi = w_in_odd[0]
    osp = np.cumsum([0, C_WIDTH, MOBA_KV_HEADS * MOBA_DH, MOBA_KV_HEADS * MOBA_DH, C_WIDTH])
    odd_ws = [wi[:, osp[k]:osp[k + 1]].astype(BF16) for k in range(4)]
    h1, cq, ck, cv, cgate = _layer_io(
        h0, [(oa, wo[:A_WIDTH]), (ob, wo[A_WIDTH:])], norm_gain[1:2], odd_ws, [BF16] * 4,
        [MOBA_DH ** -0.5 * LOG2E, 1.0, 1.0, 1.0], write_h=True)

    oc = _moba(cq, ck, cv, cgate, rel_bias)

    (out,) = _layer_io(h1, [(oc, w_out_odd[0].astype(BF16))], final_gain.reshape(1, D_MODEL), [], [], [],
                       write_h=False)
    return out.reshape(BATCH, SEQ, D_MODEL)
```

```python
import functools
import math

import numpy as np
import jax
import jax.numpy as jnp
from jax import lax
from jax.experimental import pallas as pl
from jax.experimental.pallas import tpu as pltpu

D_MODEL = 1024
BATCH = 2
SEQ = 8192
TOKENS = BATCH * SEQ

REL_BUCKETS = 32
REL_MAX_EXACT = 16
REL_MAX_DIST = 128
ATTN_HEADS = 8

GLA_HEADS = 4
GLA_DK = 64
GLA_DV = 128
GLA_RANK = 16
GLA_TAU = 16.0
GLA_CHUNK = 64

SWA_HEADS = 8
SWA_KV_HEADS = 2
SWA_DH = 64
SWA_WINDOW = 128

MOBA_HEADS = 8
MOBA_KV_HEADS = 2
MOBA_DH = 128
MOBA_BLOCK = 256
MOBA_TOPK = 3
MOBA_GROUP = MOBA_HEADS // MOBA_KV_HEADS
MOBA_NBLK = SEQ // MOBA_BLOCK
MOBA_VT_ROWS = MOBA_DH + 16
LOG2E = math.log2(math.e)
SWA_VT_ROWS = SWA_KV_HEADS * SWA_DH + 16

EPS = 1e-6
LANES = 128
NEG_BIG = -1e30
VMEM_LIMIT = 48 * 1024 * 1024
MOBA_VMEM_LIMIT = 56 * 1024 * 1024

A_WIDTH = GLA_HEADS * GLA_DV
B_WIDTH = SWA_HEADS * SWA_DH
C_WIDTH = MOBA_HEADS * MOBA_DH
GLA_QK = GLA_HEADS * GLA_DK

ROW_TILE = 1024
GLA_TILE = 1024
SWA_TILE = 1024

F32 = jnp.float32
BF16 = jnp.bfloat16


def _dot(a, b):
    return jnp.dot(a, b, preferred_element_type=F32)


def _dot_nt(a, b):
    return lax.dot_general(a, b, (((1,), (1,)), ((), ())), preferred_element_type=F32)


def _dot_tn(a, b):
    return lax.dot_general(a, b, (((0,), (0,)), ((), ())), preferred_element_type=F32)


def _silu(x):
    return x * (1.0 / (1.0 + jnp.exp(-x)))


def _t5_thresholds():
    d = np.arange(REL_MAX_DIST + 1)
    nf = np.maximum(d, 1).astype(np.float32)
    large = REL_MAX_EXACT + (np.log(nf / np.float32(REL_MAX_EXACT))
                             / np.float32(math.log(REL_MAX_DIST / REL_MAX_EXACT))
                             * np.float32(REL_BUCKETS - REL_MAX_EXACT)).astype(np.int32)
    bucket = np.where(d < REL_MAX_EXACT, d, np.minimum(large, REL_BUCKETS - 1))
    assert np.all(np.diff(bucket) >= 0) and bucket[-1] == REL_BUCKETS - 1
    return [int(np.argmax(bucket >= b)) for b in range(REL_BUCKETS)]


_T5_THRESH = _t5_thresholds()


def _t5_bias(dist, rb_ref, head):
    out = jnp.full(dist.shape, rb_ref[0, head], F32)
    for b in range(1, REL_BUCKETS):
        out = jnp.where(dist >= _T5_THRESH[b], rb_ref[b, head], out)
    return out


def _layer_io_kernel(*refs, n_acc, n_proj, write_h, scales):
    h_ref = refs[0]
    pos = 1
    acc_pairs = []
    for _ in range(n_acc):
        acc_pairs.append((refs[pos], refs[pos + 1]))
        pos += 2
    gain_ref = refs[pos]
    pos += 1
    wp_refs = refs[pos:pos + n_proj]
    pos += n_proj
    out_refs = refs[pos:]

    h = h_ref[...]
    for a_ref, w_ref in acc_pairs:
        h = h + _dot(a_ref[...], w_ref[...])
    o = 0
    if write_h:
        out_refs[0][...] = h
        o = 1
    y = h * lax.rsqrt(jnp.mean(h * h, axis=-1, keepdims=True) + EPS) * gain_ref[...]
    if n_proj == 0:
        out_refs[o][...] = y
        return
    yb = y.astype(BF16)
    for k in range(n_proj):
        r = _dot(yb, wp_refs[k][...])
        if scales[k] != 1.0:
            r = r * scales[k]
        out_refs[o + k][...] = r.astype(out_refs[o + k].dtype)


def _layer_io(h, acc_pairs, gain, proj_ws, proj_dtypes, scales, write_h):
    n_rows = h.shape[0]
    grid = (n_rows // ROW_TILE,)
    row_spec = lambda n: pl.BlockSpec((ROW_TILE, n), lambda i: (i, 0))
    full_spec = lambda a: pl.BlockSpec(a.shape, lambda i: (0,) * a.ndim, pipeline_mode=pl.Buffered(1))

    args = [h]
    in_specs = [row_spec(D_MODEL)]
    for a, w in acc_pairs:
        args += [a, w]
        in_specs += [row_spec(a.shape[1]), full_spec(w)]
    args.append(gain)
    in_specs.append(full_spec(gain))
    for w in proj_ws:
        args.append(w)
        in_specs.append(full_spec(w))

    out_shape, out_specs = [], []
    if write_h:
        out_shape.append(jax.ShapeDtypeStruct((n_rows, D_MODEL), F32))
        out_specs.append(row_spec(D_MODEL))
    if proj_ws:
        for w, dt in zip(proj_ws, proj_dtypes):
            out_shape.append(jax.ShapeDtypeStruct((n_rows, w.shape[1]), dt))
            out_specs.append(row_spec(w.shape[1]))
    else:
        out_shape.append(jax.ShapeDtypeStruct((n_rows, D_MODEL), F32))
        out_specs.append(row_spec(D_MODEL))

    kern = functools.partial(_layer_io_kernel, n_acc=len(acc_pairs), n_proj=len(proj_ws),
                             write_h=write_h, scales=tuple(scales))
    return pl.pallas_call(
        kern, grid=grid, in_specs=in_specs, out_specs=out_specs, out_shape=out_shape,
        compiler_params=pltpu.CompilerParams(dimension_semantics=("arbitrary",),
                                             vmem_limit_bytes=VMEM_LIMIT),
        name="layer_io",
    )(*args)


def _gla_kernel(q_ref, k_ref, v_ref, gate_ref, down_ref, wup_ref, bup_ref, gain_ref, o_ref, st_ref):
    C = GLA_CHUNK

    @pl.when(pl.program_id(1) == 0)
    def _():
        st_ref[...] = jnp.zeros_like(st_ref)

    a = down_ref[...]
    a_hi = a.astype(BF16)
    a_lo = (a - a_hi.astype(F32)).astype(BF16)
    lane_a = lax.broadcasted_iota(jnp.int32, a.shape, 1) // GLA_RANK
    z = _dot(jnp.where(lane_a == 1, a_lo, a_hi), wup_ref[...]) + bup_ref[...]
    log_a = (jnp.minimum(z, 0.0) - jnp.log1p(jnp.exp(-jnp.abs(z)))) * (1.0 / GLA_TAU)

    r_i = lax.broadcasted_iota(jnp.int32, (C, C), 0)
    c_i = lax.broadcasted_iota(jnp.int32, (C, C), 1)
    tri = (c_i <= r_i).astype(BF16)
    lane_qk = lax.broadcasted_iota(jnp.int32, (C, GLA_QK), 1)
    head_masks = [(lane_qk // GLA_DK) == h for h in range(GLA_HEADS)]
    rs = lax.broadcasted_iota(jnp.int32, (GLA_HEADS * C, C), 0)
    cs = lax.broadcasted_iota(jnp.int32, (GLA_HEADS * C, C), 1)
    causal = (rs % C) >= cs
    st_r = lax.broadcasted_iota(jnp.int32, (A_WIDTH, GLA_QK), 0)
    st_c = lax.broadcasted_iota(jnp.int32, (A_WIDTH, GLA_QK), 1)
    same_head = (st_r // GLA_DV) == (st_c // GLA_DK)
    gain = gain_ref[...]

    for c in range(GLA_TILE // C):
        rows = slice(c * C, (c + 1) * C)
        g = log_a[rows]
        g_hi = g.astype(BF16)
        g_r = g - g_hi.astype(F32)
        g_mid = g_r.astype(BF16)
        g_lo = (g_r - g_mid.astype(F32)).astype(BF16)
        b = _dot(tri, g_hi) + _dot(tri, g_mid) + _dot(tri, g_lo)
        b_last = b[C - 1:C]
        q = q_ref[rows, :].astype(F32)
        k = k_ref[rows, :].astype(F32)
        v = v_ref[rows, :]
        q_e = q * jnp.exp(b)
        k_e = (k * jnp.exp(-b)).astype(BF16)
        k_l = (k * jnp.exp(b_last - b)).astype(BF16)
        decay = jnp.exp(b_last)

        q_stack = jnp.concatenate([jnp.where(m, q_e, 0.0) for m in head_masks], axis=0).astype(BF16)
        att = jnp.where(causal, _dot_nt(q_stack, k_e), 0.0).astype(BF16)
        o_full = _dot(att, v)
        o_intra = jnp.concatenate(
            [o_full[h * C:(h + 1) * C, h * GLA_DV:(h + 1) * GLA_DV] for h in range(GLA_HEADS)], axis=1)
        st = st_ref[...]
        o = o_intra + _dot_nt(q_e.astype(BF16), st.astype(BF16))
        kv_t = _dot_tn(v, k_l)
        st_ref[...] = st * decay + jnp.where(same_head, kv_t, 0.0)

        outs = []
        for h in range(GLA_HEADS):
            oh = o[:, h * GLA_DV:(h + 1) * GLA_DV]
            outs.append(oh * lax.rsqrt(jnp.mean(oh * oh, axis=-1, keepdims=True) + EPS) * gain)
        on = jnp.concatenate(outs, axis=1)
        o_ref[rows, :] = (on * _silu(gate_ref[rows, :].astype(F32))).astype(o_ref.dtype)


def _gla(aq, ak, av, agate, adown, w_up, b_up, gain):
    nt = SEQ // GLA_TILE
    row = lambda n: pl.BlockSpec((GLA_TILE, n), lambda b, s: (b * nt + s, 0))
    full = lambda a: pl.BlockSpec(a.shape, lambda b, s: (0,) * a.ndim)
    return pl.pallas_call(
        _gla_kernel, grid=(BATCH, nt),
        in_specs=[row(GLA_QK), row(GLA_QK), row(A_WIDTH), row(A_WIDTH), row(LANES),
                  full(w_up), full(b_up), full(gain)],
        out_specs=row(A_WIDTH),
        out_shape=jax.ShapeDtypeStruct((TOKENS, A_WIDTH), BF16),
        scratch_shapes=[pltpu.VMEM((A_WIDTH, GLA_QK), F32)],
        compiler_params=pltpu.CompilerParams(dimension_semantics=("arbitrary", "arbitrary"),
                                             vmem_limit_bytes=VMEM_LIMIT),
        name="gla",
    )(aq, ak, av, agate, adown, w_up, b_up, gain)


def _swa_kernel(sink_ref, rb_ref, q_ref, kp_ref, k_ref, vp_ref, v_ref, gate_ref, o_ref, bias_ref):
    W = SWA_WINDOW
    H = SWA_HEADS
    DH = SWA_DH
    G = SWA_HEADS // SWA_KV_HEADS
    t = pl.program_id(1)

    @pl.when(jnp.logical_and(pl.program_id(0) == 0, t == 0))
    def _():
        kj = lax.broadcasted_iota(jnp.int32, (2 * W, W), 0)
        qi = lax.broadcasted_iota(jnp.int32, (2 * W, W), 1)
        dist = qi + W - kj
        in_win = jnp.logical_and(dist >= 0, dist < W)
        for h in range(H):
            bias_ref[:, h * W:(h + 1) * W] = jnp.where(in_win, _t5_bias(dist, rb_ref, h) * LOG2E, -jnp.inf)

    kcat = jnp.concatenate([kp_ref[...], k_ref[...]], axis=0)
    vcat = jnp.concatenate([vp_ref[...], v_ref[...]], axis=0)
    vt = vcat.astype(F32).T.astype(BF16)
    ones_rows = jnp.ones((SWA_VT_ROWS - LANES, 2 * W), BF16)
    q_t = q_ref[...].astype(F32).T.astype(BF16)
    zeros = jnp.zeros((DH, W), BF16)
    colh = lax.broadcasted_iota(jnp.int32, (1, H * W), 1) // W
    sink = jnp.full((1, H * W), sink_ref[H - 1], F32)
    for h in range(H - 2, -1, -1):
        sink = jnp.where(colh == h, sink_ref[h], sink)
    sink = sink * LOG2E
    key_prev = lax.broadcasted_iota(jnp.int32, (2 * W, H * W), 0) < W

    for blk in range(SWA_TILE // W):
        tok = slice(blk * W, (blk + 1) * W)
        win = slice(blk * W, blk * W + 2 * W)
        pieces = []
        for h in range(H):
            piece = q_t[h * DH:(h + 1) * DH, tok]
            pieces.append(jnp.concatenate([piece, zeros] if h < G else [zeros, piece], axis=0))
        q_pad = jnp.concatenate(pieces, axis=1)
        s = _dot(kcat[win], q_pad) + bias_ref[...]
        if blk == 0:
            s = jnp.where(jnp.logical_and(t == 0, key_prev), -jnp.inf, s)
        m = jnp.maximum(jnp.max(s, axis=0, keepdims=True), sink)
        p = jnp.exp2(s - m).astype(BF16)
        pv = _dot(jnp.concatenate([vt[:, win], ones_rows], axis=0), p)
        inv = 1.0 / (pv[LANES:LANES + 1] + jnp.exp2(sink - m))
        o_t = jnp.concatenate(
            [pv[(h // G) * DH:(h // G + 1) * DH, h * W:(h + 1) * W] * inv[:, h * W:(h + 1) * W]
             for h in range(H)], axis=0)
        gate = gate_ref[tok, :].astype(F32)
        o_ref[tok, :] = (o_t.T * _silu(gate)).astype(o_ref.dtype)


def _swa(bq, bkv, bgate, rel_bias, sinks):
    nt = SEQ // SWA_TILE
    per = SWA_TILE // SWA_WINDOW
    row = lambda n: pl.BlockSpec((SWA_TILE, n), lambda b, t: (b * nt + t, 0))
    own = lambda c: pl.BlockSpec((SWA_TILE, LANES), lambda b, t: (b * nt + t, c))
    prev = lambda c: pl.BlockSpec((SWA_WINDOW, LANES),
                                  lambda b, t: (jnp.maximum((b * nt + t) * per - 1, 0), c))
    smem = pl.BlockSpec(memory_space=pltpu.SMEM)
    return pl.pallas_call(
        _swa_kernel, grid=(BATCH, nt),
        in_specs=[smem, smem, row(B_WIDTH), prev(0), own(0), prev(1), own(1), row(B_WIDTH)],
        out_specs=row(B_WIDTH),
        out_shape=jax.ShapeDtypeStruct((TOKENS, B_WIDTH), BF16),
        scratch_shapes=[pltpu.VMEM((2 * SWA_WINDOW, SWA_HEADS * SWA_WINDOW), F32)],
        compiler_params=pltpu.CompilerParams(dimension_semantics=("arbitrary", "arbitrary"),
                                             vmem_limit_bytes=VMEM_LIMIT),
        name="swa",
    )(sinks, rel_bias, bq, bkv, bkv, bkv, bkv, bgate)


def _moba_kernel(rb_ref, q_ref, k_ref, v_ref, gate_ref, o_ref,
                 kaug_ref, vt_ref, km_ref, bias_ref, qat_ref, s_ref, smax_ref, m_ref, acc_ref):
    BLK = MOBA_BLOCK
    G = MOBA_GROUP
    KVH = MOBA_KV_HEADS
    R = G * BLK
    NP1 = MOBA_NBLK + 1
    i = pl.program_id(1)

    @pl.when(i == 0)
    def _():
        lane = lax.broadcasted_iota(jnp.int32, (BLK, LANES), 1)
        ones_rows = jnp.ones((MOBA_VT_ROWS - MOBA_DH, BLK), BF16)

        def prep(j, carry):
            rows = pl.ds(pl.multiple_of(j * BLK, BLK), BLK)
            onehot = jnp.where(lane == j, 1.0, 0.0).astype(BF16)
            for kh in range(KVH):
                kb = k_ref[rows, kh * MOBA_DH:(kh + 1) * MOBA_DH]
                kaug_ref[kh * NP1 + j + 1, :, 0:LANES] = kb
                kaug_ref[kh * NP1 + j + 1, :, LANES:2 * LANES] = onehot
                km_ref[pl.ds(kh * MOBA_NBLK + j, 1), :] = jnp.mean(kb.astype(F32), axis=0, keepdims=True)
                vb = v_ref[rows, kh * MOBA_DH:(kh + 1) * MOBA_DH]
                vt_ref[kh * NP1 + j + 1, 0:MOBA_DH, :] = vb.astype(F32).T.astype(BF16)
                vt_ref[kh * NP1 + j + 1, MOBA_DH:, :] = ones_rows
            return carry

        lax.fori_loop(0, MOBA_NBLK, prep, 0)
        for kh in range(KVH):
            kaug_ref[kh * NP1, :, 0:LANES] = jnp.zeros((BLK, LANES), BF16)
            kaug_ref[kh * NP1, :, LANES:2 * LANES] = jnp.where(lane == MOBA_NBLK, 1.0, 0.0).astype(BF16)
            vt_ref[kh * NP1] = jnp.zeros((MOBA_VT_ROWS, BLK), BF16)
            qat_ref[kh, MOBA_DH + MOBA_NBLK:, :] = jnp.full((2 * LANES - MOBA_DH - MOBA_NBLK, R), NEG_BIG, BF16)

    @pl.when(jnp.logical_and(pl.program_id(0) == 0, i == 0))
    def _():
        tk = lax.broadcasted_iota(jnp.int32, (BLK, BLK), 0)
        tq = lax.broadcasted_iota(jnp.int32, (BLK, BLK), 1)
        d_own = tq - tk
        for head in range(MOBA_HEADS):
            cols = slice(head * BLK, (head + 1) * BLK)
            bias_ref[0:BLK, cols] = _t5_bias(d_own + BLK, rb_ref, head) * LOG2E
            bias_ref[BLK:2 * BLK, cols] = jnp.where(d_own >= 0, _t5_bias(d_own, rb_ref, head) * LOG2E, -jnp.inf)

    blk = lax.broadcasted_iota(jnp.int32, (MOBA_NBLK, R), 0)
    blkf = blk.astype(F32)
    past = blk < i
    colh = lax.broadcasted_iota(jnp.int32, (1, R), 1) // BLK
    far_row = REL_BUCKETS - 1

    own_max = []
    for kh in range(KVH):
        q = jnp.concatenate([q_ref[:, (kh * G + g) * MOBA_DH:(kh * G + g + 1) * MOBA_DH] for g in range(G)],
                            axis=0)
        q_t = q.astype(F32).T.astype(BF16)
        qat_ref[kh, 0:MOBA_DH, :] = q_t
        s_own = (_dot(kaug_ref[kh * NP1 + i + 1, :, 0:MOBA_DH], q_t)
                 + bias_ref[BLK:2 * BLK, kh * R:(kh + 1) * R])
        s_ref[kh, BLK:2 * BLK, :] = s_own
        own_max.append(jnp.max(s_own, axis=0, keepdims=True))

        km = km_ref[kh * MOBA_NBLK:(kh + 1) * MOBA_NBLK, :]
        km_hi = km.astype(BF16)
        km_lo = (km - km_hi.astype(F32)).astype(BF16)
        gate = _dot(km_hi, q_t) + _dot(km_lo, q_t)
        g_ = jnp.where(past, gate, -jnp.inf)
        selected = jnp.zeros((MOBA_NBLK, R), dtype=jnp.bool_)
        for _ in range(MOBA_TOPK):
            mx = jnp.max(g_, axis=0, keepdims=True)
            first = jnp.min(jnp.where(g_ == mx, blkf, 1e9), axis=0, keepdims=True)
            pick = jnp.logical_and(blkf == first, past)
            selected = jnp.logical_or(selected, pick)
            g_ = jnp.where(pick, -jnp.inf, g_)
        cfar = jnp.where(colh == 0, rb_ref[far_row, kh * G],
                         jnp.where(colh == 1, rb_ref[far_row, kh * G + 1],
                                   jnp.where(colh == 2, rb_ref[far_row, kh * G + 2],
                                             rb_ref[far_row, kh * G + 3])))
        sel_bias = jnp.where(selected, jnp.where(blk < i - 1, cfar * LOG2E, 0.0), NEG_BIG)
        qat_ref[kh, MOBA_DH:MOBA_DH + MOBA_NBLK, :] = sel_bias.astype(BF16)

    for kh in range(KVH):
        s_near = _dot(kaug_ref[kh * NP1 + i], qat_ref[kh]) + bias_ref[0:BLK, kh * R:(kh + 1) * R]
        s_ref[kh, 0:BLK, :] = s_near
        smax_ref[kh] = jnp.maximum(own_max[kh], jnp.max(s_near, axis=0, keepdims=True))

    def produce(slot, kh, start):
        k2 = kaug_ref[pl.ds(kh * NP1 + start, 2)].reshape(2 * BLK, 2 * LANES)
        s = _dot(k2, qat_ref[kh])
        s_ref[slot * KVH + kh] = s
        smax_ref[slot * KVH + kh] = jnp.max(s, axis=0, keepdims=True)

    def consume(slot, kh, start, first=False):
        m_new = smax_ref[slot * KVH + kh]
        if not first:
            m_old = m_ref[kh]
            m_new = jnp.maximum(m_old, m_new)
        p = jnp.exp2(s_ref[slot * KVH + kh] - m_new).astype(BF16)
        pv = (_dot(vt_ref[kh * NP1 + start], p[0:BLK])
              + _dot(vt_ref[kh * NP1 + start + 1], p[BLK:2 * BLK]))
        acc_ref[kh] = pv if first else jnp.exp2(m_old - m_new) * acc_ref[kh] + pv
        m_ref[kh] = m_new

    n_far = jnp.maximum(i - 1, 0)
    n_pairs = (n_far + 1) // 2

    def far_start(t):
        return jnp.maximum(n_far - 2 * t - 1, 0)

    def far_step(t, slot):
        for kh in range(KVH):
            produce(1 - slot, kh, far_start(t + 1))
        for kh in range(KVH):
            consume(slot, kh, far_start(t))

    for kh in range(KVH):
        produce(1, kh, far_start(0))
    for kh in range(KVH):
        consume(0, kh, i, first=True)

    def far_two(u, carry):
        far_step(2 * u, 1)
        far_step(2 * u + 1, 0)
        return carry

    lax.fori_loop(0, n_pairs // 2, far_two, 0)

    @pl.when(n_pairs % 2 == 1)
    def _():
        far_step(n_pairs - 1, 1)

    for kh in range(KVH):
        acc = acc_ref[kh]
        o = (acc[0:MOBA_DH] * (1.0 / acc[MOBA_DH:MOBA_DH + 1])).T
        for g in range(G):
            cols = slice((kh * G + g) * MOBA_DH, (kh * G + g + 1) * MOBA_DH)
            gt = gate_ref[:, cols].astype(F32)
            o_ref[:, cols] = (o[g * BLK:(g + 1) * BLK] * _silu(gt)).astype(o_ref.dtype)


def _moba(cq, ck, cv, cgate, rel_bias):
    R = MOBA_GROUP * MOBA_BLOCK
    KVH = MOBA_KV_HEADS
    NP1 = MOBA_NBLK + 1
    qspec = pl.BlockSpec((MOBA_BLOCK, C_WIDTH), lambda b, i: (b * MOBA_NBLK + i, 0))
    kvspec = pl.BlockSpec((SEQ, KVH * MOBA_DH), lambda b, i: (b, 0), pipeline_mode=pl.Buffered(1))
    return pl.pallas_call(
        _moba_kernel, grid=(BATCH, MOBA_NBLK),
        in_specs=[pl.BlockSpec(memory_space=pltpu.SMEM), qspec, kvspec, kvspec, qspec],
        out_specs=qspec,
        out_shape=jax.ShapeDtypeStruct((TOKENS, C_WIDTH), BF16),
        scratch_shapes=[pltpu.VMEM((KVH * NP1, MOBA_BLOCK, 2 * LANES), BF16),
                        pltpu.VMEM((KVH * NP1, MOBA_VT_ROWS, MOBA_BLOCK), BF16),
                        pltpu.VMEM((KVH * MOBA_NBLK, MOBA_DH), F32),
                        pltpu.VMEM((2 * MOBA_BLOCK, KVH * R), F32),
                        pltpu.VMEM((KVH, 2 * LANES, R), BF16),
                        pltpu.VMEM((2 * KVH, 2 * MOBA_BLOCK, R), F32),
                        pltpu.VMEM((2 * KVH, 1, R), F32),
                        pltpu.VMEM((KVH, 1, R), F32),
                        pltpu.VMEM((KVH, MOBA_VT_ROWS, R), F32)],
        compiler_params=pltpu.CompilerParams(
            dimension_semantics=("arbitrary", "arbitrary"), vmem_limit_bytes=MOBA_VMEM_LIMIT),
        name="moba",
    )(rel_bias, cq, ck, cv, cgate)


def kernel(x, norm_gain, final_gain, rel_bias, w_in_even, gla_w_up, gla_b_up, gla_norm_gain, swa_sinks,
           w_out_even, w_in_odd, w_out_odd):
    assert x.shape == (BATCH, SEQ, D_MODEL)
    h0 = x.reshape(TOKENS, D_MODEL)

    we = w_in_even[0]
    splits = np.cumsum([0, GLA_QK, GLA_QK, A_WIDTH, GLA_RANK, A_WIDTH, B_WIDTH, SWA_KV_HEADS * SWA_DH,
                        SWA_KV_HEADS * SWA_DH, B_WIDTH])
    w_aq, w_ak, w_av, w_down, w_agate, w_bq = [we[:, splits[k]:splits[k + 1]] for k in range(6)]
    w_bkv = we[:, splits[6]:splits[8]]
    w_bgate = we[:, splits[8]:splits[9]]
    w_down = jnp.pad(jnp.concatenate([w_down] * 3, axis=1), ((0, 0), (0, LANES - 3 * GLA_RANK)))
    proj_ws = [w.astype(BF16) for w in (w_aq, w_ak, w_av, w_down, w_agate, w_bq, w_bkv, w_bgate)]
    proj_dt = [BF16, BF16, BF16, F32, BF16, BF16, BF16, BF16]
    scales = [GLA_DK ** -0.5, 1.0, 1.0, 1.0, 1.0, SWA_DH ** -0.5 * LOG2E, 1.0, 1.0]
    aq, ak, av, adown, agate, bq, bkv, bgate = _layer_io(
        h0, [], norm_gain[0:1], proj_ws, proj_dt, scales, write_h=False)

    w_up_hi = gla_w_up[0].astype(BF16)
    w_up_lo = (gla_w_up[0] - w_up_hi.astype(F32)).astype(BF16)
    w_up = jnp.pad(jnp.concatenate([w_up_hi, w_up_hi, w_up_lo], axis=0), ((0, LANES - 3 * GLA_RANK), (0, 0)))
    oa = _gla(aq, ak, av, agate, adown, w_up, gla_b_up[0:1], gla_norm_gain[0:1])
    ob = _swa(bq, bkv, bgate, rel_bias, swa_sinks[0])

    wo = w_out_even[0].astype(BF16)
    wi = w_in_odd[0]
    osp = np.cumsum([0, C_WIDTH, MOBA_KV_HEADS * MOBA_DH, MOBA_KV_HEADS * MOBA_DH, C_WIDTH])
    odd_ws = [wi[:, osp[k]:osp[k + 1]].astype(BF16) for k in range(4)]
    h1, cq, ck, cv, cgate = _layer_io(
        h0, [(oa, wo[:A_WIDTH]), (ob, wo[A_WIDTH:])], norm_gain[1:2], odd_ws, [BF16] * 4,
        [MOBA_DH ** -0.5 * LOG2E, 1.0, 1.0, 1.0], write_h=True)

    oc = _moba(cq, ck, cv, cgate, rel_bias)

    (out,) = _layer_io(h1, [(oc, w_out_odd[0].astype(BF16))], final_gain.reshape(1, D_MODEL), [], [], [],
                       write_h=False)
    return out.reshape(BATCH, SEQ, D_MODEL)
```

```python
import functools
import math

import numpy as np
import jax
import jax.numpy as jnp
from jax import lax
from jax.experimental import pallas as pl
from jax.experimental.pallas import tpu as pltpu

D_MODEL = 1024
BATCH = 2
SEQ = 8192
TOKENS = BATCH * SEQ

REL_BUCKETS = 32
REL_MAX_EXACT = 16
REL_MAX_DIST = 128
ATTN_HEADS = 8

GLA_HEADS = 4
GLA_DK = 64
GLA_DV = 128
GLA_RANK = 16
GLA_TAU = 16.0
GLA_CHUNK = 64

SWA_HEADS = 8
SWA_KV_HEADS = 2
SWA_DH = 64
SWA_WINDOW = 128

MOBA_HEADS = 8
MOBA_KV_HEADS = 2
MOBA_DH = 128
MOBA_BLOCK = 256
MOBA_TOPK = 3
MOBA_GROUP = MOBA_HEADS // MOBA_KV_HEADS
MOBA_NBLK = SEQ // MOBA_BLOCK
MOBA_VT_ROWS = MOBA_DH + 16
LOG2E = math.log2(math.e)
SWA_VT_ROWS = SWA_KV_HEADS * SWA_DH + 16

EPS = 1e-6
LANES = 128
NEG_BIG = -1e30
VMEM_LIMIT = 48 * 1024 * 1024
MOBA_VMEM_LIMIT = 56 * 1024 * 1024

A_WIDTH = GLA_HEADS * GLA_DV
B_WIDTH = SWA_HEADS * SWA_DH
C_WIDTH = MOBA_HEADS * MOBA_DH
GLA_QK = GLA_HEADS * GLA_DK

ROW_TILE = 1024
GLA_TILE = 1024
SWA_TILE = 1024

F32 = jnp.float32
BF16 = jnp.bfloat16


def _dot(a, b):
    return jnp.dot(a, b, preferred_element_type=F32)


def _dot_nt(a, b):
    return lax.dot_general(a, b, (((1,), (1,)), ((), ())), preferred_element_type=F32)


def _dot_tn(a, b):
    return lax.dot_general(a, b, (((0,), (0,)), ((), ())), preferred_element_type=F32)


def _silu(x):
    return x * (1.0 / (1.0 + jnp.exp(-x)))


def _t5_thresholds():
    d = np.arange(REL_MAX_DIST + 1)
    nf = np.maximum(d, 1).astype(np.float32)
    large = REL_MAX_EXACT + (np.log(nf / np.float32(REL_MAX_EXACT))
                             / np.float32(math.log(REL_MAX_DIST / REL_MAX_EXACT))
                             * np.float32(REL_BUCKETS - REL_MAX_EXACT)).astype(np.int32)
    bucket = np.where(d < REL_MAX_EXACT, d, np.minimum(large, REL_BUCKETS - 1))
    assert np.all(np.diff(bucket) >= 0) and bucket[-1] == REL_BUCKETS - 1
    return [int(np.argmax(bucket >= b)) for b in range(REL_BUCKETS)]


_T5_THRESH = _t5_thresholds()


def _t5_bias(dist, rb_ref, head):
    out = jnp.full(dist.shape, rb_ref[0, head], F32)
    for b in range(1, REL_BUCKETS):
        out = jnp.where(dist >= _T5_THRESH[b], rb_ref[b, head], out)
    return out


def _layer_io_kernel(*refs, n_acc, n_proj, write_h, scales):
    h_ref = refs[0]
    pos = 1
    acc_pairs = []
    for _ in range(n_acc):
        acc_pairs.append((refs[pos], refs[pos + 1]))
        pos += 2
    gain_ref = refs[pos]
    pos += 1
    wp_refs = refs[pos:pos + n_proj]
    pos += n_proj
    out_refs = refs[pos:]

    h = h_ref[...]
    for a_ref, w_ref in acc_pairs:
        h = h + _dot(a_ref[...], w_ref[...])
    o = 0
    if write_h:
        out_refs[0][...] = h
        o = 1
    y = h * lax.rsqrt(jnp.mean(h * h, axis=-1, keepdims=True) + EPS) * gain_ref[...]
    if n_proj == 0:
        out_refs[o][...] = y
        return
    yb = y.astype(BF16)
    for k in range(n_proj):
        r = _dot(yb, wp_refs[k][...])
        if scales[k] != 1.0:
            r = r * scales[k]
        out_refs[o + k][...] = r.astype(out_refs[o + k].dtype)


def _layer_io(h, acc_pairs, gain, proj_ws, proj_dtypes, scales, write_h):
    n_rows = h.shape[0]
    grid = (n_rows // ROW_TILE,)
    row_spec = lambda n: pl.BlockSpec((ROW_TILE, n), lambda i: (i, 0))
    full_spec = lambda a: pl.BlockSpec(a.shape, lambda i: (0,) * a.ndim, pipeline_mode=pl.Buffered(1))

    args = [h]
    in_specs = [row_spec(D_MODEL)]
    for a, w in acc_pairs:
        args += [a, w]
        in_specs += [row_spec(a.shape[1]), full_spec(w)]
    args.append(gain)
    in_specs.append(full_spec(gain))
    for w in proj_ws:
        args.append(w)
        in_specs.append(full_spec(w))

    out_shape, out_specs = [], []
    if write_h:
        out_shape.append(jax.ShapeDtypeStruct((n_rows, D_MODEL), F32))
        out_specs.append(row_spec(D_MODEL))
    if proj_ws:
        for w, dt in zip(proj_ws, proj_dtypes):
            out_shape.append(jax.ShapeDtypeStruct((n_rows, w.shape[1]), dt))
            out_specs.append(row_spec(w.shape[1]))
    else:
        out_shape.append(jax.ShapeDtypeStruct((n_rows, D_MODEL), F32))
        out_specs.append(row_spec(D_MODEL))

    kern = functools.partial(_layer_io_kernel, n_acc=len(acc_pairs), n_proj=len(proj_ws),
                             write_h=write_h, scales=tuple(scales))
    return pl.pallas_call(
        kern, grid=grid, in_specs=in_specs, out_specs=out_specs, out_shape=out_shape,
        compiler_params=pltpu.CompilerParams(dimension_semantics=("arbitrary",),
                                             vmem_limit_bytes=VMEM_LIMIT),
        name="layer_io",
    )(*args)


def _gla_kernel(q_ref, k_ref, v_ref, gate_ref, down_ref, wup_ref, bup_ref, gain_ref, o_ref, st_ref):
    C = GLA_CHUNK

    @pl.when(pl.program_id(1) == 0)
    def _():
        st_ref[...] = jnp.zeros_like(st_ref)

    a = down_ref[...]
    a_hi = a.astype(BF16)
    a_lo = (a - a_hi.astype(F32)).astype(BF16)
    lane_a = lax.broadcasted_iota(jnp.int32, a.shape, 1) // GLA_RANK
    z = _dot(jnp.where(lane_a == 1, a_lo, a_hi), wup_ref[...]) + bup_ref[...]
    log_a = (jnp.minimum(z, 0.0) - jnp.log(1.0 + jnp.exp(-jnp.abs(z)))) * (1.0 / GLA_TAU)

    r_i = lax.broadcasted_iota(jnp.int32, (C, C), 0)
    c_i = lax.broadcasted_iota(jnp.int32, (C, C), 1)
    tri = (c_i <= r_i).astype(BF16)
    lane_qk = lax.broadcasted_iota(jnp.int32, (C, GLA_QK), 1)
    head_masks = [(lane_qk // GLA_DK) == h for h in range(GLA_HEADS)]
    rs = lax.broadcasted_iota(jnp.int32, (GLA_HEADS * C, C), 0)
    cs = lax.broadcasted_iota(jnp.int32, (GLA_HEADS * C, C), 1)
    causal = (rs % C) >= cs
    st_r = lax.broadcasted_iota(jnp.int32, (A_WIDTH, GLA_QK), 0)
    st_c = lax.broadcasted_iota(jnp.int32, (A_WIDTH, GLA_QK), 1)
    same_head = (st_r // GLA_DV) == (st_c // GLA_DK)
    gain = gain_ref[...]

    for c in range(GLA_TILE // C):
        rows = slice(c * C, (c + 1) * C)
        g = log_a[rows]
        g_hi = g.astype(BF16)
        g_lo = (g - g_hi.astype(F32)).astype(BF16)
        b = _dot(tri, g_hi) + _dot(tri, g_lo)
        b_last = b[C - 1:C]
        q = q_ref[rows, :].astype(F32)
        k = k_ref[rows, :].astype(F32)
        v = v_ref[rows, :]
        q_e = q * jnp.exp(b)
        k_e = (k * jnp.exp(-b)).astype(BF16)
        k_l = (k * jnp.exp(b_last - b)).astype(BF16)
        decay = jnp.exp(b_last)

        q_stack = jnp.concatenate([jnp.where(m, q_e, 0.0) for m in head_masks], axis=0).astype(BF16)
        att = jnp.where(causal, _dot_nt(q_stack, k_e), 0.0).astype(BF16)
        o_full = _dot(att, v)
        o_intra = jnp.concatenate(
            [o_full[h * C:(h + 1) * C, h * GLA_DV:(h + 1) * GLA_DV] for h in range(GLA_HEADS)], axis=1)
        st = st_ref[...]
        o = o_intra + _dot_nt(q_e.astype(BF16), st.astype(BF16))
        kv_t = _dot_tn(v, k_l)
        st_ref[...] = st * decay + jnp.where(same_head, kv_t, 0.0)

        outs = []
        for h in range(GLA_HEADS):
            oh = o[:, h * GLA_DV:(h + 1) * GLA_DV]
            outs.append(oh * lax.rsqrt(jnp.mean(oh * oh, axis=-1, keepdims=True) + EPS) * gain)
        on = jnp.concatenate(outs, axis=1)
        o_ref[rows, :] = (on * _silu(gate_ref[rows, :].astype(F32))).astype(o_ref.dtype)


def _gla(aq, ak, av, agate, adown, w_up, b_up, gain):
    nt = SEQ // GLA_TILE
    row = lambda n: pl.BlockSpec((GLA_TILE, n), lambda b, s: (b * nt + s, 0))
    full = lambda a: pl.BlockSpec(a.shape, lambda b, s: (0,) * a.ndim)
    return pl.pallas_call(
        _gla_kernel, grid=(BATCH, nt),
        in_specs=[row(GLA_QK), row(GLA_QK), row(A_WIDTH), row(A_WIDTH), row(LANES),
                  full(w_up), full(b_up), full(gain)],
        out_specs=row(A_WIDTH),
        out_shape=jax.ShapeDtypeStruct((TOKENS, A_WIDTH), BF16),
        scratch_shapes=[pltpu.VMEM((A_WIDTH, GLA_QK), F32)],
        compiler_params=pltpu.CompilerParams(dimension_semantics=("arbitrary", "arbitrary"),
                                             vmem_limit_bytes=VMEM_LIMIT),
        name="gla",
    )(aq, ak, av, agate, adown, w_up, b_up, gain)


def _swa_kernel(sink_ref, rb_ref, q_ref, kp_ref, k_ref, vp_ref, v_ref, gate_ref, o_ref, bias_ref):
    W = SWA_WINDOW
    H = SWA_HEADS
    DH = SWA_DH
    G = SWA_HEADS // SWA_KV_HEADS
    t = pl.program_id(1)

    @pl.when(jnp.logical_and(pl.program_id(0) == 0, t == 0))
    def _():
        kj = lax.broadcasted_iota(jnp.int32, (2 * W, W), 0)
        qi = lax.broadcasted_iota(jnp.int32, (2 * W, W), 1)
        dist = qi + W - kj
        in_win = jnp.logical_and(dist >= 0, dist < W)
        for h in range(H):
            bias_ref[:, h * W:(h + 1) * W] = jnp.where(in_win, _t5_bias(dist, rb_ref, h) * LOG2E, -jnp.inf)

    kcat = jnp.concatenate([kp_ref[...], k_ref[...]], axis=0)
    vcat = jnp.concatenate([vp_ref[...], v_ref[...]], axis=0)
    vt = vcat.astype(F32).T.astype(BF16)
    ones_rows = jnp.ones((SWA_VT_ROWS - LANES, 2 * W), BF16)
    q_t = q_ref[...].astype(F32).T.astype(BF16)
    zeros = jnp.zeros((DH, W), BF16)
    colh = lax.broadcasted_iota(jnp.int32, (1, H * W), 1) // W
    sink = jnp.full((1, H * W), sink_ref[H - 1], F32)
    for h in range(H - 2, -1, -1):
        sink = jnp.where(colh == h, sink_ref[h], sink)
    sink = sink * LOG2E
    key_prev = lax.broadcasted_iota(jnp.int32, (2 * W, H * W), 0) < W

    for blk in range(SWA_TILE // W):
        tok = slice(blk * W, (blk + 1) * W)
        win = slice(blk * W, blk * W + 2 * W)
        pieces = []
        for h in range(H):
            piece = q_t[h * DH:(h + 1) * DH, tok]
            pieces.append(jnp.concatenate([piece, zeros] if h < G else [zeros, piece], axis=0))
        q_pad = jnp.concatenate(pieces, axis=1)
        s = _dot(kcat[win], q_pad) + bias_ref[...]
        if blk == 0:
            s = jnp.where(jnp.logical_and(t == 0, key_prev), -jnp.inf, s)
        m = jnp.maximum(jnp.max(s, axis=0, keepdims=True), sink)
        p = jnp.exp2(s - m).astype(BF16)
        pv = _dot(jnp.concatenate([vt[:, win], ones_rows], axis=0), p)
        inv = 1.0 / (pv[LANES:LANES + 1] + jnp.exp2(sink - m))
        o_t = jnp.concatenate(
            [pv[(h // G) * DH:(h // G + 1) * DH, h * W:(h + 1) * W] * inv[:, h * W:(h + 1) * W]
             for h in range(H)], axis=0)
        gate = gate_ref[tok, :].astype(F32)
        o_ref[tok, :] = (o_t.T * _silu(gate)).astype(o_ref.dtype)


def _swa(bq, bkv, bgate, rel_bias, sinks):
    nt = SEQ // SWA_TILE
    per = SWA_TILE // SWA_WINDOW
    row = lambda n: pl.BlockSpec((SWA_TILE, n), lambda b, t: (b * nt + t, 0))
    own = lambda c: pl.BlockSpec((SWA_TILE, LANES), lambda b, t: (b * nt + t, c))
    prev = lambda c: pl.BlockSpec((SWA_WINDOW, LANES),
                                  lambda b, t: (jnp.maximum((b * nt + t) * per - 1, 0), c))
    smem = pl.BlockSpec(memory_space=pltpu.SMEM)
    return pl.pallas_call(
        _swa_kernel, grid=(BATCH, nt),
        in_specs=[smem, smem, row(B_WIDTH), prev(0), own(0), prev(1), own(1), row(B_WIDTH)],
        out_specs=row(B_WIDTH),
        out_shape=jax.ShapeDtypeStruct((TOKENS, B_WIDTH), BF16),
        scratch_shapes=[pltpu.VMEM((2 * SWA_WINDOW, SWA_HEADS * SWA_WINDOW), F32)],
        compiler_params=pltpu.CompilerParams(dimension_semantics=("arbitrary", "arbitrary"),
                                             vmem_limit_bytes=VMEM_LIMIT),
        name="swa",
    )(sinks, rel_bias, bq, bkv, bkv, bkv, bkv, bgate)


def _moba_kernel(rb_ref, q_ref, k_ref, v_ref, gate_ref, o_ref,
                 kaug_ref, vt_ref, km_ref, bias_ref, qat_ref, s_ref, smax_ref, m_ref, acc_ref):
    BLK = MOBA_BLOCK
    G = MOBA_GROUP
    KVH = MOBA_KV_HEADS
    R = G * BLK
    NP1 = MOBA_NBLK + 1
    i = pl.program_id(1)

    @pl.when(i == 0)
    def _():
        lane = lax.broadcasted_iota(jnp.int32, (BLK, LANES), 1)
        ones_rows = jnp.ones((MOBA_VT_ROWS - MOBA_DH, BLK), BF16)

        def prep(j, carry):
            rows = pl.ds(pl.multiple_of(j * BLK, BLK), BLK)
            onehot = jnp.where(lane == j, 1.0, 0.0).astype(BF16)
            for kh in range(KVH):
                kb = k_ref[rows, kh * MOBA_DH:(kh + 1) * MOBA_DH]
                kaug_ref[kh * NP1 + j + 1, :, 0:LANES] = kb
                kaug_ref[kh * NP1 + j + 1, :, LANES:2 * LANES] = onehot
                km_ref[pl.ds(kh * MOBA_NBLK + j, 1), :] = jnp.mean(kb.astype(F32), axis=0, keepdims=True)
                vb = v_ref[rows, kh * MOBA_DH:(kh + 1) * MOBA_DH]
                vt_ref[kh * NP1 + j + 1, 0:MOBA_DH, :] = vb.astype(F32).T.astype(BF16)
                vt_ref[kh * NP1 + j + 1, MOBA_DH:, :] = ones_rows
            return carry

        lax.fori_loop(0, MOBA_NBLK, prep, 0)
        for kh in range(KVH):
            kaug_ref[kh * NP1, :, 0:LANES] = jnp.zeros((BLK, LANES), BF16)
            kaug_ref[kh * NP1, :, LANES:2 * LANES] = jnp.where(lane == MOBA_NBLK, 1.0, 0.0).astype(BF16)
            vt_ref[kh * NP1] = jnp.zeros((MOBA_VT_ROWS, BLK), BF16)
            qat_ref[kh, MOBA_DH + MOBA_NBLK:, :] = jnp.full((2 * LANES - MOBA_DH - MOBA_NBLK, R), NEG_BIG, BF16)

    @pl.when(jnp.logical_and(pl.program_id(0) == 0, i == 0))
    def _():
        tk = lax.broadcasted_iota(jnp.int32, (BLK, BLK), 0)
        tq = lax.broadcasted_iota(jnp.int32, (BLK, BLK), 1)
        d_own = tq - tk
        for head in range(MOBA_HEADS):
            cols = slice(head * BLK, (head + 1) * BLK)
            bias_ref[0:BLK, cols] = _t5_bias(d_own + BLK, rb_ref, head) * LOG2E
            bias_ref[BLK:2 * BLK, cols] = jnp.where(d_own >= 0, _t5_bias(d_own, rb_ref, head) * LOG2E, -jnp.inf)

    blk = lax.broadcasted_iota(jnp.int32, (MOBA_NBLK, R), 0)
    blkf = blk.astype(F32)
    past = blk < i
    colh = lax.broadcasted_iota(jnp.int32, (1, R), 1) // BLK
    far_row = REL_BUCKETS - 1

    own_max = []
    for kh in range(KVH):
        q = jnp.concatenate([q_ref[:, (kh * G + g) * MOBA_DH:(kh * G + g + 1) * MOBA_DH] for g in range(G)],
                            axis=0)
        q_t = q.astype(F32).T.astype(BF16)
        qat_ref[kh, 0:MOBA_DH, :] = q_t
        s_own = (_dot(kaug_ref[kh * NP1 + i + 1, :, 0:MOBA_DH], q_t)
                 + bias_ref[BLK:2 * BLK, kh * R:(kh + 1) * R])
        s_ref[kh, BLK:2 * BLK, :] = s_own
        own_max.append(jnp.max(s_own, axis=0, keepdims=True))

        km = km_ref[kh * MOBA_NBLK:(kh + 1) * MOBA_NBLK, :]
        km_hi = km.astype(BF16)
        km_lo = (km - km_hi.astype(F32)).astype(BF16)
        gate = _dot(km_hi, q_t) + _dot(km_lo, q_t)
        g_ = jnp.where(past, gate, -jnp.inf)
        selected = jnp.zeros((MOBA_NBLK, R), dtype=jnp.bool_)
        for _ in range(MOBA_TOPK):
            mx = jnp.max(g_, axis=0, keepdims=True)
            first = jnp.min(jnp.where(g_ == mx, blkf, 1e9), axis=0, keepdims=True)
            pick = jnp.logical_and(blkf == first, past)
            selected = jnp.logical_or(selected, pick)
            g_ = jnp.where(pick, -jnp.inf, g_)
        cfar = jnp.where(colh == 0, rb_ref[far_row, kh * G],
                         jnp.where(colh == 1, rb_ref[far_row, kh * G + 1],
                                   jnp.where(colh == 2, rb_ref[far_row, kh * G + 2],
                                             rb_ref[far_row, kh * G + 3])))
        sel_bias = jnp.where(selected, jnp.where(blk < i - 1, cfar * LOG2E, 0.0), NEG_BIG)
        qat_ref[kh, MOBA_DH:MOBA_DH + MOBA_NBLK, :] = sel_bias.astype(BF16)

    for kh in range(KVH):
        s_near = _dot(kaug_ref[kh * NP1 + i], qat_ref[kh]) + bias_ref[0:BLK, kh * R:(kh + 1) * R]
        s_ref[kh, 0:BLK, :] = s_near
        smax_ref[kh] = jnp.maximum(own_max[kh], jnp.max(s_near, axis=0, keepdims=True))

    def produce(slot, kh, start):
        k2 = kaug_ref[pl.ds(kh * NP1 + start, 2)].reshape(2 * BLK, 2 * LANES)
        s = _dot(k2, qat_ref[kh])
        s_ref[slot * KVH + kh] = s
        smax_ref[slot * KVH + kh] = jnp.max(s, axis=0, keepdims=True)

    def consume(slot, kh, start, first=False):
        m_new = smax_ref[slot * KVH + kh]
        if not first:
            m_old = m_ref[kh]
            m_new = jnp.maximum(m_old, m_new)
        p = jnp.exp2(s_ref[slot * KVH + kh] - m_new).astype(BF16)
        pv = (_dot(vt_ref[kh * NP1 + start], p[0:BLK])
              + _dot(vt_ref[kh * NP1 + start + 1], p[BLK:2 * BLK]))
        acc_ref[kh] = pv if first else jnp.exp2(m_old - m_new) * acc_ref[kh] + pv
        m_ref[kh] = m_new

    n_far = jnp.maximum(i - 1, 0)
    n_pairs = (n_far + 1) // 2

    def far_start(t):
        return jnp.maximum(n_far - 2 * t - 1, 0)

    def far_step(t, slot):
        for kh in range(KVH):
            produce(1 - slot, kh, far_start(t + 1))
        for kh in range(KVH):
            consume(slot, kh, far_start(t))

    for kh in range(KVH):
        produce(1, kh, far_start(0))
    for kh in range(KVH):
        consume(0, kh, i, first=True)

    def far_four(u, carry):
        for d in range(4):
            far_step(4 * u + d, (d + 1) % 2)
        return carry

    lax.fori_loop(0, n_pairs // 4, far_four, 0)
    done = (n_pairs // 4) * 4

    @pl.when(n_pairs - done >= 2)
    def _():
        far_step(done, 1)
        far_step(done + 1, 0)

    @pl.when(n_pairs % 2 == 1)
    def _():
        far_step(n_pairs - 1, 1)

    for kh in range(KVH):
        acc = acc_ref[kh]
        o = (acc[0:MOBA_DH] * (1.0 / acc[MOBA_DH:MOBA_DH + 1])).T
        for g in range(G):
            cols = slice((kh * G + g) * MOBA_DH, (kh * G + g + 1) * MOBA_DH)
            gt = gate_ref[:, cols].astype(F32)
            o_ref[:, cols] = (o[g * BLK:(g + 1) * BLK] * _silu(gt)).astype(o_ref.dtype)


def _moba(cq, ck, cv, cgate, rel_bias):
    R = MOBA_GROUP * MOBA_BLOCK
    KVH = MOBA_KV_HEADS
    NP1 = MOBA_NBLK + 1
    qspec = pl.BlockSpec((MOBA_BLOCK, C_WIDTH), lambda b, i: (b * MOBA_NBLK + i, 0))
    kvspec = pl.BlockSpec((SEQ, KVH * MOBA_DH), lambda b, i: (b, 0), pipeline_mode=pl.Buffered(1))
    return pl.pallas_call(
        _moba_kernel, grid=(BATCH, MOBA_NBLK),
        in_specs=[pl.BlockSpec(memory_space=pltpu.SMEM), qspec, kvspec, kvspec, qspec],
        out_specs=qspec,
        out_shape=jax.ShapeDtypeStruct((TOKENS, C_WIDTH), BF16),
        scratch_shapes=[pltpu.VMEM((KVH * NP1, MOBA_BLOCK, 2 * LANES), BF16),
                        pltpu.VMEM((KVH * NP1, MOBA_VT_ROWS, MOBA_BLOCK), BF16),
                        pltpu.VMEM((KVH * MOBA_NBLK, MOBA_DH), F32),
                        pltpu.VMEM((2 * MOBA_BLOCK, KVH * R), F32),
                        pltpu.VMEM((KVH, 2 * LANES, R), BF16),
                        pltpu.VMEM((2 * KVH, 2 * MOBA_BLOCK, R), F32),
                        pltpu.VMEM((2 * KVH, 1, R), F32),
                        pltpu.VMEM((KVH, 1, R), F32),
                        pltpu.VMEM((KVH, MOBA_VT_ROWS, R), F32)],
        compiler_params=pltpu.CompilerParams(
            dimension_semantics=("arbitrary", "arbitrary"), vmem_limit_bytes=MOBA_VMEM_LIMIT),
        name="moba",
    )(rel_bias, cq, ck, cv, cgate)


def kernel(x, norm_gain, final_gain, rel_bias, w_in_even, gla_w_up, gla_b_up, gla_norm_gain, swa_sinks,
           w_out_even, w_in_odd, w_out_odd):
    assert x.shape == (BATCH, SEQ, D_MODEL)
    h0 = x.reshape(TOKENS, D_MODEL)

    we = w_in_even[0]
    splits = np.cumsum([0, GLA_QK, GLA_QK, A_WIDTH, GLA_RANK, A_WIDTH, B_WIDTH, SWA_KV_HEADS * SWA_DH,
                        SWA_KV_HEADS * SWA_DH, B_WIDTH])
    w_aq, w_ak, w_av, w_down, w_agate, w_bq = [we[:, splits[k]:splits[k + 1]] for k in range(6)]
    w_bkv = we[:, splits[6]:splits[8]]
    w_bgate = we[:, splits[8]:splits[9]]
    w_down = jnp.pad(jnp.concatenate([w_down] * 3, axis=1), ((0, 0), (0, LANES - 3 * GLA_RANK)))
    proj_ws = [w.astype(BF16) for w in (w_aq, w_ak, w_av, w_down, w_agate, w_bq, w_bkv, w_bgate)]
    proj_dt = [BF16, BF16, BF16, F32, BF16, BF16, BF16, BF16]
    scales = [GLA_DK ** -0.5, 1.0, 1.0, 1.0, 1.0, SWA_DH ** -0.5 * LOG2E, 1.0, 1.0]
    aq, ak, av, adown, agate, bq, bkv, bgate = _layer_io(
        h0, [], norm_gain[0:1], proj_ws, proj_dt, scales, write_h=False)

    w_up_hi = gla_w_up[0].astype(BF16)
    w_up_lo = (gla_w_up[0] - w_up_hi.astype(F32)).astype(BF16)
    w_up = jnp.pad(jnp.concatenate([w_up_hi, w_up_hi, w_up_lo], axis=0), ((0, LANES - 3 * GLA_RANK), (0, 0)))
    oa = _gla(aq, ak, av, agate, adown, w_up, gla_b_up[0:1], gla_norm_gain[0:1])
    ob = _swa(bq, bkv, bgate, rel_bias, swa_sinks[0])

    wo = w_out_even[0].astype(BF16)
    wi = w_in_odd[0]
    osp = np.cumsum([0, C_WIDTH, MOBA_KV_HEADS * MOBA_DH, MOBA_KV_HEADS * MOBA_DH, C_WIDTH])
    odd_ws = [wi[:, osp[k]:osp[k + 1]].astype(BF16) for k in range(4)]
    h1, cq, ck, cv, cgate = _layer_io(
        h0, [(oa, wo[:A_WIDTH]), (ob, wo[A_WIDTH:])], norm_gain[1:2], odd_ws, [BF16] * 4,
        [MOBA_DH ** -0.5 * LOG2E, 1.0, 1.0, 1.0], write_h=True)

    oc = _moba(cq, ck, cv, cgate, rel_bias)

    (out,) = _layer_io(h1, [(oc, w_out_odd[0].astype(BF16))], final_gain.reshape(1, D_MODEL), [], [], [],
                       write_h=False)
    return out.reshape(BATCH, SEQ, D_MODEL)
```

```python
import functools
import math

import numpy as np
import jax
import jax.numpy as jnp
from jax import lax
from jax.experimental import pallas as pl
from jax.experimental.pallas import tpu as pltpu

D_MODEL = 1024
BATCH = 2
SEQ = 8192
TOKENS = BATCH * SEQ

REL_BUCKETS = 32
REL_MAX_EXACT = 16
REL_MAX_DIST = 128
ATTN_HEADS = 8

GLA_HEADS = 4
GLA_DK = 64
GLA_DV = 128
GLA_RANK = 16
GLA_TAU = 16.0
GLA_CHUNK = 64

SWA_HEADS = 8
SWA_KV_HEADS = 2
SWA_DH = 64
SWA_WINDOW = 128

MOBA_HEADS = 8
MOBA_KV_HEADS = 2
MOBA_DH = 128
MOBA_BLOCK = 256
MOBA_TOPK = 3
MOBA_GROUP = MOBA_HEADS // MOBA_KV_HEADS
MOBA_NBLK = SEQ // MOBA_BLOCK
MOBA_VT_ROWS = MOBA_DH + 16
LOG2E = math.log2(math.e)
SWA_VT_ROWS = SWA_KV_HEADS * SWA_DH + 16

EPS = 1e-6
LANES = 128
NEG_BIG = -1e30
VMEM_LIMIT = 48 * 1024 * 1024
MOBA_VMEM_LIMIT = 56 * 1024 * 1024

A_WIDTH = GLA_HEADS * GLA_DV
B_WIDTH = SWA_HEADS * SWA_DH
C_WIDTH = MOBA_HEADS * MOBA_DH
GLA_QK = GLA_HEADS * GLA_DK

ROW_TILE = 1024
GLA_TILE = 1024
SWA_TILE = 1024

F32 = jnp.float32
BF16 = jnp.bfloat16


def _dot(a, b):
    return jnp.dot(a, b, preferred_element_type=F32)


def _dot_nt(a, b):
    return lax.dot_general(a, b, (((1,), (1,)), ((), ())), preferred_element_type=F32)


def _dot_tn(a, b):
    return lax.dot_general(a, b, (((0,), (0,)), ((), ())), preferred_element_type=F32)


def _silu(x):
    return x * (1.0 / (1.0 + jnp.exp(-x)))


def _t5_thresholds():
    d = np.arange(REL_MAX_DIST + 1)
    nf = np.maximum(d, 1).astype(np.float32)
    large = REL_MAX_EXACT + (np.log(nf / np.float32(REL_MAX_EXACT))
                             / np.float32(math.log(REL_MAX_DIST / REL_MAX_EXACT))
                             * np.float32(REL_BUCKETS - REL_MAX_EXACT)).astype(np.int32)
    bucket = np.where(d < REL_MAX_EXACT, d, np.minimum(large, REL_BUCKETS - 1))
    assert np.all(np.diff(bucket) >= 0) and bucket[-1] == REL_BUCKETS - 1
    return [int(np.argmax(bucket >= b)) for b in range(REL_BUCKETS)]


_T5_THRESH = _t5_thresholds()


def _t5_bias(dist, rb_ref, head):
    out = jnp.full(dist.shape, rb_ref[0, head], F32)
    for b in range(1, REL_BUCKETS):
        out = jnp.where(dist >= _T5_THRESH[b], rb_ref[b, head], out)
    return out


def _layer_io_kernel(*refs, n_acc, n_proj, write_h, scales):
    h_ref = refs[0]
    pos = 1
    acc_pairs = []
    for _ in range(n_acc):
        acc_pairs.append((refs[pos], refs[pos + 1]))
        pos += 2
    gain_ref = refs[pos]
    pos += 1
    wp_refs = refs[pos:pos + n_proj]
    pos += n_proj
    out_refs = refs[pos:]

    h = h_ref[...]
    for a_ref, w_ref in acc_pairs:
        h = h + _dot(a_ref[...], w_ref[...])
    o = 0
    if write_h:
        out_refs[0][...] = h
        o = 1
    y = h * lax.rsqrt(jnp.mean(h * h, axis=-1, keepdims=True) + EPS) * gain_ref[...]
    if n_proj == 0:
        out_refs[o][...] = y
        return
    yb = y.astype(BF16)
    for k in range(n_proj):
        r = _dot(yb, wp_refs[k][...])
        if scales[k] != 1.0:
            r = r * scales[k]
        out_refs[o + k][...] = r.astype(out_refs[o + k].dtype)


def _layer_io(h, acc_pairs, gain, proj_ws, proj_dtypes, scales, write_h):
    n_rows = h.shape[0]
    grid = (n_rows // ROW_TILE,)
    row_spec = lambda n: pl.BlockSpec((ROW_TILE, n), lambda i: (i, 0))
    full_spec = lambda a: pl.BlockSpec(a.shape, lambda i: (0,) * a.ndim, pipeline_mode=pl.Buffered(1))

    args = [h]
    in_specs = [row_spec(D_MODEL)]
    for a, w in acc_pairs:
        args += [a, w]
        in_specs += [row_spec(a.shape[1]), full_spec(w)]
    args.append(gain)
    in_specs.append(full_spec(gain))
    for w in proj_ws:
        args.append(w)
        in_specs.append(full_spec(w))

    out_shape, out_specs = [], []
    if write_h:
        out_shape.append(jax.ShapeDtypeStruct((n_rows, D_MODEL), F32))
        out_specs.append(row_spec(D_MODEL))
    if proj_ws:
        for w, dt in zip(proj_ws, proj_dtypes):
            out_shape.append(jax.ShapeDtypeStruct((n_rows, w.shape[1]), dt))
            out_specs.append(row_spec(w.shape[1]))
    else:
        out_shape.append(jax.ShapeDtypeStruct((n_rows, D_MODEL), F32))
        out_specs.append(row_spec(D_MODEL))

    kern = functools.partial(_layer_io_kernel, n_acc=len(acc_pairs), n_proj=len(proj_ws),
                             write_h=write_h, scales=tuple(scales))
    return pl.pallas_call(
        kern, grid=grid, in_specs=in_specs, out_specs=out_specs, out_shape=out_shape,
        compiler_params=pltpu.CompilerParams(dimension_semantics=("arbitrary",),
                                             vmem_limit_bytes=VMEM_LIMIT),
        name="layer_io",
    )(*args)


def _gla_init(st_ref):
    @pl.when(pl.program_id(1) == 0)
    def _():
        st_ref[...] = jnp.zeros_like(st_ref)


def _gla_steps(q_ref, k_ref, v_ref, gate_ref, down_ref, wup_ref, bup_ref, gain_ref, o_ref, st_ref):
    C = GLA_CHUNK

    a = down_ref[...]
    a_hi = a.astype(BF16)
    a_lo = (a - a_hi.astype(F32)).astype(BF16)
    lane_a = lax.broadcasted_iota(jnp.int32, a.shape, 1) // GLA_RANK
    z = _dot(jnp.where(lane_a == 1, a_lo, a_hi), wup_ref[...]) + bup_ref[...]
    log_a = (jnp.minimum(z, 0.0) - jnp.log(1.0 + jnp.exp(-jnp.abs(z)))) * (1.0 / GLA_TAU)

    r_i = lax.broadcasted_iota(jnp.int32, (C, C), 0)
    c_i = lax.broadcasted_iota(jnp.int32, (C, C), 1)
    tri = (c_i <= r_i).astype(BF16)
    lane_qk = lax.broadcasted_iota(jnp.int32, (C, GLA_QK), 1)
    head_masks = [(lane_qk // GLA_DK) == h for h in range(GLA_HEADS)]
    rs = lax.broadcasted_iota(jnp.int32, (GLA_HEADS * C, C), 0)
    cs = lax.broadcasted_iota(jnp.int32, (GLA_HEADS * C, C), 1)
    causal = (rs % C) >= cs
    st_r = lax.broadcasted_iota(jnp.int32, (A_WIDTH, GLA_QK), 0)
    st_c = lax.broadcasted_iota(jnp.int32, (A_WIDTH, GLA_QK), 1)
    same_head = (st_r // GLA_DV) == (st_c // GLA_DK)
    gain = gain_ref[...]

    def chunk(c):
        rows = slice(c * C, (c + 1) * C)
        g = log_a[rows]
        g_hi = g.astype(BF16)
        g_lo = (g - g_hi.astype(F32)).astype(BF16)
        b = _dot(tri, g_hi) + _dot(tri, g_lo)
        b_last = b[C - 1:C]
        q = q_ref[rows, :].astype(F32)
        k = k_ref[rows, :].astype(F32)
        v = v_ref[rows, :]
        q_e = q * jnp.exp(b)
        k_e = (k * jnp.exp(-b)).astype(BF16)
        k_l = (k * jnp.exp(b_last - b)).astype(BF16)
        decay = jnp.exp(b_last)

        q_stack = jnp.concatenate([jnp.where(m, q_e, 0.0) for m in head_masks], axis=0).astype(BF16)
        att = jnp.where(causal, _dot_nt(q_stack, k_e), 0.0).astype(BF16)
        o_full = _dot(att, v)
        o_intra = jnp.concatenate(
            [o_full[h * C:(h + 1) * C, h * GLA_DV:(h + 1) * GLA_DV] for h in range(GLA_HEADS)], axis=1)
        st = st_ref[...]
        o = o_intra + _dot_nt(q_e.astype(BF16), st.astype(BF16))
        kv_t = _dot_tn(v, k_l)
        st_ref[...] = st * decay + jnp.where(same_head, kv_t, 0.0)

        outs = []
        for h in range(GLA_HEADS):
            oh = o[:, h * GLA_DV:(h + 1) * GLA_DV]
            outs.append(oh * lax.rsqrt(jnp.mean(oh * oh, axis=-1, keepdims=True) + EPS) * gain)
        on = jnp.concatenate(outs, axis=1)
        o_ref[rows, :] = (on * _silu(gate_ref[rows, :].astype(F32))).astype(o_ref.dtype)

    return [functools.partial(chunk, c) for c in range(GLA_TILE // C)]


def _swa_init(rb_ref, bias_ref):
    W = SWA_WINDOW

    @pl.when(jnp.logical_and(pl.program_id(0) == 0, pl.program_id(1) == 0))
    def _():
        kj = lax.broadcasted_iota(jnp.int32, (2 * W, W), 0)
        qi = lax.broadcasted_iota(jnp.int32, (2 * W, W), 1)
        dist = qi + W - kj
        in_win = jnp.logical_and(dist >= 0, dist < W)
        for h in range(SWA_HEADS):
            bias_ref[:, h * W:(h + 1) * W] = jnp.where(in_win, _t5_bias(dist, rb_ref, h) * LOG2E, -jnp.inf)


def _swa_steps(sink_ref, q_ref, kp_ref, k_ref, vp_ref, v_ref, gate_ref, o_ref, bias_ref):
    W = SWA_WINDOW
    H = SWA_HEADS
    DH = SWA_DH
    G = SWA_HEADS // SWA_KV_HEADS
    t = pl.program_id(1)

    kcat = jnp.concatenate([kp_ref[...], k_ref[...]], axis=0)
    vcat = jnp.concatenate([vp_ref[...], v_ref[...]], axis=0)
    vt = vcat.astype(F32).T.astype(BF16)
    ones_rows = jnp.ones((SWA_VT_ROWS - LANES, 2 * W), BF16)
    q_t = q_ref[...].astype(F32).T.astype(BF16)
    zeros = jnp.zeros((DH, W), BF16)
    colh = lax.broadcasted_iota(jnp.int32, (1, H * W), 1) // W
    sink = jnp.full((1, H * W), sink_ref[H - 1], F32)
    for h in range(H - 2, -1, -1):
        sink = jnp.where(colh == h, sink_ref[h], sink)
    sink = sink * LOG2E
    key_prev = lax.broadcasted_iota(jnp.int32, (2 * W, H * W), 0) < W

    def block(blk):
        tok = slice(blk * W, (blk + 1) * W)
        win = slice(blk * W, blk * W + 2 * W)
        pieces = []
        for h in range(H):
            piece = q_t[h * DH:(h + 1) * DH, tok]
            pieces.append(jnp.concatenate([piece, zeros] if h < G else [zeros, piece], axis=0))
        q_pad = jnp.concatenate(pieces, axis=1)
        s = _dot(kcat[win], q_pad) + bias_ref[...]
        if blk == 0:
            s = jnp.where(jnp.logical_and(t == 0, key_prev), -jnp.inf, s)
        m = jnp.maximum(jnp.max(s, axis=0, keepdims=True), sink)
        p = jnp.exp2(s - m).astype(BF16)
        pv = _dot(jnp.concatenate([vt[:, win], ones_rows], axis=0), p)
        inv = 1.0 / (pv[LANES:LANES + 1] + jnp.exp2(sink - m))
        o_t = jnp.concatenate(
            [pv[(h // G) * DH:(h // G + 1) * DH, h * W:(h + 1) * W] * inv[:, h * W:(h + 1) * W]
             for h in range(H)], axis=0)
        gate = gate_ref[tok, :].astype(F32)
        o_ref[tok, :] = (o_t.T * _silu(gate)).astype(o_ref.dtype)

    return [functools.partial(block, blk) for blk in range(SWA_TILE // W)]


def _gla_swa_kernel(aq_ref, ak_ref, av_ref, agate_ref, adown_ref, wup_ref, bup_ref, gain_ref,
                    sink_ref, rb_ref, bq_ref, kp_ref, k_ref, vp_ref, v_ref, bgate_ref,
                    oa_ref, ob_ref, st_ref, bias_ref):
    _gla_init(st_ref)
    _swa_init(rb_ref, bias_ref)
    gla = _gla_steps(aq_ref, ak_ref, av_ref, agate_ref, adown_ref, wup_ref, bup_ref, gain_ref, oa_ref, st_ref)
    swa = _swa_steps(sink_ref, bq_ref, kp_ref, k_ref, vp_ref, v_ref, bgate_ref, ob_ref, bias_ref)
    for step in gla + swa:
        step()


def _gla_swa(aq, ak, av, agate, adown, w_up, b_up, gain, bq, bkv, bgate, rel_bias, sinks):
    assert GLA_TILE == SWA_TILE
    nt = SEQ // SWA_TILE
    per = SWA_TILE // SWA_WINDOW
    row = lambda n: pl.BlockSpec((SWA_TILE, n), lambda b, t: (b * nt + t, 0))
    full = lambda a: pl.BlockSpec(a.shape, lambda b, t: (0,) * a.ndim)
    own = lambda c: pl.BlockSpec((SWA_TILE, LANES), lambda b, t: (b * nt + t, c))
    prev = lambda c: pl.BlockSpec((SWA_WINDOW, LANES),
                                  lambda b, t: (jnp.maximum((b * nt + t) * per - 1, 0), c))
    smem = pl.BlockSpec(memory_space=pltpu.SMEM)
    return pl.pallas_call(
        _gla_swa_kernel, grid=(BATCH, nt),
        in_specs=[row(GLA_QK), row(GLA_QK), row(A_WIDTH), row(A_WIDTH), row(LANES),
                  full(w_up), full(b_up), full(gain),
                  smem, smem, row(B_WIDTH), prev(0), own(0), prev(1), own(1), row(B_WIDTH)],
        out_specs=[row(A_WIDTH), row(B_WIDTH)],
        out_shape=[jax.ShapeDtypeStruct((TOKENS, A_WIDTH), BF16),
                   jax.ShapeDtypeStruct((TOKENS, B_WIDTH), BF16)],
        scratch_shapes=[pltpu.VMEM((A_WIDTH, GLA_QK), F32),
                        pltpu.VMEM((2 * SWA_WINDOW, SWA_HEADS * SWA_WINDOW), F32)],
        compiler_params=pltpu.CompilerParams(dimension_semantics=("arbitrary", "arbitrary"),
                                             vmem_limit_bytes=VMEM_LIMIT),
        name="gla_swa",
    )(aq, ak, av, agate, adown, w_up, b_up, gain, sinks, rel_bias, bq, bkv, bkv, bkv, bkv, bgate)


def _moba_kernel(rb_ref, q_ref, k_ref, v_ref, gate_ref, o_ref,
                 kaug_ref, vt_ref, km_ref, bias_ref, qat_ref, s_ref, smax_ref, m_ref, acc_ref):
    BLK = MOBA_BLOCK
    G = MOBA_GROUP
    KVH = MOBA_KV_HEADS
    R = G * BLK
    NP1 = MOBA_NBLK + 1
    i = pl.program_id(1)

    @pl.when(i == 0)
    def _():
        lane = lax.broadcasted_iota(jnp.int32, (BLK, LANES), 1)
        ones_rows = jnp.ones((MOBA_VT_ROWS - MOBA_DH, BLK), BF16)

        def prep(j, carry):
            rows = pl.ds(pl.multiple_of(j * BLK, BLK), BLK)
            onehot = jnp.where(lane == j, 1.0, 0.0).astype(BF16)
            for kh in range(KVH):
                kb = k_ref[rows, kh * MOBA_DH:(kh + 1) * MOBA_DH]
                kaug_ref[kh * NP1 + j + 1, :, 0:LANES] = kb
                kaug_ref[kh * NP1 + j + 1, :, LANES:2 * LANES] = onehot
                km_ref[pl.ds(kh * MOBA_NBLK + j, 1), :] = jnp.mean(kb.astype(F32), axis=0, keepdims=True)
                vb = v_ref[rows, kh * MOBA_DH:(kh + 1) * MOBA_DH]
                vt_ref[kh * NP1 + j + 1, 0:MOBA_DH, :] = vb.astype(F32).T.astype(BF16)
                vt_ref[kh * NP1 + j + 1, MOBA_DH:, :] = ones_rows
            return carry

        lax.fori_loop(0, MOBA_NBLK, prep, 0)
        for kh in range(KVH):
            kaug_ref[kh * NP1, :, 0:LANES] = jnp.zeros((BLK, LANES), BF16)
            kaug_ref[kh * NP1, :, LANES:2 * LANES] = jnp.where(lane == MOBA_NBLK, 1.0, 0.0).astype(BF16)
            vt_ref[kh * NP1] = jnp.zeros((MOBA_VT_ROWS, BLK), BF16)
            qat_ref[kh, MOBA_DH + MOBA_NBLK:, :] = jnp.full((2 * LANES - MOBA_DH - MOBA_NBLK, R), NEG_BIG, BF16)

    @pl.when(jnp.logical_and(pl.program_id(0) == 0, i == 0))
    def _():
        tk = lax.broadcasted_iota(jnp.int32, (BLK, BLK), 0)
        tq = lax.broadcasted_iota(jnp.int32, (BLK, BLK), 1)
        d_own = tq - tk
        for head in range(MOBA_HEADS):
            cols = slice(head * BLK, (head + 1) * BLK)
            bias_ref[0:BLK, cols] = _t5_bias(d_own + BLK, rb_ref, head) * LOG2E
            bias_ref[BLK:2 * BLK, cols] = jnp.where(d_own >= 0, _t5_bias(d_own, rb_ref, head) * LOG2E, -jnp.inf)

    blk = lax.broadcasted_iota(jnp.int32, (MOBA_NBLK, R), 0)
    blkf = blk.astype(F32)
    past = blk < i
    colh = lax.broadcasted_iota(jnp.int32, (1, R), 1) // BLK
    far_row = REL_BUCKETS - 1

    own_max = []
    for kh in range(KVH):
        q = jnp.concatenate([q_ref[:, (kh * G + g) * MOBA_DH:(kh * G + g + 1) * MOBA_DH] for g in range(G)],
                            axis=0)
        q_t = q.astype(F32).T.astype(BF16)
        qat_ref[kh, 0:MOBA_DH, :] = q_t
        s_own = (_dot(kaug_ref[kh * NP1 + i + 1, :, 0:MOBA_DH], q_t)
                 + bias_ref[BLK:2 * BLK, kh * R:(kh + 1) * R])
        s_ref[kh, BLK:2 * BLK, :] = s_own
        own_max.append(jnp.max(s_own, axis=0, keepdims=True))

        km = km_ref[kh * MOBA_NBLK:(kh + 1) * MOBA_NBLK, :]
        km_hi = km.astype(BF16)
        km_lo = (km - km_hi.astype(F32)).astype(BF16)
        gate = _dot(km_hi, q_t) + _dot(km_lo, q_t)
        g_ = jnp.where(past, gate, -jnp.inf)
        selected = jnp.zeros((MOBA_NBLK, R), dtype=jnp.bool_)
        for _ in range(MOBA_TOPK):
            mx = jnp.max(g_, axis=0, keepdims=True)
            first = jnp.min(jnp.where(g_ == mx, blkf, 1e9), axis=0, keepdims=True)
            pick = jnp.logical_and(blkf == first, past)
            selected = jnp.logical_or(selected, pick)
            g_ = jnp.where(pick, -jnp.inf, g_)
        cfar = jnp.where(colh == 0, rb_ref[far_row, kh * G],
                         jnp.where(colh == 1, rb_ref[far_row, kh * G + 1],
                                   jnp.where(colh == 2, rb_ref[far_row, kh * G + 2],
                                             rb_ref[far_row, kh * G + 3])))
        sel_bias = jnp.where(selected, jnp.where(blk < i - 1, cfar * LOG2E, 0.0), NEG_BIG)
        qat_ref[kh, MOBA_DH:MOBA_DH + MOBA_NBLK, :] = sel_bias.astype(BF16)

    for kh in range(KVH):
        s_near = _dot(kaug_ref[kh * NP1 + i], qat_ref[kh]) + bias_ref[0:BLK, kh * R:(kh + 1) * R]
        s_ref[kh, 0:BLK, :] = s_near
        smax_ref[kh] = jnp.maximum(own_max[kh], jnp.max(s_near, axis=0, keepdims=True))

    def produce(slot, kh, start):
        k2 = kaug_ref[pl.ds(kh * NP1 + start, 2)].reshape(2 * BLK, 2 * LANES)
        s = _dot(k2, qat_ref[kh])
        s_ref[slot * KVH + kh] = s
        smax_ref[slot * KVH + kh] = jnp.max(s, axis=0, keepdims=True)

    def consume(slot, kh, start, first=False):
        m_new = smax_ref[slot * KVH + kh]
        if not first:
            m_old = m_ref[kh]
            m_new = jnp.maximum(m_old, m_new)
        p = jnp.exp2(s_ref[slot * KVH + kh] - m_new).astype(BF16)
        pv = (_dot(vt_ref[kh * NP1 + start], p[0:BLK])
              + _dot(vt_ref[kh * NP1 + start + 1], p[BLK:2 * BLK]))
        acc_ref[kh] = pv if first else jnp.exp2(m_old - m_new) * acc_ref[kh] + pv
        m_ref[kh] = m_new

    n_far = jnp.maximum(i - 1, 0)
    n_pairs = (n_far + 1) // 2

    def far_start(t):
        return jnp.maximum(n_far - 2 * t - 1, 0)

    def far_step(t, slot):
        for kh in range(KVH):
            produce(1 - slot, kh, far_start(t + 1))
        for kh in range(KVH):
            consume(slot, kh, far_start(t))

    for kh in range(KVH):
        produce(1, kh, far_start(0))
    for kh in range(KVH):
        consume(0, kh, i, first=True)

    def far_four(u, carry):
        for d in range(4):
            far_step(4 * u + d, (d + 1) % 2)
        return carry

    lax.fori_loop(0, n_pairs // 4, far_four, 0)
    done = (n_pairs // 4) * 4

    @pl.when(n_pairs - done >= 2)
    def _():
        far_step(done, 1)
        far_step(done + 1, 0)

    @pl.when(n_pairs % 2 == 1)
    def _():
        far_step(n_pairs - 1, 1)

    for kh in range(KVH):
        acc = acc_ref[kh]
        o = (acc[0:MOBA_DH] * (1.0 / acc[MOBA_DH:MOBA_DH + 1])).T
        for g in range(G):
            cols = slice((kh * G + g) * MOBA_DH, (kh * G + g + 1) * MOBA_DH)
            gt = gate_ref[:, cols].astype(F32)
            o_ref[:, cols] = (o[g * BLK:(g + 1) * BLK] * _silu(gt)).astype(o_ref.dtype)


def _moba(cq, ck, cv, cgate, rel_bias):
    R = MOBA_GROUP * MOBA_BLOCK
    KVH = MOBA_KV_HEADS
    NP1 = MOBA_NBLK + 1
    qspec = pl.BlockSpec((MOBA_BLOCK, C_WIDTH), lambda b, i: (b * MOBA_NBLK + i, 0))
    kvspec = pl.BlockSpec((SEQ, KVH * MOBA_DH), lambda b, i: (b, 0), pipeline_mode=pl.Buffered(1))
    return pl.pallas_call(
        _moba_kernel, grid=(BATCH, MOBA_NBLK),
        in_specs=[pl.BlockSpec(memory_space=pltpu.SMEM), qspec, kvspec, kvspec, qspec],
        out_specs=qspec,
        out_shape=jax.ShapeDtypeStruct((TOKENS, C_WIDTH), BF16),
        scratch_shapes=[pltpu.VMEM((KVH * NP1, MOBA_BLOCK, 2 * LANES), BF16),
                        pltpu.VMEM((KVH * NP1, MOBA_VT_ROWS, MOBA_BLOCK), BF16),
                        pltpu.VMEM((KVH * MOBA_NBLK, MOBA_DH), F32),
                        pltpu.VMEM((2 * MOBA_BLOCK, KVH * R), F32),
                        pltpu.VMEM((KVH, 2 * LANES, R), BF16),
                        pltpu.VMEM((2 * KVH, 2 * MOBA_BLOCK, R), F32),
                        pltpu.VMEM((2 * KVH, 1, R), F32),
                        pltpu.VMEM((KVH, 1, R), F32),
                        pltpu.VMEM((KVH, MOBA_VT_ROWS, R), F32)],
        compiler_params=pltpu.CompilerParams(
            dimension_semantics=("arbitrary", "arbitrary"), vmem_limit_bytes=MOBA_VMEM_LIMIT),
        name="moba",
    )(rel_bias, cq, ck, cv, cgate)


def kernel(x, norm_gain, final_gain, rel_bias, w_in_even, gla_w_up, gla_b_up, gla_norm_gain, swa_sinks,
           w_out_even, w_in_odd, w_out_odd):
    assert x.shape == (BATCH, SEQ, D_MODEL)
    h0 = x.reshape(TOKENS, D_MODEL)

    we = w_in_even[0]
    splits = np.cumsum([0, GLA_QK, GLA_QK, A_WIDTH, GLA_RANK, A_WIDTH, B_WIDTH, SWA_KV_HEADS * SWA_DH,
                        SWA_KV_HEADS * SWA_DH, B_WIDTH])
    w_aq, w_ak, w_av, w_down, w_agate, w_bq = [we[:, splits[k]:splits[k + 1]] for k in range(6)]
    w_bkv = we[:, splits[6]:splits[8]]
    w_bgate = we[:, splits[8]:splits[9]]
    w_down = jnp.pad(jnp.concatenate([w_down] * 3, axis=1), ((0, 0), (0, LANES - 3 * GLA_RANK)))
    proj_ws = [w.astype(BF16) for w in (w_aq, w_ak, w_av, w_down, w_agate, w_bq, w_bkv, w_bgate)]
    proj_dt = [BF16, BF16, BF16, F32, BF16, BF16, BF16, BF16]
    scales = [GLA_DK ** -0.5, 1.0, 1.0, 1.0, 1.0, SWA_DH ** -0.5 * LOG2E, 1.0, 1.0]
    aq, ak, av, adown, agate, bq, bkv, bgate = _layer_io(
        h0, [], norm_gain[0:1], proj_ws, proj_dt, scales, write_h=False)

    w_up_hi = gla_w_up[0].astype(BF16)
    w_up_lo = (gla_w_up[0] - w_up_hi.astype(F32)).astype(BF16)
    w_up = jnp.pad(jnp.concatenate([w_up_hi, w_up_hi, w_up_lo], axis=0), ((0, LANES - 3 * GLA_RANK), (0, 0)))
    oa, ob = _gla_swa(aq, ak, av, agate, adown, w_up, gla_b_up[0:1], gla_norm_gain[0:1],
                      bq, bkv, bgate, rel_bias, swa_sinks[0])

    wo = w_out_even[0].astype(BF16)
    wi = w_in_odd[0]
    osp = np.cumsum([0, C_WIDTH, MOBA_KV_HEADS * MOBA_DH, MOBA_KV_HEADS * MOBA_DH, C_WIDTH])
    odd_ws = [wi[:, osp[k]:osp[k + 1]].astype(BF16) for k in range(4)]
    h1, cq, ck, cv, cgate = _layer_io(
        h0, [(oa, wo[:A_WIDTH]), (ob, wo[A_WIDTH:])], norm_gain[1:2], odd_ws, [BF16] * 4,
        [MOBA_DH ** -0.5 * LOG2E, 1.0, 1.0, 1.0], write_h=True)

    oc = _moba(cq, ck, cv, cgate, rel_bias)

    (out,) = _layer_io(h1, [(oc, w_out_odd[0].astype(BF16))], final_gain.reshape(1, D_MODEL), [], [], [],
                       write_h=False)
    return out.reshape(BATCH, SEQ, D_MODEL)
```

```python
import functools
import math

import numpy as np
import jax
import jax.numpy as jnp
from jax import lax
from jax.experimental import pallas as pl
from jax.experimental.pallas import tpu as pltpu

D_MODEL = 1024
BATCH = 2
SEQ = 8192
TOKENS = BATCH * SEQ

REL_BUCKETS = 32
REL_MAX_EXACT = 16
REL_MAX_DIST = 128
ATTN_HEADS = 8

GLA_HEADS = 4
GLA_DK = 64
GLA_DV = 128
GLA_RANK = 16
GLA_TAU = 16.0
GLA_CHUNK = 64

SWA_HEADS = 8
SWA_KV_HEADS = 2
SWA_DH = 64
SWA_WINDOW = 128

MOBA_HEADS = 8
MOBA_KV_HEADS = 2
MOBA_DH = 128
MOBA_BLOCK = 256
MOBA_TOPK = 3
MOBA_GROUP = MOBA_HEADS // MOBA_KV_HEADS
MOBA_NBLK = SEQ // MOBA_BLOCK
MOBA_VT_ROWS = MOBA_DH + 16
LOG2E = math.log2(math.e)
SWA_VT_ROWS = SWA_KV_HEADS * SWA_DH + 16

EPS = 1e-6
LANES = 128
NEG_BIG = -1e30
VMEM_LIMIT = 48 * 1024 * 1024
MOBA_VMEM_LIMIT = 56 * 1024 * 1024

A_WIDTH = GLA_HEADS * GLA_DV
B_WIDTH = SWA_HEADS * SWA_DH
C_WIDTH = MOBA_HEADS * MOBA_DH
GLA_QK = GLA_HEADS * GLA_DK

ROW_TILE = 1024
GLA_TILE = 1024
SWA_TILE = 1024

F32 = jnp.float32
BF16 = jnp.bfloat16


def _dot(a, b):
    return jnp.dot(a, b, preferred_element_type=F32)


def _dot_nt(a, b):
    return lax.dot_general(a, b, (((1,), (1,)), ((), ())), preferred_element_type=F32)


def _dot_tn(a, b):
    return lax.dot_general(a, b, (((0,), (0,)), ((), ())), preferred_element_type=F32)


def _silu(x):
    return x * (1.0 / (1.0 + jnp.exp(-x)))


def _t5_thresholds():
    d = np.arange(REL_MAX_DIST + 1)
    nf = np.maximum(d, 1).astype(np.float32)
    large = REL_MAX_EXACT + (np.log(nf / np.float32(REL_MAX_EXACT))
                             / np.float32(math.log(REL_MAX_DIST / REL_MAX_EXACT))
                             * np.float32(REL_BUCKETS - REL_MAX_EXACT)).astype(np.int32)
    bucket = np.where(d < REL_MAX_EXACT, d, np.minimum(large, REL_BUCKETS - 1))
    assert np.all(np.diff(bucket) >= 0) and bucket[-1] == REL_BUCKETS - 1
    return [int(np.argmax(bucket >= b)) for b in range(REL_BUCKETS)]


_T5_THRESH = _t5_thresholds()


def _t5_bias(dist, rb_ref, head):
    out = jnp.full(dist.shape, rb_ref[0, head], F32)
    for b in range(1, REL_BUCKETS):
        out = jnp.where(dist >= _T5_THRESH[b], rb_ref[b, head], out)
    return out


def _layer_io_kernel(*refs, n_acc, n_proj, write_h, scales):
    h_ref = refs[0]
    pos = 1
    acc_pairs = []
    for _ in range(n_acc):
        acc_pairs.append((refs[pos], refs[pos + 1]))
        pos += 2
    gain_ref = refs[pos]
    pos += 1
    wp_refs = refs[pos:pos + n_proj]
    pos += n_proj
    out_refs = refs[pos:]

    h = h_ref[...]
    for a_ref, w_ref in acc_pairs:
        h = h + _dot(a_ref[...], w_ref[...])
    o = 0
    if write_h:
        out_refs[0][...] = h
        o = 1
    y = h * lax.rsqrt(jnp.mean(h * h, axis=-1, keepdims=True) + EPS) * gain_ref[...]
    if n_proj == 0:
        out_refs[o][...] = y
        return
    yb = y.astype(BF16)
    for k in range(n_proj):
        r = _dot(yb, wp_refs[k][...])
        if scales[k] != 1.0:
            r = r * scales[k]
        out_refs[o + k][...] = r.astype(out_refs[o + k].dtype)


def _layer_io(h, acc_pairs, gain, proj_ws, proj_dtypes, scales, write_h):
    n_rows = h.shape[0]
    grid = (n_rows // ROW_TILE,)
    row_spec = lambda n: pl.BlockSpec((ROW_TILE, n), lambda i: (i, 0))
    full_spec = lambda a: pl.BlockSpec(a.shape, lambda i: (0,) * a.ndim, pipeline_mode=pl.Buffered(1))

    args = [h]
    in_specs = [row_spec(D_MODEL)]
    for a, w in acc_pairs:
        args += [a, w]
        in_specs += [row_spec(a.shape[1]), full_spec(w)]
    args.append(gain)
    in_specs.append(full_spec(gain))
    for w in proj_ws:
        args.append(w)
        in_specs.append(full_spec(w))

    out_shape, out_specs = [], []
    if write_h:
        out_shape.append(jax.ShapeDtypeStruct((n_rows, D_MODEL), F32))
        out_specs.append(row_spec(D_MODEL))
    if proj_ws:
        for w, dt in zip(proj_ws, proj_dtypes):
            out_shape.append(jax.ShapeDtypeStruct((n_rows, w.shape[1]), dt))
            out_specs.append(row_spec(w.shape[1]))
    else:
        out_shape.append(jax.ShapeDtypeStruct((n_rows, D_MODEL), F32))
        out_specs.append(row_spec(D_MODEL))

    kern = functools.partial(_layer_io_kernel, n_acc=len(acc_pairs), n_proj=len(proj_ws),
                             write_h=write_h, scales=tuple(scales))
    return pl.pallas_call(
        kern, grid=grid, in_specs=in_specs, out_specs=out_specs, out_shape=out_shape,
        compiler_params=pltpu.CompilerParams(dimension_semantics=("arbitrary",),
                                             vmem_limit_bytes=VMEM_LIMIT),
        name="layer_io",
    )(*args)


def _gla_init(st_ref):
    @pl.when(pl.program_id(1) == 0)
    def _():
        st_ref[...] = jnp.zeros_like(st_ref)


def _gla_steps(q_ref, k_ref, v_ref, gate_ref, down_ref, wup_ref, bup_ref, gain_ref, o_ref, st_ref):
    C = GLA_CHUNK

    a = down_ref[...]
    a_hi = a.astype(BF16)
    a_lo = (a - a_hi.astype(F32)).astype(BF16)
    lane_a = lax.broadcasted_iota(jnp.int32, a.shape, 1) // GLA_RANK
    z = _dot(jnp.where(lane_a == 1, a_lo, a_hi), wup_ref[...]) + bup_ref[...]
    log_a = (jnp.minimum(z, 0.0) - jnp.log(1.0 + jnp.exp(-jnp.abs(z)))) * (1.0 / GLA_TAU)

    r_i = lax.broadcasted_iota(jnp.int32, (C, C), 0)
    c_i = lax.broadcasted_iota(jnp.int32, (C, C), 1)
    tri = (c_i <= r_i).astype(BF16)
    lane_qk = lax.broadcasted_iota(jnp.int32, (C, GLA_QK), 1)
    head_masks = [(lane_qk // GLA_DK) == h for h in range(GLA_HEADS)]
    rs = lax.broadcasted_iota(jnp.int32, (GLA_HEADS * C, C), 0)
    cs = lax.broadcasted_iota(jnp.int32, (GLA_HEADS * C, C), 1)
    causal = (rs % C) >= cs
    st_r = lax.broadcasted_iota(jnp.int32, (A_WIDTH, GLA_QK), 0)
    st_c = lax.broadcasted_iota(jnp.int32, (A_WIDTH, GLA_QK), 1)
    same_head = (st_r // GLA_DV) == (st_c // GLA_DK)
    gain = gain_ref[...]

    def chunk(c):
        rows = slice(c * C, (c + 1) * C)
        g = log_a[rows]
        g_hi = g.astype(BF16)
        g_lo = (g - g_hi.astype(F32)).astype(BF16)
        b = _dot(tri, g_hi) + _dot(tri, g_lo)
        b_last = b[C - 1:C]
        q = q_ref[rows, :].astype(F32)
        k = k_ref[rows, :].astype(F32)
        v = v_ref[rows, :]
        q_e = q * jnp.exp(b)
        k_e = (k * jnp.exp(-b)).astype(BF16)
        k_l = (k * jnp.exp(b_last - b)).astype(BF16)
        decay = jnp.exp(b_last)

        q_stack = jnp.concatenate([jnp.where(m, q_e, 0.0) for m in head_masks], axis=0).astype(BF16)
        att = jnp.where(causal, _dot_nt(q_stack, k_e), 0.0).astype(BF16)
        o_full = _dot(att, v)
        o_intra = jnp.concatenate(
            [o_full[h * C:(h + 1) * C, h * GLA_DV:(h + 1) * GLA_DV] for h in range(GLA_HEADS)], axis=1)
        st = st_ref[...]
        o = o_intra + _dot_nt(q_e.astype(BF16), st.astype(BF16))
        kv_t = _dot_tn(v, k_l)
        st_ref[...] = st * decay + jnp.where(same_head, kv_t, 0.0)

        outs = []
        for h in range(GLA_HEADS):
            oh = o[:, h * GLA_DV:(h + 1) * GLA_DV]
            outs.append(oh * lax.rsqrt(jnp.mean(oh * oh, axis=-1, keepdims=True) + EPS) * gain)
        on = jnp.concatenate(outs, axis=1)
        o_ref[rows, :] = (on * _silu(gate_ref[rows, :].astype(F32))).astype(o_ref.dtype)

    return [functools.partial(chunk, c) for c in range(GLA_TILE // C)]


def _swa_init(rb_ref, bias_ref):
    W = SWA_WINDOW

    @pl.when(jnp.logical_and(pl.program_id(0) == 0, pl.program_id(1) == 0))
    def _():
        kj = lax.broadcasted_iota(jnp.int32, (2 * W, W), 0)
        qi = lax.broadcasted_iota(jnp.int32, (2 * W, W), 1)
        dist = qi + W - kj
        in_win = jnp.logical_and(dist >= 0, dist < W)
        for h in range(SWA_HEADS):
            bias_ref[:, h * W:(h + 1) * W] = jnp.where(in_win, _t5_bias(dist, rb_ref, h) * LOG2E, -jnp.inf)


def _swa_steps(sink_ref, q_ref, kp_ref, k_ref, vp_ref, v_ref, gate_ref, o_ref, bias_ref):
    W = SWA_WINDOW
    H = SWA_HEADS
    DH = SWA_DH
    G = SWA_HEADS // SWA_KV_HEADS
    t = pl.program_id(1)

    kcat = jnp.concatenate([kp_ref[...], k_ref[...]], axis=0)
    vcat = jnp.concatenate([vp_ref[...], v_ref[...]], axis=0)
    vt = vcat.astype(F32).T.astype(BF16)
    ones_rows = jnp.ones((SWA_VT_ROWS - LANES, 2 * W), BF16)
    q_t = q_ref[...].astype(F32).T.astype(BF16)
    zeros = jnp.zeros((DH, W), BF16)
    colh = lax.broadcasted_iota(jnp.int32, (1, H * W), 1) // W
    sink = jnp.full((1, H * W), sink_ref[H - 1], F32)
    for h in range(H - 2, -1, -1):
        sink = jnp.where(colh == h, sink_ref[h], sink)
    sink = sink * LOG2E
    key_prev = lax.broadcasted_iota(jnp.int32, (2 * W, H * W), 0) < W

    def block(blk):
        tok = slice(blk * W, (blk + 1) * W)
        win = slice(blk * W, blk * W + 2 * W)
        pieces = []
        for h in range(H):
            piece = q_t[h * DH:(h + 1) * DH, tok]
            pieces.append(jnp.concatenate([piece, zeros] if h < G else [zeros, piece], axis=0))
        q_pad = jnp.concatenate(pieces, axis=1)
        s = _dot(kcat[win], q_pad) + bias_ref[...]
        if blk == 0:
            s = jnp.where(jnp.logical_and(t == 0, key_prev), -jnp.inf, s)
        m = jnp.maximum(jnp.max(s, axis=0, keepdims=True), sink)
        p = jnp.exp2(s - m).astype(BF16)
        pv = _dot(jnp.concatenate([vt[:, win], ones_rows], axis=0), p)
        inv = 1.0 / (pv[LANES:LANES + 1] + jnp.exp2(sink - m))
        o_t = jnp.concatenate(
            [pv[(h // G) * DH:(h // G + 1) * DH, h * W:(h + 1) * W] * inv[:, h * W:(h + 1) * W]
             for h in range(H)], axis=0)
        gate = gate_ref[tok, :].astype(F32)
        o_ref[tok, :] = (o_t.T * _silu(gate)).astype(o_ref.dtype)

    return [functools.partial(block, blk) for blk in range(SWA_TILE // W)]


def _gla_swa_kernel(aq_ref, ak_ref, av_ref, agate_ref, adown_ref, wup_ref, bup_ref, gain_ref,
                    sink_ref, rb_ref, bq_ref, kp_ref, k_ref, vp_ref, v_ref, bgate_ref,
                    oa_ref, ob_ref, st_ref, bias_ref):
    _gla_init(st_ref)
    _swa_init(rb_ref, bias_ref)
    gla = _gla_steps(aq_ref, ak_ref, av_ref, agate_ref, adown_ref, wup_ref, bup_ref, gain_ref, oa_ref, st_ref)
    swa = _swa_steps(sink_ref, bq_ref, kp_ref, k_ref, vp_ref, v_ref, bgate_ref, ob_ref, bias_ref)
    for step in gla + swa:
        step()


def _gla_swa(aq, ak, av, agate, adown, w_up, b_up, gain, bq, bkv, bgate, rel_bias, sinks):
    assert GLA_TILE == SWA_TILE
    nt = SEQ // SWA_TILE
    per = SWA_TILE // SWA_WINDOW
    row = lambda n: pl.BlockSpec((SWA_TILE, n), lambda b, t: (b * nt + t, 0))
    full = lambda a: pl.BlockSpec(a.shape, lambda b, t: (0,) * a.ndim)
    own = lambda c: pl.BlockSpec((SWA_TILE, LANES), lambda b, t: (b * nt + t, c))
    prev = lambda c: pl.BlockSpec((SWA_WINDOW, LANES),
                                  lambda b, t: (jnp.maximum((b * nt + t) * per - 1, 0), c))
    smem = pl.BlockSpec(memory_space=pltpu.SMEM)
    return pl.pallas_call(
        _gla_swa_kernel, grid=(BATCH, nt),
        in_specs=[row(GLA_QK), row(GLA_QK), row(A_WIDTH), row(A_WIDTH), row(LANES),
                  full(w_up), full(b_up), full(gain),
                  smem, smem, row(B_WIDTH), prev(0), own(0), prev(1), own(1), row(B_WIDTH)],
        out_specs=[row(A_WIDTH), row(B_WIDTH)],
        out_shape=[jax.ShapeDtypeStruct((TOKENS, A_WIDTH), BF16),
                   jax.ShapeDtypeStruct((TOKENS, B_WIDTH), BF16)],
        scratch_shapes=[pltpu.VMEM((A_WIDTH, GLA_QK), F32),
                        pltpu.VMEM((2 * SWA_WINDOW, SWA_HEADS * SWA_WINDOW), F32)],
        compiler_params=pltpu.CompilerParams(dimension_semantics=("arbitrary", "arbitrary"),
                                             vmem_limit_bytes=VMEM_LIMIT),
        name="gla_swa",
    )(aq, ak, av, agate, adown, w_up, b_up, gain, sinks, rel_bias, bq, bkv, bkv, bkv, bkv, bgate)


def _moba_kernel(rb_ref, q_ref, k_ref, v_ref, gate_ref, o_ref,
                 kaug_ref, vt_ref, km_ref, bias_ref, qat_ref, s_ref, smax_ref, m_ref, acc_ref):
    BLK = MOBA_BLOCK
    G = MOBA_GROUP
    KVH = MOBA_KV_HEADS
    R = G * BLK
    NP1 = MOBA_NBLK + 1
    i = pl.program_id(1)

    @pl.when(i == 0)
    def _():
        lane = lax.broadcasted_iota(jnp.int32, (BLK, LANES), 1)
        ones_rows = jnp.ones((MOBA_VT_ROWS - MOBA_DH, BLK), BF16)

        def prep(j, carry):
            rows = pl.ds(pl.multiple_of(j * BLK, BLK), BLK)
            onehot = jnp.where(lane == j, 1.0, 0.0).astype(BF16)
            for kh in range(KVH):
                kb = k_ref[rows, kh * MOBA_DH:(kh + 1) * MOBA_DH]
                kaug_ref[kh * NP1 + j + 1, :, 0:LANES] = kb
                kaug_ref[kh * NP1 + j + 1, :, LANES:2 * LANES] = onehot
                km_ref[pl.ds(kh * MOBA_NBLK + j, 1), :] = jnp.mean(kb.astype(F32), axis=0, keepdims=True)
                vb = v_ref[rows, kh * MOBA_DH:(kh + 1) * MOBA_DH]
                vt_ref[kh * NP1 + j + 1, 0:MOBA_DH, :] = vb.astype(F32).T.astype(BF16)
                vt_ref[kh * NP1 + j + 1, MOBA_DH:, :] = ones_rows
            return carry

        lax.fori_loop(0, MOBA_NBLK, prep, 0)
        for kh in range(KVH):
            kaug_ref[kh * NP1, :, 0:LANES] = jnp.zeros((BLK, LANES), BF16)
            kaug_ref[kh * NP1, :, LANES:2 * LANES] = jnp.where(lane == MOBA_NBLK, 1.0, 0.0).astype(BF16)
            vt_ref[kh * NP1] = jnp.zeros((MOBA_VT_ROWS, BLK), BF16)
            qat_ref[kh, MOBA_DH + MOBA_NBLK:, :] = jnp.full((2 * LANES - MOBA_DH - MOBA_NBLK, R), NEG_BIG, BF16)

    @pl.when(jnp.logical_and(pl.program_id(0) == 0, i == 0))
    def _():
        tk = lax.broadcasted_iota(jnp.int32, (BLK, BLK), 0)
        tq = lax.broadcasted_iota(jnp.int32, (BLK, BLK), 1)
        d_own = tq - tk
        for head in range(MOBA_HEADS):
            cols = slice(head * BLK, (head + 1) * BLK)
            bias_ref[0:BLK, cols] = _t5_bias(d_own + BLK, rb_ref, head) * LOG2E
            bias_ref[BLK:2 * BLK, cols] = jnp.where(d_own >= 0, _t5_bias(d_own, rb_ref, head) * LOG2E, -jnp.inf)

    blk = lax.broadcasted_iota(jnp.int32, (MOBA_NBLK, R), 0)
    blkf = blk.astype(F32)
    past = blk < i
    colh = lax.broadcasted_iota(jnp.int32, (1, R), 1) // BLK
    far_row = REL_BUCKETS - 1

    own_max = []
    for kh in range(KVH):
        q = jnp.concatenate([q_ref[:, (kh * G + g) * MOBA_DH:(kh * G + g + 1) * MOBA_DH] for g in range(G)],
                            axis=0)
        q_t = q.astype(F32).T.astype(BF16)
        qat_ref[kh, 0:MOBA_DH, :] = q_t
        s_own = (_dot(kaug_ref[kh * NP1 + i + 1, :, 0:MOBA_DH], q_t)
                 + bias_ref[BLK:2 * BLK, kh * R:(kh + 1) * R])
        s_ref[kh, BLK:2 * BLK, :] = s_own
        own_max.append(jnp.max(s_own, axis=0, keepdims=True))

        km = km_ref[kh * MOBA_NBLK:(kh + 1) * MOBA_NBLK, :]
        km_hi = km.astype(BF16)
        km_lo = (km - km_hi.astype(F32)).astype(BF16)
        gate = _dot(km_hi, q_t) + _dot(km_lo, q_t)
        g_ = jnp.where(past, gate, -jnp.inf)
        selected = jnp.zeros((MOBA_NBLK, R), dtype=jnp.bool_)
        for _ in range(MOBA_TOPK):
            mx = jnp.max(g_, axis=0, keepdims=True)
            first = jnp.min(jnp.where(g_ == mx, blkf, 1e9), axis=0, keepdims=True)
            pick = jnp.logical_and(blkf == first, past)
            selected = jnp.logical_or(selected, pick)
            g_ = jnp.where(pick, -jnp.inf, g_)
        cfar = jnp.where(colh == 0, rb_ref[far_row, kh * G],
                         jnp.where(colh == 1, rb_ref[far_row, kh * G + 1],
                                   jnp.where(colh == 2, rb_ref[far_row, kh * G + 2],
                                             rb_ref[far_row, kh * G + 3])))
        sel_bias = jnp.where(selected, jnp.where(blk < i - 1, cfar * LOG2E, 0.0), NEG_BIG)
        qat_ref[kh, MOBA_DH:MOBA_DH + MOBA_NBLK, :] = sel_bias.astype(BF16)

    for kh in range(KVH):
        s_near = _dot(kaug_ref[kh * NP1 + i], qat_ref[kh]) + bias_ref[0:BLK, kh * R:(kh + 1) * R]
        s_ref[kh, 0:BLK, :] = s_near
        smax_ref[kh] = jnp.maximum(own_max[kh], jnp.max(s_near, axis=0, keepdims=True))

    def produce(slot, kh, start):
        k2 = kaug_ref[pl.ds(kh * NP1 + start, 2)].reshape(2 * BLK, 2 * LANES)
        s = _dot(k2, qat_ref[kh])
        s_ref[slot * KVH + kh] = s
        smax_ref[slot * KVH + kh] = jnp.max(s, axis=0, keepdims=True)

    def consume(slot, kh, start, first=False):
        m_new = smax_ref[slot * KVH + kh]
        if not first:
            m_old = m_ref[kh]
            m_new = jnp.maximum(m_old, m_new)
        p = jnp.exp2(s_ref[slot * KVH + kh] - m_new).astype(BF16)
        pv = (_dot(vt_ref[kh * NP1 + start], p[0:BLK])
              + _dot(vt_ref[kh * NP1 + start + 1], p[BLK:2 * BLK]))
        acc_ref[kh] = pv if first else jnp.exp2(m_old - m_new) * acc_ref[kh] + pv
        m_ref[kh] = m_new

    n_far = jnp.maximum(i - 1, 0)
    n_pairs = (n_far + 1) // 2

    def far_start(t):
        return jnp.maximum(n_far - 2 * t - 1, 0)

    def far_step(t, slot):
        for kh in range(KVH):
            produce(1 - slot, kh, far_start(t + 1))
        for kh in range(KVH):
            consume(slot, kh, far_start(t))

    for kh in range(KVH):
        produce(1, kh, far_start(0))
    for kh in range(KVH):
        consume(0, kh, i, first=True)

    def far_four(u, carry):
        for d in range(4):
            far_step(4 * u + d, (d + 1) % 2)
        return carry

    n_full = jnp.maximum(n_pairs - 1, 0)
    lax.fori_loop(0, n_full // 4, far_four, 0)
    done = (n_full // 4) * 4

    @pl.when(n_full - done >= 2)
    def _():
        far_step(done, 1)
        far_step(done + 1, 0)

    @pl.when(n_full % 2 == 1)
    def _():
        far_step(n_full - 1, 1)

    for slot in range(2):
        @pl.when(jnp.logical_and(n_pairs >= 1, n_pairs % 2 == slot))
        def _():
            for kh in range(KVH):
                consume(slot, kh, far_start(n_pairs - 1))

    for kh in range(KVH):
        acc = acc_ref[kh]
        o = (acc[0:MOBA_DH] * (1.0 / acc[MOBA_DH:MOBA_DH + 1])).T
        for g in range(G):
            cols = slice((kh * G + g) * MOBA_DH, (kh * G + g + 1) * MOBA_DH)
            gt = gate_ref[:, cols].astype(F32)
            o_ref[:, cols] = (o[g * BLK:(g + 1) * BLK] * _silu(gt)).astype(o_ref.dtype)


def _moba(cq, ck, cv, cgate, rel_bias):
    R = MOBA_GROUP * MOBA_BLOCK
    KVH = MOBA_KV_HEADS
    NP1 = MOBA_NBLK + 1
    qspec = pl.BlockSpec((MOBA_BLOCK, C_WIDTH), lambda b, i: (b * MOBA_NBLK + i, 0))
    kvspec = pl.BlockSpec((SEQ, KVH * MOBA_DH), lambda b, i: (b, 0), pipeline_mode=pl.Buffered(1))
    return pl.pallas_call(
        _moba_kernel, grid=(BATCH, MOBA_NBLK),
        in_specs=[pl.BlockSpec(memory_space=pltpu.SMEM), qspec, kvspec, kvspec, qspec],
        out_specs=qspec,
        out_shape=jax.ShapeDtypeStruct((TOKENS, C_WIDTH), BF16),
        scratch_shapes=[pltpu.VMEM((KVH * NP1, MOBA_BLOCK, 2 * LANES), BF16),
                        pltpu.VMEM((KVH * NP1, MOBA_VT_ROWS, MOBA_BLOCK), BF16),
                        pltpu.VMEM((KVH * MOBA_NBLK, MOBA_DH), F32),
                        pltpu.VMEM((2 * MOBA_BLOCK, KVH * R), F32),
                        pltpu.VMEM((KVH, 2 * LANES, R), BF16),
                        pltpu.VMEM((2 * KVH, 2 * MOBA_BLOCK, R), F32),
                        pltpu.VMEM((2 * KVH, 1, R), F32),
                        pltpu.VMEM((KVH, 1, R), F32),
                        pltpu.VMEM((KVH, MOBA_VT_ROWS, R), F32)],
        compiler_params=pltpu.CompilerParams(
            dimension_semantics=("arbitrary", "arbitrary"), vmem_limit_bytes=MOBA_VMEM_LIMIT),
        name="moba",
    )(rel_bias, cq, ck, cv, cgate)


def kernel(x, norm_gain, final_gain, rel_bias, w_in_even, gla_w_up, gla_b_up, gla_norm_gain, swa_sinks,
           w_out_even, w_in_odd, w_out_odd):
    assert x.shape == (BATCH, SEQ, D_MODEL)
    h0 = x.reshape(TOKENS, D_MODEL)

    we = w_in_even[0]
    splits = np.cumsum([0, GLA_QK, GLA_QK, A_WIDTH, GLA_RANK, A_WIDTH, B_WIDTH, SWA_KV_HEADS * SWA_DH,
                        SWA_KV_HEADS * SWA_DH, B_WIDTH])
    w_aq, w_ak, w_av, w_down, w_agate, w_bq = [we[:, splits[k]:splits[k + 1]] for k in range(6)]
    w_bkv = we[:, splits[6]:splits[8]]
    w_bgate = we[:, splits[8]:splits[9]]
    w_down = jnp.pad(jnp.concatenate([w_down] * 3, axis=1), ((0, 0), (0, LANES - 3 * GLA_RANK)))
    proj_ws = [w.astype(BF16) for w in (w_aq, w_ak, w_av, w_down, w_agate, w_bq, w_bkv, w_bgate)]
    proj_dt = [BF16, BF16, BF16, F32, BF16, BF16, BF16, BF16]
    scales = [GLA_DK ** -0.5, 1.0, 1.0, 1.0, 1.0, SWA_DH ** -0.5 * LOG2E, 1.0, 1.0]
    aq, ak, av, adown, agate, bq, bkv, bgate = _layer_io(
        h0, [], norm_gain[0:1], proj_ws, proj_dt, scales, write_h=False)

    w_up_hi = gla_w_up[0].astype(BF16)
    w_up_lo = (gla_w_up[0] - w_up_hi.astype(F32)).astype(BF16)
    w_up = jnp.pad(jnp.concatenate([w_up_hi, w_up_hi, w_up_lo], axis=0), ((0, LANES - 3 * GLA_RANK), (0, 0)))
    oa, ob = _gla_swa(aq, ak, av, agate, adown, w_up, gla_b_up[0:1], gla_norm_gain[0:1],
                      bq, bkv, bgate, rel_bias, swa_sinks[0])

    wo = w_out_even[0].astype(BF16)
    wi = w_in_odd[0]
    osp = np.cumsum([0, C_WIDTH, MOBA_KV_HEADS * MOBA_DH, MOBA_KV_HEADS * MOBA_DH, C_WIDTH])
    odd_ws = [wi[:, osp[k]:osp[k + 1]].astype(BF16) for k in range(4)]
    h1, cq, ck, cv, cgate = _layer_io(
        h0, [(oa, wo[:A_WIDTH]), (ob, wo[A_WIDTH:])], norm_gain[1:2], odd_ws, [BF16] * 4,
        [MOBA_DH ** -0.5 * LOG2E, 1.0, 1.0, 1.0], write_h=True)

    oc = _moba(cq, ck, cv, cgate, rel_bias)

    (out,) = _layer_io(h1, [(oc, w_out_odd[0].astype(BF16))], final_gain.reshape(1, D_MODEL), [], [], [],
                       write_h=False)
    return out.reshape(BATCH, SEQ, D_MODEL)
```

```python
import functools
import math

import numpy as np
import jax
import jax.numpy as jnp
from jax import lax
from jax.experimental import pallas as pl
from jax.experimental.pallas import tpu as pltpu

D_MODEL = 1024
BATCH = 2
SEQ = 8192
TOKENS = BATCH * SEQ

REL_BUCKETS = 32
REL_MAX_EXACT = 16
REL_MAX_DIST = 128
ATTN_HEADS = 8

GLA_HEADS = 4
GLA_DK = 64
GLA_DV = 128
GLA_RANK = 16
GLA_TAU = 16.0
GLA_CHUNK = 64

SWA_HEADS = 8
SWA_KV_HEADS = 2
SWA_DH = 64
SWA_WINDOW = 128

MOBA_HEADS = 8
MOBA_KV_HEADS = 2
MOBA_DH = 128
MOBA_BLOCK = 256
MOBA_TOPK = 3
MOBA_GROUP = MOBA_HEADS // MOBA_KV_HEADS
MOBA_NBLK = SEQ // MOBA_BLOCK
MOBA_VT_ROWS = MOBA_DH + 16
LOG2E = math.log2(math.e)
SWA_VT_ROWS = SWA_KV_HEADS * SWA_DH + 16

EPS = 1e-6
LANES = 128
NEG_BIG = -1e30
VMEM_LIMIT = 48 * 1024 * 1024
MOBA_VMEM_LIMIT = 56 * 1024 * 1024

A_WIDTH = GLA_HEADS * GLA_DV
B_WIDTH = SWA_HEADS * SWA_DH
C_WIDTH = MOBA_HEADS * MOBA_DH
GLA_QK = GLA_HEADS * GLA_DK

ROW_TILE = 1024
GLA_TILE = 1024
SWA_TILE = 1024

F32 = jnp.float32
BF16 = jnp.bfloat16


def _dot(a, b):
    return jnp.dot(a, b, preferred_element_type=F32)


def _dot_nt(a, b):
    return lax.dot_general(a, b, (((1,), (1,)), ((), ())), preferred_element_type=F32)


def _dot_tn(a, b):
    return lax.dot_general(a, b, (((0,), (0,)), ((), ())), preferred_element_type=F32)


def _silu(x):
    return x * (1.0 / (1.0 + jnp.exp(-x)))


def _t5_thresholds():
    d = np.arange(REL_MAX_DIST + 1)
    nf = np.maximum(d, 1).astype(np.float32)
    large = REL_MAX_EXACT + (np.log(nf / np.float32(REL_MAX_EXACT))
                             / np.float32(math.log(REL_MAX_DIST / REL_MAX_EXACT))
                             * np.float32(REL_BUCKETS - REL_MAX_EXACT)).astype(np.int32)
    bucket = np.where(d < REL_MAX_EXACT, d, np.minimum(large, REL_BUCKETS - 1))
    assert np.all(np.diff(bucket) >= 0) and bucket[-1] == REL_BUCKETS - 1
    return [int(np.argmax(bucket >= b)) for b in range(REL_BUCKETS)]


_T5_THRESH = _t5_thresholds()


def _t5_bias(dist, rb_ref, head):
    out = jnp.full(dist.shape, rb_ref[0, head], F32)
    for b in range(1, REL_BUCKETS):
        out = jnp.where(dist >= _T5_THRESH[b], rb_ref[b, head], out)
    return out


def _layer_io_kernel(*refs, n_acc, n_proj, write_h, scales, row_sub):
    h_ref = refs[0]
    pos = 1
    acc_pairs = []
    for _ in range(n_acc):
        acc_pairs.append((refs[pos], refs[pos + 1]))
        pos += 2
    gain_ref = refs[pos]
    pos += 1
    wp_refs = refs[pos:pos + n_proj]
    pos += n_proj
    out_refs = refs[pos:]

    o = 1 if write_h else 0
    for s in range(ROW_TILE // row_sub):
        rows = slice(s * row_sub, (s + 1) * row_sub)
        h = h_ref[rows, :]
        for a_ref, w_ref in acc_pairs:
            h = h + _dot(a_ref[rows, :], w_ref[...])
        if write_h:
            out_refs[0][rows, :] = h
        y = h * lax.rsqrt(jnp.mean(h * h, axis=-1, keepdims=True) + EPS) * gain_ref[...]
        if n_proj == 0:
            out_refs[o][rows, :] = y
            continue
        yb = y.astype(BF16)
        for k in range(n_proj):
            r = _dot(yb, wp_refs[k][...])
            if scales[k] != 1.0:
                r = r * scales[k]
            out_refs[o + k][rows, :] = r.astype(out_refs[o + k].dtype)


def _layer_io(h, acc_pairs, gain, proj_ws, proj_dtypes, scales, write_h, row_sub):
    n_rows = h.shape[0]
    grid = (n_rows // ROW_TILE,)
    row_spec = lambda n: pl.BlockSpec((ROW_TILE, n), lambda i: (i, 0))
    full_spec = lambda a: pl.BlockSpec(a.shape, lambda i: (0,) * a.ndim, pipeline_mode=pl.Buffered(1))

    args = [h]
    in_specs = [row_spec(D_MODEL)]
    for a, w in acc_pairs:
        args += [a, w]
        in_specs += [row_spec(a.shape[1]), full_spec(w)]
    args.append(gain)
    in_specs.append(full_spec(gain))
    for w in proj_ws:
        args.append(w)
        in_specs.append(full_spec(w))

    out_shape, out_specs = [], []
    if write_h:
        out_shape.append(jax.ShapeDtypeStruct((n_rows, D_MODEL), F32))
        out_specs.append(row_spec(D_MODEL))
    if proj_ws:
        for w, dt in zip(proj_ws, proj_dtypes):
            out_shape.append(jax.ShapeDtypeStruct((n_rows, w.shape[1]), dt))
            out_specs.append(row_spec(w.shape[1]))
    else:
        out_shape.append(jax.ShapeDtypeStruct((n_rows, D_MODEL), F32))
        out_specs.append(row_spec(D_MODEL))

    kern = functools.partial(_layer_io_kernel, n_acc=len(acc_pairs), n_proj=len(proj_ws),
                             write_h=write_h, scales=tuple(scales), row_sub=row_sub)
    return pl.pallas_call(
        kern, grid=grid, in_specs=in_specs, out_specs=out_specs, out_shape=out_shape,
        compiler_params=pltpu.CompilerParams(dimension_semantics=("arbitrary",),
                                             vmem_limit_bytes=VMEM_LIMIT),
        name="layer_io",
    )(*args)


def _gla_init(st_ref):
    @pl.when(pl.program_id(1) == 0)
    def _():
        st_ref[...] = jnp.zeros_like(st_ref)


def _gla_steps(q_ref, k_ref, v_ref, gate_ref, down_ref, wup_ref, bup_ref, gain_ref, o_ref, st_ref):
    C = GLA_CHUNK

    a = down_ref[...]
    a_hi = a.astype(BF16)
    a_lo = (a - a_hi.astype(F32)).astype(BF16)
    lane_a = lax.broadcasted_iota(jnp.int32, a.shape, 1) // GLA_RANK
    z = _dot(jnp.where(lane_a == 1, a_lo, a_hi), wup_ref[...]) + bup_ref[...]
    log_a = (jnp.minimum(z, 0.0) - jnp.log(1.0 + jnp.exp(-jnp.abs(z)))) * (1.0 / GLA_TAU)

    r_i = lax.broadcasted_iota(jnp.int32, (C, C), 0)
    c_i = lax.broadcasted_iota(jnp.int32, (C, C), 1)
    tri = (c_i <= r_i).astype(BF16)
    lane_qk = lax.broadcasted_iota(jnp.int32, (C, GLA_QK), 1)
    head_masks = [(lane_qk // GLA_DK) == h for h in range(GLA_HEADS)]
    rs = lax.broadcasted_iota(jnp.int32, (GLA_HEADS * C, C), 0)
    cs = lax.broadcasted_iota(jnp.int32, (GLA_HEADS * C, C), 1)
    causal = (rs % C) >= cs
    st_r = lax.broadcasted_iota(jnp.int32, (A_WIDTH, GLA_QK), 0)
    st_c = lax.broadcasted_iota(jnp.int32, (A_WIDTH, GLA_QK), 1)
    same_head = (st_r // GLA_DV) == (st_c // GLA_DK)
    gain = gain_ref[...]

    def chunk(c):
        rows = slice(c * C, (c + 1) * C)
        g = log_a[rows]
        g_hi = g.astype(BF16)
        g_lo = (g - g_hi.astype(F32)).astype(BF16)
        b = _dot(tri, g_hi) + _dot(tri, g_lo)
        b_last = b[C - 1:C]
        q = q_ref[rows, :].astype(F32)
        k = k_ref[rows, :].astype(F32)
        v = v_ref[rows, :]
        q_e = q * jnp.exp(b)
        k_e = (k * jnp.exp(-b)).astype(BF16)
        k_l = (k * jnp.exp(b_last - b)).astype(BF16)
        decay = jnp.exp(b_last)

        q_stack = jnp.concatenate([jnp.where(m, q_e, 0.0) for m in head_masks], axis=0).astype(BF16)
        att = jnp.where(causal, _dot_nt(q_stack, k_e), 0.0).astype(BF16)
        o_full = _dot(att, v)
        o_intra = jnp.concatenate(
            [o_full[h * C:(h + 1) * C, h * GLA_DV:(h + 1) * GLA_DV] for h in range(GLA_HEADS)], axis=1)
        st = st_ref[...]
        o = o_intra + _dot_nt(q_e.astype(BF16), st.astype(BF16))
        kv_t = _dot_tn(v, k_l)
        st_ref[...] = st * decay + jnp.where(same_head, kv_t, 0.0)

        outs = []
        for h in range(GLA_HEADS):
            oh = o[:, h * GLA_DV:(h + 1) * GLA_DV]
            outs.append(oh * lax.rsqrt(jnp.mean(oh * oh, axis=-1, keepdims=True) + EPS) * gain)
        on = jnp.concatenate(outs, axis=1)
        o_ref[rows, :] = (on * _silu(gate_ref[rows, :].astype(F32))).astype(o_ref.dtype)

    return [functools.partial(chunk, c) for c in range(GLA_TILE // C)]


def _swa_init(rb_ref, bias_ref):
    W = SWA_WINDOW

    @pl.when(jnp.logical_and(pl.program_id(0) == 0, pl.program_id(1) == 0))
    def _():
        kj = lax.broadcasted_iota(jnp.int32, (2 * W, W), 0)
        qi = lax.broadcasted_iota(jnp.int32, (2 * W, W), 1)
        dist = qi + W - kj
        in_win = jnp.logical_and(dist >= 0, dist < W)
        for h in range(SWA_HEADS):
            bias_ref[:, h * W:(h + 1) * W] = jnp.where(in_win, _t5_bias(dist, rb_ref, h) * LOG2E, -jnp.inf)


def _swa_steps(sink_ref, q_ref, kp_ref, k_ref, vp_ref, v_ref, gate_ref, o_ref, bias_ref):
    W = SWA_WINDOW
    H = SWA_HEADS
    DH = SWA_DH
    G = SWA_HEADS // SWA_KV_HEADS
    t = pl.program_id(1)

    kcat = jnp.concatenate([kp_ref[...], k_ref[...]], axis=0)
    vcat = jnp.concatenate([vp_ref[...], v_ref[...]], axis=0)
    vt = vcat.astype(F32).T.astype(BF16)
    ones_rows = jnp.ones((SWA_VT_ROWS - LANES, 2 * W), BF16)
    q_t = q_ref[...].astype(F32).T.astype(BF16)
    zeros = jnp.zeros((DH, W), BF16)
    colh = lax.broadcasted_iota(jnp.int32, (1, H * W), 1) // W
    sink = jnp.full((1, H * W), sink_ref[H - 1], F32)
    for h in range(H - 2, -1, -1):
        sink = jnp.where(colh == h, sink_ref[h], sink)
    sink = sink * LOG2E
    key_prev = lax.broadcasted_iota(jnp.int32, (2 * W, H * W), 0) < W

    def block(blk):
        tok = slice(blk * W, (blk + 1) * W)
        win = slice(blk * W, blk * W + 2 * W)
        pieces = []
        for h in range(H):
            piece = q_t[h * DH:(h + 1) * DH, tok]
            pieces.append(jnp.concatenate([piece, zeros] if h < G else [zeros, piece], axis=0))
        q_pad = jnp.concatenate(pieces, axis=1)
        s = _dot(kcat[win], q_pad) + bias_ref[...]
        if blk == 0:
            s = jnp.where(jnp.logical_and(t == 0, key_prev), -jnp.inf, s)
        m = jnp.maximum(jnp.max(s, axis=0, keepdims=True), sink)
        p = jnp.exp2(s - m).astype(BF16)
        pv = _dot(jnp.concatenate([vt[:, win], ones_rows], axis=0), p)
        inv = 1.0 / (pv[LANES:LANES + 1] + jnp.exp2(sink - m))
        o_t = jnp.concatenate(
            [pv[(h // G) * DH:(h // G + 1) * DH, h * W:(h + 1) * W] * inv[:, h * W:(h + 1) * W]
             for h in range(H)], axis=0)
        gate = gate_ref[tok, :].astype(F32)
        o_ref[tok, :] = (o_t.T * _silu(gate)).astype(o_ref.dtype)

    return [functools.partial(block, blk) for blk in range(SWA_TILE // W)]


def _gla_swa_kernel(aq_ref, ak_ref, av_ref, agate_ref, adown_ref, wup_ref, bup_ref, gain_ref,
                    sink_ref, rb_ref, bq_ref, kp_ref, k_ref, vp_ref, v_ref, bgate_ref,
                    oa_ref, ob_ref, st_ref, bias_ref):
    _gla_init(st_ref)
    _swa_init(rb_ref, bias_ref)
    gla = _gla_steps(aq_ref, ak_ref, av_ref, agate_ref, adown_ref, wup_ref, bup_ref, gain_ref, oa_ref, st_ref)
    swa = _swa_steps(sink_ref, bq_ref, kp_ref, k_ref, vp_ref, v_ref, bgate_ref, ob_ref, bias_ref)
    for step in gla + swa:
        step()


def _gla_swa(aq, ak, av, agate, adown, w_up, b_up, gain, bq, bkv, bgate, rel_bias, sinks):
    assert GLA_TILE == SWA_TILE
    nt = SEQ // SWA_TILE
    per = SWA_TILE // SWA_WINDOW
    row = lambda n: pl.BlockSpec((SWA_TILE, n), lambda b, t: (b * nt + t, 0))
    full = lambda a: pl.BlockSpec(a.shape, lambda b, t: (0,) * a.ndim)
    own = lambda c: pl.BlockSpec((SWA_TILE, LANES), lambda b, t: (b * nt + t, c))
    prev = lambda c: pl.BlockSpec((SWA_WINDOW, LANES),
                                  lambda b, t: (jnp.maximum((b * nt + t) * per - 1, 0), c))
    smem = pl.BlockSpec(memory_space=pltpu.SMEM)
    return pl.pallas_call(
        _gla_swa_kernel, grid=(BATCH, nt),
        in_specs=[row(GLA_QK), row(GLA_QK), row(A_WIDTH), row(A_WIDTH), row(LANES),
                  full(w_up), full(b_up), full(gain),
                  smem, smem, row(B_WIDTH), prev(0), own(0), prev(1), own(1), row(B_WIDTH)],
        out_specs=[row(A_WIDTH), row(B_WIDTH)],
        out_shape=[jax.ShapeDtypeStruct((TOKENS, A_WIDTH), BF16),
                   jax.ShapeDtypeStruct((TOKENS, B_WIDTH), BF16)],
        scratch_shapes=[pltpu.VMEM((A_WIDTH, GLA_QK), F32),
                        pltpu.VMEM((2 * SWA_WINDOW, SWA_HEADS * SWA_WINDOW), F32)],
        compiler_params=pltpu.CompilerParams(dimension_semantics=("arbitrary", "arbitrary"),
                                             vmem_limit_bytes=VMEM_LIMIT),
        name="gla_swa",
    )(aq, ak, av, agate, adown, w_up, b_up, gain, sinks, rel_bias, bq, bkv, bkv, bkv, bkv, bgate)


def _moba_kernel(rb_ref, q_ref, k_ref, v_ref, gate_ref, o_ref,
                 kaug_ref, vt_ref, km_ref, bias_ref, qat_ref, s_ref, smax_ref, m_ref, acc_ref):
    BLK = MOBA_BLOCK
    G = MOBA_GROUP
    KVH = MOBA_KV_HEADS
    R = G * BLK
    NP1 = MOBA_NBLK + 1
    i = pl.program_id(1)

    @pl.when(i == 0)
    def _():
        lane = lax.broadcasted_iota(jnp.int32, (BLK, LANES), 1)
        ones_rows = jnp.ones((MOBA_VT_ROWS - MOBA_DH, BLK), BF16)

        def prep(j, carry):
            rows = pl.ds(pl.multiple_of(j * BLK, BLK), BLK)
            onehot = jnp.where(lane == j, 1.0, 0.0).astype(BF16)
            for kh in range(KVH):
                kb = k_ref[rows, kh * MOBA_DH:(kh + 1) * MOBA_DH]
                kaug_ref[kh * NP1 + j + 1, :, 0:LANES] = kb
                kaug_ref[kh * NP1 + j + 1, :, LANES:2 * LANES] = onehot
                km_ref[pl.ds(kh * MOBA_NBLK + j, 1), :] = jnp.mean(kb.astype(F32), axis=0, keepdims=True)
                vb = v_ref[rows, kh * MOBA_DH:(kh + 1) * MOBA_DH]
                vt_ref[kh * NP1 + j + 1, 0:MOBA_DH, :] = vb.astype(F32).T.astype(BF16)
                vt_ref[kh * NP1 + j + 1, MOBA_DH:, :] = ones_rows
            return carry

        lax.fori_loop(0, MOBA_NBLK, prep, 0)
        for kh in range(KVH):
            kaug_ref[kh * NP1, :, 0:LANES] = jnp.zeros((BLK, LANES), BF16)
            kaug_ref[kh * NP1, :, LANES:2 * LANES] = jnp.where(lane == MOBA_NBLK, 1.0, 0.0).astype(BF16)
            vt_ref[kh * NP1] = jnp.zeros((MOBA_VT_ROWS, BLK), BF16)
            qat_ref[kh, MOBA_DH + MOBA_NBLK:, :] = jnp.full((2 * LANES - MOBA_DH - MOBA_NBLK, R), NEG_BIG, BF16)

    @pl.when(jnp.logical_and(pl.program_id(0) == 0, i == 0))
    def _():
        tk = lax.broadcasted_iota(jnp.int32, (BLK, BLK), 0)
        tq = lax.broadcasted_iota(jnp.int32, (BLK, BLK), 1)
        d_own = tq - tk
        for head in range(MOBA_HEADS):
            cols = slice(head * BLK, (head + 1) * BLK)
            bias_ref[0:BLK, cols] = _t5_bias(d_own + BLK, rb_ref, head) * LOG2E
            bias_ref[BLK:2 * BLK, cols] = jnp.where(d_own >= 0, _t5_bias(d_own, rb_ref, head) * LOG2E, -jnp.inf)

    blk = lax.broadcasted_iota(jnp.int32, (MOBA_NBLK, R), 0)
    blkf = blk.astype(F32)
    past = blk < i
    colh = lax.broadcasted_iota(jnp.int32, (1, R), 1) // BLK
    far_row = REL_BUCKETS - 1

    own_max = []
    for kh in range(KVH):
        q = jnp.concatenate([q_ref[:, (kh * G + g) * MOBA_DH:(kh * G + g + 1) * MOBA_DH] for g in range(G)],
                            axis=0)
        q_t = q.astype(F32).T.astype(BF16)
        qat_ref[kh, 0:MOBA_DH, :] = q_t
        s_own = (_dot(kaug_ref[kh * NP1 + i + 1, :, 0:MOBA_DH], q_t)
                 + bias_ref[BLK:2 * BLK, kh * R:(kh + 1) * R])
        s_ref[kh, BLK:2 * BLK, :] = s_own
        own_max.append(jnp.max(s_own, axis=0, keepdims=True))

        km = km_ref[kh * MOBA_NBLK:(kh + 1) * MOBA_NBLK, :]
        km_hi = km.astype(BF16)
        km_lo = (km - km_hi.astype(F32)).astype(BF16)
        gate = _dot(km_hi, q_t) + _dot(km_lo, q_t)
        g_ = jnp.where(past, gate, -jnp.inf)
        selected = jnp.zeros((MOBA_NBLK, R), dtype=jnp.bool_)
        for _ in range(MOBA_TOPK):
            mx = jnp.max(g_, axis=0, keepdims=True)
            first = jnp.min(jnp.where(g_ == mx, blkf, 1e9), axis=0, keepdims=True)
            pick = jnp.logical_and(blkf == first, past)
            selected = jnp.logical_or(selected, pick)
            g_ = jnp.where(pick, -jnp.inf, g_)
        cfar = jnp.where(colh == 0, rb_ref[far_row, kh * G],
                         jnp.where(colh == 1, rb_ref[far_row, kh * G + 1],
                                   jnp.where(colh == 2, rb_ref[far_row, kh * G + 2],
                                             rb_ref[far_row, kh * G + 3])))
        sel_bias = jnp.where(selected, jnp.where(blk < i - 1, cfar * LOG2E, 0.0), NEG_BIG)
        qat_ref[kh, MOBA_DH:MOBA_DH + MOBA_NBLK, :] = sel_bias.astype(BF16)

    for kh in range(KVH):
        s_near = _dot(kaug_ref[kh * NP1 + i], qat_ref[kh]) + bias_ref[0:BLK, kh * R:(kh + 1) * R]
        s_ref[kh, 0:BLK, :] = s_near
        smax_ref[kh] = jnp.maximum(own_max[kh], jnp.max(s_near, axis=0, keepdims=True))

    def produce(slot, kh, start):
        k2 = kaug_ref[pl.ds(kh * NP1 + start, 2)].reshape(2 * BLK, 2 * LANES)
        s = _dot(k2, qat_ref[kh])
        s_ref[slot * KVH + kh] = s
        smax_ref[slot * KVH + kh] = jnp.max(s, axis=0, keepdims=True)

    def consume(slot, kh, start, first=False):
        m_new = smax_ref[slot * KVH + kh]
        if not first:
            m_old = m_ref[kh]
            m_new = jnp.maximum(m_old, m_new)
        p = jnp.exp2(s_ref[slot * KVH + kh] - m_new).astype(BF16)
        pv = (_dot(vt_ref[kh * NP1 + start], p[0:BLK])
              + _dot(vt_ref[kh * NP1 + start + 1], p[BLK:2 * BLK]))
        acc_ref[kh] = pv if first else jnp.exp2(m_old - m_new) * acc_ref[kh] + pv
        m_ref[kh] = m_new

    n_far = jnp.maximum(i - 1, 0)
    n_pairs = (n_far + 1) // 2

    def far_start(t):
        return jnp.maximum(n_far - 2 * t - 1, 0)

    def far_step(t, slot):
        for kh in range(KVH):
            produce(1 - slot, kh, far_start(t + 1))
        for kh in range(KVH):
            consume(slot, kh, far_start(t))

    for kh in range(KVH):
        produce(1, kh, far_start(0))
    for kh in range(KVH):
        consume(0, kh, i, first=True)

    def far_four(u, carry):
        for d in range(4):
            far_step(4 * u + d, (d + 1) % 2)
        return carry

    n_full = jnp.maximum(n_pairs - 1, 0)
    lax.fori_loop(0, n_full // 4, far_four, 0)
    done = (n_full // 4) * 4

    @pl.when(n_full - done >= 2)
    def _():
        far_step(done, 1)
        far_step(done + 1, 0)

    @pl.when(n_full % 2 == 1)
    def _():
        far_step(n_full - 1, 1)

    def finalize(kh):
        acc = acc_ref[kh]
        o = (acc[0:MOBA_DH] * (1.0 / acc[MOBA_DH:MOBA_DH + 1])).T
        for g in range(G):
            cols = slice((kh * G + g) * MOBA_DH, (kh * G + g + 1) * MOBA_DH)
            gt = gate_ref[:, cols].astype(F32)
            o_ref[:, cols] = (o[g * BLK:(g + 1) * BLK] * _silu(gt)).astype(o_ref.dtype)

    for slot in range(2):
        @pl.when(jnp.logical_and(n_pairs >= 1, n_pairs % 2 == slot))
        def _():
            for kh in range(KVH):
                consume(slot, kh, far_start(n_pairs - 1))
                finalize(kh)

    @pl.when(n_pairs == 0)
    def _():
        for kh in range(KVH):
            finalize(kh)


def _moba(cq, ck, cv, cgate, rel_bias):
    R = MOBA_GROUP * MOBA_BLOCK
    KVH = MOBA_KV_HEADS
    NP1 = MOBA_NBLK + 1
    qspec = pl.BlockSpec((MOBA_BLOCK, C_WIDTH), lambda b, i: (b * MOBA_NBLK + i, 0))
    kvspec = pl.BlockSpec((SEQ, KVH * MOBA_DH), lambda b, i: (b, 0), pipeline_mode=pl.Buffered(1))
    return pl.pallas_call(
        _moba_kernel, grid=(BATCH, MOBA_NBLK),
        in_specs=[pl.BlockSpec(memory_space=pltpu.SMEM), qspec, kvspec, kvspec, qspec],
        out_specs=qspec,
        out_shape=jax.ShapeDtypeStruct((TOKENS, C_WIDTH), BF16),
        scratch_shapes=[pltpu.VMEM((KVH * NP1, MOBA_BLOCK, 2 * LANES), BF16),
                        pltpu.VMEM((KVH * NP1, MOBA_VT_ROWS, MOBA_BLOCK), BF16),
                        pltpu.VMEM((KVH * MOBA_NBLK, MOBA_DH), F32),
                        pltpu.VMEM((2 * MOBA_BLOCK, KVH * R), F32),
                        pltpu.VMEM((KVH, 2 * LANES, R), BF16),
                        pltpu.VMEM((2 * KVH, 2 * MOBA_BLOCK, R), F32),
                        pltpu.VMEM((2 * KVH, 1, R), F32),
                        pltpu.VMEM((KVH, 1, R), F32),
                        pltpu.VMEM((KVH, MOBA_VT_ROWS, R), F32)],
        compiler_params=pltpu.CompilerParams(
            dimension_semantics=("arbitrary", "arbitrary"), vmem_limit_bytes=MOBA_VMEM_LIMIT),
        name="moba",
    )(rel_bias, cq, ck, cv, cgate)


def kernel(x, norm_gain, final_gain, rel_bias, w_in_even, gla_w_up, gla_b_up, gla_norm_gain, swa_sinks,
           w_out_even, w_in_odd, w_out_odd):
    assert x.shape == (BATCH, SEQ, D_MODEL)
    h0 = x.reshape(TOKENS, D_MODEL)

    we = w_in_even[0]
    splits = np.cumsum([0, GLA_QK, GLA_QK, A_WIDTH, GLA_RANK, A_WIDTH, B_WIDTH, SWA_KV_HEADS * SWA_DH,
                        SWA_KV_HEADS * SWA_DH, B_WIDTH])
    w_aq, w_ak, w_av, w_down, w_agate, w_bq = [we[:, splits[k]:splits[k + 1]] for k in range(6)]
    w_bkv = we[:, splits[6]:splits[8]]
    w_bgate = we[:, splits[8]:splits[9]]
    w_down = jnp.pad(jnp.concatenate([w_down] * 3, axis=1), ((0, 0), (0, LANES - 3 * GLA_RANK)))
    proj_ws = [w.astype(BF16) for w in (w_aq, w_ak, w_av, w_down, w_agate, w_bq, w_bkv, w_bgate)]
    proj_dt = [BF16, BF16, BF16, F32, BF16, BF16, BF16, BF16]
    scales = [GLA_DK ** -0.5, 1.0, 1.0, 1.0, 1.0, SWA_DH ** -0.5 * LOG2E, 1.0, 1.0]
    aq, ak, av, adown, agate, bq, bkv, bgate = _layer_io(
        h0, [], norm_gain[0:1], proj_ws, proj_dt, scales, write_h=False, row_sub=256)

    w_up_hi = gla_w_up[0].astype(BF16)
    w_up_lo = (gla_w_up[0] - w_up_hi.astype(F32)).astype(BF16)
    w_up = jnp.pad(jnp.concatenate([w_up_hi, w_up_hi, w_up_lo], axis=0), ((0, LANES - 3 * GLA_RANK), (0, 0)))
    oa, ob = _gla_swa(aq, ak, av, agate, adown, w_up, gla_b_up[0:1], gla_norm_gain[0:1],
                      bq, bkv, bgate, rel_bias, swa_sinks[0])

    wo = w_out_even[0].astype(BF16)
    wi = w_in_odd[0]
    osp = np.cumsum([0, C_WIDTH, MOBA_KV_HEADS * MOBA_DH, MOBA_KV_HEADS * MOBA_DH, C_WIDTH])
    odd_ws = [wi[:, osp[k]:osp[k + 1]].astype(BF16) for k in range(4)]
    h1, cq, ck, cv, cgate = _layer_io(
        h0, [(oa, wo[:A_WIDTH]), (ob, wo[A_WIDTH:])], norm_gain[1:2], odd_ws, [BF16] * 4,
        [MOBA_DH ** -0.5 * LOG2E, 1.0, 1.0, 1.0], write_h=True, row_sub=ROW_TILE)

    oc = _moba(cq, ck, cv, cgate, rel_bias)

    (out,) = _layer_io(h1, [(oc, w_out_odd[0].astype(BF16))], final_gain.reshape(1, D_MODEL), [], [], [],
                       write_h=False, row_sub=256)
    return out.reshape(BATCH, SEQ, D_MODEL)
```

```python
import functools
import math

import numpy as np
import jax
import jax.numpy as jnp
from jax import lax
from jax.experimental import pallas as pl
from jax.experimental.pallas import tpu as pltpu

D_MODEL = 1024
BATCH = 2
SEQ = 8192
TOKENS = BATCH * SEQ

REL_BUCKETS = 32
REL_MAX_EXACT = 16
REL_MAX_DIST = 128
ATTN_HEADS = 8

GLA_HEADS = 4
GLA_DK = 64
GLA_DV = 128
GLA_RANK = 16
GLA_TAU = 16.0
GLA_CHUNK = 64

SWA_HEADS = 8
SWA_KV_HEADS = 2
SWA_DH = 64
SWA_WINDOW = 128

MOBA_HEADS = 8
MOBA_KV_HEADS = 2
MOBA_DH = 128
MOBA_BLOCK = 256
MOBA_TOPK = 3
MOBA_GROUP = MOBA_HEADS // MOBA_KV_HEADS
MOBA_NBLK = SEQ // MOBA_BLOCK
MOBA_VT_ROWS = MOBA_DH + 16
LOG2E = math.log2(math.e)
SWA_VT_ROWS = SWA_KV_HEADS * SWA_DH + 16

EPS = 1e-6
LANES = 128
NEG_BIG = -1e30
VMEM_LIMIT = 48 * 1024 * 1024
MOBA_VMEM_LIMIT = 56 * 1024 * 1024

A_WIDTH = GLA_HEADS * GLA_DV
B_WIDTH = SWA_HEADS * SWA_DH
C_WIDTH = MOBA_HEADS * MOBA_DH
GLA_QK = GLA_HEADS * GLA_DK

ROW_TILE = 1024
GLA_TILE = 1024
SWA_TILE = 1024

F32 = jnp.float32
BF16 = jnp.bfloat16


def _dot(a, b):
    return jnp.dot(a, b, preferred_element_type=F32)


def _dot_nt(a, b):
    return lax.dot_general(a, b, (((1,), (1,)), ((), ())), preferred_element_type=F32)


def _dot_tn(a, b):
    return lax.dot_general(a, b, (((0,), (0,)), ((), ())), preferred_element_type=F32)


def _silu(x):
    return x * (1.0 / (1.0 + jnp.exp(-x)))


def _t5_thresholds():
    d = np.arange(REL_MAX_DIST + 1)
    nf = np.maximum(d, 1).astype(np.float32)
    large = REL_MAX_EXACT + (np.log(nf / np.float32(REL_MAX_EXACT))
                             / np.float32(math.log(REL_MAX_DIST / REL_MAX_EXACT))
                             * np.float32(REL_BUCKETS - REL_MAX_EXACT)).astype(np.int32)
    bucket = np.where(d < REL_MAX_EXACT, d, np.minimum(large, REL_BUCKETS - 1))
    assert np.all(np.diff(bucket) >= 0) and bucket[-1] == REL_BUCKETS - 1
    return [int(np.argmax(bucket >= b)) for b in range(REL_BUCKETS)]


_T5_THRESH = _t5_thresholds()


def _t5_bias(dist, rb_ref, head):
    out = jnp.full(dist.shape, rb_ref[0, head], F32)
    for b in range(1, REL_BUCKETS):
        out = jnp.where(dist >= _T5_THRESH[b], rb_ref[b, head], out)
    return out


def _layer_io_kernel(*refs, n_acc, n_proj, write_h, scales, row_sub):
    h_ref = refs[0]
    pos = 1
    acc_pairs = []
    for _ in range(n_acc):
        acc_pairs.append((refs[pos], refs[pos + 1]))
        pos += 2
    gain_ref = refs[pos]
    pos += 1
    wp_refs = refs[pos:pos + n_proj]
    pos += n_proj
    out_refs = refs[pos:]

    o = 1 if write_h else 0
    for s in range(ROW_TILE // row_sub):
        rows = slice(s * row_sub, (s + 1) * row_sub)
        h = h_ref[rows, :]
        for a_ref, w_ref in acc_pairs:
            h = h + _dot(a_ref[rows, :], w_ref[...])
        if write_h:
            out_refs[0][rows, :] = h
        y = h * lax.rsqrt(jnp.mean(h * h, axis=-1, keepdims=True) + EPS) * gain_ref[...]
        if n_proj == 0:
            out_refs[o][rows, :] = y
            continue
        yb = y.astype(BF16)
        for k in range(n_proj):
            r = _dot(yb, wp_refs[k][...])
            if scales[k] != 1.0:
                r = r * scales[k]
            out_refs[o + k][rows, :] = r.astype(out_refs[o + k].dtype)


def _layer_io(h, acc_pairs, gain, proj_ws, proj_dtypes, scales, write_h, row_sub):
    n_rows = h.shape[0]
    grid = (n_rows // ROW_TILE,)
    row_spec = lambda n: pl.BlockSpec((ROW_TILE, n), lambda i: (i, 0))
    full_spec = lambda a: pl.BlockSpec(a.shape, lambda i: (0,) * a.ndim, pipeline_mode=pl.Buffered(1))

    args = [h]
    in_specs = [row_spec(D_MODEL)]
    for a, w in acc_pairs:
        args += [a, w]
        in_specs += [row_spec(a.shape[1]), full_spec(w)]
    args.append(gain)
    in_specs.append(full_spec(gain))
    for w in proj_ws:
        args.append(w)
        in_specs.append(full_spec(w))

    out_shape, out_specs = [], []
    if write_h:
        out_shape.append(jax.ShapeDtypeStruct((n_rows, D_MODEL), F32))
        out_specs.append(row_spec(D_MODEL))
    if proj_ws:
        for w, dt in zip(proj_ws, proj_dtypes):
            out_shape.append(jax.ShapeDtypeStruct((n_rows, w.shape[1]), dt))
            out_specs.append(row_spec(w.shape[1]))
    else:
        out_shape.append(jax.ShapeDtypeStruct((n_rows, D_MODEL), F32))
        out_specs.append(row_spec(D_MODEL))

    kern = functools.partial(_layer_io_kernel, n_acc=len(acc_pairs), n_proj=len(proj_ws),
                             write_h=write_h, scales=tuple(scales), row_sub=row_sub)
    return pl.pallas_call(
        kern, grid=grid, in_specs=in_specs, out_specs=out_specs, out_shape=out_shape,
        compiler_params=pltpu.CompilerParams(dimension_semantics=("arbitrary",),
                                             vmem_limit_bytes=VMEM_LIMIT),
        name="layer_io",
    )(*args)


def _gla_init(st_ref):
    @pl.when(pl.program_id(1) == 0)
    def _():
        st_ref[...] = jnp.zeros_like(st_ref)


def _gla_steps(q_ref, k_ref, v_ref, gate_ref, down_ref, wup_ref, bup_ref, gain_ref, o_ref, st_ref):
    C = GLA_CHUNK

    a = down_ref[...]
    a_hi = a.astype(BF16)
    a_lo = (a - a_hi.astype(F32)).astype(BF16)
    lane_a = lax.broadcasted_iota(jnp.int32, a.shape, 1) // GLA_RANK
    z = _dot(jnp.where(lane_a == 1, a_lo, a_hi), wup_ref[...]) + bup_ref[...]
    log_a = (jnp.minimum(z, 0.0) - jnp.log(1.0 + jnp.exp(-jnp.abs(z)))) * (1.0 / GLA_TAU)

    r_i = lax.broadcasted_iota(jnp.int32, (C, C), 0)
    c_i = lax.broadcasted_iota(jnp.int32, (C, C), 1)
    tri = (c_i <= r_i).astype(BF16)
    lane_qk = lax.broadcasted_iota(jnp.int32, (C, GLA_QK), 1)
    head_masks = [(lane_qk // GLA_DK) == h for h in range(GLA_HEADS)]
    rs = lax.broadcasted_iota(jnp.int32, (GLA_HEADS * C, C), 0)
    cs = lax.broadcasted_iota(jnp.int32, (GLA_HEADS * C, C), 1)
    causal = (rs % C) >= cs
    st_r = lax.broadcasted_iota(jnp.int32, (A_WIDTH, GLA_QK), 0)
    st_c = lax.broadcasted_iota(jnp.int32, (A_WIDTH, GLA_QK), 1)
    same_head = (st_r // GLA_DV) == (st_c // GLA_DK)
    gain = gain_ref[...]

    def chunk(c):
        rows = slice(c * C, (c + 1) * C)
        g = log_a[rows]
        g_hi = g.astype(BF16)
        g_lo = (g - g_hi.astype(F32)).astype(BF16)
        b = _dot(tri, g_hi) + _dot(tri, g_lo)
        b_last = b[C - 1:C]
        q = q_ref[rows, :].astype(F32)
        k = k_ref[rows, :].astype(F32)
        v = v_ref[rows, :]
        q_e = q * jnp.exp(b)
        k_e = (k * jnp.exp(-b)).astype(BF16)
        k_l = (k * jnp.exp(b_last - b)).astype(BF16)
        decay = jnp.exp(b_last)

        q_stack = jnp.concatenate([jnp.where(m, q_e, 0.0) for m in head_masks], axis=0).astype(BF16)
        att = jnp.where(causal, _dot_nt(q_stack, k_e), 0.0).astype(BF16)
        o_full = _dot(att, v)
        o_intra = jnp.concatenate(
            [o_full[h * C:(h + 1) * C, h * GLA_DV:(h + 1) * GLA_DV] for h in range(GLA_HEADS)], axis=1)
        st = st_ref[...]
        o = o_intra + _dot_nt(q_e.astype(BF16), st.astype(BF16))
        kv_t = _dot_tn(v, k_l)
        st_ref[...] = st * decay + jnp.where(same_head, kv_t, 0.0)

        outs = []
        for h in range(GLA_HEADS):
            oh = o[:, h * GLA_DV:(h + 1) * GLA_DV]
            outs.append(oh * lax.rsqrt(jnp.mean(oh * oh, axis=-1, keepdims=True) + EPS) * gain)
        on = jnp.concatenate(outs, axis=1)
        o_ref[rows, :] = (on * _silu(gate_ref[rows, :].astype(F32))).astype(o_ref.dtype)

    return [functools.partial(chunk, c) for c in range(GLA_TILE // C)]


def _swa_init(rb_ref, bias_ref):
    W = SWA_WINDOW

    @pl.when(jnp.logical_and(pl.program_id(0) == 0, pl.program_id(1) == 0))
    def _():
        kj = lax.broadcasted_iota(jnp.int32, (2 * W, W), 0)
        qi = lax.broadcasted_iota(jnp.int32, (2 * W, W), 1)
        dist = qi + W - kj
        in_win = jnp.logical_and(dist >= 0, dist < W)
        for h in range(SWA_HEADS):
            bias_ref[:, h * W:(h + 1) * W] = jnp.where(in_win, _t5_bias(dist, rb_ref, h) * LOG2E, -jnp.inf)


def _swa_steps(sink_ref, q_ref, kp_ref, k_ref, vp_ref, v_ref, gate_ref, o_ref, bias_ref, s_ref, smax_ref):
    W = SWA_WINDOW
    H = SWA_HEADS
    DH = SWA_DH
    G = SWA_HEADS // SWA_KV_HEADS
    t = pl.program_id(1)

    kcat = jnp.concatenate([kp_ref[...], k_ref[...]], axis=0)
    vcat = jnp.concatenate([vp_ref[...], v_ref[...]], axis=0)
    vt = vcat.astype(F32).T.astype(BF16)
    ones_rows = jnp.ones((SWA_VT_ROWS - LANES, 2 * W), BF16)
    q_t = q_ref[...].astype(F32).T.astype(BF16)
    zeros = jnp.zeros((DH, W), BF16)
    colh = lax.broadcasted_iota(jnp.int32, (1, H * W), 1) // W
    sink = jnp.full((1, H * W), sink_ref[H - 1], F32)
    for h in range(H - 2, -1, -1):
        sink = jnp.where(colh == h, sink_ref[h], sink)
    sink = sink * LOG2E
    key_prev = lax.broadcasted_iota(jnp.int32, (2 * W, H * W), 0) < W

    def produce(blk):
        tok = slice(blk * W, (blk + 1) * W)
        win = slice(blk * W, blk * W + 2 * W)
        pieces = []
        for h in range(H):
            piece = q_t[h * DH:(h + 1) * DH, tok]
            pieces.append(jnp.concatenate([piece, zeros] if h < G else [zeros, piece], axis=0))
        q_pad = jnp.concatenate(pieces, axis=1)
        s = _dot(kcat[win], q_pad) + bias_ref[...]
        if blk == 0:
            s = jnp.where(jnp.logical_and(t == 0, key_prev), -jnp.inf, s)
        s_ref[blk % 2] = s
        smax_ref[blk % 2] = jnp.maximum(jnp.max(s, axis=0, keepdims=True), sink)

    def consume(blk):
        tok = slice(blk * W, (blk + 1) * W)
        win = slice(blk * W, blk * W + 2 * W)
        m = smax_ref[blk % 2]
        p = jnp.exp2(s_ref[blk % 2] - m).astype(BF16)
        pv = _dot(jnp.concatenate([vt[:, win], ones_rows], axis=0), p)
        inv = 1.0 / (pv[LANES:LANES + 1] + jnp.exp2(sink - m))
        o_t = jnp.concatenate(
            [pv[(h // G) * DH:(h // G + 1) * DH, h * W:(h + 1) * W] * inv[:, h * W:(h + 1) * W]
             for h in range(H)], axis=0)
        gate = gate_ref[tok, :].astype(F32)
        o_ref[tok, :] = (o_t.T * _silu(gate)).astype(o_ref.dtype)

    n_blk = SWA_TILE // W
    steps = [functools.partial(produce, 0)]
    for blk in range(n_blk):
        if blk + 1 < n_blk:
            steps.append(functools.partial(produce, blk + 1))
        steps.append(functools.partial(consume, blk))
    return steps


def _gla_swa_kernel(aq_ref, ak_ref, av_ref, agate_ref, adown_ref, wup_ref, bup_ref, gain_ref,
                    sink_ref, rb_ref, bq_ref, kp_ref, k_ref, vp_ref, v_ref, bgate_ref,
                    oa_ref, ob_ref, st_ref, bias_ref, s_ref, smax_ref):
    _gla_init(st_ref)
    _swa_init(rb_ref, bias_ref)
    gla = _gla_steps(aq_ref, ak_ref, av_ref, agate_ref, adown_ref, wup_ref, bup_ref, gain_ref, oa_ref, st_ref)
    swa = _swa_steps(sink_ref, bq_ref, kp_ref, k_ref, vp_ref, v_ref, bgate_ref, ob_ref, bias_ref, s_ref, smax_ref)
    for step in gla + swa:
        step()


def _gla_swa(aq, ak, av, agate, adown, w_up, b_up, gain, bq, bkv, bgate, rel_bias, sinks):
    assert GLA_TILE == SWA_TILE
    nt = SEQ // SWA_TILE
    per = SWA_TILE // SWA_WINDOW
    row = lambda n: pl.BlockSpec((SWA_TILE, n), lambda b, t: (b * nt + t, 0))
    full = lambda a: pl.BlockSpec(a.shape, lambda b, t: (0,) * a.ndim)
    own = lambda c: pl.BlockSpec((SWA_TILE, LANES), lambda b, t: (b * nt + t, c))
    prev = lambda c: pl.BlockSpec((SWA_WINDOW, LANES),
                                  lambda b, t: (jnp.maximum((b * nt + t) * per - 1, 0), c))
    smem = pl.BlockSpec(memory_space=pltpu.SMEM)
    return pl.pallas_call(
        _gla_swa_kernel, grid=(BATCH, nt),
        in_specs=[row(GLA_QK), row(GLA_QK), row(A_WIDTH), row(A_WIDTH), row(LANES),
                  full(w_up), full(b_up), full(gain),
                  smem, smem, row(B_WIDTH), prev(0), own(0), prev(1), own(1), row(B_WIDTH)],
        out_specs=[row(A_WIDTH), row(B_WIDTH)],
        out_shape=[jax.ShapeDtypeStruct((TOKENS, A_WIDTH), BF16),
                   jax.ShapeDtypeStruct((TOKENS, B_WIDTH), BF16)],
        scratch_shapes=[pltpu.VMEM((A_WIDTH, GLA_QK), F32),
                        pltpu.VMEM((2 * SWA_WINDOW, SWA_HEADS * SWA_WINDOW), F32),
                        pltpu.VMEM((2, 2 * SWA_WINDOW, SWA_HEADS * SWA_WINDOW), F32),
                        pltpu.VMEM((2, 1, SWA_HEADS * SWA_WINDOW), F32)],
        compiler_params=pltpu.CompilerParams(dimension_semantics=("arbitrary", "arbitrary"),
                                             vmem_limit_bytes=VMEM_LIMIT),
        name="gla_swa",
    )(aq, ak, av, agate, adown, w_up, b_up, gain, sinks, rel_bias, bq, bkv, bkv, bkv, bkv, bgate)


def _moba_kernel(rb_ref, q_ref, k_ref, v_ref, gate_ref, h_ref, wout_ref, fgain_ref, o_ref,
                 kaug_ref, vt_ref, km_ref, bias_ref, qat_ref, s_ref, smax_ref, m_ref, acc_ref, og_ref):
    BLK = MOBA_BLOCK
    G = MOBA_GROUP
    KVH = MOBA_KV_HEADS
    R = G * BLK
    NP1 = MOBA_NBLK + 1
    i = pl.program_id(1)

    @pl.when(i == 0)
    def _():
        lane = lax.broadcasted_iota(jnp.int32, (BLK, LANES), 1)
        ones_rows = jnp.ones((MOBA_VT_ROWS - MOBA_DH, BLK), BF16)

        def prep(j, carry):
            rows = pl.ds(pl.multiple_of(j * BLK, BLK), BLK)
            onehot = jnp.where(lane == j, 1.0, 0.0).astype(BF16)
            for kh in range(KVH):
                kb = k_ref[rows, kh * MOBA_DH:(kh + 1) * MOBA_DH]
                kaug_ref[kh * NP1 + j + 1, :, 0:LANES] = kb
                kaug_ref[kh * NP1 + j + 1, :, LANES:2 * LANES] = onehot
                km_ref[pl.ds(kh * MOBA_NBLK + j, 1), :] = jnp.mean(kb.astype(F32), axis=0, keepdims=True)
                vb = v_ref[rows, kh * MOBA_DH:(kh + 1) * MOBA_DH]
                vt_ref[kh * NP1 + j + 1, 0:MOBA_DH, :] = vb.astype(F32).T.astype(BF16)
                vt_ref[kh * NP1 + j + 1, MOBA_DH:, :] = ones_rows
            return carry

        lax.fori_loop(0, MOBA_NBLK, prep, 0)
        for kh in range(KVH):
            kaug_ref[kh * NP1, :, 0:LANES] = jnp.zeros((BLK, LANES), BF16)
            kaug_ref[kh * NP1, :, LANES:2 * LANES] = jnp.where(lane == MOBA_NBLK, 1.0, 0.0).astype(BF16)
            vt_ref[kh * NP1] = jnp.zeros((MOBA_VT_ROWS, BLK), BF16)
            qat_ref[kh, MOBA_DH + MOBA_NBLK:, :] = jnp.full((2 * LANES - MOBA_DH - MOBA_NBLK, R), NEG_BIG, BF16)

    @pl.when(jnp.logical_and(pl.program_id(0) == 0, i == 0))
    def _():
        tk = lax.broadcasted_iota(jnp.int32, (BLK, BLK), 0)
        tq = lax.broadcasted_iota(jnp.int32, (BLK, BLK), 1)
        d_own = tq - tk
        for head in range(MOBA_HEADS):
            cols = slice(head * BLK, (head + 1) * BLK)
            bias_ref[0:BLK, cols] = _t5_bias(d_own + BLK, rb_ref, head) * LOG2E
            bias_ref[BLK:2 * BLK, cols] = jnp.where(d_own >= 0, _t5_bias(d_own, rb_ref, head) * LOG2E, -jnp.inf)

    blk = lax.broadcasted_iota(jnp.int32, (MOBA_NBLK, R), 0)
    blkf = blk.astype(F32)
    past = blk < i
    colh = lax.broadcasted_iota(jnp.int32, (1, R), 1) // BLK
    far_row = REL_BUCKETS - 1

    own_max = []
    for kh in range(KVH):
        q = jnp.concatenate([q_ref[:, (kh * G + g) * MOBA_DH:(kh * G + g + 1) * MOBA_DH] for g in range(G)],
                            axis=0)
        q_t = q.astype(F32).T.astype(BF16)
        qat_ref[kh, 0:MOBA_DH, :] = q_t
        s_own = (_dot(kaug_ref[kh * NP1 + i + 1, :, 0:MOBA_DH], q_t)
                 + bias_ref[BLK:2 * BLK, kh * R:(kh + 1) * R])
        s_ref[kh, BLK:2 * BLK, :] = s_own
        own_max.append(jnp.max(s_own, axis=0, keepdims=True))

        km = km_ref[kh * MOBA_NBLK:(kh + 1) * MOBA_NBLK, :]
        km_hi = km.astype(BF16)
        km_lo = (km - km_hi.astype(F32)).astype(BF16)
        gate = _dot(km_hi, q_t) + _dot(km_lo, q_t)
        g_ = jnp.where(past, gate, -jnp.inf)
        selected = jnp.zeros((MOBA_NBLK, R), dtype=jnp.bool_)
        for _ in range(MOBA_TOPK):
            mx = jnp.max(g_, axis=0, keepdims=True)
            first = jnp.min(jnp.where(g_ == mx, blkf, 1e9), axis=0, keepdims=True)
            pick = jnp.logical_and(blkf == first, past)
            selected = jnp.logical_or(selected, pick)
            g_ = jnp.where(pick, -jnp.inf, g_)
        cfar = jnp.where(colh == 0, rb_ref[far_row, kh * G],
                         jnp.where(colh == 1, rb_ref[far_row, kh * G + 1],
                                   jnp.where(colh == 2, rb_ref[far_row, kh * G + 2],
                                             rb_ref[far_row, kh * G + 3])))
        sel_bias = jnp.where(selected, jnp.where(blk < i - 1, cfar * LOG2E, 0.0), NEG_BIG)
        qat_ref[kh, MOBA_DH:MOBA_DH + MOBA_NBLK, :] = sel_bias.astype(BF16)

    for kh in range(KVH):
        s_near = _dot(kaug_ref[kh * NP1 + i], qat_ref[kh]) + bias_ref[0:BLK, kh * R:(kh + 1) * R]
        s_ref[kh, 0:BLK, :] = s_near
        smax_ref[kh] = jnp.maximum(own_max[kh], jnp.max(s_near, axis=0, keepdims=True))

    def produce(slot, kh, start):
        k2 = kaug_ref[pl.ds(kh * NP1 + start, 2)].reshape(2 * BLK, 2 * LANES)
        s = _dot(k2, qat_ref[kh])
        s_ref[slot * KVH + kh] = s
        smax_ref[slot * KVH + kh] = jnp.max(s, axis=0, keepdims=True)

    def consume(slot, kh, start, first=False):
        m_new = smax_ref[slot * KVH + kh]
        if not first:
            m_old = m_ref[kh]
            m_new = jnp.maximum(m_old, m_new)
        p = jnp.exp2(s_ref[slot * KVH + kh] - m_new).astype(BF16)
        pv = (_dot(vt_ref[kh * NP1 + start], p[0:BLK])
              + _dot(vt_ref[kh * NP1 + start + 1], p[BLK:2 * BLK]))
        acc_ref[kh] = pv if first else jnp.exp2(m_old - m_new) * acc_ref[kh] + pv
        m_ref[kh] = m_new

    n_far = jnp.maximum(i - 1, 0)
    n_pairs = (n_far + 1) // 2

    def far_start(t):
        return jnp.maximum(n_far - 2 * t - 1, 0)

    def far_step(t, slot):
        for kh in range(KVH):
            produce(1 - slot, kh, far_start(t + 1))
        for kh in range(KVH):
            consume(slot, kh, far_start(t))

    for kh in range(KVH):
        produce(1, kh, far_start(0))
    for kh in range(KVH):
        consume(0, kh, i, first=True)

    def far_four(u, carry):
        for d in range(4):
            far_step(4 * u + d, (d + 1) % 2)
        return carry

    n_full = jnp.maximum(n_pairs - 1, 0)
    lax.fori_loop(0, n_full // 4, far_four, 0)
    done = (n_full // 4) * 4

    @pl.when(n_full - done >= 2)
    def _():
        far_step(done, 1)
        far_step(done + 1, 0)

    @pl.when(n_full % 2 == 1)
    def _():
        far_step(n_full - 1, 1)

    def finalize(kh):
        acc = acc_ref[kh]
        o = (acc[0:MOBA_DH] * (1.0 / acc[MOBA_DH:MOBA_DH + 1])).T
        for g in range(G):
            cols = slice((kh * G + g) * MOBA_DH, (kh * G + g + 1) * MOBA_DH)
            gt = gate_ref[:, cols].astype(F32)
            og_ref[:, cols] = (o[g * BLK:(g + 1) * BLK] * _silu(gt)).astype(og_ref.dtype)

    def out_proj():
        h = h_ref[...] + _dot(og_ref[...], wout_ref[...])
        o_ref[...] = h * lax.rsqrt(jnp.mean(h * h, axis=-1, keepdims=True) + EPS) * fgain_ref[...]

    for slot in range(2):
        @pl.when(jnp.logical_and(n_pairs >= 1, n_pairs % 2 == slot))
        def _():
            for kh in range(KVH):
                consume(slot, kh, far_start(n_pairs - 1))
                finalize(kh)
            out_proj()

    @pl.when(n_pairs == 0)
    def _():
        for kh in range(KVH):
            finalize(kh)
        out_proj()


def _moba(cq, ck, cv, cgate, rel_bias, h, w_out, final_gain):
    R = MOBA_GROUP * MOBA_BLOCK
    KVH = MOBA_KV_HEADS
    NP1 = MOBA_NBLK + 1
    qspec = pl.BlockSpec((MOBA_BLOCK, C_WIDTH), lambda b, i: (b * MOBA_NBLK + i, 0))
    kvspec = pl.BlockSpec((SEQ, KVH * MOBA_DH), lambda b, i: (b, 0), pipeline_mode=pl.Buffered(1))
    return pl.pallas_call(
        _moba_kernel, grid=(BATCH, MOBA_NBLK),
        in_specs=[pl.BlockSpec(memory_space=pltpu.SMEM), qspec, kvspec, kvspec, qspec, qspec,
                  pl.BlockSpec(w_out.shape, lambda b, i: (0, 0), pipeline_mode=pl.Buffered(1)),
                  pl.BlockSpec(final_gain.shape, lambda b, i: (0, 0))],
        out_specs=qspec,
        out_shape=jax.ShapeDtypeStruct((TOKENS, D_MODEL), F32),
        scratch_shapes=[pltpu.VMEM((KVH * NP1, MOBA_BLOCK, 2 * LANES), BF16),
                        pltpu.VMEM((KVH * NP1, MOBA_VT_ROWS, MOBA_BLOCK), BF16),
                        pltpu.VMEM((KVH * MOBA_NBLK, MOBA_DH), F32),
                        pltpu.VMEM((2 * MOBA_BLOCK, KVH * R), F32),
                        pltpu.VMEM((KVH, 2 * LANES, R), BF16),
                        pltpu.VMEM((2 * KVH, 2 * MOBA_BLOCK, R), F32),
                        pltpu.VMEM((2 * KVH, 1, R), F32),
                        pltpu.VMEM((KVH, 1, R), F32),
                        pltpu.VMEM((KVH, MOBA_VT_ROWS, R), F32),
                        pltpu.VMEM((MOBA_BLOCK, C_WIDTH), BF16)],
        compiler_params=pltpu.CompilerParams(
            dimension_semantics=("arbitrary", "arbitrary"), vmem_limit_bytes=MOBA_VMEM_LIMIT),
        name="moba",
    )(rel_bias, cq, ck, cv, cgate, h, w_out, final_gain)


def kernel(x, norm_gain, final_gain, rel_bias, w_in_even, gla_w_up, gla_b_up, gla_norm_gain, swa_sinks,
           w_out_even, w_in_odd, w_out_odd):
    assert x.shape == (BATCH, SEQ, D_MODEL)
    h0 = x.reshape(TOKENS, D_MODEL)

    we = w_in_even[0]
    splits = np.cumsum([0, GLA_QK, GLA_QK, A_WIDTH, GLA_RANK, A_WIDTH, B_WIDTH, SWA_KV_HEADS * SWA_DH,
                        SWA_KV_HEADS * SWA_DH, B_WIDTH])
    w_aq, w_ak, w_av, w_down, w_agate, w_bq = [we[:, splits[k]:splits[k + 1]] for k in range(6)]
    w_bkv = we[:, splits[6]:splits[8]]
    w_bgate = we[:, splits[8]:splits[9]]
    w_down = jnp.pad(jnp.concatenate([w_down] * 3, axis=1), ((0, 0), (0, LANES - 3 * GLA_RANK)))
    proj_ws = [w.astype(BF16) for w in (w_aq, w_ak, w_av, w_down, w_agate, w_bq, w_bkv, w_bgate)]
    proj_dt = [BF16, BF16, BF16, F32, BF16, BF16, BF16, BF16]
    scales = [GLA_DK ** -0.5, 1.0, 1.0, 1.0, 1.0, SWA_DH ** -0.5 * LOG2E, 1.0, 1.0]
    aq, ak, av, adown, agate, bq, bkv, bgate = _layer_io(
        h0, [], norm_gain[0:1], proj_ws, proj_dt, scales, write_h=False, row_sub=256)

    w_up_hi = gla_w_up[0].astype(BF16)
    w_up_lo = (gla_w_up[0] - w_up_hi.astype(F32)).astype(BF16)
    w_up = jnp.pad(jnp.concatenate([w_up_hi, w_up_hi, w_up_lo], axis=0), ((0, LANES - 3 * GLA_RANK), (0, 0)))
    oa, ob = _gla_swa(aq, ak, av, agate, adown, w_up, gla_b_up[0:1], gla_norm_gain[0:1],
                      bq, bkv, bgate, rel_bias, swa_sinks[0])

    wo = w_out_even[0].astype(BF16)
    wi = w_in_odd[0]
    osp = np.cumsum([0, C_WIDTH, MOBA_KV_HEADS * MOBA_DH, MOBA_KV_HEADS * MOBA_DH, C_WIDTH])
    odd_ws = [wi[:, osp[k]:osp[k + 1]].astype(BF16) for k in range(4)]
    h1, cq, ck, cv, cgate = _layer_io(
        h0, [(oa, wo[:A_WIDTH]), (ob, wo[A_WIDTH:])], norm_gain[1:2], odd_ws, [BF16] * 4,
        [MOBA_DH ** -0.5 * LOG2E, 1.0, 1.0, 1.0], write_h=True, row_sub=ROW_TILE)

    out = _moba(cq, ck, cv, cgate, rel_bias, h1, w_out_odd[0].astype(BF16), final_gain.reshape(1, D_MODEL))
    return out.reshape(BATCH, SEQ, D_MODEL)
```

```python
import functools
import math

import numpy as np
import jax
import jax.numpy as jnp
from jax import lax
from jax.experimental import pallas as pl
from jax.experimental.pallas import tpu as pltpu

D_MODEL = 1024
BATCH = 2
SEQ = 8192
TOKENS = BATCH * SEQ

REL_BUCKETS = 32
REL_MAX_EXACT = 16
REL_MAX_DIST = 128
ATTN_HEADS = 8

GLA_HEADS = 4
GLA_DK = 64
GLA_DV = 128
GLA_RANK = 16
GLA_TAU = 16.0
GLA_CHUNK = 64

SWA_HEADS = 8
SWA_KV_HEADS = 2
SWA_DH = 64
SWA_WINDOW = 128

MOBA_HEADS = 8
MOBA_KV_HEADS = 2
MOBA_DH = 128
MOBA_BLOCK = 256
MOBA_TOPK = 3
MOBA_GROUP = MOBA_HEADS // MOBA_KV_HEADS
MOBA_NBLK = SEQ // MOBA_BLOCK
MOBA_VT_ROWS = MOBA_DH + 16
LOG2E = math.log2(math.e)
SWA_VT_ROWS = SWA_KV_HEADS * SWA_DH + 16

EPS = 1e-6
LANES = 128
NEG_BIG = -1e30
VMEM_LIMIT = 48 * 1024 * 1024
MOBA_VMEM_LIMIT = 56 * 1024 * 1024

A_WIDTH = GLA_HEADS * GLA_DV
B_WIDTH = SWA_HEADS * SWA_DH
C_WIDTH = MOBA_HEADS * MOBA_DH
GLA_QK = GLA_HEADS * GLA_DK

ROW_TILE = 1024
GLA_TILE = 1024
SWA_TILE = 1024

F32 = jnp.float32
BF16 = jnp.bfloat16


def _dot(a, b):
    return jnp.dot(a, b, preferred_element_type=F32)


def _dot_nt(a, b):
    return lax.dot_general(a, b, (((1,), (1,)), ((), ())), preferred_element_type=F32)


def _dot_tn(a, b):
    return lax.dot_general(a, b, (((0,), (0,)), ((), ())), preferred_element_type=F32)


def _silu(x):
    return x * (1.0 / (1.0 + jnp.exp(-x)))


def _t5_thresholds():
    d = np.arange(REL_MAX_DIST + 1)
    nf = np.maximum(d, 1).astype(np.float32)
    large = REL_MAX_EXACT + (np.log(nf / np.float32(REL_MAX_EXACT))
                             / np.float32(math.log(REL_MAX_DIST / REL_MAX_EXACT))
                             * np.float32(REL_BUCKETS - REL_MAX_EXACT)).astype(np.int32)
    bucket = np.where(d < REL_MAX_EXACT, d, np.minimum(large, REL_BUCKETS - 1))
    assert np.all(np.diff(bucket) >= 0) and bucket[-1] == REL_BUCKETS - 1
    return [int(np.argmax(bucket >= b)) for b in range(REL_BUCKETS)]


_T5_THRESH = _t5_thresholds()


def _t5_bias(dist, rb_ref, head):
    out = jnp.full(dist.shape, rb_ref[0, head], F32)
    for b in range(1, REL_BUCKETS):
        out = jnp.where(dist >= _T5_THRESH[b], rb_ref[b, head], out)
    return out


def _layer_io_kernel(*refs, n_acc, n_proj, write_h, scales, row_sub):
    h_ref = refs[0]
    pos = 1
    acc_pairs = []
    for _ in range(n_acc):
        acc_pairs.append((refs[pos], refs[pos + 1]))
        pos += 2
    gain_ref = refs[pos]
    pos += 1
    wp_refs = refs[pos:pos + n_proj]
    pos += n_proj
    out_refs = refs[pos:]

    o = 1 if write_h else 0
    for s in range(ROW_TILE // row_sub):
        rows = slice(s * row_sub, (s + 1) * row_sub)
        h = h_ref[rows, :]
        for a_ref, w_ref in acc_pairs:
            h = h + _dot(a_ref[rows, :], w_ref[...])
        if write_h:
            out_refs[0][rows, :] = h
        y = h * lax.rsqrt(jnp.mean(h * h, axis=-1, keepdims=True) + EPS) * gain_ref[...]
        if n_proj == 0:
            out_refs[o][rows, :] = y
            continue
        yb = y.astype(BF16)
        for k in range(n_proj):
            r = _dot(yb, wp_refs[k][...])
            if scales[k] != 1.0:
                r = r * scales[k]
            out_refs[o + k][rows, :] = r.astype(out_refs[o + k].dtype)


def _layer_io(h, acc_pairs, gain, proj_ws, proj_dtypes, scales, write_h, row_sub):
    n_rows = h.shape[0]
    grid = (n_rows // ROW_TILE,)
    row_spec = lambda n: pl.BlockSpec((ROW_TILE, n), lambda i: (i, 0))
    full_spec = lambda a: pl.BlockSpec(a.shape, lambda i: (0,) * a.ndim, pipeline_mode=pl.Buffered(1))

    args = [h]
    in_specs = [row_spec(D_MODEL)]
    for a, w in acc_pairs:
        args += [a, w]
        in_specs += [row_spec(a.shape[1]), full_spec(w)]
    args.append(gain)
    in_specs.append(full_spec(gain))
    for w in proj_ws:
        args.append(w)
        in_specs.append(full_spec(w))

    out_shape, out_specs = [], []
    if write_h:
        out_shape.append(jax.ShapeDtypeStruct((n_rows, D_MODEL), F32))
        out_specs.append(row_spec(D_MODEL))
    if proj_ws:
        for w, dt in zip(proj_ws, proj_dtypes):
            out_shape.append(jax.ShapeDtypeStruct((n_rows, w.shape[1]), dt))
            out_specs.append(row_spec(w.shape[1]))
    else:
        out_shape.append(jax.ShapeDtypeStruct((n_rows, D_MODEL), F32))
        out_specs.append(row_spec(D_MODEL))

    kern = functools.partial(_layer_io_kernel, n_acc=len(acc_pairs), n_proj=len(proj_ws),
                             write_h=write_h, scales=tuple(scales), row_sub=row_sub)
    return pl.pallas_call(
        kern, grid=grid, in_specs=in_specs, out_specs=out_specs, out_shape=out_shape,
        compiler_params=pltpu.CompilerParams(dimension_semantics=("arbitrary",),
                                             vmem_limit_bytes=VMEM_LIMIT),
        name="layer_io",
    )(*args)


def _gla_init(st_ref):
    @pl.when(pl.program_id(1) == 0)
    def _():
        st_ref[...] = jnp.zeros_like(st_ref)


def _gla_steps(q_ref, k_ref, v_ref, gate_ref, down_ref, wup_ref, bup_ref, gain_ref, o_ref, st_ref):
    C = GLA_CHUNK

    a = down_ref[...]
    a_hi = a.astype(BF16)
    a_lo = (a - a_hi.astype(F32)).astype(BF16)
    lane_a = lax.broadcasted_iota(jnp.int32, a.shape, 1) // GLA_RANK
    z = _dot(jnp.where(lane_a == 1, a_lo, a_hi), wup_ref[...]) + bup_ref[...]
    log_a = (jnp.minimum(z, 0.0) - jnp.log(1.0 + jnp.exp(-jnp.abs(z)))) * (1.0 / GLA_TAU)

    r_i = lax.broadcasted_iota(jnp.int32, (C, C), 0)
    c_i = lax.broadcasted_iota(jnp.int32, (C, C), 1)
    tri = (c_i <= r_i).astype(BF16)
    lane_qk = lax.broadcasted_iota(jnp.int32, (C, GLA_QK), 1)
    head_masks = [(lane_qk // GLA_DK) == h for h in range(GLA_HEADS)]
    rs = lax.broadcasted_iota(jnp.int32, (GLA_HEADS * C, C), 0)
    cs = lax.broadcasted_iota(jnp.int32, (GLA_HEADS * C, C), 1)
    causal = (rs % C) >= cs
    st_r = lax.broadcasted_iota(jnp.int32, (A_WIDTH, GLA_QK), 0)
    st_c = lax.broadcasted_iota(jnp.int32, (A_WIDTH, GLA_QK), 1)
    same_head = (st_r // GLA_DV) == (st_c // GLA_DK)
    gain = gain_ref[...]

    def chunk(c):
        rows = slice(c * C, (c + 1) * C)
        g = log_a[rows]
        g_hi = g.astype(BF16)
        g_lo = (g - g_hi.astype(F32)).astype(BF16)
        b = _dot(tri, g_hi) + _dot(tri, g_lo)
        b_last = b[C - 1:C]
        q = q_ref[rows, :].astype(F32)
        k = k_ref[rows, :].astype(F32)
        v = v_ref[rows, :]
        q_e = q * jnp.exp(b)
        k_e = (k * jnp.exp(-b)).astype(BF16)
        k_l = (k * jnp.exp(b_last - b)).astype(BF16)
        decay = jnp.exp(b_last)

        q_stack = jnp.concatenate([jnp.where(m, q_e, 0.0) for m in head_masks], axis=0).astype(BF16)
        att = jnp.where(causal, _dot_nt(q_stack, k_e), 0.0).astype(BF16)
        o_full = _dot(att, v)
        o_intra = jnp.concatenate(
            [o_full[h * C:(h + 1) * C, h * GLA_DV:(h + 1) * GLA_DV] for h in range(GLA_HEADS)], axis=1)
        st = st_ref[...]
        o = o_intra + _dot_nt(q_e.astype(BF16), st.astype(BF16))
        kv_t = _dot_tn(v, k_l)
        st_ref[...] = st * decay + jnp.where(same_head, kv_t, 0.0)

        outs = []
        for h in range(GLA_HEADS):
            oh = o[:, h * GLA_DV:(h + 1) * GLA_DV]
            outs.append(oh * lax.rsqrt(jnp.mean(oh * oh, axis=-1, keepdims=True) + EPS) * gain)
        on = jnp.concatenate(outs, axis=1)
        o_ref[rows, :] = (on * _silu(gate_ref[rows, :].astype(F32))).astype(o_ref.dtype)

    return [functools.partial(chunk, c) for c in range(GLA_TILE // C)]


def _swa_init(rb_ref, bias_ref):
    W = SWA_WINDOW

    @pl.when(jnp.logical_and(pl.program_id(0) == 0, pl.program_id(1) == 0))
    def _():
        kj = lax.broadcasted_iota(jnp.int32, (2 * W, W), 0)
        qi = lax.broadcasted_iota(jnp.int32, (2 * W, W), 1)
        dist = qi + W - kj
        in_win = jnp.logical_and(dist >= 0, dist < W)
        for h in range(SWA_HEADS):
            bias_ref[:, h * W:(h + 1) * W] = jnp.where(in_win, _t5_bias(dist, rb_ref, h) * LOG2E, -jnp.inf)


def _swa_steps(sink_ref, q_ref, kp_ref, k_ref, vp_ref, v_ref, gate_ref, o_ref, bias_ref, s_ref, smax_ref):
    W = SWA_WINDOW
    H = SWA_HEADS
    DH = SWA_DH
    G = SWA_HEADS // SWA_KV_HEADS
    t = pl.program_id(1)

    kcat = jnp.concatenate([kp_ref[...], k_ref[...]], axis=0)
    vcat = jnp.concatenate([vp_ref[...], v_ref[...]], axis=0)
    vt = vcat.astype(F32).T.astype(BF16)
    ones_rows = jnp.ones((SWA_VT_ROWS - LANES, 2 * W), BF16)
    q_t = q_ref[...].astype(F32).T.astype(BF16)
    zeros = jnp.zeros((DH, W), BF16)
    colh = lax.broadcasted_iota(jnp.int32, (1, H * W), 1) // W
    sink = jnp.full((1, H * W), sink_ref[H - 1], F32)
    for h in range(H - 2, -1, -1):
        sink = jnp.where(colh == h, sink_ref[h], sink)
    sink = sink * LOG2E
    key_prev = lax.broadcasted_iota(jnp.int32, (2 * W, H * W), 0) < W

    def produce(blk):
        tok = slice(blk * W, (blk + 1) * W)
        win = slice(blk * W, blk * W + 2 * W)
        pieces = []
        for h in range(H):
            piece = q_t[h * DH:(h + 1) * DH, tok]
            pieces.append(jnp.concatenate([piece, zeros] if h < G else [zeros, piece], axis=0))
        q_pad = jnp.concatenate(pieces, axis=1)
        s = _dot(kcat[win], q_pad) + bias_ref[...]
        if blk == 0:
            s = jnp.where(jnp.logical_and(t == 0, key_prev), -jnp.inf, s)
        s_ref[blk % 2] = s
        smax_ref[blk % 2] = jnp.maximum(jnp.max(s, axis=0, keepdims=True), sink)

    def consume(blk):
        tok = slice(blk * W, (blk + 1) * W)
        win = slice(blk * W, blk * W + 2 * W)
        m = smax_ref[blk % 2]
        p = jnp.exp2(s_ref[blk % 2] - m).astype(BF16)
        pv = _dot(jnp.concatenate([vt[:, win], ones_rows], axis=0), p)
        inv = 1.0 / (pv[LANES:LANES + 1] + jnp.exp2(sink - m))
        o_t = jnp.concatenate(
            [pv[(h // G) * DH:(h // G + 1) * DH, h * W:(h + 1) * W] * inv[:, h * W:(h + 1) * W]
             for h in range(H)], axis=0)
        gate = gate_ref[tok, :].astype(F32)
        o_ref[tok, :] = (o_t.T * _silu(gate)).astype(o_ref.dtype)

    n_blk = SWA_TILE // W
    steps = [functools.partial(produce, 0)]
    for blk in range(n_blk):
        if blk + 1 < n_blk:
            steps.append(functools.partial(produce, blk + 1))
        steps.append(functools.partial(consume, blk))
    return steps


def _gla_swa_kernel(aq_ref, ak_ref, av_ref, agate_ref, adown_ref, wup_ref, bup_ref, gain_ref,
                    sink_ref, rb_ref, bq_ref, kp_ref, k_ref, vp_ref, v_ref, bgate_ref,
                    oa_ref, ob_ref, st_ref, bias_ref, s_ref, smax_ref):
    _gla_init(st_ref)
    _swa_init(rb_ref, bias_ref)
    gla = _gla_steps(aq_ref, ak_ref, av_ref, agate_ref, adown_ref, wup_ref, bup_ref, gain_ref, oa_ref, st_ref)
    swa = _swa_steps(sink_ref, bq_ref, kp_ref, k_ref, vp_ref, v_ref, bgate_ref, ob_ref, bias_ref, s_ref, smax_ref)
    for step in gla + swa:
        step()


def _gla_swa(aq, ak, av, agate, adown, w_up, b_up, gain, bq, bkv, bgate, rel_bias, sinks):
    assert GLA_TILE == SWA_TILE
    nt = SEQ // SWA_TILE
    per = SWA_TILE // SWA_WINDOW
    row = lambda n: pl.BlockSpec((SWA_TILE, n), lambda b, t: (b * nt + t, 0))
    full = lambda a: pl.BlockSpec(a.shape, lambda b, t: (0,) * a.ndim)
    own = lambda c: pl.BlockSpec((SWA_TILE, LANES), lambda b, t: (b * nt + t, c))
    prev = lambda c: pl.BlockSpec((SWA_WINDOW, LANES),
                                  lambda b, t: (jnp.maximum((b * nt + t) * per - 1, 0), c))
    smem = pl.BlockSpec(memory_space=pltpu.SMEM)
    return pl.pallas_call(
        _gla_swa_kernel, grid=(BATCH, nt),
        in_specs=[row(GLA_QK), row(GLA_QK), row(A_WIDTH), row(A_WIDTH), row(LANES),
                  full(w_up), full(b_up), full(gain),
                  smem, smem, row(B_WIDTH), prev(0), own(0), prev(1), own(1), row(B_WIDTH)],
        out_specs=[row(A_WIDTH), row(B_WIDTH)],
        out_shape=[jax.ShapeDtypeStruct((TOKENS, A_WIDTH), BF16),
                   jax.ShapeDtypeStruct((TOKENS, B_WIDTH), BF16)],
        scratch_shapes=[pltpu.VMEM((A_WIDTH, GLA_QK), F32),
                        pltpu.VMEM((2 * SWA_WINDOW, SWA_HEADS * SWA_WINDOW), F32),
                        pltpu.VMEM((2, 2 * SWA_WINDOW, SWA_HEADS * SWA_WINDOW), F32),
                        pltpu.VMEM((2, 1, SWA_HEADS * SWA_WINDOW), F32)],
        compiler_params=pltpu.CompilerParams(dimension_semantics=("arbitrary", "arbitrary"),
                                             vmem_limit_bytes=VMEM_LIMIT),
        name="gla_swa",
    )(aq, ak, av, agate, adown, w_up, b_up, gain, sinks, rel_bias, bq, bkv, bkv, bkv, bkv, bgate)


def _moba_kernel(rb_ref, q_ref, k_ref, v_ref, gate_ref, o_ref,
                 kaug_ref, vt_ref, km_ref, bias_ref, qat_ref, s_ref, smax_ref, m_ref, acc_ref):
    BLK = MOBA_BLOCK
    G = MOBA_GROUP
    KVH = MOBA_KV_HEADS
    R = G * BLK
    NP1 = MOBA_NBLK + 1
    i = pl.program_id(1)

    @pl.when(i == 0)
    def _():
        lane = lax.broadcasted_iota(jnp.int32, (BLK, LANES), 1)
        ones_rows = jnp.ones((MOBA_VT_ROWS - MOBA_DH, BLK), BF16)

        def prep(j, carry):
            rows = pl.ds(pl.multiple_of(j * BLK, BLK), BLK)
            onehot = jnp.where(lane == j, 1.0, 0.0).astype(BF16)
            for kh in range(KVH):
                kb = k_ref[rows, kh * MOBA_DH:(kh + 1) * MOBA_DH]
                kaug_ref[kh * NP1 + j + 1, :, 0:LANES] = kb
                kaug_ref[kh * NP1 + j + 1, :, LANES:2 * LANES] = onehot
                km_ref[pl.ds(kh * MOBA_NBLK + j, 1), :] = jnp.mean(kb.astype(F32), axis=0, keepdims=True)
                vb = v_ref[rows, kh * MOBA_DH:(kh + 1) * MOBA_DH]
                vt_ref[kh * NP1 + j + 1, 0:MOBA_DH, :] = vb.astype(F32).T.astype(BF16)
                vt_ref[kh * NP1 + j + 1, MOBA_DH:, :] = ones_rows
            return carry

        lax.fori_loop(0, MOBA_NBLK, prep, 0)
        for kh in range(KVH):
            kaug_ref[kh * NP1, :, 0:LANES] = jnp.zeros((BLK, LANES), BF16)
            kaug_ref[kh * NP1, :, LANES:2 * LANES] = jnp.where(lane == MOBA_NBLK, 1.0, 0.0).astype(BF16)
            vt_ref[kh * NP1] = jnp.zeros((MOBA_VT_ROWS, BLK), BF16)
            qat_ref[kh, MOBA_DH + MOBA_NBLK:, :] = jnp.full((2 * LANES - MOBA_DH - MOBA_NBLK, R), NEG_BIG, BF16)

    @pl.when(jnp.logical_and(pl.program_id(0) == 0, i == 0))
    def _():
        tk = lax.broadcasted_iota(jnp.int32, (BLK, BLK), 0)
        tq = lax.broadcasted_iota(jnp.int32, (BLK, BLK), 1)
        d_own = tq - tk
        for head in range(MOBA_HEADS):
            cols = slice(head * BLK, (head + 1) * BLK)
            bias_ref[0:BLK, cols] = _t5_bias(d_own + BLK, rb_ref, head) * LOG2E
            bias_ref[BLK:2 * BLK, cols] = jnp.where(d_own >= 0, _t5_bias(d_own, rb_ref, head) * LOG2E, -jnp.inf)

    blk = lax.broadcasted_iota(jnp.int32, (MOBA_NBLK, R), 0)
    blkf = blk.astype(F32)
    past = blk < i
    colh = lax.broadcasted_iota(jnp.int32, (1, R), 1) // BLK
    far_row = REL_BUCKETS - 1

    own_max = []
    for kh in range(KVH):
        q = jnp.concatenate([q_ref[:, (kh * G + g) * MOBA_DH:(kh * G + g + 1) * MOBA_DH] for g in range(G)],
                            axis=0)
        q_t = q.astype(F32).T.astype(BF16)
        qat_ref[kh, 0:MOBA_DH, :] = q_t
        s_own = (_dot(kaug_ref[kh * NP1 + i + 1, :, 0:MOBA_DH], q_t)
                 + bias_ref[BLK:2 * BLK, kh * R:(kh + 1) * R])
        s_ref[kh, BLK:2 * BLK, :] = s_own
        own_max.append(jnp.max(s_own, axis=0, keepdims=True))

        km = km_ref[kh * MOBA_NBLK:(kh + 1) * MOBA_NBLK, :]
        km_hi = km.astype(BF16)
        km_lo = (km - km_hi.astype(F32)).astype(BF16)
        gate = _dot(km_hi, q_t) + _dot(km_lo, q_t)
        g_ = jnp.where(past, gate, -jnp.inf)
        selected = jnp.zeros((MOBA_NBLK, R), dtype=jnp.bool_)
        for _ in range(MOBA_TOPK):
            mx = jnp.max(g_, axis=0, keepdims=True)
            first = jnp.min(jnp.where(g_ == mx, blkf, 1e9), axis=0, keepdims=True)
            pick = jnp.logical_and(blkf == first, past)
            selected = jnp.logical_or(selected, pick)
            g_ = jnp.where(pick, -jnp.inf, g_)
        cfar = jnp.where(colh == 0, rb_ref[far_row, kh * G],
                         jnp.where(colh == 1, rb_ref[far_row, kh * G + 1],
                                   jnp.where(colh == 2, rb_ref[far_row, kh * G + 2],
                                             rb_ref[far_row, kh * G + 3])))
        sel_bias = jnp.where(selected, jnp.where(blk < i - 1, cfar * LOG2E, 0.0), NEG_BIG)
        qat_ref[kh, MOBA_DH:MOBA_DH + MOBA_NBLK, :] = sel_bias.astype(BF16)

    for kh in range(KVH):
        s_near = _dot(kaug_ref[kh * NP1 + i], qat_ref[kh]) + bias_ref[0:BLK, kh * R:(kh + 1) * R]
        s_ref[kh, 0:BLK, :] = s_near
        smax_ref[kh] = jnp.maximum(own_max[kh], jnp.max(s_near, axis=0, keepdims=True))

    def produce(slot, kh, start):
        k2 = kaug_ref[pl.ds(kh * NP1 + start, 2)].reshape(2 * BLK, 2 * LANES)
        s = _dot(k2, qat_ref[kh])
        s_ref[slot * KVH + kh] = s
        smax_ref[slot * KVH + kh] = jnp.max(s, axis=0, keepdims=True)

    def consume(slot, kh, start, first=False):
        m_new = smax_ref[slot * KVH + kh]
        if not first:
            m_old = m_ref[kh]
            m_new = jnp.maximum(m_old, m_new)
        p = jnp.exp2(s_ref[slot * KVH + kh] - m_new).astype(BF16)
        pv = (_dot(vt_ref[kh * NP1 + start], p[0:BLK])
              + _dot(vt_ref[kh * NP1 + start + 1], p[BLK:2 * BLK]))
        acc_ref[kh] = pv if first else jnp.exp2(m_old - m_new) * acc_ref[kh] + pv
        m_ref[kh] = m_new

    n_far = jnp.maximum(i - 1, 0)
    n_pairs = (n_far + 1) // 2

    def far_start(t):
        return jnp.maximum(n_far - 2 * t - 1, 0)

    def far_step(t, slot):
        for kh in range(KVH):
            produce(1 - slot, kh, far_start(t + 1))
        for kh in range(KVH):
            consume(slot, kh, far_start(t))

    for kh in range(KVH):
        produce(1, kh, far_start(0))
    for kh in range(KVH):
        consume(0, kh, i, first=True)

    def far_four(u, carry):
        for d in range(4):
            far_step(4 * u + d, (d + 1) % 2)
        return carry

    n_full = jnp.maximum(n_pairs - 1, 0)
    lax.fori_loop(0, n_full // 4, far_four, 0)
    done = (n_full // 4) * 4

    @pl.when(n_full - done >= 2)
    def _():
        far_step(done, 1)
        far_step(done + 1, 0)

    @pl.when(n_full % 2 == 1)
    def _():
        far_step(n_full - 1, 1)

    def finalize(kh):
        acc = acc_ref[kh]
        o = (acc[0:MOBA_DH] * (1.0 / acc[MOBA_DH:MOBA_DH + 1])).T
        for g in range(G):
            cols = slice((kh * G + g) * MOBA_DH, (kh * G + g + 1) * MOBA_DH)
            gt = gate_ref[:, cols].astype(F32)
            o_ref[:, cols] = (o[g * BLK:(g + 1) * BLK] * _silu(gt)).astype(o_ref.dtype)

    for slot in range(2):
        @pl.when(jnp.logical_and(n_pairs >= 1, n_pairs % 2 == slot))
        def _():
            for kh in range(KVH):
                consume(slot, kh, far_start(n_pairs - 1))
                finalize(kh)

    @pl.when(n_pairs == 0)
    def _():
        for kh in range(KVH):
            finalize(kh)


def _moba(cq, ck, cv, cgate, rel_bias):
    R = MOBA_GROUP * MOBA_BLOCK
    KVH = MOBA_KV_HEADS
    NP1 = MOBA_NBLK + 1
    qspec = pl.BlockSpec((MOBA_BLOCK, C_WIDTH), lambda b, i: (b * MOBA_NBLK + i, 0))
    kvspec = pl.BlockSpec((SEQ, KVH * MOBA_DH), lambda b, i: (b, 0), pipeline_mode=pl.Buffered(1))
    return pl.pallas_call(
        _moba_kernel, grid=(BATCH, MOBA_NBLK),
        in_specs=[pl.BlockSpec(memory_space=pltpu.SMEM), qspec, kvspec, kvspec, qspec],
        out_specs=qspec,
        out_shape=jax.ShapeDtypeStruct((TOKENS, C_WIDTH), BF16),
        scratch_shapes=[pltpu.VMEM((KVH * NP1, MOBA_BLOCK, 2 * LANES), BF16),
                        pltpu.VMEM((KVH * NP1, MOBA_VT_ROWS, MOBA_BLOCK), BF16),
                        pltpu.VMEM((KVH * MOBA_NBLK, MOBA_DH), F32),
                        pltpu.VMEM((2 * MOBA_BLOCK, KVH * R), F32),
                        pltpu.VMEM((KVH, 2 * LANES, R), BF16),
                        pltpu.VMEM((2 * KVH, 2 * MOBA_BLOCK, R), F32),
                        pltpu.VMEM((2 * KVH, 1, R), F32),
                        pltpu.VMEM((KVH, 1, R), F32),
                        pltpu.VMEM((KVH, MOBA_VT_ROWS, R), F32)],
        compiler_params=pltpu.CompilerParams(
            dimension_semantics=("arbitrary", "arbitrary"), vmem_limit_bytes=MOBA_VMEM_LIMIT),
        name="moba",
    )(rel_bias, cq, ck, cv, cgate)


def kernel(x, norm_gain, final_gain, rel_bias, w_in_even, gla_w_up, gla_b_up, gla_norm_gain, swa_sinks,
           w_out_even, w_in_odd, w_out_odd):
    assert x.shape == (BATCH, SEQ, D_MODEL)
    h0 = x.reshape(TOKENS, D_MODEL)

    we = w_in_even[0]
    splits = np.cumsum([0, GLA_QK, GLA_QK, A_WIDTH, GLA_RANK, A_WIDTH, B_WIDTH, SWA_KV_HEADS * SWA_DH,
                        SWA_KV_HEADS * SWA_DH, B_WIDTH])
    w_aq, w_ak, w_av, w_down, w_agate, w_bq = [we[:, splits[k]:splits[k + 1]] for k in range(6)]
    w_bkv = we[:, splits[6]:splits[8]]
    w_bgate = we[:, splits[8]:splits[9]]
    w_down = jnp.pad(jnp.concatenate([w_down] * 3, axis=1), ((0, 0), (0, LANES - 3 * GLA_RANK)))
    proj_ws = [w.astype(BF16) for w in (w_aq, w_ak, w_av, w_down, w_agate, w_bq, w_bkv, w_bgate)]
    proj_dt = [BF16, BF16, BF16, F32, BF16, BF16, BF16, BF16]
    scales = [GLA_DK ** -0.5, 1.0, 1.0, 1.0, 1.0, SWA_DH ** -0.5 * LOG2E, 1.0, 1.0]
    aq, ak, av, adown, agate, bq, bkv, bgate = _layer_io(
        h0, [], norm_gain[0:1], proj_ws, proj_dt, scales, write_h=False, row_sub=256)

    w_up_hi = gla_w_up[0].astype(BF16)
    w_up_lo = (gla_w_up[0] - w_up_hi.astype(F32)).astype(BF16)
    w_up = jnp.pad(jnp.concatenate([w_up_hi, w_up_hi, w_up_lo], axis=0), ((0, LANES - 3 * GLA_RANK), (0, 0)))
    oa, ob = _gla_swa(aq, ak, av, agate, adown, w_up, gla_b_up[0:1], gla_norm_gain[0:1],
                      bq, bkv, bgate, rel_bias, swa_sinks[0])

    wo = w_out_even[0].astype(BF16)
    wi = w_in_odd[0]
    osp = np.cumsum([0, C_WIDTH, MOBA_KV_HEADS * MOBA_DH, MOBA_KV_HEADS * MOBA_DH, C_WIDTH])
    odd_ws = [wi[:, osp[k]:osp[k + 1]].astype(BF16) for k in range(4)]
    h1, cq, ck, cv, cgate = _layer_io(
        h0, [(oa, wo[:A_WIDTH]), (ob, wo[A_WIDTH:])], norm_gain[1:2], odd_ws, [BF16] * 4,
        [MOBA_DH ** -0.5 * LOG2E, 1.0, 1.0, 1.0], write_h=True, row_sub=ROW_TILE)

    oc = _moba(cq, ck, cv, cgate, rel_bias)

    (out,) = _layer_io(h1, [(oc, w_out_odd[0].astype(BF16))], final_gain.reshape(1, D_MODEL), [], [], [],
                       write_h=False, row_sub=256)
    return out.reshape(BATCH, SEQ, D_MODEL)
```

```python
import functools
import math

import numpy as np
import jax
import jax.numpy as jnp
from jax import lax
from jax.experimental import pallas as pl
from jax.experimental.pallas import tpu as pltpu

D_MODEL = 1024
BATCH = 2
SEQ = 8192
TOKENS = BATCH * SEQ

REL_BUCKETS = 32
REL_MAX_EXACT = 16
REL_MAX_DIST = 128
ATTN_HEADS = 8

GLA_HEADS = 4
GLA_DK = 64
GLA_DV = 128
GLA_RANK = 16
GLA_TAU = 16.0
GLA_CHUNK = 64

SWA_HEADS = 8
SWA_KV_HEADS = 2
SWA_DH = 64
SWA_WINDOW = 128

MOBA_HEADS = 8
MOBA_KV_HEADS = 2
MOBA_DH = 128
MOBA_BLOCK = 256
MOBA_TOPK = 3
MOBA_GROUP = MOBA_HEADS // MOBA_KV_HEADS
MOBA_NBLK = SEQ // MOBA_BLOCK
MOBA_VT_ROWS = MOBA_DH + 16
LOG2E = math.log2(math.e)
SWA_VT_ROWS = SWA_KV_HEADS * SWA_DH + 16

EPS = 1e-6
LANES = 128
NEG_BIG = -1e30
VMEM_LIMIT = 48 * 1024 * 1024
MOBA_VMEM_LIMIT = 56 * 1024 * 1024

A_WIDTH = GLA_HEADS * GLA_DV
B_WIDTH = SWA_HEADS * SWA_DH
C_WIDTH = MOBA_HEADS * MOBA_DH
GLA_QK = GLA_HEADS * GLA_DK

GLA_TILE = 1024
SWA_TILE = 1024

F32 = jnp.float32
BF16 = jnp.bfloat16


def _dot(a, b):
    return jnp.dot(a, b, preferred_element_type=F32)


def _dot_nt(a, b):
    return lax.dot_general(a, b, (((1,), (1,)), ((), ())), preferred_element_type=F32)


def _dot_tn(a, b):
    return lax.dot_general(a, b, (((0,), (0,)), ((), ())), preferred_element_type=F32)


def _silu(x):
    return x * (1.0 / (1.0 + jnp.exp(-x)))


def _t5_thresholds():
    d = np.arange(REL_MAX_DIST + 1)
    nf = np.maximum(d, 1).astype(np.float32)
    large = REL_MAX_EXACT + (np.log(nf / np.float32(REL_MAX_EXACT))
                             / np.float32(math.log(REL_MAX_DIST / REL_MAX_EXACT))
                             * np.float32(REL_BUCKETS - REL_MAX_EXACT)).astype(np.int32)
    bucket = np.where(d < REL_MAX_EXACT, d, np.minimum(large, REL_BUCKETS - 1))
    assert np.all(np.diff(bucket) >= 0) and bucket[-1] == REL_BUCKETS - 1
    return [int(np.argmax(bucket >= b)) for b in range(REL_BUCKETS)]


_T5_THRESH = _t5_thresholds()


def _t5_bias(dist, rb_ref, head):
    out = jnp.full(dist.shape, rb_ref[0, head], F32)
    for b in range(1, REL_BUCKETS):
        out = jnp.where(dist >= _T5_THRESH[b], rb_ref[b, head], out)
    return out


def _layer_io_kernel(*refs, n_acc, n_proj, write_h, scales, row_tile, row_sub):
    h_ref = refs[0]
    pos = 1
    acc_pairs = []
    for _ in range(n_acc):
        acc_pairs.append((refs[pos], refs[pos + 1]))
        pos += 2
    gain_ref = refs[pos]
    pos += 1
    wp_refs = refs[pos:pos + n_proj]
    pos += n_proj
    out_refs = refs[pos:]

    o = 1 if write_h else 0
    for s in range(row_tile // row_sub):
        rows = slice(s * row_sub, (s + 1) * row_sub)
        h = h_ref[rows, :]
        for a_ref, w_ref in acc_pairs:
            h = h + _dot(a_ref[rows, :], w_ref[...])
        if write_h:
            out_refs[0][rows, :] = h
        y = h * lax.rsqrt(jnp.mean(h * h, axis=-1, keepdims=True) + EPS) * gain_ref[...]
        if n_proj == 0:
            out_refs[o][rows, :] = y
            continue
        yb = y.astype(BF16)
        for k in range(n_proj):
            r = _dot(yb, wp_refs[k][...])
            if scales[k] != 1.0:
                r = r * scales[k]
            out_refs[o + k][rows, :] = r.astype(out_refs[o + k].dtype)


def _layer_io(h, acc_pairs, gain, proj_ws, proj_dtypes, scales, write_h, row_tile, row_sub):
    n_rows = h.shape[0]
    grid = (n_rows // row_tile,)
    row_spec = lambda n: pl.BlockSpec((row_tile, n), lambda i: (i, 0))
    full_spec = lambda a: pl.BlockSpec(a.shape, lambda i: (0,) * a.ndim, pipeline_mode=pl.Buffered(1))

    args = [h]
    in_specs = [row_spec(D_MODEL)]
    for a, w in acc_pairs:
        args += [a, w]
        in_specs += [row_spec(a.shape[1]), full_spec(w)]
    args.append(gain)
    in_specs.append(full_spec(gain))
    for w in proj_ws:
        args.append(w)
        in_specs.append(full_spec(w))

    out_shape, out_specs = [], []
    if write_h:
        out_shape.append(jax.ShapeDtypeStruct((n_rows, D_MODEL), F32))
        out_specs.append(row_spec(D_MODEL))
    if proj_ws:
        for w, dt in zip(proj_ws, proj_dtypes):
            out_shape.append(jax.ShapeDtypeStruct((n_rows, w.shape[1]), dt))
            out_specs.append(row_spec(w.shape[1]))
    else:
        out_shape.append(jax.ShapeDtypeStruct((n_rows, D_MODEL), F32))
        out_specs.append(row_spec(D_MODEL))

    kern = functools.partial(_layer_io_kernel, n_acc=len(acc_pairs), n_proj=len(proj_ws),
                             write_h=write_h, scales=tuple(scales), row_tile=row_tile, row_sub=row_sub)
    return pl.pallas_call(
        kern, grid=grid, in_specs=in_specs, out_specs=out_specs, out_shape=out_shape,
        compiler_params=pltpu.CompilerParams(dimension_semantics=("arbitrary",),
                                             vmem_limit_bytes=VMEM_LIMIT),
        name="layer_io",
    )(*args)


def _gla_init(st_ref):
    @pl.when(pl.program_id(1) == 0)
    def _():
        st_ref[...] = jnp.zeros_like(st_ref)


def _gla_steps(q_ref, k_ref, v_ref, gate_ref, down_ref, wup_ref, bup_ref, gain_ref, o_ref, st_ref):
    C = GLA_CHUNK

    a = down_ref[...]
    a_hi = a.astype(BF16)
    a_lo = (a - a_hi.astype(F32)).astype(BF16)
    lane_a = lax.broadcasted_iota(jnp.int32, a.shape, 1) // GLA_RANK
    z = _dot(jnp.where(lane_a == 1, a_lo, a_hi), wup_ref[...]) + bup_ref[...]
    log_a = (jnp.minimum(z, 0.0) - jnp.log(1.0 + jnp.exp(-jnp.abs(z)))) * (1.0 / GLA_TAU)

    r_i = lax.broadcasted_iota(jnp.int32, (C, C), 0)
    c_i = lax.broadcasted_iota(jnp.int32, (C, C), 1)
    tri = (c_i <= r_i).astype(BF16)
    lane_qk = lax.broadcasted_iota(jnp.int32, (C, GLA_QK), 1)
    head_masks = [(lane_qk // GLA_DK) == h for h in range(GLA_HEADS)]
    rs = lax.broadcasted_iota(jnp.int32, (GLA_HEADS * C, C), 0)
    cs = lax.broadcasted_iota(jnp.int32, (GLA_HEADS * C, C), 1)
    causal = (rs % C) >= cs
    st_r = lax.broadcasted_iota(jnp.int32, (A_WIDTH, GLA_QK), 0)
    st_c = lax.broadcasted_iota(jnp.int32, (A_WIDTH, GLA_QK), 1)
    same_head = (st_r // GLA_DV) == (st_c // GLA_DK)
    gain = gain_ref[...]

    def chunk(c):
        rows = slice(c * C, (c + 1) * C)
        g = log_a[rows]
        g_hi = g.astype(BF16)
        g_lo = (g - g_hi.astype(F32)).astype(BF16)
        b = _dot(tri, g_hi) + _dot(tri, g_lo)
        b_last = b[C - 1:C]
        q = q_ref[rows, :].astype(F32)
        k = k_ref[rows, :].astype(F32)
        v = v_ref[rows, :]
        q_e = q * jnp.exp(b)
        k_e = (k * jnp.exp(-b)).astype(BF16)
        k_l = (k * jnp.exp(b_last - b)).astype(BF16)
        decay = jnp.exp(b_last)

        q_stack = jnp.concatenate([jnp.where(m, q_e, 0.0) for m in head_masks], axis=0).astype(BF16)
        att = jnp.where(causal, _dot_nt(q_stack, k_e), 0.0).astype(BF16)
        o_full = _dot(att, v)
        o_intra = jnp.concatenate(
            [o_full[h * C:(h + 1) * C, h * GLA_DV:(h + 1) * GLA_DV] for h in range(GLA_HEADS)], axis=1)
        st = st_ref[...]
        o = o_intra + _dot_nt(q_e.astype(BF16), st.astype(BF16))
        kv_t = _dot_tn(v, k_l)
        st_ref[...] = st * decay + jnp.where(same_head, kv_t, 0.0)

        outs = []
        for h in range(GLA_HEADS):
            oh = o[:, h * GLA_DV:(h + 1) * GLA_DV]
            outs.append(oh * lax.rsqrt(jnp.mean(oh * oh, axis=-1, keepdims=True) + EPS) * gain)
        on = jnp.concatenate(outs, axis=1)
        o_ref[rows, :] = (on * _silu(gate_ref[rows, :].astype(F32))).astype(o_ref.dtype)

    return [functools.partial(chunk, c) for c in range(GLA_TILE // C)]


def _swa_init(rb_ref, bias_ref):
    W = SWA_WINDOW

    @pl.when(jnp.logical_and(pl.program_id(0) == 0, pl.program_id(1) == 0))
    def _():
        kj = lax.broadcasted_iota(jnp.int32, (2 * W, W), 0)
        qi = lax.broadcasted_iota(jnp.int32, (2 * W, W), 1)
        dist = qi + W - kj
        in_win = jnp.logical_and(dist >= 0, dist < W)
        for h in range(SWA_HEADS):
            bias_ref[:, h * W:(h + 1) * W] = jnp.where(in_win, _t5_bias(dist, rb_ref, h) * LOG2E, -jnp.inf)


def _swa_steps(sink_ref, q_ref, kp_ref, k_ref, vp_ref, v_ref, gate_ref, o_ref, bias_ref, s_ref, smax_ref):
    W = SWA_WINDOW
    H = SWA_HEADS
    DH = SWA_DH
    G = SWA_HEADS // SWA_KV_HEADS
    t = pl.program_id(1)

    kcat = jnp.concatenate([kp_ref[...], k_ref[...]], axis=0)
    vcat = jnp.concatenate([vp_ref[...], v_ref[...]], axis=0)
    vt = vcat.astype(F32).T.astype(BF16)
    ones_rows = jnp.ones((SWA_VT_ROWS - LANES, 2 * W), BF16)
    q_t = q_ref[...].astype(F32).T.astype(BF16)
    zeros = jnp.zeros((DH, W), BF16)
    colh = lax.broadcasted_iota(jnp.int32, (1, H * W), 1) // W
    sink = jnp.full((1, H * W), sink_ref[H - 1], F32)
    for h in range(H - 2, -1, -1):
        sink = jnp.where(colh == h, sink_ref[h], sink)
    sink = sink * LOG2E
    key_prev = lax.broadcasted_iota(jnp.int32, (2 * W, H * W), 0) < W

    def produce(blk):
        tok = slice(blk * W, (blk + 1) * W)
        win = slice(blk * W, blk * W + 2 * W)
        pieces = []
        for h in range(H):
            piece = q_t[h * DH:(h + 1) * DH, tok]
            pieces.append(jnp.concatenate([piece, zeros] if h < G else [zeros, piece], axis=0))
        q_pad = jnp.concatenate(pieces, axis=1)
        s = _dot(kcat[win], q_pad) + bias_ref[...]
        if blk == 0:
            s = jnp.where(jnp.logical_and(t == 0, key_prev), -jnp.inf, s)
        s_ref[blk % 2] = s
        smax_ref[blk % 2] = jnp.maximum(jnp.max(s, axis=0, keepdims=True), sink)

    def consume(blk):
        tok = slice(blk * W, (blk + 1) * W)
        win = slice(blk * W, blk * W + 2 * W)
        m = smax_ref[blk % 2]
        p = jnp.exp2(s_ref[blk % 2] - m).astype(BF16)
        pv = _dot(jnp.concatenate([vt[:, win], ones_rows], axis=0), p)
        inv = 1.0 / (pv[LANES:LANES + 1] + jnp.exp2(sink - m))
        o_t = jnp.concatenate(
            [pv[(h // G) * DH:(h // G + 1) * DH, h * W:(h + 1) * W] * inv[:, h * W:(h + 1) * W]
             for h in range(H)], axis=0)
        gate = gate_ref[tok, :].astype(F32)
        o_ref[tok, :] = (o_t.T * _silu(gate)).astype(o_ref.dtype)

    n_blk = SWA_TILE // W
    steps = [functools.partial(produce, 0)]
    for blk in range(n_blk):
        if blk + 1 < n_blk:
            steps.append(functools.partial(produce, blk + 1))
        steps.append(functools.partial(consume, blk))
    return steps


def _gla_swa_kernel(aq_ref, ak_ref, av_ref, agate_ref, adown_ref, wup_ref, bup_ref, gain_ref,
                    sink_ref, rb_ref, bq_ref, kp_ref, k_ref, vp_ref, v_ref, bgate_ref,
                    oa_ref, ob_ref, st_ref, bias_ref, s_ref, smax_ref):
    _gla_init(st_ref)
    _swa_init(rb_ref, bias_ref)
    gla = _gla_steps(aq_ref, ak_ref, av_ref, agate_ref, adown_ref, wup_ref, bup_ref, gain_ref, oa_ref, st_ref)
    swa = _swa_steps(sink_ref, bq_ref, kp_ref, k_ref, vp_ref, v_ref, bgate_ref, ob_ref, bias_ref, s_ref, smax_ref)
    for step in gla + swa:
        step()


def _gla_swa(aq, ak, av, agate, adown, w_up, b_up, gain, bq, bkv, bgate, rel_bias, sinks):
    assert GLA_TILE == SWA_TILE
    nt = SEQ // SWA_TILE
    per = SWA_TILE // SWA_WINDOW
    row = lambda n: pl.BlockSpec((SWA_TILE, n), lambda b, t: (b * nt + t, 0))
    full = lambda a: pl.BlockSpec(a.shape, lambda b, t: (0,) * a.ndim)
    own = lambda c: pl.BlockSpec((SWA_TILE, LANES), lambda b, t: (b * nt + t, c))
    prev = lambda c: pl.BlockSpec((SWA_WINDOW, LANES),
                                  lambda b, t: (jnp.maximum((b * nt + t) * per - 1, 0), c))
    smem = pl.BlockSpec(memory_space=pltpu.SMEM)
    return pl.pallas_call(
        _gla_swa_kernel, grid=(BATCH, nt),
        in_specs=[row(GLA_QK), row(GLA_QK), row(A_WIDTH), row(A_WIDTH), row(LANES),
                  full(w_up), full(b_up), full(gain),
                  smem, smem, row(B_WIDTH), prev(0), own(0), prev(1), own(1), row(B_WIDTH)],
        out_specs=[row(A_WIDTH), row(B_WIDTH)],
        out_shape=[jax.ShapeDtypeStruct((TOKENS, A_WIDTH), BF16),
                   jax.ShapeDtypeStruct((TOKENS, B_WIDTH), BF16)],
        scratch_shapes=[pltpu.VMEM((A_WIDTH, GLA_QK), F32),
                        pltpu.VMEM((2 * SWA_WINDOW, SWA_HEADS * SWA_WINDOW), F32),
                        pltpu.VMEM((2, 2 * SWA_WINDOW, SWA_HEADS * SWA_WINDOW), F32),
                        pltpu.VMEM((2, 1, SWA_HEADS * SWA_WINDOW), F32)],
        compiler_params=pltpu.CompilerParams(dimension_semantics=("arbitrary", "arbitrary"),
                                             vmem_limit_bytes=VMEM_LIMIT),
        name="gla_swa",
    )(aq, ak, av, agate, adown, w_up, b_up, gain, sinks, rel_bias, bq, bkv, bkv, bkv, bkv, bgate)


def _moba_kernel(rb_ref, q_ref, k_ref, v_ref, gate_ref, o_ref,
                 kaug_ref, vt_ref, km_ref, bias_ref, qat_ref, s_ref, smax_ref, m_ref, acc_ref):
    BLK = MOBA_BLOCK
    G = MOBA_GROUP
    KVH = MOBA_KV_HEADS
    R = G * BLK
    NP1 = MOBA_NBLK + 1
    i = pl.program_id(1)

    @pl.when(i == 0)
    def _():
        lane = lax.broadcasted_iota(jnp.int32, (BLK, LANES), 1)
        ones_rows = jnp.ones((MOBA_VT_ROWS - MOBA_DH, BLK), BF16)

        def prep(j, carry):
            rows = pl.ds(pl.multiple_of(j * BLK, BLK), BLK)
            onehot = jnp.where(lane == j, 1.0, 0.0).astype(BF16)
            for kh in range(KVH):
                kb = k_ref[rows, kh * MOBA_DH:(kh + 1) * MOBA_DH]
                kaug_ref[kh * NP1 + j + 1, :, 0:LANES] = kb
                kaug_ref[kh * NP1 + j + 1, :, LANES:2 * LANES] = onehot
                km_ref[pl.ds(kh * MOBA_NBLK + j, 1), :] = jnp.mean(kb.astype(F32), axis=0, keepdims=True)
                vb = v_ref[rows, kh * MOBA_DH:(kh + 1) * MOBA_DH]
                vt_ref[kh * NP1 + j + 1, 0:MOBA_DH, :] = vb.astype(F32).T.astype(BF16)
                vt_ref[kh * NP1 + j + 1, MOBA_DH:, :] = ones_rows
            return carry

        lax.fori_loop(0, MOBA_NBLK, prep, 0)
        for kh in range(KVH):
            kaug_ref[kh * NP1, :, 0:LANES] = jnp.zeros((BLK, LANES), BF16)
            kaug_ref[kh * NP1, :, LANES:2 * LANES] = jnp.where(lane == MOBA_NBLK, 1.0, 0.0).astype(BF16)
            vt_ref[kh * NP1] = jnp.zeros((MOBA_VT_ROWS, BLK), BF16)
            qat_ref[kh, MOBA_DH + MOBA_NBLK:, :] = jnp.full((2 * LANES - MOBA_DH - MOBA_NBLK, R), NEG_BIG, BF16)

    @pl.when(jnp.logical_and(pl.program_id(0) == 0, i == 0))
    def _():
        tk = lax.broadcasted_iota(jnp.int32, (BLK, BLK), 0)
        tq = lax.broadcasted_iota(jnp.int32, (BLK, BLK), 1)
        d_own = tq - tk
        for head in range(MOBA_HEADS):
            cols = slice(head * BLK, (head + 1) * BLK)
            bias_ref[0:BLK, cols] = _t5_bias(d_own + BLK, rb_ref, head) * LOG2E
            bias_ref[BLK:2 * BLK, cols] = jnp.where(d_own >= 0, _t5_bias(d_own, rb_ref, head) * LOG2E, -jnp.inf)

    blk = lax.broadcasted_iota(jnp.int32, (MOBA_NBLK, R), 0)
    blkf = blk.astype(F32)
    past = blk < i
    colh = lax.broadcasted_iota(jnp.int32, (1, R), 1) // BLK
    far_row = REL_BUCKETS - 1

    own_max = []
    for kh in range(KVH):
        q = jnp.concatenate([q_ref[:, (kh * G + g) * MOBA_DH:(kh * G + g + 1) * MOBA_DH] for g in range(G)],
                            axis=0)
        q_t = q.astype(F32).T.astype(BF16)
        qat_ref[kh, 0:MOBA_DH, :] = q_t
        s_own = (_dot(kaug_ref[kh * NP1 + i + 1, :, 0:MOBA_DH], q_t)
                 + bias_ref[BLK:2 * BLK, kh * R:(kh + 1) * R])
        s_ref[kh, BLK:2 * BLK, :] = s_own
        own_max.append(jnp.max(s_own, axis=0, keepdims=True))

        km = km_ref[kh * MOBA_NBLK:(kh + 1) * MOBA_NBLK, :]
        km_hi = km.astype(BF16)
        km_lo = (km - km_hi.astype(F32)).astype(BF16)
        gate = _dot(km_hi, q_t) + _dot(km_lo, q_t)
        g_ = jnp.where(past, gate, -jnp.inf)
        selected = jnp.zeros((MOBA_NBLK, R), dtype=jnp.bool_)
        for _ in range(MOBA_TOPK):
            mx = jnp.max(g_, axis=0, keepdims=True)
            first = jnp.min(jnp.where(g_ == mx, blkf, 1e9), axis=0, keepdims=True)
            pick = jnp.logical_and(blkf == first, past)
            selected = jnp.logical_or(selected, pick)
            g_ = jnp.where(pick, -jnp.inf, g_)
        cfar = jnp.where(colh == 0, rb_ref[far_row, kh * G],
                         jnp.where(colh == 1, rb_ref[far_row, kh * G + 1],
                                   jnp.where(colh == 2, rb_ref[far_row, kh * G + 2],
                                             rb_ref[far_row, kh * G + 3])))
        sel_bias = jnp.where(selected, jnp.where(blk < i - 1, cfar * LOG2E, 0.0), NEG_BIG)
        qat_ref[kh, MOBA_DH:MOBA_DH + MOBA_NBLK, :] = sel_bias.astype(BF16)

    for kh in range(KVH):
        s_near = _dot(kaug_ref[kh * NP1 + i], qat_ref[kh]) + bias_ref[0:BLK, kh * R:(kh + 1) * R]
        s_ref[kh, 0:BLK, :] = s_near
        smax_ref[kh] = jnp.maximum(own_max[kh], jnp.max(s_near, axis=0, keepdims=True))

    def produce(slot, kh, start):
        k2 = kaug_ref[pl.ds(kh * NP1 + start, 2)].reshape(2 * BLK, 2 * LANES)
        s = _dot(k2, qat_ref[kh])
        s_ref[slot * KVH + kh] = s
        smax_ref[slot * KVH + kh] = jnp.max(s, axis=0, keepdims=True)

    def consume(slot, kh, start, first=False):
        m_new = smax_ref[slot * KVH + kh]
        if not first:
            m_old = m_ref[kh]
            m_new = jnp.maximum(m_old, m_new)
        p = jnp.exp2(s_ref[slot * KVH + kh] - m_new).astype(BF16)
        pv = (_dot(vt_ref[kh * NP1 + start], p[0:BLK])
              + _dot(vt_ref[kh * NP1 + start + 1], p[BLK:2 * BLK]))
        acc_ref[kh] = pv if first else jnp.exp2(m_old - m_new) * acc_ref[kh] + pv
        m_ref[kh] = m_new

    n_far = jnp.maximum(i - 1, 0)
    n_pairs = (n_far + 1) // 2

    def far_start(t):
        return jnp.maximum(n_far - 2 * t - 1, 0)

    def far_step(t, slot):
        for kh in range(KVH):
            produce(1 - slot, kh, far_start(t + 1))
        for kh in range(KVH):
            consume(slot, kh, far_start(t))

    for kh in range(KVH):
        produce(1, kh, far_start(0))
    for kh in range(KVH):
        consume(0, kh, i, first=True)

    def far_four(u, carry):
        for d in range(4):
            far_step(4 * u + d, (d + 1) % 2)
        return carry

    n_full = jnp.maximum(n_pairs - 1, 0)
    lax.fori_loop(0, n_full // 4, far_four, 0)
    done = (n_full // 4) * 4

    @pl.when(n_full - done >= 2)
    def _():
        far_step(done, 1)
        far_step(done + 1, 0)

    @pl.when(n_full % 2 == 1)
    def _():
        far_step(n_full - 1, 1)

    def finalize(kh):
        acc = acc_ref[kh]
        o = (acc[0:MOBA_DH] * (1.0 / acc[MOBA_DH:MOBA_DH + 1])).T
        for g in range(G):
            cols = slice((kh * G + g) * MOBA_DH, (kh * G + g + 1) * MOBA_DH)
            gt = gate_ref[:, cols].astype(F32)
            o_ref[:, cols] = (o[g * BLK:(g + 1) * BLK] * _silu(gt)).astype(o_ref.dtype)

    for slot in range(2):
        @pl.when(jnp.logical_and(n_pairs >= 1, n_pairs % 2 == slot))
        def _():
            for kh in range(KVH):
                consume(slot, kh, far_start(n_pairs - 1))
                finalize(kh)

    @pl.when(n_pairs == 0)
    def _():
        for kh in range(KVH):
            finalize(kh)


def _moba(cq, ck, cv, cgate, rel_bias):
    R = MOBA_GROUP * MOBA_BLOCK
    KVH = MOBA_KV_HEADS
    NP1 = MOBA_NBLK + 1
    qspec = pl.BlockSpec((MOBA_BLOCK, C_WIDTH), lambda b, i: (b * MOBA_NBLK + i, 0))
    kvspec = pl.BlockSpec((SEQ, KVH * MOBA_DH), lambda b, i: (b, 0), pipeline_mode=pl.Buffered(1))
    return pl.pallas_call(
        _moba_kernel, grid=(BATCH, MOBA_NBLK),
        in_specs=[pl.BlockSpec(memory_space=pltpu.SMEM), qspec, kvspec, kvspec, qspec],
        out_specs=qspec,
        out_shape=jax.ShapeDtypeStruct((TOKENS, C_WIDTH), BF16),
        scratch_shapes=[pltpu.VMEM((KVH * NP1, MOBA_BLOCK, 2 * LANES), BF16),
                        pltpu.VMEM((KVH * NP1, MOBA_VT_ROWS, MOBA_BLOCK), BF16),
                        pltpu.VMEM((KVH * MOBA_NBLK, MOBA_DH), F32),
                        pltpu.VMEM((2 * MOBA_BLOCK, KVH * R), F32),
                        pltpu.VMEM((KVH, 2 * LANES, R), BF16),
                        pltpu.VMEM((2 * KVH, 2 * MOBA_BLOCK, R), F32),
                        pltpu.VMEM((2 * KVH, 1, R), F32),
                        pltpu.VMEM((KVH, 1, R), F32),
                        pltpu.VMEM((KVH, MOBA_VT_ROWS, R), F32)],
        compiler_params=pltpu.CompilerParams(
            dimension_semantics=("arbitrary", "arbitrary"), vmem_limit_bytes=MOBA_VMEM_LIMIT),
        name="moba",
    )(rel_bias, cq, ck, cv, cgate)


def kernel(x, norm_gain, final_gain, rel_bias, w_in_even, gla_w_up, gla_b_up, gla_norm_gain, swa_sinks,
           w_out_even, w_in_odd, w_out_odd):
    assert x.shape == (BATCH, SEQ, D_MODEL)
    h0 = x.reshape(TOKENS, D_MODEL)

    we = w_in_even[0]
    splits = np.cumsum([0, GLA_QK, GLA_QK, A_WIDTH, GLA_RANK, A_WIDTH, B_WIDTH, SWA_KV_HEADS * SWA_DH,
                        SWA_KV_HEADS * SWA_DH, B_WIDTH])
    w_aq, w_ak, w_av, w_down, w_agate, w_bq = [we[:, splits[k]:splits[k + 1]] for k in range(6)]
    w_bkv = we[:, splits[6]:splits[8]]
    w_bgate = we[:, splits[8]:splits[9]]
    w_down = jnp.pad(jnp.concatenate([w_down] * 3, axis=1), ((0, 0), (0, LANES - 3 * GLA_RANK)))
    proj_ws = [w.astype(BF16) for w in (w_aq, w_ak, w_av, w_down, w_agate, w_bq, w_bkv, w_bgate)]
    proj_dt = [BF16, BF16, BF16, F32, BF16, BF16, BF16, BF16]
    scales = [GLA_DK ** -0.5, 1.0, 1.0, 1.0, 1.0, SWA_DH ** -0.5 * LOG2E, 1.0, 1.0]
    aq, ak, av, adown, agate, bq, bkv, bgate = _layer_io(
        h0, [], norm_gain[0:1], proj_ws, proj_dt, scales, write_h=False, row_tile=2048, row_sub=256)

    w_up_hi = gla_w_up[0].astype(BF16)
    w_up_lo = (gla_w_up[0] - w_up_hi.astype(F32)).astype(BF16)
    w_up = jnp.pad(jnp.concatenate([w_up_hi, w_up_hi, w_up_lo], axis=0), ((0, LANES - 3 * GLA_RANK), (0, 0)))
    oa, ob = _gla_swa(aq, ak, av, agate, adown, w_up, gla_b_up[0:1], gla_norm_gain[0:1],
                      bq, bkv, bgate, rel_bias, swa_sinks[0])

    wo = w_out_even[0].astype(BF16)
    wi = w_in_odd[0]
    osp = np.cumsum([0, C_WIDTH, MOBA_KV_HEADS * MOBA_DH, MOBA_KV_HEADS * MOBA_DH, C_WIDTH])
    odd_ws = [wi[:, osp[k]:osp[k + 1]].astype(BF16) for k in range(4)]
    h1, cq, ck, cv, cgate = _layer_io(
        h0, [(oa, wo[:A_WIDTH]), (ob, wo[A_WIDTH:])], norm_gain[1:2], odd_ws, [BF16] * 4,
        [MOBA_DH ** -0.5 * LOG2E, 1.0, 1.0, 1.0], write_h=True, row_tile=1024, row_sub=1024)

    oc = _moba(cq, ck, cv, cgate, rel_bias)

    (out,) = _layer_io(h1, [(oc, w_out_odd[0].astype(BF16))], final_gain.reshape(1, D_MODEL), [], [], [],
                       write_h=False, row_tile=2048, row_sub=256)
    return out.reshape(BATCH, SEQ, D_MODEL)
```

```python
import functools
import math

import numpy as np
import jax
import jax.numpy as jnp
from jax import lax
from jax.experimental import pallas as pl
from jax.experimental.pallas import tpu as pltpu

D_MODEL = 1024
BATCH = 2
SEQ = 8192
TOKENS = BATCH * SEQ

REL_BUCKETS = 32
REL_MAX_EXACT = 16
REL_MAX_DIST = 128
ATTN_HEADS = 8

GLA_HEADS = 4
GLA_DK = 64
GLA_DV = 128
GLA_RANK = 16
GLA_TAU = 16.0
GLA_CHUNK = 64

SWA_HEADS = 8
SWA_KV_HEADS = 2
SWA_DH = 64
SWA_WINDOW = 128

MOBA_HEADS = 8
MOBA_KV_HEADS = 2
MOBA_DH = 128
MOBA_BLOCK = 256
MOBA_TOPK = 3
MOBA_GROUP = MOBA_HEADS // MOBA_KV_HEADS
MOBA_NBLK = SEQ // MOBA_BLOCK
MOBA_VT_ROWS = MOBA_DH + 16
LOG2E = math.log2(math.e)
SWA_VT_ROWS = SWA_KV_HEADS * SWA_DH + 16

EPS = 1e-6
LANES = 128
NEG_BIG = -1e30
VMEM_LIMIT = 48 * 1024 * 1024
MOBA_VMEM_LIMIT = 56 * 1024 * 1024

A_WIDTH = GLA_HEADS * GLA_DV
B_WIDTH = SWA_HEADS * SWA_DH
C_WIDTH = MOBA_HEADS * MOBA_DH
GLA_QK = GLA_HEADS * GLA_DK

ROW_TILE = 1024
GLA_TILE = 1024
SWA_TILE = 1024

F32 = jnp.float32
BF16 = jnp.bfloat16


def _dot(a, b):
    return jnp.dot(a, b, preferred_element_type=F32)


def _dot_nt(a, b):
    return lax.dot_general(a, b, (((1,), (1,)), ((), ())), preferred_element_type=F32)


def _dot_tn(a, b):
    return lax.dot_general(a, b, (((0,), (0,)), ((), ())), preferred_element_type=F32)


def _silu(x):
    return x * (1.0 / (1.0 + jnp.exp(-x)))


def _t5_thresholds():
    d = np.arange(REL_MAX_DIST + 1)
    nf = np.maximum(d, 1).astype(np.float32)
    large = REL_MAX_EXACT + (np.log(nf / np.float32(REL_MAX_EXACT))
                             / np.float32(math.log(REL_MAX_DIST / REL_MAX_EXACT))
                             * np.float32(REL_BUCKETS - REL_MAX_EXACT)).astype(np.int32)
    bucket = np.where(d < REL_MAX_EXACT, d, np.minimum(large, REL_BUCKETS - 1))
    assert np.all(np.diff(bucket) >= 0) and bucket[-1] == REL_BUCKETS - 1
    return [int(np.argmax(bucket >= b)) for b in range(REL_BUCKETS)]


_T5_THRESH = _t5_thresholds()


def _t5_bias(dist, rb_ref, head):
    out = jnp.full(dist.shape, rb_ref[0, head], F32)
    for b in range(1, REL_BUCKETS):
        out = jnp.where(dist >= _T5_THRESH[b], rb_ref[b, head], out)
    return out


def _layer_io_kernel(*refs, n_acc, n_proj, write_h, scales, row_sub):
    h_ref = refs[0]
    pos = 1
    acc_pairs = []
    for _ in range(n_acc):
        acc_pairs.append((refs[pos], refs[pos + 1]))
        pos += 2
    gain_ref = refs[pos]
    pos += 1
    wp_refs = refs[pos:pos + n_proj]
    pos += n_proj
    out_refs = refs[pos:]

    o = 1 if write_h else 0
    for s in range(ROW_TILE // row_sub):
        rows = slice(s * row_sub, (s + 1) * row_sub)
        h = h_ref[rows, :]
        for a_ref, w_ref in acc_pairs:
            h = h + _dot(a_ref[rows, :], w_ref[...])
        if write_h:
            out_refs[0][rows, :] = h
        y = h * lax.rsqrt(jnp.mean(h * h, axis=-1, keepdims=True) + EPS) * gain_ref[...]
        if n_proj == 0:
            out_refs[o][rows, :] = y
            continue
        yb = y.astype(BF16)
        for k in range(n_proj):
            r = _dot(yb, wp_refs[k][...])
            if scales[k] != 1.0:
                r = r * scales[k]
            out_refs[o + k][rows, :] = r.astype(out_refs[o + k].dtype)


def _layer_io(h, acc_pairs, gain, proj_ws, proj_dtypes, scales, write_h, row_sub):
    n_rows = h.shape[0]
    grid = (n_rows // ROW_TILE,)
    row_spec = lambda n: pl.BlockSpec((ROW_TILE, n), lambda i: (i, 0))
    full_spec = lambda a: pl.BlockSpec(a.shape, lambda i: (0,) * a.ndim, pipeline_mode=pl.Buffered(1))

    args = [h]
    in_specs = [row_spec(D_MODEL)]
    for a, w in acc_pairs:
        args += [a, w]
        in_specs += [row_spec(a.shape[1]), full_spec(w)]
    args.append(gain)
    in_specs.append(full_spec(gain))
    for w in proj_ws:
        args.append(w)
        in_specs.append(full_spec(w))

    out_shape, out_specs = [], []
    if write_h:
        out_shape.append(jax.ShapeDtypeStruct((n_rows, D_MODEL), F32))
        out_specs.append(row_spec(D_MODEL))
    if proj_ws:
        for w, dt in zip(proj_ws, proj_dtypes):
            out_shape.append(jax.ShapeDtypeStruct((n_rows, w.shape[1]), dt))
            out_specs.append(row_spec(w.shape[1]))
    else:
        out_shape.append(jax.ShapeDtypeStruct((n_rows, D_MODEL), F32))
        out_specs.append(row_spec(D_MODEL))

    kern = functools.partial(_layer_io_kernel, n_acc=len(acc_pairs), n_proj=len(proj_ws),
                             write_h=write_h, scales=tuple(scales), row_sub=row_sub)
    return pl.pallas_call(
        kern, grid=grid, in_specs=in_specs, out_specs=out_specs, out_shape=out_shape,
        compiler_params=pltpu.CompilerParams(dimension_semantics=("arbitrary",),
                                             vmem_limit_bytes=VMEM_LIMIT),
        name="layer_io",
    )(*args)


def _gla_init(st_ref):
    @pl.when(pl.program_id(1) == 0)
    def _():
        st_ref[...] = jnp.zeros_like(st_ref)


def _gla_steps(q_ref, k_ref, v_ref, gate_ref, down_ref, wup_ref, bup_ref, gain_ref, o_ref, st_ref):
    C = GLA_CHUNK

    a = down_ref[...]
    a_hi = a.astype(BF16)
    a_lo = (a - a_hi.astype(F32)).astype(BF16)
    lane_a = lax.broadcasted_iota(jnp.int32, a.shape, 1) // GLA_RANK
    z = _dot(jnp.where(lane_a == 1, a_lo, a_hi), wup_ref[...]) + bup_ref[...]
    log_a = (jnp.minimum(z, 0.0) - jnp.log(1.0 + jnp.exp(-jnp.abs(z)))) * (1.0 / GLA_TAU)

    r_i = lax.broadcasted_iota(jnp.int32, (C, C), 0)
    c_i = lax.broadcasted_iota(jnp.int32, (C, C), 1)
    tri = (c_i <= r_i).astype(BF16)
    lane_qk = lax.broadcasted_iota(jnp.int32, (C, GLA_QK), 1)
    head_masks = [(lane_qk // GLA_DK) == h for h in range(GLA_HEADS)]
    rs = lax.broadcasted_iota(jnp.int32, (GLA_HEADS * C, C), 0)
    cs = lax.broadcasted_iota(jnp.int32, (GLA_HEADS * C, C), 1)
    causal = (rs % C) >= cs
    st_r = lax.broadcasted_iota(jnp.int32, (A_WIDTH, GLA_QK), 0)
    st_c = lax.broadcasted_iota(jnp.int32, (A_WIDTH, GLA_QK), 1)
    same_head = (st_r // GLA_DV) == (st_c // GLA_DK)
    gain = gain_ref[...]

    pending = {}

    def prep(c):
        rows = slice(c * C, (c + 1) * C)
        g = log_a[rows]
        g_hi = g.astype(BF16)
        g_lo = (g - g_hi.astype(F32)).astype(BF16)
        b = _dot(tri, g_hi) + _dot(tri, g_lo)
        b_last = b[C - 1:C]
        q = q_ref[rows, :].astype(F32)
        k = k_ref[rows, :].astype(F32)
        q_e = q * jnp.exp(b)
        k_e = (k * jnp.exp(-b)).astype(BF16)
        k_l = (k * jnp.exp(b_last - b)).astype(BF16)
        decay = jnp.exp(b_last)
        q_stack = jnp.concatenate([jnp.where(m, q_e, 0.0) for m in head_masks], axis=0).astype(BF16)
        pending[c] = (q_stack, q_e.astype(BF16), k_e, k_l, decay)

    def scores(c):
        q_stack, q_eb, k_e, k_l, decay = pending.pop(c)
        att = jnp.where(causal, _dot_nt(q_stack, k_e), 0.0).astype(BF16)
        pending[c] = (att, q_eb, k_l, decay)

    def main(c):
        rows = slice(c * C, (c + 1) * C)
        att, q_eb, k_l, decay = pending.pop(c)
        v = v_ref[rows, :]
        o_full = _dot(att, v)
        o_intra = jnp.concatenate(
            [o_full[h * C:(h + 1) * C, h * GLA_DV:(h + 1) * GLA_DV] for h in range(GLA_HEADS)], axis=1)
        st = st_ref[...]
        o = o_intra + _dot_nt(q_eb, st.astype(BF16))
        kv_t = _dot_tn(v, k_l)
        st_ref[...] = st * decay + jnp.where(same_head, kv_t, 0.0)

        outs = []
        for h in range(GLA_HEADS):
            oh = o[:, h * GLA_DV:(h + 1) * GLA_DV]
            outs.append(oh * lax.rsqrt(jnp.mean(oh * oh, axis=-1, keepdims=True) + EPS) * gain)
        on = jnp.concatenate(outs, axis=1)
        o_ref[rows, :] = (on * _silu(gate_ref[rows, :].astype(F32))).astype(o_ref.dtype)

    n_chunks = GLA_TILE // C
    steps = [functools.partial(prep, 0), functools.partial(prep, 1), functools.partial(scores, 0)]
    for c in range(n_chunks):
        if c + 2 < n_chunks:
            steps.append(functools.partial(prep, c + 2))
        if c + 1 < n_chunks:
            steps.append(functools.partial(scores, c + 1))
        steps.append(functools.partial(main, c))
    return steps


def _swa_init(rb_ref, bias_ref):
    W = SWA_WINDOW

    @pl.when(jnp.logical_and(pl.program_id(0) == 0, pl.program_id(1) == 0))
    def _():
        kj = lax.broadcasted_iota(jnp.int32, (2 * W, W), 0)
        qi = lax.broadcasted_iota(jnp.int32, (2 * W, W), 1)
        dist = qi + W - kj
        in_win = jnp.logical_and(dist >= 0, dist < W)
        for h in range(SWA_HEADS):
            bias_ref[:, h * W:(h + 1) * W] = jnp.where(in_win, _t5_bias(dist, rb_ref, h) * LOG2E, -jnp.inf)


def _swa_steps(sink_ref, q_ref, kp_ref, k_ref, vp_ref, v_ref, gate_ref, o_ref, bias_ref, s_ref, smax_ref):
    W = SWA_WINDOW
    H = SWA_HEADS
    DH = SWA_DH
    G = SWA_HEADS // SWA_KV_HEADS
    t = pl.program_id(1)

    kcat = jnp.concatenate([kp_ref[...], k_ref[...]], axis=0)
    vcat = jnp.concatenate([vp_ref[...], v_ref[...]], axis=0)
    vt = vcat.astype(F32).T.astype(BF16)
    ones_rows = jnp.ones((SWA_VT_ROWS - LANES, 2 * W), BF16)
    q_t = q_ref[...].astype(F32).T.astype(BF16)
    zeros = jnp.zeros((DH, W), BF16)
    colh = lax.broadcasted_iota(jnp.int32, (1, H * W), 1) // W
    sink = jnp.full((1, H * W), sink_ref[H - 1], F32)
    for h in range(H - 2, -1, -1):
        sink = jnp.where(colh == h, sink_ref[h], sink)
    sink = sink * LOG2E
    key_prev = lax.broadcasted_iota(jnp.int32, (2 * W, H * W), 0) < W

    def produce(blk):
        tok = slice(blk * W, (blk + 1) * W)
        win = slice(blk * W, blk * W + 2 * W)
        pieces = []
        for h in range(H):
            piece = q_t[h * DH:(h + 1) * DH, tok]
            pieces.append(jnp.concatenate([piece, zeros] if h < G else [zeros, piece], axis=0))
        q_pad = jnp.concatenate(pieces, axis=1)
        s = _dot(kcat[win], q_pad) + bias_ref[...]
        if blk == 0:
            s = jnp.where(jnp.logical_and(t == 0, key_prev), -jnp.inf, s)
        s_ref[blk % 2] = s
        smax_ref[blk % 2] = jnp.maximum(jnp.max(s, axis=0, keepdims=True), sink)

    def consume(blk):
        tok = slice(blk * W, (blk + 1) * W)
        win = slice(blk * W, blk * W + 2 * W)
        m = smax_ref[blk % 2]
        p = jnp.exp2(s_ref[blk % 2] - m).astype(BF16)
        pv = _dot(jnp.concatenate([vt[:, win], ones_rows], axis=0), p)
        inv = 1.0 / (pv[LANES:LANES + 1] + jnp.exp2(sink - m))
        o_t = jnp.concatenate(
            [pv[(h // G) * DH:(h // G + 1) * DH, h * W:(h + 1) * W] * inv[:, h * W:(h + 1) * W]
             for h in range(H)], axis=0)
        gate = gate_ref[tok, :].astype(F32)
        o_ref[tok, :] = (o_t.T * _silu(gate)).astype(o_ref.dtype)

    n_blk = SWA_TILE // W
    steps = [functools.partial(produce, 0)]
    for blk in range(n_blk):
        if blk + 1 < n_blk:
            steps.append(functools.partial(produce, blk + 1))
        steps.append(functools.partial(consume, blk))
    return steps


def _gla_swa_kernel(aq_ref, ak_ref, av_ref, agate_ref, adown_ref, wup_ref, bup_ref, gain_ref,
                    sink_ref, rb_ref, bq_ref, kp_ref, k_ref, vp_ref, v_ref, bgate_ref,
                    oa_ref, ob_ref, st_ref, bias_ref, s_ref, smax_ref):
    _gla_init(st_ref)
    _swa_init(rb_ref, bias_ref)
    gla = _gla_steps(aq_ref, ak_ref, av_ref, agate_ref, adown_ref, wup_ref, bup_ref, gain_ref, oa_ref, st_ref)
    swa = _swa_steps(sink_ref, bq_ref, kp_ref, k_ref, vp_ref, v_ref, bgate_ref, ob_ref, bias_ref, s_ref, smax_ref)
    for step in gla + swa:
        step()


def _gla_swa(aq, ak, av, agate, adown, w_up, b_up, gain, bq, bkv, bgate, rel_bias, sinks):
    assert GLA_TILE == SWA_TILE
    nt = SEQ // SWA_TILE
    per = SWA_TILE // SWA_WINDOW
    row = lambda n: pl.BlockSpec((SWA_TILE, n), lambda b, t: (b * nt + t, 0))
    full = lambda a: pl.BlockSpec(a.shape, lambda b, t: (0,) * a.ndim)
    own = lambda c: pl.BlockSpec((SWA_TILE, LANES), lambda b, t: (b * nt + t, c))
    prev = lambda c: pl.BlockSpec((SWA_WINDOW, LANES),
                                  lambda b, t: (jnp.maximum((b * nt + t) * per - 1, 0), c))
    smem = pl.BlockSpec(memory_space=pltpu.SMEM)
    return pl.pallas_call(
        _gla_swa_kernel, grid=(BATCH, nt),
        in_specs=[row(GLA_QK), row(GLA_QK), row(A_WIDTH), row(A_WIDTH), row(LANES),
                  full(w_up), full(b_up), full(gain),
                  smem, smem, row(B_WIDTH), prev(0), own(0), prev(1), own(1), row(B_WIDTH)],
        out_specs=[row(A_WIDTH), row(B_WIDTH)],
        out_shape=[jax.ShapeDtypeStruct((TOKENS, A_WIDTH), BF16),
                   jax.ShapeDtypeStruct((TOKENS, B_WIDTH), BF16)],
        scratch_shapes=[pltpu.VMEM((A_WIDTH, GLA_QK), F32),
                        pltpu.VMEM((2 * SWA_WINDOW, SWA_HEADS * SWA_WINDOW), F32),
                        pltpu.VMEM((2, 2 * SWA_WINDOW, SWA_HEADS * SWA_WINDOW), F32),
                        pltpu.VMEM((2, 1, SWA_HEADS * SWA_WINDOW), F32)],
        compiler_params=pltpu.CompilerParams(dimension_semantics=("arbitrary", "arbitrary"),
                                             vmem_limit_bytes=VMEM_LIMIT),
        name="gla_swa",
    )(aq, ak, av, agate, adown, w_up, b_up, gain, sinks, rel_bias, bq, bkv, bkv, bkv, bkv, bgate)


def _moba_kernel(rb_ref, q_ref, k_ref, v_ref, gate_ref, o_ref,
                 kaug_ref, vt_ref, km_ref, bias_ref, qat_ref, s_ref, smax_ref, m_ref, acc_ref):
    BLK = MOBA_BLOCK
    G = MOBA_GROUP
    KVH = MOBA_KV_HEADS
    R = G * BLK
    NP1 = MOBA_NBLK + 1
    i = pl.program_id(1)

    @pl.when(i == 0)
    def _():
        lane = lax.broadcasted_iota(jnp.int32, (BLK, LANES), 1)
        ones_rows = jnp.ones((MOBA_VT_ROWS - MOBA_DH, BLK), BF16)

        def prep(j, carry):
            rows = pl.ds(pl.multiple_of(j * BLK, BLK), BLK)
            onehot = jnp.where(lane == j, 1.0, 0.0).astype(BF16)
            for kh in range(KVH):
                kb = k_ref[rows, kh * MOBA_DH:(kh + 1) * MOBA_DH]
                kaug_ref[kh * NP1 + j + 1, :, 0:LANES] = kb
                kaug_ref[kh * NP1 + j + 1, :, LANES:2 * LANES] = onehot
                km_ref[pl.ds(kh * MOBA_NBLK + j, 1), :] = jnp.mean(kb.astype(F32), axis=0, keepdims=True)
                vb = v_ref[rows, kh * MOBA_DH:(kh + 1) * MOBA_DH]
                vt_ref[kh * NP1 + j + 1, 0:MOBA_DH, :] = vb.astype(F32).T.astype(BF16)
                vt_ref[kh * NP1 + j + 1, MOBA_DH:, :] = ones_rows
            return carry

        lax.fori_loop(0, MOBA_NBLK, prep, 0)
        for kh in range(KVH):
            kaug_ref[kh * NP1, :, 0:LANES] = jnp.zeros((BLK, LANES), BF16)
            kaug_ref[kh * NP1, :, LANES:2 * LANES] = jnp.where(lane == MOBA_NBLK, 1.0, 0.0).astype(BF16)
            vt_ref[kh * NP1] = jnp.zeros((MOBA_VT_ROWS, BLK), BF16)
            qat_ref[kh, MOBA_DH + MOBA_NBLK:, :] = jnp.full((2 * LANES - MOBA_DH - MOBA_NBLK, R), NEG_BIG, BF16)

    @pl.when(jnp.logical_and(pl.program_id(0) == 0, i == 0))
    def _():
        tk = lax.broadcasted_iota(jnp.int32, (BLK, BLK), 0)
        tq = lax.broadcasted_iota(jnp.int32, (BLK, BLK), 1)
        d_own = tq - tk
        for head in range(MOBA_HEADS):
            cols = slice(head * BLK, (head + 1) * BLK)
            bias_ref[0:BLK, cols] = _t5_bias(d_own + BLK, rb_ref, head) * LOG2E
            bias_ref[BLK:2 * BLK, cols] = jnp.where(d_own >= 0, _t5_bias(d_own, rb_ref, head) * LOG2E, -jnp.inf)

    blk = lax.broadcasted_iota(jnp.int32, (MOBA_NBLK, R), 0)
    blkf = blk.astype(F32)
    past = blk < i
    colh = lax.broadcasted_iota(jnp.int32, (1, R), 1) // BLK
    far_row = REL_BUCKETS - 1

    own_max = []
    for kh in range(KVH):
        q = jnp.concatenate([q_ref[:, (kh * G + g) * MOBA_DH:(kh * G + g + 1) * MOBA_DH] for g in range(G)],
                            axis=0)
        q_t = q.astype(F32).T.astype(BF16)
        qat_ref[kh, 0:MOBA_DH, :] = q_t
        s_own = (_dot(kaug_ref[kh * NP1 + i + 1, :, 0:MOBA_DH], q_t)
                 + bias_ref[BLK:2 * BLK, kh * R:(kh + 1) * R])
        s_ref[kh, BLK:2 * BLK, :] = s_own
        own_max.append(jnp.max(s_own, axis=0, keepdims=True))

        km = km_ref[kh * MOBA_NBLK:(kh + 1) * MOBA_NBLK, :]
        km_hi = km.astype(BF16)
        km_lo = (km - km_hi.astype(F32)).astype(BF16)
        gate = _dot(km_hi, q_t) + _dot(km_lo, q_t)
        g_ = jnp.where(past, gate, -jnp.inf)
        selected = jnp.zeros((MOBA_NBLK, R), dtype=jnp.bool_)
        for _ in range(MOBA_TOPK):
            mx = jnp.max(g_, axis=0, keepdims=True)
            first = jnp.min(jnp.where(g_ == mx, blkf, 1e9), axis=0, keepdims=True)
            pick = jnp.logical_and(blkf == first, past)
            selected = jnp.logical_or(selected, pick)
            g_ = jnp.where(pick, -jnp.inf, g_)
        cfar = jnp.where(colh == 0, rb_ref[far_row, kh * G],
                         jnp.where(colh == 1, rb_ref[far_row, kh * G + 1],
                                   jnp.where(colh == 2, rb_ref[far_row, kh * G + 2],
                                             rb_ref[far_row, kh * G + 3])))
        sel_bias = jnp.where(selected, jnp.where(blk < i - 1, cfar * LOG2E, 0.0), NEG_BIG)
        qat_ref[kh, MOBA_DH:MOBA_DH + MOBA_NBLK, :] = sel_bias.astype(BF16)

    for kh in range(KVH):
        s_near = _dot(kaug_ref[kh * NP1 + i], qat_ref[kh]) + bias_ref[0:BLK, kh * R:(kh + 1) * R]
        s_ref[kh, 0:BLK, :] = s_near
        smax_ref[kh] = jnp.maximum(own_max[kh], jnp.max(s_near, axis=0, keepdims=True))

    def produce(slot, kh, start):
        k2 = kaug_ref[pl.ds(kh * NP1 + start, 2)].reshape(2 * BLK, 2 * LANES)
        s = _dot(k2, qat_ref[kh])
        s_ref[slot * KVH + kh] = s
        smax_ref[slot * KVH + kh] = jnp.max(s, axis=0, keepdims=True)

    def consume(slot, kh, start, first=False):
        m_new = smax_ref[slot * KVH + kh]
        if not first:
            m_old = m_ref[kh]
            m_new = jnp.maximum(m_old, m_new)
        p = jnp.exp2(s_ref[slot * KVH + kh] - m_new).astype(BF16)
        pv = (_dot(vt_ref[kh * NP1 + start], p[0:BLK])
              + _dot(vt_ref[kh * NP1 + start + 1], p[BLK:2 * BLK]))
        acc_ref[kh] = pv if first else jnp.exp2(m_old - m_new) * acc_ref[kh] + pv
        m_ref[kh] = m_new

    n_far = jnp.maximum(i - 1, 0)
    n_pairs = (n_far + 1) // 2

    def far_start(t):
        return jnp.maximum(n_far - 2 * t - 1, 0)

    def far_step(t, slot):
        for kh in range(KVH):
            produce(1 - slot, kh, far_start(t + 1))
            consume(slot, kh, far_start(t))

    for kh in range(KVH):
        produce(1, kh, far_start(0))
        consume(0, kh, i, first=True)

    def far_four(u, carry):
        for d in range(4):
            far_step(4 * u + d, (d + 1) % 2)
        return carry

    n_full = jnp.maximum(n_pairs - 1, 0)
    lax.fori_loop(0, n_full // 4, far_four, 0)
    done = (n_full // 4) * 4

    @pl.when(n_full - done >= 2)
    def _():
        far_step(done, 1)
        far_step(done + 1, 0)

    @pl.when(n_full % 2 == 1)
    def _():
        far_step(n_full - 1, 1)

    def finalize(kh):
        acc = acc_ref[kh]
        o = (acc[0:MOBA_DH] * (1.0 / acc[MOBA_DH:MOBA_DH + 1])).T
        for g in range(G):
            cols = slice((kh * G + g) * MOBA_DH, (kh * G + g + 1) * MOBA_DH)
            gt = gate_ref[:, cols].astype(F32)
            o_ref[:, cols] = (o[g * BLK:(g + 1) * BLK] * _silu(gt)).astype(o_ref.dtype)

    for slot in range(2):
        @pl.when(jnp.logical_and(n_pairs >= 1, n_pairs % 2 == slot))
        def _():
            for kh in range(KVH):
                consume(slot, kh, far_start(n_pairs - 1))
                finalize(kh)

    @pl.when(n_pairs == 0)
    def _():
        for kh in range(KVH):
            finalize(kh)


def _moba(cq, ck, cv, cgate, rel_bias):
    R = MOBA_GROUP * MOBA_BLOCK
    KVH = MOBA_KV_HEADS
    NP1 = MOBA_NBLK + 1
    qspec = pl.BlockSpec((MOBA_BLOCK, C_WIDTH), lambda b, i: (b * MOBA_NBLK + i, 0))
    kvspec = pl.BlockSpec((SEQ, KVH * MOBA_DH), lambda b, i: (b, 0), pipeline_mode=pl.Buffered(1))
    return pl.pallas_call(
        _moba_kernel, grid=(BATCH, MOBA_NBLK),
        in_specs=[pl.BlockSpec(memory_space=pltpu.SMEM), qspec, kvspec, kvspec, qspec],
        out_specs=qspec,
        out_shape=jax.ShapeDtypeStruct((TOKENS, C_WIDTH), BF16),
        scratch_shapes=[pltpu.VMEM((KVH * NP1, MOBA_BLOCK, 2 * LANES), BF16),
                        pltpu.VMEM((KVH * NP1, MOBA_VT_ROWS, MOBA_BLOCK), BF16),
                        pltpu.VMEM((KVH * MOBA_NBLK, MOBA_DH), F32),
                        pltpu.VMEM((2 * MOBA_BLOCK, KVH * R), F32),
                        pltpu.VMEM((KVH, 2 * LANES, R), BF16),
                        pltpu.VMEM((2 * KVH, 2 * MOBA_BLOCK, R), F32),
                        pltpu.VMEM((2 * KVH, 1, R), F32),
                        pltpu.VMEM((KVH, 1, R), F32),
                        pltpu.VMEM((KVH, MOBA_VT_ROWS, R), F32)],
        compiler_params=pltpu.CompilerParams(
            dimension_semantics=("arbitrary", "arbitrary"), vmem_limit_bytes=MOBA_VMEM_LIMIT),
        name="moba",
    )(rel_bias, cq, ck, cv, cgate)


def kernel(x, norm_gain, final_gain, rel_bias, w_in_even, gla_w_up, gla_b_up, gla_norm_gain, swa_sinks,
           w_out_even, w_in_odd, w_out_odd):
    assert x.shape == (BATCH, SEQ, D_MODEL)
    h0 = x.reshape(TOKENS, D_MODEL)

    we = w_in_even[0]
    splits = np.cumsum([0, GLA_QK, GLA_QK, A_WIDTH, GLA_RANK, A_WIDTH, B_WIDTH, SWA_KV_HEADS * SWA_DH,
                        SWA_KV_HEADS * SWA_DH, B_WIDTH])
    w_aq, w_ak, w_av, w_down, w_agate, w_bq = [we[:, splits[k]:splits[k + 1]] for k in range(6)]
    w_bkv = we[:, splits[6]:splits[8]]
    w_bgate = we[:, splits[8]:splits[9]]
    w_down = jnp.pad(jnp.concatenate([w_down] * 3, axis=1), ((0, 0), (0, LANES - 3 * GLA_RANK)))
    proj_ws = [w.astype(BF16) for w in (w_aq, w_ak, w_av, w_down, w_agate, w_bq, w_bkv, w_bgate)]
    proj_dt = [BF16, BF16, BF16, F32, BF16, BF16, BF16, BF16]
    scales = [GLA_DK ** -0.5, 1.0, 1.0, 1.0, 1.0, SWA_DH ** -0.5 * LOG2E, 1.0, 1.0]
    aq, ak, av, adown, agate, bq, bkv, bgate = _layer_io(
        h0, [], norm_gain[0:1], proj_ws, proj_dt, scales, write_h=False, row_sub=256)

    w_up_hi = gla_w_up[0].astype(BF16)
    w_up_lo = (gla_w_up[0] - w_up_hi.astype(F32)).astype(BF16)
    w_up = jnp.pad(jnp.concatenate([w_up_hi, w_up_hi, w_up_lo], axis=0), ((0, LANES - 3 * GLA_RANK), (0, 0)))
    oa, ob = _gla_swa(aq, ak, av, agate, adown, w_up, gla_b_up[0:1], gla_norm_gain[0:1],
                      bq, bkv, bgate, rel_bias, swa_sinks[0])

    wo = w_out_even[0].astype(BF16)
    wi = w_in_odd[0]
    osp = np.cumsum([0, C_WIDTH, MOBA_KV_HEADS * MOBA_DH, MOBA_KV_HEADS * MOBA_DH, C_WIDTH])
    odd_ws = [wi[:, osp[k]:osp[k + 1]].astype(BF16) for k in range(4)]
    h1, cq, ck, cv, cgate = _layer_io(
        h0, [(oa, wo[:A_WIDTH]), (ob, wo[A_WIDTH:])], norm_gain[1:2], odd_ws, [BF16] * 4,
        [MOBA_DH ** -0.5 * LOG2E, 1.0, 1.0, 1.0], write_h=True, row_sub=ROW_TILE)

    oc = _moba(cq, ck, cv, cgate, rel_bias)

    (out,) = _layer_io(h1, [(oc, w_out_odd[0].astype(BF16))], final_gain.reshape(1, D_MODEL), [], [], [],
                       write_h=False, row_sub=256)
    return out.reshape(BATCH, SEQ, D_MODEL)
```

```python
import functools
import math

import numpy as np
import jax
import jax.numpy as jnp
from jax import lax
from jax.experimental import pallas as pl
from jax.experimental.pallas import tpu as pltpu

D_MODEL = 1024
BATCH = 2
SEQ = 8192
TOKENS = BATCH * SEQ

REL_BUCKETS = 32
REL_MAX_EXACT = 16
REL_MAX_DIST = 128
ATTN_HEADS = 8

GLA_HEADS = 4
GLA_DK = 64
GLA_DV = 128
GLA_RANK = 16
GLA_TAU = 16.0
GLA_CHUNK = 64

SWA_HEADS = 8
SWA_KV_HEADS = 2
SWA_DH = 64
SWA_WINDOW = 128

MOBA_HEADS = 8
MOBA_KV_HEADS = 2
MOBA_DH = 128
MOBA_BLOCK = 256
MOBA_TOPK = 3
MOBA_GROUP = MOBA_HEADS // MOBA_KV_HEADS
MOBA_NBLK = SEQ // MOBA_BLOCK
MOBA_VT_ROWS = MOBA_DH + 16
LOG2E = math.log2(math.e)
SWA_VT_ROWS = SWA_KV_HEADS * SWA_DH + 16

EPS = 1e-6
LANES = 128
NEG_BIG = -1e30
VMEM_LIMIT = 48 * 1024 * 1024
MOBA_VMEM_LIMIT = 56 * 1024 * 1024

A_WIDTH = GLA_HEADS * GLA_DV
B_WIDTH = SWA_HEADS * SWA_DH
C_WIDTH = MOBA_HEADS * MOBA_DH
GLA_QK = GLA_HEADS * GLA_DK

ROW_TILE = 1024
GLA_TILE = 1024
SWA_TILE = 1024
SWA_SLOTS = 2

F32 = jnp.float32
BF16 = jnp.bfloat16


def _dot(a, b):
    return jnp.dot(a, b, preferred_element_type=F32)


def _dot_nt(a, b):
    return lax.dot_general(a, b, (((1,), (1,)), ((), ())), preferred_element_type=F32)


def _dot_tn(a, b):
    return lax.dot_general(a, b, (((0,), (0,)), ((), ())), preferred_element_type=F32)


def _silu(x):
    return x * (1.0 / (1.0 + jnp.exp(-x)))


def _t5_thresholds():
    d = np.arange(REL_MAX_DIST + 1)
    nf = np.maximum(d, 1).astype(np.float32)
    large = REL_MAX_EXACT + (np.log(nf / np.float32(REL_MAX_EXACT))
                             / np.float32(math.log(REL_MAX_DIST / REL_MAX_EXACT))
                             * np.float32(REL_BUCKETS - REL_MAX_EXACT)).astype(np.int32)
    bucket = np.where(d < REL_MAX_EXACT, d, np.minimum(large, REL_BUCKETS - 1))
    assert np.all(np.diff(bucket) >= 0) and bucket[-1] == REL_BUCKETS - 1
    return [int(np.argmax(bucket >= b)) for b in range(REL_BUCKETS)]


_T5_THRESH = _t5_thresholds()


def _t5_bias(dist, rb_ref, head):
    out = jnp.full(dist.shape, rb_ref[0, head], F32)
    for b in range(1, REL_BUCKETS):
        out = jnp.where(dist >= _T5_THRESH[b], rb_ref[b, head], out)
    return out


def _layer_io_kernel(*refs, n_acc, n_proj, write_h, scales, row_sub):
    h_ref = refs[0]
    pos = 1
    acc_pairs = []
    for _ in range(n_acc):
        acc_pairs.append((refs[pos], refs[pos + 1]))
        pos += 2
    gain_ref = refs[pos]
    pos += 1
    wp_refs = refs[pos:pos + n_proj]
    pos += n_proj
    out_refs = refs[pos:]

    o = 1 if write_h else 0
    for s in range(ROW_TILE // row_sub):
        rows = slice(s * row_sub, (s + 1) * row_sub)
        h = h_ref[rows, :]
        for a_ref, w_ref in acc_pairs:
            h = h + _dot(a_ref[rows, :], w_ref[...])
        if write_h:
            out_refs[0][rows, :] = h
        y = h * lax.rsqrt(jnp.mean(h * h, axis=-1, keepdims=True) + EPS) * gain_ref[...]
        if n_proj == 0:
            out_refs[o][rows, :] = y
            continue
        yb = y.astype(BF16)
        for k in range(n_proj):
            r = _dot(yb, wp_refs[k][...])
            if scales[k] != 1.0:
                r = r * scales[k]
            out_refs[o + k][rows, :] = r.astype(out_refs[o + k].dtype)


def _layer_io(h, acc_pairs, gain, proj_ws, proj_dtypes, scales, write_h, row_sub):
    n_rows = h.shape[0]
    grid = (n_rows // ROW_TILE,)
    row_spec = lambda n: pl.BlockSpec((ROW_TILE, n), lambda i: (i, 0))
    full_spec = lambda a: pl.BlockSpec(a.shape, lambda i: (0,) * a.ndim, pipeline_mode=pl.Buffered(1))

    args = [h]
    in_specs = [row_spec(D_MODEL)]
    for a, w in acc_pairs:
        args += [a, w]
        in_specs += [row_spec(a.shape[1]), full_spec(w)]
    args.append(gain)
    in_specs.append(full_spec(gain))
    for w in proj_ws:
        args.append(w)
        in_specs.append(full_spec(w))

    out_shape, out_specs = [], []
    if write_h:
        out_shape.append(jax.ShapeDtypeStruct((n_rows, D_MODEL), F32))
        out_specs.append(row_spec(D_MODEL))
    if proj_ws:
        for w, dt in zip(proj_ws, proj_dtypes):
            out_shape.append(jax.ShapeDtypeStruct((n_rows, w.shape[1]), dt))
            out_specs.append(row_spec(w.shape[1]))
    else:
        out_shape.append(jax.ShapeDtypeStruct((n_rows, D_MODEL), F32))
        out_specs.append(row_spec(D_MODEL))

    kern = functools.partial(_layer_io_kernel, n_acc=len(acc_pairs), n_proj=len(proj_ws),
                             write_h=write_h, scales=tuple(scales), row_sub=row_sub)
    return pl.pallas_call(
        kern, grid=grid, in_specs=in_specs, out_specs=out_specs, out_shape=out_shape,
        compiler_params=pltpu.CompilerParams(dimension_semantics=("arbitrary",),
                                             vmem_limit_bytes=VMEM_LIMIT),
        name="layer_io",
    )(*args)


def _gla_init(st_ref):
    @pl.when(pl.program_id(1) == 0)
    def _():
        st_ref[...] = jnp.zeros_like(st_ref)


def _gla_steps(q_ref, k_ref, v_ref, gate_ref, down_ref, wup_ref, bup_ref, gain_ref, o_ref, st_ref):
    C = GLA_CHUNK

    a = down_ref[...]
    a_hi = a.astype(BF16)
    a_lo = (a - a_hi.astype(F32)).astype(BF16)
    lane_a = lax.broadcasted_iota(jnp.int32, a.shape, 1) // GLA_RANK
    z = _dot(jnp.where(lane_a == 1, a_lo, a_hi), wup_ref[...]) + bup_ref[...]
    log_a = (jnp.minimum(z, 0.0) - jnp.log(1.0 + jnp.exp(-jnp.abs(z)))) * (1.0 / GLA_TAU)

    r_i = lax.broadcasted_iota(jnp.int32, (C, C), 0)
    c_i = lax.broadcasted_iota(jnp.int32, (C, C), 1)
    tri = (c_i <= r_i).astype(BF16)
    lane_qk = lax.broadcasted_iota(jnp.int32, (C, GLA_QK), 1)
    head_masks = [(lane_qk // GLA_DK) == h for h in range(GLA_HEADS)]
    rs = lax.broadcasted_iota(jnp.int32, (GLA_HEADS * C, C), 0)
    cs = lax.broadcasted_iota(jnp.int32, (GLA_HEADS * C, C), 1)
    causal = (rs % C) >= cs
    st_r = lax.broadcasted_iota(jnp.int32, (A_WIDTH, GLA_QK), 0)
    st_c = lax.broadcasted_iota(jnp.int32, (A_WIDTH, GLA_QK), 1)
    same_head = (st_r // GLA_DV) == (st_c // GLA_DK)
    gain = gain_ref[...]

    pending = {}

    def cumsum(c):
        g = log_a[c * C:(c + 1) * C]
        g_hi = g.astype(BF16)
        g_lo = (g - g_hi.astype(F32)).astype(BF16)
        pending[c] = _dot(tri, g_hi) + _dot(tri, g_lo)

    def prep(c):
        rows = slice(c * C, (c + 1) * C)
        b = pending.pop(c)
        b_last = b[C - 1:C]
        q = q_ref[rows, :].astype(F32)
        k = k_ref[rows, :].astype(F32)
        q_e = q * jnp.exp(b)
        k_e = (k * jnp.exp(-b)).astype(BF16)
        k_l = (k * jnp.exp(b_last - b)).astype(BF16)
        decay = jnp.exp(b_last)
        q_stack = jnp.concatenate([jnp.where(m, q_e, 0.0) for m in head_masks], axis=0).astype(BF16)
        pending[c] = (q_stack, q_e.astype(BF16), k_e, k_l, decay)

    def scores(c):
        q_stack, q_eb, k_e, k_l, decay = pending.pop(c)
        att = jnp.where(causal, _dot_nt(q_stack, k_e), 0.0).astype(BF16)
        pending[c] = (att, q_eb, k_l, decay)

    def main(c):
        rows = slice(c * C, (c + 1) * C)
        att, q_eb, k_l, decay = pending.pop(c)
        v = v_ref[rows, :]
        o_full = _dot(att, v)
        o_intra = jnp.concatenate(
            [o_full[h * C:(h + 1) * C, h * GLA_DV:(h + 1) * GLA_DV] for h in range(GLA_HEADS)], axis=1)
        st = st_ref[...]
        o = o_intra + _dot_nt(q_eb, st.astype(BF16))
        kv_t = _dot_tn(v, k_l)
        st_ref[...] = st * decay + jnp.where(same_head, kv_t, 0.0)

        outs = []
        for h in range(GLA_HEADS):
            oh = o[:, h * GLA_DV:(h + 1) * GLA_DV]
            outs.append(oh * lax.rsqrt(jnp.mean(oh * oh, axis=-1, keepdims=True) + EPS) * gain)
        on = jnp.concatenate(outs, axis=1)
        o_ref[rows, :] = (on * _silu(gate_ref[rows, :].astype(F32))).astype(o_ref.dtype)

    n_chunks = GLA_TILE // C
    steps = [functools.partial(cumsum, 0), functools.partial(cumsum, 1), functools.partial(cumsum, 2),
             functools.partial(prep, 0), functools.partial(prep, 1), functools.partial(scores, 0)]
    for c in range(n_chunks):
        if c + 3 < n_chunks:
            steps.append(functools.partial(cumsum, c + 3))
        if c + 2 < n_chunks:
            steps.append(functools.partial(prep, c + 2))
        if c + 1 < n_chunks:
            steps.append(functools.partial(scores, c + 1))
        steps.append(functools.partial(main, c))
    return steps


def _swa_init(rb_ref, bias_ref):
    W = SWA_WINDOW

    @pl.when(jnp.logical_and(pl.program_id(0) == 0, pl.program_id(1) == 0))
    def _():
        kj = lax.broadcasted_iota(jnp.int32, (2 * W, W), 0)
        qi = lax.broadcasted_iota(jnp.int32, (2 * W, W), 1)
        dist = qi + W - kj
        in_win = jnp.logical_and(dist >= 0, dist < W)
        for h in range(SWA_HEADS):
            bias_ref[:, h * W:(h + 1) * W] = jnp.where(in_win, _t5_bias(dist, rb_ref, h) * LOG2E, -jnp.inf)


def _swa_steps(sink_ref, q_ref, kp_ref, k_ref, vp_ref, v_ref, gate_ref, o_ref, bias_ref, s_ref, smax_ref):
    W = SWA_WINDOW
    H = SWA_HEADS
    DH = SWA_DH
    G = SWA_HEADS // SWA_KV_HEADS
    t = pl.program_id(1)

    kcat = jnp.concatenate([kp_ref[...], k_ref[...]], axis=0)
    vcat = jnp.concatenate([vp_ref[...], v_ref[...]], axis=0)
    vt = vcat.astype(F32).T.astype(BF16)
    ones_rows = jnp.ones((SWA_VT_ROWS - LANES, 2 * W), BF16)
    q_t = q_ref[...].astype(F32).T.astype(BF16)
    zeros = jnp.zeros((DH, W), BF16)
    colh = lax.broadcasted_iota(jnp.int32, (1, H * W), 1) // W
    sink = jnp.full((1, H * W), sink_ref[H - 1], F32)
    for h in range(H - 2, -1, -1):
        sink = jnp.where(colh == h, sink_ref[h], sink)
    sink = sink * LOG2E
    key_prev = lax.broadcasted_iota(jnp.int32, (2 * W, H * W), 0) < W

    def produce(blk):
        tok = slice(blk * W, (blk + 1) * W)
        win = slice(blk * W, blk * W + 2 * W)
        pieces = []
        for h in range(H):
            piece = q_t[h * DH:(h + 1) * DH, tok]
            pieces.append(jnp.concatenate([piece, zeros] if h < G else [zeros, piece], axis=0))
        q_pad = jnp.concatenate(pieces, axis=1)
        s = _dot(kcat[win], q_pad) + bias_ref[...]
        if blk == 0:
            s = jnp.where(jnp.logical_and(t == 0, key_prev), -jnp.inf, s)
        s_ref[blk % SWA_SLOTS] = s
        smax_ref[blk % SWA_SLOTS] = jnp.maximum(jnp.max(s, axis=0, keepdims=True), sink)

    pending = {}

    def attend(blk):
        win = slice(blk * W, blk * W + 2 * W)
        m = smax_ref[blk % SWA_SLOTS]
        p = jnp.exp2(s_ref[blk % SWA_SLOTS] - m).astype(BF16)
        pv = _dot(jnp.concatenate([vt[:, win], ones_rows], axis=0), p)
        pending[blk] = (pv, m)

    def finish(blk):
        tok = slice(blk * W, (blk + 1) * W)
        pv, m = pending.pop(blk)
        inv = 1.0 / (pv[LANES:LANES + 1] + jnp.exp2(sink - m))
        o_t = jnp.concatenate(
            [pv[(h // G) * DH:(h // G + 1) * DH, h * W:(h + 1) * W] * inv[:, h * W:(h + 1) * W]
             for h in range(H)], axis=0)
        gate = gate_ref[tok, :].astype(F32)
        o_ref[tok, :] = (o_t.T * _silu(gate)).astype(o_ref.dtype)

    n_blk = SWA_TILE // W
    ahead = SWA_SLOTS - 1
    steps = [functools.partial(produce, blk) for blk in range(ahead)]
    for blk in range(n_blk):
        if blk + ahead < n_blk:
            steps.append(functools.partial(produce, blk + ahead))
        steps.append(functools.partial(attend, blk))
        if blk >= 1:
            steps.append(functools.partial(finish, blk - 1))
    steps.append(functools.partial(finish, n_blk - 1))
    return steps


def _gla_swa_kernel(aq_ref, ak_ref, av_ref, agate_ref, adown_ref, wup_ref, bup_ref, gain_ref,
                    sink_ref, rb_ref, bq_ref, kp_ref, k_ref, vp_ref, v_ref, bgate_ref,
                    oa_ref, ob_ref, st_ref, bias_ref, s_ref, smax_ref):
    _gla_init(st_ref)
    _swa_init(rb_ref, bias_ref)
    gla = _gla_steps(aq_ref, ak_ref, av_ref, agate_ref, adown_ref, wup_ref, bup_ref, gain_ref, oa_ref, st_ref)
    swa = _swa_steps(sink_ref, bq_ref, kp_ref, k_ref, vp_ref, v_ref, bgate_ref, ob_ref, bias_ref, s_ref, smax_ref)
    for step in gla + swa:
        step()


def _gla_swa(aq, ak, av, agate, adown, w_up, b_up, gain, bq, bkv, bgate, rel_bias, sinks):
    assert GLA_TILE == SWA_TILE
    nt = SEQ // SWA_TILE
    per = SWA_TILE // SWA_WINDOW
    row = lambda n: pl.BlockSpec((SWA_TILE, n), lambda b, t: (b * nt + t, 0))
    full = lambda a: pl.BlockSpec(a.shape, lambda b, t: (0,) * a.ndim)
    own = lambda c: pl.BlockSpec((SWA_TILE, LANES), lambda b, t: (b * nt + t, c))
    prev = lambda c: pl.BlockSpec((SWA_WINDOW, LANES),
                                  lambda b, t: (jnp.maximum((b * nt + t) * per - 1, 0), c))
    smem = pl.BlockSpec(memory_space=pltpu.SMEM)
    return pl.pallas_call(
        _gla_swa_kernel, grid=(BATCH, nt),
        in_specs=[row(GLA_QK), row(GLA_QK), row(A_WIDTH), row(A_WIDTH), row(LANES),
                  full(w_up), full(b_up), full(gain),
                  smem, smem, row(B_WIDTH), prev(0), own(0), prev(1), own(1), row(B_WIDTH)],
        out_specs=[row(A_WIDTH), row(B_WIDTH)],
        out_shape=[jax.ShapeDtypeStruct((TOKENS, A_WIDTH), BF16),
                   jax.ShapeDtypeStruct((TOKENS, B_WIDTH), BF16)],
        scratch_shapes=[pltpu.VMEM((A_WIDTH, GLA_QK), F32),
                        pltpu.VMEM((2 * SWA_WINDOW, SWA_HEADS * SWA_WINDOW), F32),
                        pltpu.VMEM((SWA_SLOTS, 2 * SWA_WINDOW, SWA_HEADS * SWA_WINDOW), F32),
                        pltpu.VMEM((SWA_SLOTS, 1, SWA_HEADS * SWA_WINDOW), F32)],
        compiler_params=pltpu.CompilerParams(dimension_semantics=("arbitrary", "arbitrary"),
                                             vmem_limit_bytes=VMEM_LIMIT),
        name="gla_swa",
    )(aq, ak, av, agate, adown, w_up, b_up, gain, sinks, rel_bias, bq, bkv, bkv, bkv, bkv, bgate)


def _moba_kernel(rb_ref, q_ref, k_ref, v_ref, gate_ref, o_ref,
                 kaug_ref, vt_ref, km_ref, bias_ref, qat_ref, s_ref, smax_ref, m_ref, acc_ref):
    BLK = MOBA_BLOCK
    G = MOBA_GROUP
    KVH = MOBA_KV_HEADS
    R = G * BLK
    NP1 = MOBA_NBLK + 1
    i = pl.program_id(1)

    @pl.when(i == 0)
    def _():
        lane = lax.broadcasted_iota(jnp.int32, (BLK, LANES), 1)
        ones_rows = jnp.ones((MOBA_VT_ROWS - MOBA_DH, BLK), BF16)

        def prep(j, carry):
            rows = pl.ds(pl.multiple_of(j * BLK, BLK), BLK)
            onehot = jnp.where(lane == j, 1.0, 0.0).astype(BF16)
            for kh in range(KVH):
                kb = k_ref[rows, kh * MOBA_DH:(kh + 1) * MOBA_DH]
                kaug_ref[kh * NP1 + j + 1, :, 0:LANES] = kb
                kaug_ref[kh * NP1 + j + 1, :, LANES:2 * LANES] = onehot
                km_ref[pl.ds(kh * MOBA_NBLK + j, 1), :] = jnp.mean(kb.astype(F32), axis=0, keepdims=True)
                vb = v_ref[rows, kh * MOBA_DH:(kh + 1) * MOBA_DH]
                vt_ref[kh * NP1 + j + 1, 0:MOBA_DH, :] = vb.astype(F32).T.astype(BF16)
                vt_ref[kh * NP1 + j + 1, MOBA_DH:, :] = ones_rows
            return carry

        lax.fori_loop(0, MOBA_NBLK, prep, 0)
        for kh in range(KVH):
            kaug_ref[kh * NP1, :, 0:LANES] = jnp.zeros((BLK, LANES), BF16)
            kaug_ref[kh * NP1, :, LANES:2 * LANES] = jnp.where(lane == MOBA_NBLK, 1.0, 0.0).astype(BF16)
            vt_ref[kh * NP1] = jnp.zeros((MOBA_VT_ROWS, BLK), BF16)
            qat_ref[kh, MOBA_DH + MOBA_NBLK:, :] = jnp.full((2 * LANES - MOBA_DH - MOBA_NBLK, R), NEG_BIG, BF16)

    @pl.when(jnp.logical_and(pl.program_id(0) == 0, i == 0))
    def _():
        tk = lax.broadcasted_iota(jnp.int32, (BLK, BLK), 0)
        tq = lax.broadcasted_iota(jnp.int32, (BLK, BLK), 1)
        d_own = tq - tk
        for head in range(MOBA_HEADS):
            cols = slice(head * BLK, (head + 1) * BLK)
            bias_ref[0:BLK, cols] = _t5_bias(d_own + BLK, rb_ref, head) * LOG2E
            bias_ref[BLK:2 * BLK, cols] = jnp.where(d_own >= 0, _t5_bias(d_own, rb_ref, head) * LOG2E, -jnp.inf)

    blk = lax.broadcasted_iota(jnp.int32, (MOBA_NBLK, R), 0)
    blkf = blk.astype(F32)
    past = blk < i
    colh = lax.broadcasted_iota(jnp.int32, (1, R), 1) // BLK
    far_row = REL_BUCKETS - 1

    own_max = []
    for kh in range(KVH):
        q = jnp.concatenate([q_ref[:, (kh * G + g) * MOBA_DH:(kh * G + g + 1) * MOBA_DH] for g in range(G)],
                            axis=0)
        q_t = q.astype(F32).T.astype(BF16)
        qat_ref[kh, 0:MOBA_DH, :] = q_t
        s_own = (_dot(kaug_ref[kh * NP1 + i + 1, :, 0:MOBA_DH], q_t)
                 + bias_ref[BLK:2 * BLK, kh * R:(kh + 1) * R])
        s_ref[kh, BLK:2 * BLK, :] = s_own
        own_max.append(jnp.max(s_own, axis=0, keepdims=True))

        km = km_ref[kh * MOBA_NBLK:(kh + 1) * MOBA_NBLK, :]
        km_hi = km.astype(BF16)
        km_lo = (km - km_hi.astype(F32)).astype(BF16)
        gate = _dot(km_hi, q_t) + _dot(km_lo, q_t)
        g_ = jnp.where(past, gate, -jnp.inf)
        selected = jnp.zeros((MOBA_NBLK, R), dtype=jnp.bool_)
        for _ in range(MOBA_TOPK):
            mx = jnp.max(g_, axis=0, keepdims=True)
            first = jnp.min(jnp.where(g_ == mx, blkf, 1e9), axis=0, keepdims=True)
            pick = jnp.logical_and(blkf == first, past)
            selected = jnp.logical_or(selected, pick)
            g_ = jnp.where(pick, -jnp.inf, g_)
        cfar = jnp.where(colh == 0, rb_ref[far_row, kh * G],
                         jnp.where(colh == 1, rb_ref[far_row, kh * G + 1],
                                   jnp.where(colh == 2, rb_ref[far_row, kh * G + 2],
                                             rb_ref[far_row, kh * G + 3])))
        sel_bias = jnp.where(selected, jnp.where(blk < i - 1, cfar * LOG2E, 0.0), NEG_BIG)
        qat_ref[kh, MOBA_DH:MOBA_DH + MOBA_NBLK, :] = sel_bias.astype(BF16)

    for kh in range(KVH):
        s_near = _dot(kaug_ref[kh * NP1 + i], qat_ref[kh]) + bias_ref[0:BLK, kh * R:(kh + 1) * R]
        s_ref[kh, 0:BLK, :] = s_near
        smax_ref[kh] = jnp.maximum(own_max[kh], jnp.max(s_near, axis=0, keepdims=True))

    def produce(slot, kh, start):
        k2 = kaug_ref[pl.ds(kh * NP1 + start, 2)].reshape(2 * BLK, 2 * LANES)
        s = _dot(k2, qat_ref[kh])
        s_ref[slot * KVH + kh] = s
        smax_ref[slot * KVH + kh] = jnp.max(s, axis=0, keepdims=True)

    def consume(slot, kh, start, first=False):
        m_new = smax_ref[slot * KVH + kh]
        if not first:
            m_old = m_ref[kh]
            m_new = jnp.maximum(m_old, m_new)
        p = jnp.exp2(s_ref[slot * KVH + kh] - m_new).astype(BF16)
        pv = (_dot(vt_ref[kh * NP1 + start], p[0:BLK])
              + _dot(vt_ref[kh * NP1 + start + 1], p[BLK:2 * BLK]))
        acc_ref[kh] = pv if first else jnp.exp2(m_old - m_new) * acc_ref[kh] + pv
        m_ref[kh] = m_new

    n_far = jnp.maximum(i - 1, 0)
    n_pairs = (n_far + 1) // 2

    def far_start(t):
        return jnp.maximum(n_far - 2 * t - 1, 0)

    def far_step(t, slot):
        for kh in range(KVH):
            produce(1 - slot, kh, far_start(t + 1))
            consume(slot, kh, far_start(t))

    for kh in range(KVH):
        produce(1, kh, far_start(0))
        consume(0, kh, i, first=True)

    def far_four(u, carry):
        for d in range(4):
            far_step(4 * u + d, (d + 1) % 2)
        return carry

    n_full = jnp.maximum(n_pairs - 1, 0)
    lax.fori_loop(0, n_full // 4, far_four, 0)
    done = (n_full // 4) * 4

    @pl.when(n_full - done >= 2)
    def _():
        far_step(done, 1)
        far_step(done + 1, 0)

    @pl.when(n_full % 2 == 1)
    def _():
        far_step(n_full - 1, 1)

    def finalize(kh):
        acc = acc_ref[kh]
        o = (acc[0:MOBA_DH] * (1.0 / acc[MOBA_DH:MOBA_DH + 1])).T
        for g in range(G):
            cols = slice((kh * G + g) * MOBA_DH, (kh * G + g + 1) * MOBA_DH)
            gt = gate_ref[:, cols].astype(F32)
            o_ref[:, cols] = (o[g * BLK:(g + 1) * BLK] * _silu(gt)).astype(o_ref.dtype)

    for slot in range(2):
        @pl.when(jnp.logical_and(n_pairs >= 1, n_pairs % 2 == slot))
        def _():
            for kh in range(KVH):
                consume(slot, kh, far_start(n_pairs - 1))
                finalize(kh)

    @pl.when(n_pairs == 0)
    def _():
        for kh in range(KVH):
            finalize(kh)


def _moba(cq, ck, cv, cgate, rel_bias):
    R = MOBA_GROUP * MOBA_BLOCK
    KVH = MOBA_KV_HEADS
    NP1 = MOBA_NBLK + 1
    qspec = pl.BlockSpec((MOBA_BLOCK, C_WIDTH), lambda b, i: (b * MOBA_NBLK + i, 0))
    kvspec = pl.BlockSpec((SEQ, KVH * MOBA_DH), lambda b, i: (b, 0), pipeline_mode=pl.Buffered(1))
    return pl.pallas_call(
        _moba_kernel, grid=(BATCH, MOBA_NBLK),
        in_specs=[pl.BlockSpec(memory_space=pltpu.SMEM), qspec, kvspec, kvspec, qspec],
        out_specs=qspec,
        out_shape=jax.ShapeDtypeStruct((TOKENS, C_WIDTH), BF16),
        scratch_shapes=[pltpu.VMEM((KVH * NP1, MOBA_BLOCK, 2 * LANES), BF16),
                        pltpu.VMEM((KVH * NP1, MOBA_VT_ROWS, MOBA_BLOCK), BF16),
                        pltpu.VMEM((KVH * MOBA_NBLK, MOBA_DH), F32),
                        pltpu.VMEM((2 * MOBA_BLOCK, KVH * R), F32),
                        pltpu.VMEM((KVH, 2 * LANES, R), BF16),
                        pltpu.VMEM((2 * KVH, 2 * MOBA_BLOCK, R), F32),
                        pltpu.VMEM((2 * KVH, 1, R), F32),
                        pltpu.VMEM((KVH, 1, R), F32),
                        pltpu.VMEM((KVH, MOBA_VT_ROWS, R), F32)],
        compiler_params=pltpu.CompilerParams(
            dimension_semantics=("arbitrary", "arbitrary"), vmem_limit_bytes=MOBA_VMEM_LIMIT),
        name="moba",
    )(rel_bias, cq, ck, cv, cgate)


def kernel(x, norm_gain, final_gain, rel_bias, w_in_even, gla_w_up, gla_b_up, gla_norm_gain, swa_sinks,
           w_out_even, w_in_odd, w_out_odd):
    assert x.shape == (BATCH, SEQ, D_MODEL)
    h0 = x.reshape(TOKENS, D_MODEL)

    we = w_in_even[0]
    splits = np.cumsum([0, GLA_QK, GLA_QK, A_WIDTH, GLA_RANK, A_WIDTH, B_WIDTH, SWA_KV_HEADS * SWA_DH,
                        SWA_KV_HEADS * SWA_DH, B_WIDTH])
    w_aq, w_ak, w_av, w_down, w_agate, w_bq = [we[:, splits[k]:splits[k + 1]] for k in range(6)]
    w_bkv = we[:, splits[6]:splits[8]]
    w_bgate = we[:, splits[8]:splits[9]]
    w_down = jnp.pad(jnp.concatenate([w_down] * 3, axis=1), ((0, 0), (0, LANES - 3 * GLA_RANK)))
    proj_ws = [w.astype(BF16) for w in (w_aq, w_ak, w_av, w_down, w_agate, w_bq, w_bkv, w_bgate)]
    proj_dt = [BF16, BF16, BF16, F32, BF16, BF16, BF16, BF16]
    scales = [GLA_DK ** -0.5, 1.0, 1.0, 1.0, 1.0, SWA_DH ** -0.5 * LOG2E, 1.0, 1.0]
    aq, ak, av, adown, agate, bq, bkv, bgate = _layer_io(
        h0, [], norm_gain[0:1], proj_ws, proj_dt, scales, write_h=False, row_sub=256)

    w_up_hi = gla_w_up[0].astype(BF16)
    w_up_lo = (gla_w_up[0] - w_up_hi.astype(F32)).astype(BF16)
    w_up = jnp.pad(jnp.concatenate([w_up_hi, w_up_hi, w_up_lo], axis=0), ((0, LANES - 3 * GLA_RANK), (0, 0)))
    oa, ob = _gla_swa(aq, ak, av, agate, adown, w_up, gla_b_up[0:1], gla_norm_gain[0:1],
                      bq, bkv, bgate, rel_bias, swa_sinks[0])

    wo = w_out_even[0].astype(BF16)
    wi = w_in_odd[0]
    osp = np.cumsum([0, C_WIDTH, MOBA_KV_HEADS * MOBA_DH, MOBA_KV_HEADS * MOBA_DH, C_WIDTH])
    odd_ws = [wi[:, osp[k]:osp[k + 1]].astype(BF16) for k in range(4)]
    h1, cq, ck, cv, cgate = _layer_io(
        h0, [(oa, wo[:A_WIDTH]), (ob, wo[A_WIDTH:])], norm_gain[1:2], odd_ws, [BF16] * 4,
        [MOBA_DH ** -0.5 * LOG2E, 1.0, 1.0, 1.0], write_h=True, row_sub=ROW_TILE)

    oc = _moba(cq, ck, cv, cgate, rel_bias)

    (out,) = _layer_io(h1, [(oc, w_out_odd[0].astype(BF16))], final_gain.reshape(1, D_MODEL), [], [], [],
                       write_h=False, row_sub=256)
    return out.reshape(BATCH, SEQ, D_MODEL)
```

```python
import functools
import math

import numpy as np
import jax
import jax.numpy as jnp
from jax import lax
from jax.experimental import pallas as pl
from jax.experimental.pallas import tpu as pltpu

D_MODEL = 1024
BATCH = 2
SEQ = 8192
TOKENS = BATCH * SEQ

REL_BUCKETS = 32
REL_MAX_EXACT = 16
REL_MAX_DIST = 128
ATTN_HEADS = 8

GLA_HEADS = 4
GLA_DK = 64
GLA_DV = 128
GLA_RANK = 16
GLA_TAU = 16.0
GLA_CHUNK = 64

SWA_HEADS = 8
SWA_KV_HEADS = 2
SWA_DH = 64
SWA_WINDOW = 128

MOBA_HEADS = 8
MOBA_KV_HEADS = 2
MOBA_DH = 128
MOBA_BLOCK = 256
MOBA_TOPK = 3
MOBA_GROUP = MOBA_HEADS // MOBA_KV_HEADS
MOBA_NBLK = SEQ // MOBA_BLOCK
MOBA_VT_ROWS = MOBA_DH + 16
LOG2E = math.log2(math.e)
SWA_VT_ROWS = SWA_KV_HEADS * SWA_DH + 16

EPS = 1e-6
LANES = 128
NEG_BIG = -1e30
VMEM_LIMIT = 48 * 1024 * 1024
MOBA_VMEM_LIMIT = 56 * 1024 * 1024

A_WIDTH = GLA_HEADS * GLA_DV
B_WIDTH = SWA_HEADS * SWA_DH
C_WIDTH = MOBA_HEADS * MOBA_DH
GLA_QK = GLA_HEADS * GLA_DK

ROW_TILE = 1024
ROW_SUB = 256
GLA_TILE = 1024
SWA_TILE = 1024
SWA_SLOTS = 2

F32 = jnp.float32
BF16 = jnp.bfloat16


def _dot(a, b):
    return jnp.dot(a, b, preferred_element_type=F32)


def _dot_nt(a, b):
    return lax.dot_general(a, b, (((1,), (1,)), ((), ())), preferred_element_type=F32)


def _dot_tn(a, b):
    return lax.dot_general(a, b, (((0,), (0,)), ((), ())), preferred_element_type=F32)


def _silu(x):
    return x * (1.0 / (1.0 + jnp.exp(-x)))


def _t5_thresholds():
    d = np.arange(REL_MAX_DIST + 1)
    nf = np.maximum(d, 1).astype(np.float32)
    large = REL_MAX_EXACT + (np.log(nf / np.float32(REL_MAX_EXACT))
                             / np.float32(math.log(REL_MAX_DIST / REL_MAX_EXACT))
                             * np.float32(REL_BUCKETS - REL_MAX_EXACT)).astype(np.int32)
    bucket = np.where(d < REL_MAX_EXACT, d, np.minimum(large, REL_BUCKETS - 1))
    assert np.all(np.diff(bucket) >= 0) and bucket[-1] == REL_BUCKETS - 1
    return [int(np.argmax(bucket >= b)) for b in range(REL_BUCKETS)]


_T5_THRESH = _t5_thresholds()


def _t5_bias(dist, rb_ref, head):
    out = jnp.full(dist.shape, rb_ref[0, head], F32)
    for b in range(1, REL_BUCKETS):
        out = jnp.where(dist >= _T5_THRESH[b], rb_ref[b, head], out)
    return out


def _layer_io_kernel(*refs, n_acc, n_proj, write_h, scales):
    h_ref = refs[0]
    pos = 1
    acc_pairs = []
    for _ in range(n_acc):
        acc_pairs.append((refs[pos], refs[pos + 1]))
        pos += 2
    gain_ref = refs[pos]
    pos += 1
    wp_refs = refs[pos:pos + n_proj]
    pos += n_proj
    out_refs = refs[pos:]

    o = 1 if write_h else 0
    n_sub = ROW_TILE // ROW_SUB
    pending = {}

    def norm_stage(s):
        rows = slice(s * ROW_SUB, (s + 1) * ROW_SUB)
        h = h_ref[rows, :]
        for a_ref, w_ref in acc_pairs:
            h = h + _dot(a_ref[rows, :], w_ref[...])
        if write_h:
            out_refs[0][rows, :] = h
        y = h * lax.rsqrt(jnp.mean(h * h, axis=-1, keepdims=True) + EPS) * gain_ref[...]
        if n_proj == 0:
            out_refs[o][rows, :] = y
        else:
            pending[s] = y.astype(BF16)

    def proj_stage(s):
        rows = slice(s * ROW_SUB, (s + 1) * ROW_SUB)
        yb = pending.pop(s)
        for k in range(n_proj):
            r = _dot(yb, wp_refs[k][...])
            if scales[k] != 1.0:
                r = r * scales[k]
            out_refs[o + k][rows, :] = r.astype(out_refs[o + k].dtype)

    norm_stage(0)
    for s in range(n_sub):
        if s + 1 < n_sub:
            norm_stage(s + 1)
        if n_proj:
            proj_stage(s)


def _layer_io(h, acc_pairs, gain, proj_ws, proj_dtypes, scales, write_h):
    n_rows = h.shape[0]
    grid = (n_rows // ROW_TILE,)
    row_spec = lambda n: pl.BlockSpec((ROW_TILE, n), lambda i: (i, 0))
    full_spec = lambda a: pl.BlockSpec(a.shape, lambda i: (0,) * a.ndim, pipeline_mode=pl.Buffered(1))

    args = [h]
    in_specs = [row_spec(D_MODEL)]
    for a, w in acc_pairs:
        args += [a, w]
        in_specs += [row_spec(a.shape[1]), full_spec(w)]
    args.append(gain)
    in_specs.append(full_spec(gain))
    for w in proj_ws:
        args.append(w)
        in_specs.append(full_spec(w))

    out_shape, out_specs = [], []
    if write_h:
        out_shape.append(jax.ShapeDtypeStruct((n_rows, D_MODEL), F32))
        out_specs.append(row_spec(D_MODEL))
    if proj_ws:
        for w, dt in zip(proj_ws, proj_dtypes):
            out_shape.append(jax.ShapeDtypeStruct((n_rows, w.shape[1]), dt))
            out_specs.append(row_spec(w.shape[1]))
    else:
        out_shape.append(jax.ShapeDtypeStruct((n_rows, D_MODEL), F32))
        out_specs.append(row_spec(D_MODEL))

    kern = functools.partial(_layer_io_kernel, n_acc=len(acc_pairs), n_proj=len(proj_ws),
                             write_h=write_h, scales=tuple(scales))
    return pl.pallas_call(
        kern, grid=grid, in_specs=in_specs, out_specs=out_specs, out_shape=out_shape,
        compiler_params=pltpu.CompilerParams(dimension_semantics=("arbitrary",),
                                             vmem_limit_bytes=VMEM_LIMIT),
        name="layer_io",
    )(*args)


def _gla_init(st_ref):
    @pl.when(pl.program_id(1) == 0)
    def _():
        st_ref[...] = jnp.zeros_like(st_ref)


def _gla_steps(q_ref, k_ref, v_ref, gate_ref, down_ref, wup_ref, bup_ref, gain_ref, o_ref, st_ref):
    C = GLA_CHUNK

    a = down_ref[...]
    a_hi = a.astype(BF16)
    a_lo = (a - a_hi.astype(F32)).astype(BF16)
    lane_a = lax.broadcasted_iota(jnp.int32, a.shape, 1) // GLA_RANK
    z = _dot(jnp.where(lane_a == 1, a_lo, a_hi), wup_ref[...]) + bup_ref[...]
    log_a = (jnp.minimum(z, 0.0) - jnp.log(1.0 + jnp.exp(-jnp.abs(z)))) * (1.0 / GLA_TAU)

    r_i = lax.broadcasted_iota(jnp.int32, (C, C), 0)
    c_i = lax.broadcasted_iota(jnp.int32, (C, C), 1)
    tri = (c_i <= r_i).astype(BF16)
    lane_qk = lax.broadcasted_iota(jnp.int32, (C, GLA_QK), 1)
    head_masks = [(lane_qk // GLA_DK) == h for h in range(GLA_HEADS)]
    rs = lax.broadcasted_iota(jnp.int32, (GLA_HEADS * C, C), 0)
    cs = lax.broadcasted_iota(jnp.int32, (GLA_HEADS * C, C), 1)
    causal = (rs % C) >= cs
    st_r = lax.broadcasted_iota(jnp.int32, (A_WIDTH, GLA_QK), 0)
    st_c = lax.broadcasted_iota(jnp.int32, (A_WIDTH, GLA_QK), 1)
    same_head = (st_r // GLA_DV) == (st_c // GLA_DK)
    gain = gain_ref[...]

    pending = {}

    def cumsum(c):
        g = log_a[c * C:(c + 1) * C]
        g_hi = g.astype(BF16)
        g_lo = (g - g_hi.astype(F32)).astype(BF16)
        pending[c] = _dot(tri, g_hi) + _dot(tri, g_lo)

    def prep(c):
        rows = slice(c * C, (c + 1) * C)
        b = pending.pop(c)
        b_last = b[C - 1:C]
        q = q_ref[rows, :].astype(F32)
        k = k_ref[rows, :].astype(F32)
        q_e = q * jnp.exp(b)
        k_e = (k * jnp.exp(-b)).astype(BF16)
        k_l = (k * jnp.exp(b_last - b)).astype(BF16)
        decay = jnp.exp(b_last)
        q_stack = jnp.concatenate([jnp.where(m, q_e, 0.0) for m in head_masks], axis=0).astype(BF16)
        pending[c] = (q_stack, q_e.astype(BF16), k_e, k_l, decay)

    def scores(c):
        q_stack, q_eb, k_e, k_l, decay = pending.pop(c)
        att = jnp.where(causal, _dot_nt(q_stack, k_e), 0.0).astype(BF16)
        pending[c] = (att, q_eb, k_l, decay)

    def main(c):
        rows = slice(c * C, (c + 1) * C)
        att, q_eb, k_l, decay = pending.pop(c)
        v = v_ref[rows, :]
        o_full = _dot(att, v)
        o_intra = jnp.concatenate(
            [o_full[h * C:(h + 1) * C, h * GLA_DV:(h + 1) * GLA_DV] for h in range(GLA_HEADS)], axis=1)
        st = st_ref[...]
        o = o_intra + _dot_nt(q_eb, st.astype(BF16))
        kv_t = _dot_tn(v, k_l)
        st_ref[...] = st * decay + jnp.where(same_head, kv_t, 0.0)

        outs = []
        for h in range(GLA_HEADS):
            oh = o[:, h * GLA_DV:(h + 1) * GLA_DV]
            outs.append(oh * lax.rsqrt(jnp.mean(oh * oh, axis=-1, keepdims=True) + EPS) * gain)
        on = jnp.concatenate(outs, axis=1)
        o_ref[rows, :] = (on * _silu(gate_ref[rows, :].astype(F32))).astype(o_ref.dtype)

    n_chunks = GLA_TILE // C
    steps = [functools.partial(cumsum, 0), functools.partial(cumsum, 1), functools.partial(cumsum, 2),
             functools.partial(prep, 0), functools.partial(prep, 1), functools.partial(scores, 0)]
    for c in range(n_chunks):
        if c + 3 < n_chunks:
            steps.append(functools.partial(cumsum, c + 3))
        if c + 2 < n_chunks:
            steps.append(functools.partial(prep, c + 2))
        if c + 1 < n_chunks:
            steps.append(functools.partial(scores, c + 1))
        steps.append(functools.partial(main, c))
    return steps


def _swa_init(rb_ref, bias_ref):
    W = SWA_WINDOW

    @pl.when(jnp.logical_and(pl.program_id(0) == 0, pl.program_id(1) == 0))
    def _():
        kj = lax.broadcasted_iota(jnp.int32, (2 * W, W), 0)
        qi = lax.broadcasted_iota(jnp.int32, (2 * W, W), 1)
        dist = qi + W - kj
        in_win = jnp.logical_and(dist >= 0, dist < W)
        for h in range(SWA_HEADS):
            bias_ref[:, h * W:(h + 1) * W] = jnp.where(in_win, _t5_bias(dist, rb_ref, h) * LOG2E, -jnp.inf)


def _swa_steps(sink_ref, q_ref, kp_ref, k_ref, vp_ref, v_ref, gate_ref, o_ref, bias_ref, s_ref, smax_ref):
    W = SWA_WINDOW
    H = SWA_HEADS
    DH = SWA_DH
    G = SWA_HEADS // SWA_KV_HEADS
    t = pl.program_id(1)

    kcat = jnp.concatenate([kp_ref[...], k_ref[...]], axis=0)
    vcat = jnp.concatenate([vp_ref[...], v_ref[...]], axis=0)
    vt = vcat.astype(F32).T.astype(BF16)
    ones_rows = jnp.ones((SWA_VT_ROWS - LANES, 2 * W), BF16)
    q_t = q_ref[...].astype(F32).T.astype(BF16)
    zeros = jnp.zeros((DH, W), BF16)
    colh = lax.broadcasted_iota(jnp.int32, (1, H * W), 1) // W
    sink = jnp.full((1, H * W), sink_ref[H - 1], F32)
    for h in range(H - 2, -1, -1):
        sink = jnp.where(colh == h, sink_ref[h], sink)
    sink = sink * LOG2E
    key_prev = lax.broadcasted_iota(jnp.int32, (2 * W, H * W), 0) < W

    def produce(blk):
        tok = slice(blk * W, (blk + 1) * W)
        win = slice(blk * W, blk * W + 2 * W)
        pieces = []
        for h in range(H):
            piece = q_t[h * DH:(h + 1) * DH, tok]
            pieces.append(jnp.concatenate([piece, zeros] if h < G else [zeros, piece], axis=0))
        q_pad = jnp.concatenate(pieces, axis=1)
        s = _dot(kcat[win], q_pad) + bias_ref[...]
        if blk == 0:
            s = jnp.where(jnp.logical_and(t == 0, key_prev), -jnp.inf, s)
        s_ref[blk % SWA_SLOTS] = s
        smax_ref[blk % SWA_SLOTS] = jnp.maximum(jnp.max(s, axis=0, keepdims=True), sink)

    pending = {}

    def attend(blk):
        win = slice(blk * W, blk * W + 2 * W)
        m = smax_ref[blk % SWA_SLOTS]
        p = jnp.exp2(s_ref[blk % SWA_SLOTS] - m).astype(BF16)
        pv = _dot(jnp.concatenate([vt[:, win], ones_rows], axis=0), p)
        pending[blk] = (pv, m)

    def finish(blk):
        tok = slice(blk * W, (blk + 1) * W)
        pv, m = pending.pop(blk)
        inv = 1.0 / (pv[LANES:LANES + 1] + jnp.exp2(sink - m))
        o_t = jnp.concatenate(
            [pv[(h // G) * DH:(h // G + 1) * DH, h * W:(h + 1) * W] * inv[:, h * W:(h + 1) * W]
             for h in range(H)], axis=0)
        gate = gate_ref[tok, :].astype(F32)
        o_ref[tok, :] = (o_t.T * _silu(gate)).astype(o_ref.dtype)

    n_blk = SWA_TILE // W
    ahead = SWA_SLOTS - 1
    steps = [functools.partial(produce, blk) for blk in range(ahead)]
    for blk in range(n_blk):
        if blk + ahead < n_blk:
            steps.append(functools.partial(produce, blk + ahead))
        steps.append(functools.partial(attend, blk))
        if blk >= 1:
            steps.append(functools.partial(finish, blk - 1))
    steps.append(functools.partial(finish, n_blk - 1))
    return steps


def _gla_swa_kernel(aq_ref, ak_ref, av_ref, agate_ref, adown_ref, wup_ref, bup_ref, gain_ref,
                    sink_ref, rb_ref, bq_ref, kp_ref, k_ref, vp_ref, v_ref, bgate_ref,
                    oa_ref, ob_ref, st_ref, bias_ref, s_ref, smax_ref):
    _gla_init(st_ref)
    _swa_init(rb_ref, bias_ref)
    gla = _gla_steps(aq_ref, ak_ref, av_ref, agate_ref, adown_ref, wup_ref, bup_ref, gain_ref, oa_ref, st_ref)
    swa = _swa_steps(sink_ref, bq_ref, kp_ref, k_ref, vp_ref, v_ref, bgate_ref, ob_ref, bias_ref, s_ref, smax_ref)
    for step in gla + swa:
        step()


def _gla_swa(aq, ak, av, agate, adown, w_up, b_up, gain, bq, bkv, bgate, rel_bias, sinks):
    assert GLA_TILE == SWA_TILE
    nt = SEQ // SWA_TILE
    per = SWA_TILE // SWA_WINDOW
    row = lambda n: pl.BlockSpec((SWA_TILE, n), lambda b, t: (b * nt + t, 0))
    full = lambda a: pl.BlockSpec(a.shape, lambda b, t: (0,) * a.ndim)
    own = lambda c: pl.BlockSpec((SWA_TILE, LANES), lambda b, t: (b * nt + t, c))
    prev = lambda c: pl.BlockSpec((SWA_WINDOW, LANES),
                                  lambda b, t: (jnp.maximum((b * nt + t) * per - 1, 0), c))
    smem = pl.BlockSpec(memory_space=pltpu.SMEM)
    return pl.pallas_call(
        _gla_swa_kernel, grid=(BATCH, nt),
        in_specs=[row(GLA_QK), row(GLA_QK), row(A_WIDTH), row(A_WIDTH), row(LANES),
                  full(w_up), full(b_up), full(gain),
                  smem, smem, row(B_WIDTH), prev(0), own(0), prev(1), own(1), row(B_WIDTH)],
        out_specs=[row(A_WIDTH), row(B_WIDTH)],
        out_shape=[jax.ShapeDtypeStruct((TOKENS, A_WIDTH), BF16),
                   jax.ShapeDtypeStruct((TOKENS, B_WIDTH), BF16)],
        scratch_shapes=[pltpu.VMEM((A_WIDTH, GLA_QK), F32),
                        pltpu.VMEM((2 * SWA_WINDOW, SWA_HEADS * SWA_WINDOW), F32),
                        pltpu.VMEM((SWA_SLOTS, 2 * SWA_WINDOW, SWA_HEADS * SWA_WINDOW), F32),
                        pltpu.VMEM((SWA_SLOTS, 1, SWA_HEADS * SWA_WINDOW), F32)],
        compiler_params=pltpu.CompilerParams(dimension_semantics=("arbitrary", "arbitrary"),
                                             vmem_limit_bytes=VMEM_LIMIT),
        name="gla_swa",
    )(aq, ak, av, agate, adown, w_up, b_up, gain, sinks, rel_bias, bq, bkv, bkv, bkv, bkv, bgate)


def _moba_kernel(rb_ref, q_ref, k_ref, v_ref, gate_ref, o_ref,
                 kaug_ref, vt_ref, km_ref, bias_ref, qat_ref, s_ref, smax_ref, m_ref, acc_ref):
    BLK = MOBA_BLOCK
    G = MOBA_GROUP
    KVH = MOBA_KV_HEADS
    R = G * BLK
    NP1 = MOBA_NBLK + 1
    i = pl.program_id(1)

    @pl.when(i == 0)
    def _():
        lane = lax.broadcasted_iota(jnp.int32, (BLK, LANES), 1)
        ones_rows = jnp.ones((MOBA_VT_ROWS - MOBA_DH, BLK), BF16)

        def prep(j, carry):
            rows = pl.ds(pl.multiple_of(j * BLK, BLK), BLK)
            onehot = jnp.where(lane == j, 1.0, 0.0).astype(BF16)
            for kh in range(KVH):
                kb = k_ref[rows, kh * MOBA_DH:(kh + 1) * MOBA_DH]
                kaug_ref[kh * NP1 + j + 1, :, 0:LANES] = kb
                kaug_ref[kh * NP1 + j + 1, :, LANES:2 * LANES] = onehot
                km_ref[pl.ds(kh * MOBA_NBLK + j, 1), :] = jnp.mean(kb.astype(F32), axis=0, keepdims=True)
                vb = v_ref[rows, kh * MOBA_DH:(kh + 1) * MOBA_DH]
                vt_ref[kh * NP1 + j + 1, 0:MOBA_DH, :] = vb.astype(F32).T.astype(BF16)
                vt_ref[kh * NP1 + j + 1, MOBA_DH:, :] = ones_rows
            return carry

        lax.fori_loop(0, MOBA_NBLK, prep, 0)
        for kh in range(KVH):
            kaug_ref[kh * NP1, :, 0:LANES] = jnp.zeros((BLK, LANES), BF16)
            kaug_ref[kh * NP1, :, LANES:2 * LANES] = jnp.where(lane == MOBA_NBLK, 1.0, 0.0).astype(BF16)
            vt_ref[kh * NP1] = jnp.zeros((MOBA_VT_ROWS, BLK), BF16)
            qat_ref[kh, MOBA_DH + MOBA_NBLK:, :] = jnp.full((2 * LANES - MOBA_DH - MOBA_NBLK, R), NEG_BIG, BF16)

    @pl.when(jnp.logical_and(pl.program_id(0) == 0, i == 0))
    def _():
        tk = lax.broadcasted_iota(jnp.int32, (BLK, BLK), 0)
        tq = lax.broadcasted_iota(jnp.int32, (BLK, BLK), 1)
        d_own = tq - tk
        for head in range(MOBA_HEADS):
            cols = slice(head * BLK, (head + 1) * BLK)
            bias_ref[0:BLK, cols] = _t5_bias(d_own + BLK, rb_ref, head) * LOG2E
            bias_ref[BLK:2 * BLK, cols] = jnp.where(d_own >= 0, _t5_bias(d_own, rb_ref, head) * LOG2E, -jnp.inf)

    blk = lax.broadcasted_iota(jnp.int32, (MOBA_NBLK, R), 0)
    blkf = blk.astype(F32)
    past = blk < i
    colh = lax.broadcasted_iota(jnp.int32, (1, R), 1) // BLK
    far_row = REL_BUCKETS - 1

    own_max = []
    for kh in range(KVH):
        q = jnp.concatenate([q_ref[:, (kh * G + g) * MOBA_DH:(kh * G + g + 1) * MOBA_DH] for g in range(G)],
                            axis=0)
        q_t = q.astype(F32).T.astype(BF16)
        qat_ref[kh, 0:MOBA_DH, :] = q_t
        s_own = (_dot(kaug_ref[kh * NP1 + i + 1, :, 0:MOBA_DH], q_t)
                 + bias_ref[BLK:2 * BLK, kh * R:(kh + 1) * R])
        s_ref[kh, BLK:2 * BLK, :] = s_own
        own_max.append(jnp.max(s_own, axis=0, keepdims=True))

        km = km_ref[kh * MOBA_NBLK:(kh + 1) * MOBA_NBLK, :]
        km_hi = km.astype(BF16)
        km_lo = (km - km_hi.astype(F32)).astype(BF16)
        gate = _dot(km_hi, q_t) + _dot(km_lo, q_t)
        g_ = jnp.where(past, gate, -jnp.inf)
        selected = jnp.zeros((MOBA_NBLK, R), dtype=jnp.bool_)
        for _ in range(MOBA_TOPK):
            mx = jnp.max(g_, axis=0, keepdims=True)
            first = jnp.min(jnp.where(g_ == mx, blkf, 1e9), axis=0, keepdims=True)
            pick = jnp.logical_and(blkf == first, past)
            selected = jnp.logical_or(selected, pick)
            g_ = jnp.where(pick, -jnp.inf, g_)
        cfar = jnp.where(colh == 0, rb_ref[far_row, kh * G],
                         jnp.where(colh == 1, rb_ref[far_row, kh * G + 1],
                                   jnp.where(colh == 2, rb_ref[far_row, kh * G + 2],
                                             rb_ref[far_row, kh * G + 3])))
        sel_bias = jnp.where(selected, jnp.where(blk < i - 1, cfar * LOG2E, 0.0), NEG_BIG)
        qat_ref[kh, MOBA_DH:MOBA_DH + MOBA_NBLK, :] = sel_bias.astype(BF16)

    for kh in range(KVH):
        s_near = _dot(kaug_ref[kh * NP1 + i], qat_ref[kh]) + bias_ref[0:BLK, kh * R:(kh + 1) * R]
        s_ref[kh, 0:BLK, :] = s_near
        smax_ref[kh] = jnp.maximum(own_max[kh], jnp.max(s_near, axis=0, keepdims=True))

    def produce(slot, kh, start):
        k2 = kaug_ref[pl.ds(kh * NP1 + start, 2)].reshape(2 * BLK, 2 * LANES)
        s = _dot(k2, qat_ref[kh])
        s_ref[slot * KVH + kh] = s
        smax_ref[slot * KVH + kh] = jnp.max(s, axis=0, keepdims=True)

    def consume(slot, kh, start, first=False):
        m_new = smax_ref[slot * KVH + kh]
        if not first:
            m_old = m_ref[kh]
            m_new = jnp.maximum(m_old, m_new)
        p = jnp.exp2(s_ref[slot * KVH + kh] - m_new).astype(BF16)
        pv = (_dot(vt_ref[kh * NP1 + start], p[0:BLK])
              + _dot(vt_ref[kh * NP1 + start + 1], p[BLK:2 * BLK]))
        acc_ref[kh] = pv if first else jnp.exp2(m_old - m_new) * acc_ref[kh] + pv
        m_ref[kh] = m_new

    n_far = jnp.maximum(i - 1, 0)
    n_pairs = (n_far + 1) // 2

    def far_start(t):
        return jnp.maximum(n_far - 2 * t - 1, 0)

    def far_step(t, slot):
        for kh in range(KVH):
            produce(1 - slot, kh, far_start(t + 1))
            consume(slot, kh, far_start(t))

    for kh in range(KVH):
        produce(1, kh, far_start(0))
        consume(0, kh, i, first=True)

    def far_four(u, carry):
        for d in range(4):
            far_step(4 * u + d, (d + 1) % 2)
        return carry

    n_full = jnp.maximum(n_pairs - 1, 0)
    lax.fori_loop(0, n_full // 4, far_four, 0)
    done = (n_full // 4) * 4

    @pl.when(n_full - done >= 2)
    def _():
        far_step(done, 1)
        far_step(done + 1, 0)

    @pl.when(n_full % 2 == 1)
    def _():
        far_step(n_full - 1, 1)

    def finalize(kh):
        acc = acc_ref[kh]
        o = (acc[0:MOBA_DH] * (1.0 / acc[MOBA_DH:MOBA_DH + 1])).T
        for g in range(G):
            cols = slice((kh * G + g) * MOBA_DH, (kh * G + g + 1) * MOBA_DH)
            gt = gate_ref[:, cols].astype(F32)
            o_ref[:, cols] = (o[g * BLK:(g + 1) * BLK] * _silu(gt)).astype(o_ref.dtype)

    for slot in range(2):
        @pl.when(jnp.logical_and(n_pairs >= 1, n_pairs % 2 == slot))
        def _():
            for kh in range(KVH):
                consume(slot, kh, far_start(n_pairs - 1))
                finalize(kh)

    @pl.when(n_pairs == 0)
    def _():
        for kh in range(KVH):
            finalize(kh)


def _moba(cq, ck, cv, cgate, rel_bias):
    R = MOBA_GROUP * MOBA_BLOCK
    KVH = MOBA_KV_HEADS
    NP1 = MOBA_NBLK + 1
    qspec = pl.BlockSpec((MOBA_BLOCK, C_WIDTH), lambda b, i: (b * MOBA_NBLK + i, 0))
    kvspec = pl.BlockSpec((SEQ, KVH * MOBA_DH), lambda b, i: (b, 0), pipeline_mode=pl.Buffered(1))
    return pl.pallas_call(
        _moba_kernel, grid=(BATCH, MOBA_NBLK),
        in_specs=[pl.BlockSpec(memory_space=pltpu.SMEM), qspec, kvspec, kvspec, qspec],
        out_specs=qspec,
        out_shape=jax.ShapeDtypeStruct((TOKENS, C_WIDTH), BF16),
        scratch_shapes=[pltpu.VMEM((KVH * NP1, MOBA_BLOCK, 2 * LANES), BF16),
                        pltpu.VMEM((KVH * NP1, MOBA_VT_ROWS, MOBA_BLOCK), BF16),
                        pltpu.VMEM((KVH * MOBA_NBLK, MOBA_DH), F32),
                        pltpu.VMEM((2 * MOBA_BLOCK, KVH * R), F32),
                        pltpu.VMEM((KVH, 2 * LANES, R), BF16),
                        pltpu.VMEM((2 * KVH, 2 * MOBA_BLOCK, R), F32),
                        pltpu.VMEM((2 * KVH, 1, R), F32),
                        pltpu.VMEM((KVH, 1, R), F32),
                        pltpu.VMEM((KVH, MOBA_VT_ROWS, R), F32)],
        compiler_params=pltpu.CompilerParams(
            dimension_semantics=("arbitrary", "arbitrary"), vmem_limit_bytes=MOBA_VMEM_LIMIT),
        name="moba",
    )(rel_bias, cq, ck, cv, cgate)


def kernel(x, norm_gain, final_gain, rel_bias, w_in_even, gla_w_up, gla_b_up, gla_norm_gain, swa_sinks,
           w_out_even, w_in_odd, w_out_odd):
    assert x.shape == (BATCH, SEQ, D_MODEL)
    h0 = x.reshape(TOKENS, D_MODEL)

    we = w_in_even[0]
    splits = np.cumsum([0, GLA_QK, GLA_QK, A_WIDTH, GLA_RANK, A_WIDTH, B_WIDTH, SWA_KV_HEADS * SWA_DH,
                        SWA_KV_HEADS * SWA_DH, B_WIDTH])
    w_aq, w_ak, w_av, w_down, w_agate, w_bq = [we[:, splits[k]:splits[k + 1]] for k in range(6)]
    w_bkv = we[:, splits[6]:splits[8]]
    w_bgate = we[:, splits[8]:splits[9]]
    w_down = jnp.pad(jnp.concatenate([w_down] * 3, axis=1), ((0, 0), (0, LANES - 3 * GLA_RANK)))
    proj_ws = [w.astype(BF16) for w in (w_aq, w_ak, w_av, w_down, w_agate, w_bq, w_bkv, w_bgate)]
    proj_dt = [BF16, BF16, BF16, F32, BF16, BF16, BF16, BF16]
    scales = [GLA_DK ** -0.5, 1.0, 1.0, 1.0, 1.0, SWA_DH ** -0.5 * LOG2E, 1.0, 1.0]
    aq, ak, av, adown, agate, bq, bkv, bgate = _layer_io(
        h0, [], norm_gain[0:1], proj_ws, proj_dt, scales, write_h=False)

    w_up_hi = gla_w_up[0].astype(BF16)
    w_up_lo = (gla_w_up[0] - w_up_hi.astype(F32)).astype(BF16)
    w_up = jnp.pad(jnp.concatenate([w_up_hi, w_up_hi, w_up_lo], axis=0), ((0, LANES - 3 * GLA_RANK), (0, 0)))
    oa, ob = _gla_swa(aq, ak, av, agate, adown, w_up, gla_b_up[0:1], gla_norm_gain[0:1],
                      bq, bkv, bgate, rel_bias, swa_sinks[0])

    wo = w_out_even[0].astype(BF16)
    wi = w_in_odd[0]
    osp = np.cumsum([0, C_WIDTH, MOBA_KV_HEADS * MOBA_DH, MOBA_KV_HEADS * MOBA_DH, C_WIDTH])
    odd_ws = [wi[:, osp[k]:osp[k + 1]].astype(BF16) for k in range(4)]
    h1, cq, ck, cv, cgate = _layer_io(
        h0, [(oa, wo[:A_WIDTH]), (ob, wo[A_WIDTH:])], norm_gain[1:2], odd_ws, [BF16] * 4,
        [MOBA_DH ** -0.5 * LOG2E, 1.0, 1.0, 1.0], write_h=True)

    oc = _moba(cq, ck, cv, cgate, rel_bias)

    (out,) = _layer_io(h1, [(oc, w_out_odd[0].astype(BF16))], final_gain.reshape(1, D_MODEL), [], [], [],
                       write_h=False)
    return out.reshape(BATCH, SEQ, D_MODEL)
```

```python
import functools
import math

import numpy as np
import jax
import jax.numpy as jnp
from jax import lax
from jax.experimental import pallas as pl
from jax.experimental.pallas import tpu as pltpu

D_MODEL = 1024
BATCH = 2
SEQ = 8192
TOKENS = BATCH * SEQ

REL_BUCKETS = 32
REL_MAX_EXACT = 16
REL_MAX_DIST = 128
ATTN_HEADS = 8

GLA_HEADS = 4
GLA_DK = 64
GLA_DV = 128
GLA_RANK = 16
GLA_TAU = 16.0
GLA_CHUNK = 64

SWA_HEADS = 8
SWA_KV_HEADS = 2
SWA_DH = 64
SWA_WINDOW = 128

MOBA_HEADS = 8
MOBA_KV_HEADS = 2
MOBA_DH = 128
MOBA_BLOCK = 256
MOBA_TOPK = 3
MOBA_GROUP = MOBA_HEADS // MOBA_KV_HEADS
MOBA_NBLK = SEQ // MOBA_BLOCK
MOBA_VT_ROWS = MOBA_DH + 16
LOG2E = math.log2(math.e)
SWA_VT_ROWS = SWA_KV_HEADS * SWA_DH + 16

EPS = 1e-6
LANES = 128
NEG_BIG = -1e30
VMEM_LIMIT = 48 * 1024 * 1024
MOBA_VMEM_LIMIT = 56 * 1024 * 1024

A_WIDTH = GLA_HEADS * GLA_DV
B_WIDTH = SWA_HEADS * SWA_DH
C_WIDTH = MOBA_HEADS * MOBA_DH
GLA_QK = GLA_HEADS * GLA_DK

ROW_TILE = 1024
ROW_SUB = 256
GLA_TILE = 1024
SWA_TILE = 1024
SWA_SLOTS = 2

F32 = jnp.float32
BF16 = jnp.bfloat16


def _dot(a, b):
    return jnp.dot(a, b, preferred_element_type=F32)


def _dot_nt(a, b):
    return lax.dot_general(a, b, (((1,), (1,)), ((), ())), preferred_element_type=F32)


def _dot_tn(a, b):
    return lax.dot_general(a, b, (((0,), (0,)), ((), ())), preferred_element_type=F32)


def _silu(x):
    return x * (1.0 / (1.0 + jnp.exp(-x)))


def _t5_thresholds():
    d = np.arange(REL_MAX_DIST + 1)
    nf = np.maximum(d, 1).astype(np.float32)
    large = REL_MAX_EXACT + (np.log(nf / np.float32(REL_MAX_EXACT))
                             / np.float32(math.log(REL_MAX_DIST / REL_MAX_EXACT))
                             * np.float32(REL_BUCKETS - REL_MAX_EXACT)).astype(np.int32)
    bucket = np.where(d < REL_MAX_EXACT, d, np.minimum(large, REL_BUCKETS - 1))
    assert np.all(np.diff(bucket) >= 0) and bucket[-1] == REL_BUCKETS - 1
    return [int(np.argmax(bucket >= b)) for b in range(REL_BUCKETS)]


_T5_THRESH = _t5_thresholds()


def _t5_bias(dist, rb_ref, head):
    out = jnp.full(dist.shape, rb_ref[0, head], F32)
    for b in range(1, REL_BUCKETS):
        out = jnp.where(dist >= _T5_THRESH[b], rb_ref[b, head], out)
    return out


def _layer_io_kernel(*refs, n_acc, n_proj, write_h, scales, row_sub):
    h_ref = refs[0]
    pos = 1
    acc_pairs = []
    for _ in range(n_acc):
        acc_pairs.append((refs[pos], refs[pos + 1]))
        pos += 2
    gain_ref = refs[pos]
    pos += 1
    wp_refs = refs[pos:pos + n_proj]
    pos += n_proj
    out_refs = refs[pos:]

    o = 1 if write_h else 0
    n_sub = ROW_TILE // row_sub
    pending = {}

    def norm_stage(s):
        rows = slice(s * row_sub, (s + 1) * row_sub)
        h = h_ref[rows, :]
        for a_ref, w_ref in acc_pairs:
            h = h + _dot(a_ref[rows, :], w_ref[...])
        if write_h:
            out_refs[0][rows, :] = h
        y = h * lax.rsqrt(jnp.mean(h * h, axis=-1, keepdims=True) + EPS) * gain_ref[...]
        if n_proj == 0:
            out_refs[o][rows, :] = y
        else:
            pending[s] = y.astype(BF16)

    def proj_stage(s):
        rows = slice(s * row_sub, (s + 1) * row_sub)
        yb = pending.pop(s)
        for k in range(n_proj):
            r = _dot(yb, wp_refs[k][...])
            if scales[k] != 1.0:
                r = r * scales[k]
            out_refs[o + k][rows, :] = r.astype(out_refs[o + k].dtype)

    norm_stage(0)
    for s in range(n_sub):
        if s + 1 < n_sub:
            norm_stage(s + 1)
        if n_proj:
            proj_stage(s)


def _layer_io(h, acc_pairs, gain, proj_ws, proj_dtypes, scales, write_h, row_sub=ROW_SUB):
    n_rows = h.shape[0]
    grid = (n_rows // ROW_TILE,)
    row_spec = lambda n: pl.BlockSpec((ROW_TILE, n), lambda i: (i, 0))
    full_spec = lambda a: pl.BlockSpec(a.shape, lambda i: (0,) * a.ndim, pipeline_mode=pl.Buffered(1))

    args = [h]
    in_specs = [row_spec(D_MODEL)]
    for a, w in acc_pairs:
        args += [a, w]
        in_specs += [row_spec(a.shape[1]), full_spec(w)]
    args.append(gain)
    in_specs.append(full_spec(gain))
    for w in proj_ws:
        args.append(w)
        in_specs.append(full_spec(w))

    out_shape, out_specs = [], []
    if write_h:
        out_shape.append(jax.ShapeDtypeStruct((n_rows, D_MODEL), F32))
        out_specs.append(row_spec(D_MODEL))
    if proj_ws:
        for w, dt in zip(proj_ws, proj_dtypes):
            out_shape.append(jax.ShapeDtypeStruct((n_rows, w.shape[1]), dt))
            out_specs.append(row_spec(w.shape[1]))
    else:
        out_shape.append(jax.ShapeDtypeStruct((n_rows, D_MODEL), F32))
        out_specs.append(row_spec(D_MODEL))

    kern = functools.partial(_layer_io_kernel, n_acc=len(acc_pairs), n_proj=len(proj_ws),
                             write_h=write_h, scales=tuple(scales), row_sub=row_sub)
    return pl.pallas_call(
        kern, grid=grid, in_specs=in_specs, out_specs=out_specs, out_shape=out_shape,
        compiler_params=pltpu.CompilerParams(dimension_semantics=("arbitrary",),
                                             vmem_limit_bytes=VMEM_LIMIT),
        name="layer_io",
    )(*args)


def _gla_init(st_ref):
    @pl.when(pl.program_id(1) == 0)
    def _():
        st_ref[...] = jnp.zeros_like(st_ref)


def _gla_steps(q_ref, k_ref, v_ref, gate_ref, down_ref, wup_ref, bup_ref, gain_ref, o_ref, st_ref):
    C = GLA_CHUNK

    a = down_ref[...]
    a_hi = a.astype(BF16)
    a_lo = (a - a_hi.astype(F32)).astype(BF16)
    lane_a = lax.broadcasted_iota(jnp.int32, a.shape, 1) // GLA_RANK
    z = _dot(jnp.where(lane_a == 1, a_lo, a_hi), wup_ref[...]) + bup_ref[...]
    log_a = (jnp.minimum(z, 0.0) - jnp.log(1.0 + jnp.exp(-jnp.abs(z)))) * (1.0 / GLA_TAU)

    r_i = lax.broadcasted_iota(jnp.int32, (C, C), 0)
    c_i = lax.broadcasted_iota(jnp.int32, (C, C), 1)
    tri = (c_i <= r_i).astype(BF16)
    lane_qk = lax.broadcasted_iota(jnp.int32, (C, GLA_QK), 1)
    head_masks = [(lane_qk // GLA_DK) == h for h in range(GLA_HEADS)]
    rs = lax.broadcasted_iota(jnp.int32, (GLA_HEADS * C, C), 0)
    cs = lax.broadcasted_iota(jnp.int32, (GLA_HEADS * C, C), 1)
    causal = (rs % C) >= cs
    st_r = lax.broadcasted_iota(jnp.int32, (A_WIDTH, GLA_QK), 0)
    st_c = lax.broadcasted_iota(jnp.int32, (A_WIDTH, GLA_QK), 1)
    same_head = (st_r // GLA_DV) == (st_c // GLA_DK)
    gain = gain_ref[...]

    pending = {}

    def cumsum(c):
        g = log_a[c * C:(c + 1) * C]
        g_hi = g.astype(BF16)
        g_lo = (g - g_hi.astype(F32)).astype(BF16)
        pending[c] = _dot(tri, g_hi) + _dot(tri, g_lo)

    def prep(c):
        rows = slice(c * C, (c + 1) * C)
        b = pending.pop(c)
        b_last = b[C - 1:C]
        q = q_ref[rows, :].astype(F32)
        k = k_ref[rows, :].astype(F32)
        q_e = q * jnp.exp(b)
        k_e = (k * jnp.exp(-b)).astype(BF16)
        k_l = (k * jnp.exp(b_last - b)).astype(BF16)
        decay = jnp.exp(b_last)
        q_stack = jnp.concatenate([jnp.where(m, q_e, 0.0) for m in head_masks], axis=0).astype(BF16)
        pending[c] = (q_stack, q_e.astype(BF16), k_e, k_l, decay)

    def scores(c):
        q_stack, q_eb, k_e, k_l, decay = pending.pop(c)
        att = jnp.where(causal, _dot_nt(q_stack, k_e), 0.0).astype(BF16)
        pending[c] = (att, q_eb, k_l, decay)

    def main(c):
        rows = slice(c * C, (c + 1) * C)
        att, q_eb, k_l, decay = pending.pop(c)
        v = v_ref[rows, :]
        o_full = _dot(att, v)
        o_intra = jnp.concatenate(
            [o_full[h * C:(h + 1) * C, h * GLA_DV:(h + 1) * GLA_DV] for h in range(GLA_HEADS)], axis=1)
        st = st_ref[...]
        o = o_intra + _dot_nt(q_eb, st.astype(BF16))
        kv_t = _dot_tn(v, k_l)
        st_ref[...] = st * decay + jnp.where(same_head, kv_t, 0.0)

        outs = []
        for h in range(GLA_HEADS):
            oh = o[:, h * GLA_DV:(h + 1) * GLA_DV]
            outs.append(oh * lax.rsqrt(jnp.mean(oh * oh, axis=-1, keepdims=True) + EPS) * gain)
        on = jnp.concatenate(outs, axis=1)
        o_ref[rows, :] = (on * _silu(gate_ref[rows, :].astype(F32))).astype(o_ref.dtype)

    n_chunks = GLA_TILE // C
    steps = [functools.partial(cumsum, 0), functools.partial(cumsum, 1), functools.partial(cumsum, 2),
             functools.partial(prep, 0), functools.partial(prep, 1), functools.partial(scores, 0)]
    for c in range(n_chunks):
        if c + 3 < n_chunks:
            steps.append(functools.partial(cumsum, c + 3))
        if c + 2 < n_chunks:
            steps.append(functools.partial(prep, c + 2))
        if c + 1 < n_chunks:
            steps.append(functools.partial(scores, c + 1))
        steps.append(functools.partial(main, c))
    return steps


def _swa_init(rb_ref, bias_ref):
    W = SWA_WINDOW

    @pl.when(jnp.logical_and(pl.program_id(0) == 0, pl.program_id(1) == 0))
    def _():
        kj = lax.broadcasted_iota(jnp.int32, (2 * W, W), 0)
        qi = lax.broadcasted_iota(jnp.int32, (2 * W, W), 1)
        dist = qi + W - kj
        in_win = jnp.logical_and(dist >= 0, dist < W)
        for h in range(SWA_HEADS):
            bias_ref[:, h * W:(h + 1) * W] = jnp.where(in_win, _t5_bias(dist, rb_ref, h) * LOG2E, -jnp.inf)


def _swa_steps(sink_ref, q_ref, kp_ref, k_ref, vp_ref, v_ref, gate_ref, o_ref, bias_ref, s_ref, smax_ref):
    W = SWA_WINDOW
    H = SWA_HEADS
    DH = SWA_DH
    G = SWA_HEADS // SWA_KV_HEADS
    t = pl.program_id(1)

    kcat = jnp.concatenate([kp_ref[...], k_ref[...]], axis=0)
    vcat = jnp.concatenate([vp_ref[...], v_ref[...]], axis=0)
    vt = vcat.astype(F32).T.astype(BF16)
    ones_rows = jnp.ones((SWA_VT_ROWS - LANES, 2 * W), BF16)
    q_t = q_ref[...].astype(F32).T.astype(BF16)
    zeros = jnp.zeros((DH, W), BF16)
    colh = lax.broadcasted_iota(jnp.int32, (1, H * W), 1) // W
    sink = jnp.full((1, H * W), sink_ref[H - 1], F32)
    for h in range(H - 2, -1, -1):
        sink = jnp.where(colh == h, sink_ref[h], sink)
    sink = sink * LOG2E
    key_prev = lax.broadcasted_iota(jnp.int32, (2 * W, H * W), 0) < W

    def produce(blk):
        tok = slice(blk * W, (blk + 1) * W)
        win = slice(blk * W, blk * W + 2 * W)
        pieces = []
        for h in range(H):
            piece = q_t[h * DH:(h + 1) * DH, tok]
            pieces.append(jnp.concatenate([piece, zeros] if h < G else [zeros, piece], axis=0))
        q_pad = jnp.concatenate(pieces, axis=1)
        s = _dot(kcat[win], q_pad) + bias_ref[...]
        if blk == 0:
            s = jnp.where(jnp.logical_and(t == 0, key_prev), -jnp.inf, s)
        s_ref[blk % SWA_SLOTS] = s
        smax_ref[blk % SWA_SLOTS] = jnp.maximum(jnp.max(s, axis=0, keepdims=True), sink)

    pending = {}

    def attend(blk):
        win = slice(blk * W, blk * W + 2 * W)
        m = smax_ref[blk % SWA_SLOTS]
        p = jnp.exp2(s_ref[blk % SWA_SLOTS] - m).astype(BF16)
        pv = _dot(jnp.concatenate([vt[:, win], ones_rows], axis=0), p)
        pending[blk] = (pv, m)

    def finish(blk):
        tok = slice(blk * W, (blk + 1) * W)
        pv, m = pending.pop(blk)
        inv = 1.0 / (pv[LANES:LANES + 1] + jnp.exp2(sink - m))
        o_t = jnp.concatenate(
            [pv[(h // G) * DH:(h // G + 1) * DH, h * W:(h + 1) * W] * inv[:, h * W:(h + 1) * W]
             for h in range(H)], axis=0)
        gate = gate_ref[tok, :].astype(F32)
        o_ref[tok, :] = (o_t.T * _silu(gate)).astype(o_ref.dtype)

    n_blk = SWA_TILE // W
    ahead = SWA_SLOTS - 1
    steps = [functools.partial(produce, blk) for blk in range(ahead)]
    for blk in range(n_blk):
        if blk + ahead < n_blk:
            steps.append(functools.partial(produce, blk + ahead))
        steps.append(functools.partial(attend, blk))
        if blk >= 1:
            steps.append(functools.partial(finish, blk - 1))
    steps.append(functools.partial(finish, n_blk - 1))
    return steps


def _gla_swa_kernel(aq_ref, ak_ref, av_ref, agate_ref, adown_ref, wup_ref, bup_ref, gain_ref,
                    sink_ref, rb_ref, bq_ref, kp_ref, k_ref, vp_ref, v_ref, bgate_ref,
                    oa_ref, ob_ref, st_ref, bias_ref, s_ref, smax_ref):
    _gla_init(st_ref)
    _swa_init(rb_ref, bias_ref)
    gla = _gla_steps(aq_ref, ak_ref, av_ref, agate_ref, adown_ref, wup_ref, bup_ref, gain_ref, oa_ref, st_ref)
    swa = _swa_steps(sink_ref, bq_ref, kp_ref, k_ref, vp_ref, v_ref, bgate_ref, ob_ref, bias_ref, s_ref, smax_ref)
    for step in gla + swa:
        step()


def _gla_swa(aq, ak, av, agate, adown, w_up, b_up, gain, bq, bkv, bgate, rel_bias, sinks):
    assert GLA_TILE == SWA_TILE
    nt = SEQ // SWA_TILE
    per = SWA_TILE // SWA_WINDOW
    row = lambda n: pl.BlockSpec((SWA_TILE, n), lambda b, t: (b * nt + t, 0))
    full = lambda a: pl.BlockSpec(a.shape, lambda b, t: (0,) * a.ndim)
    own = lambda c: pl.BlockSpec((SWA_TILE, LANES), lambda b, t: (b * nt + t, c))
    prev = lambda c: pl.BlockSpec((SWA_WINDOW, LANES),
                                  lambda b, t: (jnp.maximum((b * nt + t) * per - 1, 0), c))
    smem = pl.BlockSpec(memory_space=pltpu.SMEM)
    return pl.pallas_call(
        _gla_swa_kernel, grid=(BATCH, nt),
        in_specs=[row(GLA_QK), row(GLA_QK), row(A_WIDTH), row(A_WIDTH), row(LANES),
                  full(w_up), full(b_up), full(gain),
                  smem, smem, row(B_WIDTH), prev(0), own(0), prev(1), own(1), row(B_WIDTH)],
        out_specs=[row(A_WIDTH), row(B_WIDTH)],
        out_shape=[jax.ShapeDtypeStruct((TOKENS, A_WIDTH), BF16),
                   jax.ShapeDtypeStruct((TOKENS, B_WIDTH), BF16)],
        scratch_shapes=[pltpu.VMEM((A_WIDTH, GLA_QK), F32),
                        pltpu.VMEM((2 * SWA_WINDOW, SWA_HEADS * SWA_WINDOW), F32),
                        pltpu.VMEM((SWA_SLOTS, 2 * SWA_WINDOW, SWA_HEADS * SWA_WINDOW), F32),
                        pltpu.VMEM((SWA_SLOTS, 1, SWA_HEADS * SWA_WINDOW), F32)],
        compiler_params=pltpu.CompilerParams(dimension_semantics=("arbitrary", "arbitrary"),
                                             vmem_limit_bytes=VMEM_LIMIT),
        name="gla_swa",
    )(aq, ak, av, agate, adown, w_up, b_up, gain, sinks, rel_bias, bq, bkv, bkv, bkv, bkv, bgate)


def _moba_kernel(rb_ref, q_ref, k_ref, v_ref, gate_ref, o_ref,
                 kaug_ref, vt_ref, km_ref, bias_ref, qat_ref, s_ref, smax_ref, m_ref, acc_ref):
    BLK = MOBA_BLOCK
    G = MOBA_GROUP
    KVH = MOBA_KV_HEADS
    R = G * BLK
    NP1 = MOBA_NBLK + 1
    i = pl.program_id(1)

    @pl.when(i == 0)
    def _():
        lane = lax.broadcasted_iota(jnp.int32, (BLK, LANES), 1)
        ones_rows = jnp.ones((MOBA_VT_ROWS - MOBA_DH, BLK), BF16)

        def prep(j, carry):
            rows = pl.ds(pl.multiple_of(j * BLK, BLK), BLK)
            onehot = jnp.where(lane == j, 1.0, 0.0).astype(BF16)
            for kh in range(KVH):
                kb = k_ref[rows, kh * MOBA_DH:(kh + 1) * MOBA_DH]
                kaug_ref[kh * NP1 + j + 1, :, 0:LANES] = kb
                kaug_ref[kh * NP1 + j + 1, :, LANES:2 * LANES] = onehot
                km_ref[pl.ds(kh * MOBA_NBLK + j, 1), :] = jnp.mean(kb.astype(F32), axis=0, keepdims=True)
                vb = v_ref[rows, kh * MOBA_DH:(kh + 1) * MOBA_DH]
                vt_ref[kh * NP1 + j + 1, 0:MOBA_DH, :] = vb.astype(F32).T.astype(BF16)
                vt_ref[kh * NP1 + j + 1, MOBA_DH:, :] = ones_rows
            return carry

        lax.fori_loop(0, MOBA_NBLK, prep, 0)
        for kh in range(KVH):
            kaug_ref[kh * NP1, :, 0:LANES] = jnp.zeros((BLK, LANES), BF16)
            kaug_ref[kh * NP1, :, LANES:2 * LANES] = jnp.where(lane == MOBA_NBLK, 1.0, 0.0).astype(BF16)
            vt_ref[kh * NP1] = jnp.zeros((MOBA_VT_ROWS, BLK), BF16)
            qat_ref[kh, MOBA_DH + MOBA_NBLK:, :] = jnp.full((2 * LANES - MOBA_DH - MOBA_NBLK, R), NEG_BIG, BF16)

    @pl.when(jnp.logical_and(pl.program_id(0) == 0, i == 0))
    def _():
        tk = lax.broadcasted_iota(jnp.int32, (BLK, BLK), 0)
        tq = lax.broadcasted_iota(jnp.int32, (BLK, BLK), 1)
        d_own = tq - tk
        for head in range(MOBA_HEADS):
            cols = slice(head * BLK, (head + 1) * BLK)
            bias_ref[0:BLK, cols] = _t5_bias(d_own + BLK, rb_ref, head) * LOG2E
            bias_ref[BLK:2 * BLK, cols] = jnp.where(d_own >= 0, _t5_bias(d_own, rb_ref, head) * LOG2E, -jnp.inf)

    blk = lax.broadcasted_iota(jnp.int32, (MOBA_NBLK, R), 0)
    blkf = blk.astype(F32)
    past = blk < i
    colh = lax.broadcasted_iota(jnp.int32, (1, R), 1) // BLK
    far_row = REL_BUCKETS - 1

    own_max = []
    for kh in range(KVH):
        q = jnp.concatenate([q_ref[:, (kh * G + g) * MOBA_DH:(kh * G + g + 1) * MOBA_DH] for g in range(G)],
                            axis=0)
        q_t = q.astype(F32).T.astype(BF16)
        qat_ref[kh, 0:MOBA_DH, :] = q_t
        s_own = (_dot(kaug_ref[kh * NP1 + i + 1, :, 0:MOBA_DH], q_t)
                 + bias_ref[BLK:2 * BLK, kh * R:(kh + 1) * R])
        s_ref[kh, BLK:2 * BLK, :] = s_own
        own_max.append(jnp.max(s_own, axis=0, keepdims=True))

        km = km_ref[kh * MOBA_NBLK:(kh + 1) * MOBA_NBLK, :]
        km_hi = km.astype(BF16)
        km_lo = (km - km_hi.astype(F32)).astype(BF16)
        gate = _dot(km_hi, q_t) + _dot(km_lo, q_t)
        g_ = jnp.where(past, gate, -jnp.inf)
        selected = jnp.zeros((MOBA_NBLK, R), dtype=jnp.bool_)
        for _ in range(MOBA_TOPK):
            mx = jnp.max(g_, axis=0, keepdims=True)
            first = jnp.min(jnp.where(g_ == mx, blkf, 1e9), axis=0, keepdims=True)
            pick = jnp.logical_and(blkf == first, past)
            selected = jnp.logical_or(selected, pick)
            g_ = jnp.where(pick, -jnp.inf, g_)
        cfar = jnp.where(colh == 0, rb_ref[far_row, kh * G],
                         jnp.where(colh == 1, rb_ref[far_row, kh * G + 1],
                                   jnp.where(colh == 2, rb_ref[far_row, kh * G + 2],
                                             rb_ref[far_row, kh * G + 3])))
        sel_bias = jnp.where(selected, jnp.where(blk < i - 1, cfar * LOG2E, 0.0), NEG_BIG)
        qat_ref[kh, MOBA_DH:MOBA_DH + MOBA_NBLK, :] = sel_bias.astype(BF16)

    for kh in range(KVH):
        s_near = _dot(kaug_ref[kh * NP1 + i], qat_ref[kh]) + bias_ref[0:BLK, kh * R:(kh + 1) * R]
        s_ref[kh, 0:BLK, :] = s_near
        smax_ref[kh] = jnp.maximum(own_max[kh], jnp.max(s_near, axis=0, keepdims=True))

    def produce(slot, kh, start):
        k2 = kaug_ref[pl.ds(kh * NP1 + start, 2)].reshape(2 * BLK, 2 * LANES)
        s = _dot(k2, qat_ref[kh])
        s_ref[slot * KVH + kh] = s
        smax_ref[slot * KVH + kh] = jnp.max(s, axis=0, keepdims=True)

    def consume(slot, kh, start, first=False):
        m_new = smax_ref[slot * KVH + kh]
        if not first:
            m_old = m_ref[kh]
            m_new = jnp.maximum(m_old, m_new)
        p = jnp.exp2(s_ref[slot * KVH + kh] - m_new).astype(BF16)
        pv = (_dot(vt_ref[kh * NP1 + start], p[0:BLK])
              + _dot(vt_ref[kh * NP1 + start + 1], p[BLK:2 * BLK]))
        acc_ref[kh] = pv if first else jnp.exp2(m_old - m_new) * acc_ref[kh] + pv
        m_ref[kh] = m_new

    n_far = jnp.maximum(i - 1, 0)
    n_pairs = (n_far + 1) // 2

    def far_start(t):
        return jnp.maximum(n_far - 2 * t - 1, 0)

    def far_step(t, slot):
        for kh in range(KVH):
            produce(1 - slot, kh, far_start(t + 1))
            consume(slot, kh, far_start(t))

    for kh in range(KVH):
        produce(1, kh, far_start(0))
        consume(0, kh, i, first=True)

    def far_four(u, carry):
        for d in range(4):
            far_step(4 * u + d, (d + 1) % 2)
        return carry

    n_full = jnp.maximum(n_pairs - 1, 0)
    lax.fori_loop(0, n_full // 4, far_four, 0)
    done = (n_full // 4) * 4

    @pl.when(n_full - done >= 2)
    def _():
        far_step(done, 1)
        far_step(done + 1, 0)

    @pl.when(n_full % 2 == 1)
    def _():
        far_step(n_full - 1, 1)

    def finalize(kh):
        acc = acc_ref[kh]
        o = (acc[0:MOBA_DH] * (1.0 / acc[MOBA_DH:MOBA_DH + 1])).T
        for g in range(G):
            cols = slice((kh * G + g) * MOBA_DH, (kh * G + g + 1) * MOBA_DH)
            gt = gate_ref[:, cols].astype(F32)
            o_ref[:, cols] = (o[g * BLK:(g + 1) * BLK] * _silu(gt)).astype(o_ref.dtype)

    for slot in range(2):
        @pl.when(jnp.logical_and(n_pairs >= 1, n_pairs % 2 == slot))
        def _():
            for kh in range(KVH):
                consume(slot, kh, far_start(n_pairs - 1))
                finalize(kh)

    @pl.when(n_pairs == 0)
    def _():
        for kh in range(KVH):
            finalize(kh)


def _moba(cq, ck, cv, cgate, rel_bias):
    R = MOBA_GROUP * MOBA_BLOCK
    KVH = MOBA_KV_HEADS
    NP1 = MOBA_NBLK + 1
    qspec = pl.BlockSpec((MOBA_BLOCK, C_WIDTH), lambda b, i: (b * MOBA_NBLK + i, 0))
    kvspec = pl.BlockSpec((SEQ, KVH * MOBA_DH), lambda b, i: (b, 0), pipeline_mode=pl.Buffered(1))
    return pl.pallas_call(
        _moba_kernel, grid=(BATCH, MOBA_NBLK),
        in_specs=[pl.BlockSpec(memory_space=pltpu.SMEM), qspec, kvspec, kvspec, qspec],
        out_specs=qspec,
        out_shape=jax.ShapeDtypeStruct((TOKENS, C_WIDTH), BF16),
        scratch_shapes=[pltpu.VMEM((KVH * NP1, MOBA_BLOCK, 2 * LANES), BF16),
                        pltpu.VMEM((KVH * NP1, MOBA_VT_ROWS, MOBA_BLOCK), BF16),
                        pltpu.VMEM((KVH * MOBA_NBLK, MOBA_DH), F32),
                        pltpu.VMEM((2 * MOBA_BLOCK, KVH * R), F32),
                        pltpu.VMEM((KVH, 2 * LANES, R), BF16),
                        pltpu.VMEM((2 * KVH, 2 * MOBA_BLOCK, R), F32),
                        pltpu.VMEM((2 * KVH, 1, R), F32),
                        pltpu.VMEM((KVH, 1, R), F32),
                        pltpu.VMEM((KVH, MOBA_VT_ROWS, R), F32)],
        compiler_params=pltpu.CompilerParams(
            dimension_semantics=("arbitrary", "arbitrary"), vmem_limit_bytes=MOBA_VMEM_LIMIT),
        name="moba",
    )(rel_bias, cq, ck, cv, cgate)


def kernel(x, norm_gain, final_gain, rel_bias, w_in_even, gla_w_up, gla_b_up, gla_norm_gain, swa_sinks,
           w_out_even, w_in_odd, w_out_odd):
    assert x.shape == (BATCH, SEQ, D_MODEL)
    h0 = x.reshape(TOKENS, D_MODEL)

    we = w_in_even[0]
    splits = np.cumsum([0, GLA_QK, GLA_QK, A_WIDTH, GLA_RANK, A_WIDTH, B_WIDTH, SWA_KV_HEADS * SWA_DH,
                        SWA_KV_HEADS * SWA_DH, B_WIDTH])
    w_aq, w_ak, w_av, w_down, w_agate, w_bq = [we[:, splits[k]:splits[k + 1]] for k in range(6)]
    w_bkv = we[:, splits[6]:splits[8]]
    w_bgate = we[:, splits[8]:splits[9]]
    w_down = jnp.pad(jnp.concatenate([w_down] * 3, axis=1), ((0, 0), (0, LANES - 3 * GLA_RANK)))
    proj_ws = [w.astype(BF16) for w in (w_aq, w_ak, w_av, w_down, w_agate, w_bq, w_bkv, w_bgate)]
    proj_dt = [BF16, BF16, BF16, F32, BF16, BF16, BF16, BF16]
    scales = [GLA_DK ** -0.5, 1.0, 1.0, 1.0, 1.0, SWA_DH ** -0.5 * LOG2E, 1.0, 1.0]
    aq, ak, av, adown, agate, bq, bkv, bgate = _layer_io(
        h0, [], norm_gain[0:1], proj_ws, proj_dt, scales, write_h=False)

    w_up_hi = gla_w_up[0].astype(BF16)
    w_up_lo = (gla_w_up[0] - w_up_hi.astype(F32)).astype(BF16)
    w_up = jnp.pad(jnp.concatenate([w_up_hi, w_up_hi, w_up_lo], axis=0), ((0, LANES - 3 * GLA_RANK), (0, 0)))
    oa, ob = _gla_swa(aq, ak, av, agate, adown, w_up, gla_b_up[0:1], gla_norm_gain[0:1],
                      bq, bkv, bgate, rel_bias, swa_sinks[0])

    wo = w_out_even[0].astype(BF16)
    wi = w_in_odd[0]
    osp = np.cumsum([0, C_WIDTH, MOBA_KV_HEADS * MOBA_DH, MOBA_KV_HEADS * MOBA_DH, C_WIDTH])
    odd_ws = [wi[:, osp[k]:osp[k + 1]].astype(BF16) for k in range(4)]
    h1, cq, ck, cv, cgate = _layer_io(
        h0, [(oa, wo[:A_WIDTH]), (ob, wo[A_WIDTH:])], norm_gain[1:2], odd_ws, [BF16] * 4,
        [MOBA_DH ** -0.5 * LOG2E, 1.0, 1.0, 1.0], write_h=True, row_sub=ROW_TILE)

    oc = _moba(cq, ck, cv, cgate, rel_bias)

    (out,) = _layer_io(h1, [(oc, w_out_odd[0].astype(BF16))], final_gain.reshape(1, D_MODEL), [], [], [],
                       write_h=False)
    return out.reshape(BATCH, SEQ, D_MODEL)
```

```python
import functools
import math

import numpy as np
import jax
import jax.numpy as jnp
from jax import lax
from jax.experimental import pallas as pl
from jax.experimental.pallas import tpu as pltpu

D_MODEL = 1024
BATCH = 2
SEQ = 8192
TOKENS = BATCH * SEQ

REL_BUCKETS = 32
REL_MAX_EXACT = 16
REL_MAX_DIST = 128
ATTN_HEADS = 8

GLA_HEADS = 4
GLA_DK = 64
GLA_DV = 128
GLA_RANK = 16
GLA_TAU = 16.0
GLA_CHUNK = 64

SWA_HEADS = 8
SWA_KV_HEADS = 2
SWA_DH = 64
SWA_WINDOW = 128

MOBA_HEADS = 8
MOBA_KV_HEADS = 2
MOBA_DH = 128
MOBA_BLOCK = 256
MOBA_TOPK = 3
MOBA_GROUP = MOBA_HEADS // MOBA_KV_HEADS
MOBA_NBLK = SEQ // MOBA_BLOCK
MOBA_VT_ROWS = MOBA_DH + 16
LOG2E = math.log2(math.e)
SWA_VT_ROWS = SWA_KV_HEADS * SWA_DH + 16

EPS = 1e-6
LANES = 128
NEG_BIG = -1e30
VMEM_LIMIT = 48 * 1024 * 1024
MOBA_VMEM_LIMIT = 56 * 1024 * 1024

A_WIDTH = GLA_HEADS * GLA_DV
B_WIDTH = SWA_HEADS * SWA_DH
C_WIDTH = MOBA_HEADS * MOBA_DH
GLA_QK = GLA_HEADS * GLA_DK

ROW_TILE = 1024
ROW_SUB = 256
GLA_TILE = 1024
SWA_TILE = 1024
SWA_SLOTS = 2

F32 = jnp.float32
BF16 = jnp.bfloat16


def _dot(a, b):
    return jnp.dot(a, b, preferred_element_type=F32)


def _dot_nt(a, b):
    return lax.dot_general(a, b, (((1,), (1,)), ((), ())), preferred_element_type=F32)


def _dot_tn(a, b):
    return lax.dot_general(a, b, (((0,), (0,)), ((), ())), preferred_element_type=F32)


def _silu(x):
    h = 0.5 * x
    return h + h * jnp.tanh(h)


def _t5_thresholds():
    d = np.arange(REL_MAX_DIST + 1)
    nf = np.maximum(d, 1).astype(np.float32)
    large = REL_MAX_EXACT + (np.log(nf / np.float32(REL_MAX_EXACT))
                             / np.float32(math.log(REL_MAX_DIST / REL_MAX_EXACT))
                             * np.float32(REL_BUCKETS - REL_MAX_EXACT)).astype(np.int32)
    bucket = np.where(d < REL_MAX_EXACT, d, np.minimum(large, REL_BUCKETS - 1))
    assert np.all(np.diff(bucket) >= 0) and bucket[-1] == REL_BUCKETS - 1
    return [int(np.argmax(bucket >= b)) for b in range(REL_BUCKETS)]


_T5_THRESH = _t5_thresholds()


def _t5_bias(dist, rb_ref, head):
    out = jnp.full(dist.shape, rb_ref[0, head], F32)
    for b in range(1, REL_BUCKETS):
        out = jnp.where(dist >= _T5_THRESH[b], rb_ref[b, head], out)
    return out


def _layer_io_kernel(*refs, n_acc, n_proj, write_h, scales, row_sub):
    h_ref = refs[0]
    pos = 1
    acc_pairs = []
    for _ in range(n_acc):
        acc_pairs.append((refs[pos], refs[pos + 1]))
        pos += 2
    gain_ref = refs[pos]
    pos += 1
    wp_refs = refs[pos:pos + n_proj]
    pos += n_proj
    out_refs = refs[pos:]

    o = 1 if write_h else 0
    n_sub = ROW_TILE // row_sub
    pending = {}

    def norm_stage(s):
        rows = slice(s * row_sub, (s + 1) * row_sub)
        h = h_ref[rows, :]
        for a_ref, w_ref in acc_pairs:
            h = h + _dot(a_ref[rows, :], w_ref[...])
        if write_h:
            out_refs[0][rows, :] = h
        y = h * lax.rsqrt(jnp.mean(h * h, axis=-1, keepdims=True) + EPS) * gain_ref[...]
        if n_proj == 0:
            out_refs[o][rows, :] = y
        else:
            pending[s] = y.astype(BF16)

    def proj_stage(s):
        rows = slice(s * row_sub, (s + 1) * row_sub)
        yb = pending.pop(s)
        for k in range(n_proj):
            r = _dot(yb, wp_refs[k][...])
            if scales[k] != 1.0:
                r = r * scales[k]
            out_refs[o + k][rows, :] = r.astype(out_refs[o + k].dtype)

    norm_stage(0)
    for s in range(n_sub):
        if s + 1 < n_sub:
            norm_stage(s + 1)
        if n_proj:
            proj_stage(s)


def _layer_io(h, acc_pairs, gain, proj_ws, proj_dtypes, scales, write_h, row_sub=ROW_SUB):
    n_rows = h.shape[0]
    grid = (n_rows // ROW_TILE,)
    row_spec = lambda n: pl.BlockSpec((ROW_TILE, n), lambda i: (i, 0))
    full_spec = lambda a: pl.BlockSpec(a.shape, lambda i: (0,) * a.ndim, pipeline_mode=pl.Buffered(1))

    args = [h]
    in_specs = [row_spec(D_MODEL)]
    for a, w in acc_pairs:
        args += [a, w]
        in_specs += [row_spec(a.shape[1]), full_spec(w)]
    args.append(gain)
    in_specs.append(full_spec(gain))
    for w in proj_ws:
        args.append(w)
        in_specs.append(full_spec(w))

    out_shape, out_specs = [], []
    if write_h:
        out_shape.append(jax.ShapeDtypeStruct((n_rows, D_MODEL), F32))
        out_specs.append(row_spec(D_MODEL))
    if proj_ws:
        for w, dt in zip(proj_ws, proj_dtypes):
            out_shape.append(jax.ShapeDtypeStruct((n_rows, w.shape[1]), dt))
            out_specs.append(row_spec(w.shape[1]))
    else:
        out_shape.append(jax.ShapeDtypeStruct((n_rows, D_MODEL), F32))
        out_specs.append(row_spec(D_MODEL))

    kern = functools.partial(_layer_io_kernel, n_acc=len(acc_pairs), n_proj=len(proj_ws),
                             write_h=write_h, scales=tuple(scales), row_sub=row_sub)
    return pl.pallas_call(
        kern, grid=grid, in_specs=in_specs, out_specs=out_specs, out_shape=out_shape,
        compiler_params=pltpu.CompilerParams(dimension_semantics=("arbitrary",),
                                             vmem_limit_bytes=VMEM_LIMIT),
        name="layer_io",
    )(*args)


def _gla_init(st_ref):
    @pl.when(pl.program_id(1) == 0)
    def _():
        st_ref[...] = jnp.zeros_like(st_ref)


def _gla_steps(q_ref, k_ref, v_ref, gate_ref, down_ref, wup_ref, bup_ref, gain_ref, o_ref, st_ref):
    C = GLA_CHUNK

    a = down_ref[...]
    a_hi = a.astype(BF16)
    a_lo = (a - a_hi.astype(F32)).astype(BF16)
    lane_a = lax.broadcasted_iota(jnp.int32, a.shape, 1) // GLA_RANK
    z = _dot(jnp.where(lane_a == 1, a_lo, a_hi), wup_ref[...]) + bup_ref[...]
    log_a = (jnp.minimum(z, 0.0) - jnp.log(1.0 + jnp.exp(-jnp.abs(z)))) * (1.0 / GLA_TAU)

    r_i = lax.broadcasted_iota(jnp.int32, (C, C), 0)
    c_i = lax.broadcasted_iota(jnp.int32, (C, C), 1)
    tri = (c_i <= r_i).astype(BF16)
    lane_qk = lax.broadcasted_iota(jnp.int32, (C, GLA_QK), 1)
    head_masks = [(lane_qk // GLA_DK) == h for h in range(GLA_HEADS)]
    rs = lax.broadcasted_iota(jnp.int32, (GLA_HEADS * C, C), 0)
    cs = lax.broadcasted_iota(jnp.int32, (GLA_HEADS * C, C), 1)
    causal = (rs % C) >= cs
    st_r = lax.broadcasted_iota(jnp.int32, (A_WIDTH, GLA_QK), 0)
    st_c = lax.broadcasted_iota(jnp.int32, (A_WIDTH, GLA_QK), 1)
    same_head = (st_r // GLA_DV) == (st_c // GLA_DK)
    gain = gain_ref[...]

    pending = {}

    def cumsum(c):
        g = log_a[c * C:(c + 1) * C]
        g_hi = g.astype(BF16)
        g_lo = (g - g_hi.astype(F32)).astype(BF16)
        pending[c] = _dot(tri, g_hi) + _dot(tri, g_lo)

    def prep(c):
        rows = slice(c * C, (c + 1) * C)
        b = pending.pop(c)
        b_last = b[C - 1:C]
        q = q_ref[rows, :].astype(F32)
        k = k_ref[rows, :].astype(F32)
        q_e = q * jnp.exp(b)
        k_e = (k * jnp.exp(-b)).astype(BF16)
        k_l = (k * jnp.exp(b_last - b)).astype(BF16)
        decay = jnp.exp(b_last)
        q_stack = jnp.concatenate([jnp.where(m, q_e, 0.0) for m in head_masks], axis=0).astype(BF16)
        pending[c] = (q_stack, q_e.astype(BF16), k_e, k_l, decay)

    def scores(c):
        q_stack, q_eb, k_e, k_l, decay = pending.pop(c)
        att = jnp.where(causal, _dot_nt(q_stack, k_e), 0.0).astype(BF16)
        pending[c] = (att, q_eb, k_l, decay)

    def main(c):
        rows = slice(c * C, (c + 1) * C)
        att, q_eb, k_l, decay = pending.pop(c)
        v = v_ref[rows, :]
        o_full = _dot(att, v)
        o_intra = jnp.concatenate(
            [o_full[h * C:(h + 1) * C, h * GLA_DV:(h + 1) * GLA_DV] for h in range(GLA_HEADS)], axis=1)
        st = st_ref[...]
        o = o_intra + _dot_nt(q_eb, st.astype(BF16))
        kv_t = _dot_tn(v, k_l)
        st_ref[...] = st * decay + jnp.where(same_head, kv_t, 0.0)

        outs = []
        for h in range(GLA_HEADS):
            oh = o[:, h * GLA_DV:(h + 1) * GLA_DV]
            outs.append(oh * lax.rsqrt(jnp.mean(oh * oh, axis=-1, keepdims=True) + EPS) * gain)
        on = jnp.concatenate(outs, axis=1)
        o_ref[rows, :] = (on * _silu(gate_ref[rows, :].astype(F32))).astype(o_ref.dtype)

    n_chunks = GLA_TILE // C
    steps = [functools.partial(cumsum, 0), functools.partial(cumsum, 1), functools.partial(cumsum, 2),
             functools.partial(prep, 0), functools.partial(prep, 1), functools.partial(scores, 0)]
    for c in range(n_chunks):
        if c + 3 < n_chunks:
            steps.append(functools.partial(cumsum, c + 3))
        if c + 2 < n_chunks:
            steps.append(functools.partial(prep, c + 2))
        if c + 1 < n_chunks:
            steps.append(functools.partial(scores, c + 1))
        steps.append(functools.partial(main, c))
    return steps


def _swa_init(rb_ref, bias_ref):
    W = SWA_WINDOW

    @pl.when(jnp.logical_and(pl.program_id(0) == 0, pl.program_id(1) == 0))
    def _():
        kj = lax.broadcasted_iota(jnp.int32, (2 * W, W), 0)
        qi = lax.broadcasted_iota(jnp.int32, (2 * W, W), 1)
        dist = qi + W - kj
        in_win = jnp.logical_and(dist >= 0, dist < W)
        for h in range(SWA_HEADS):
            bias_ref[:, h * W:(h + 1) * W] = jnp.where(in_win, _t5_bias(dist, rb_ref, h) * LOG2E, -jnp.inf)


def _swa_steps(sink_ref, q_ref, kp_ref, k_ref, vp_ref, v_ref, gate_ref, o_ref, bias_ref, s_ref, smax_ref):
    W = SWA_WINDOW
    H = SWA_HEADS
    DH = SWA_DH
    G = SWA_HEADS // SWA_KV_HEADS
    t = pl.program_id(1)

    kcat = jnp.concatenate([kp_ref[...], k_ref[...]], axis=0)
    vcat = jnp.concatenate([vp_ref[...], v_ref[...]], axis=0)
    vt = vcat.astype(F32).T.astype(BF16)
    ones_rows = jnp.ones((SWA_VT_ROWS - LANES, 2 * W), BF16)
    q_t = q_ref[...].astype(F32).T.astype(BF16)
    zeros = jnp.zeros((DH, W), BF16)
    colh = lax.broadcasted_iota(jnp.int32, (1, H * W), 1) // W
    sink = jnp.full((1, H * W), sink_ref[H - 1], F32)
    for h in range(H - 2, -1, -1):
        sink = jnp.where(colh == h, sink_ref[h], sink)
    sink = sink * LOG2E
    key_prev = lax.broadcasted_iota(jnp.int32, (2 * W, H * W), 0) < W

    def produce(blk):
        tok = slice(blk * W, (blk + 1) * W)
        win = slice(blk * W, blk * W + 2 * W)
        pieces = []
        for h in range(H):
            piece = q_t[h * DH:(h + 1) * DH, tok]
            pieces.append(jnp.concatenate([piece, zeros] if h < G else [zeros, piece], axis=0))
        q_pad = jnp.concatenate(pieces, axis=1)
        s = _dot(kcat[win], q_pad) + bias_ref[...]
        if blk == 0:
            s = jnp.where(jnp.logical_and(t == 0, key_prev), -jnp.inf, s)
        s_ref[blk % SWA_SLOTS] = s
        smax_ref[blk % SWA_SLOTS] = jnp.maximum(jnp.max(s, axis=0, keepdims=True), sink)

    pending = {}

    def attend(blk):
        win = slice(blk * W, blk * W + 2 * W)
        m = smax_ref[blk % SWA_SLOTS]
        p = jnp.exp2(s_ref[blk % SWA_SLOTS] - m).astype(BF16)
        pv = _dot(jnp.concatenate([vt[:, win], ones_rows], axis=0), p)
        pending[blk] = (pv, m)

    def finish(blk):
        tok = slice(blk * W, (blk + 1) * W)
        pv, m = pending.pop(blk)
        inv = 1.0 / (pv[LANES:LANES + 1] + jnp.exp2(sink - m))
        o_t = jnp.concatenate(
            [pv[(h // G) * DH:(h // G + 1) * DH, h * W:(h + 1) * W] * inv[:, h * W:(h + 1) * W]
             for h in range(H)], axis=0)
        gate = gate_ref[tok, :].astype(F32)
        o_ref[tok, :] = (o_t.T * _silu(gate)).astype(o_ref.dtype)

    n_blk = SWA_TILE // W
    ahead = SWA_SLOTS - 1
    steps = [functools.partial(produce, blk) for blk in range(ahead)]
    for blk in range(n_blk):
        if blk + ahead < n_blk:
            steps.append(functools.partial(produce, blk + ahead))
        steps.append(functools.partial(attend, blk))
        if blk >= 1:
            steps.append(functools.partial(finish, blk - 1))
    steps.append(functools.partial(finish, n_blk - 1))
    return steps


def _gla_swa_kernel(aq_ref, ak_ref, av_ref, agate_ref, adown_ref, wup_ref, bup_ref, gain_ref,
                    sink_ref, rb_ref, bq_ref, kp_ref, k_ref, vp_ref, v_ref, bgate_ref,
                    oa_ref, ob_ref, st_ref, bias_ref, s_ref, smax_ref):
    _gla_init(st_ref)
    _swa_init(rb_ref, bias_ref)
    gla = _gla_steps(aq_ref, ak_ref, av_ref, agate_ref, adown_ref, wup_ref, bup_ref, gain_ref, oa_ref, st_ref)
    swa = _swa_steps(sink_ref, bq_ref, kp_ref, k_ref, vp_ref, v_ref, bgate_ref, ob_ref, bias_ref, s_ref, smax_ref)
    for step in gla + swa:
        step()


def _gla_swa(aq, ak, av, agate, adown, w_up, b_up, gain, bq, bkv, bgate, rel_bias, sinks):
    assert GLA_TILE == SWA_TILE
    nt = SEQ // SWA_TILE
    per = SWA_TILE // SWA_WINDOW
    row = lambda n: pl.BlockSpec((SWA_TILE, n), lambda b, t: (b * nt + t, 0))
    full = lambda a: pl.BlockSpec(a.shape, lambda b, t: (0,) * a.ndim)
    own = lambda c: pl.BlockSpec((SWA_TILE, LANES), lambda b, t: (b * nt + t, c))
    prev = lambda c: pl.BlockSpec((SWA_WINDOW, LANES),
                                  lambda b, t: (jnp.maximum((b * nt + t) * per - 1, 0), c))
    smem = pl.BlockSpec(memory_space=pltpu.SMEM)
    return pl.pallas_call(
        _gla_swa_kernel, grid=(BATCH, nt),
        in_specs=[row(GLA_QK), row(GLA_QK), row(A_WIDTH), row(A_WIDTH), row(LANES),
                  full(w_up), full(b_up), full(gain),
                  smem, smem, row(B_WIDTH), prev(0), own(0), prev(1), own(1), row(B_WIDTH)],
        out_specs=[row(A_WIDTH), row(B_WIDTH)],
        out_shape=[jax.ShapeDtypeStruct((TOKENS, A_WIDTH), BF16),
                   jax.ShapeDtypeStruct((TOKENS, B_WIDTH), BF16)],
        scratch_shapes=[pltpu.VMEM((A_WIDTH, GLA_QK), F32),
                        pltpu.VMEM((2 * SWA_WINDOW, SWA_HEADS * SWA_WINDOW), F32),
                        pltpu.VMEM((SWA_SLOTS, 2 * SWA_WINDOW, SWA_HEADS * SWA_WINDOW), F32),
                        pltpu.VMEM((SWA_SLOTS, 1, SWA_HEADS * SWA_WINDOW), F32)],
        compiler_params=pltpu.CompilerParams(dimension_semantics=("arbitrary", "arbitrary"),
                                             vmem_limit_bytes=VMEM_LIMIT),
        name="gla_swa",
    )(aq, ak, av, agate, adown, w_up, b_up, gain, sinks, rel_bias, bq, bkv, bkv, bkv, bkv, bgate)


def _moba_kernel(rb_ref, q_ref, k_ref, v_ref, gate_ref, o_ref,
                 kaug_ref, vt_ref, km_ref, bias_ref, qat_ref, s_ref, smax_ref, m_ref, acc_ref):
    BLK = MOBA_BLOCK
    G = MOBA_GROUP
    KVH = MOBA_KV_HEADS
    R = G * BLK
    NP1 = MOBA_NBLK + 1
    i = pl.program_id(1)

    @pl.when(i == 0)
    def _():
        lane = lax.broadcasted_iota(jnp.int32, (BLK, LANES), 1)
        ones_rows = jnp.ones((MOBA_VT_ROWS - MOBA_DH, BLK), BF16)

        def prep(j, carry):
            rows = pl.ds(pl.multiple_of(j * BLK, BLK), BLK)
            onehot = jnp.where(lane == j, 1.0, 0.0).astype(BF16)
            for kh in range(KVH):
                kb = k_ref[rows, kh * MOBA_DH:(kh + 1) * MOBA_DH]
                kaug_ref[kh * NP1 + j + 1, :, 0:LANES] = kb
                kaug_ref[kh * NP1 + j + 1, :, LANES:2 * LANES] = onehot
                km_ref[pl.ds(kh * MOBA_NBLK + j, 1), :] = jnp.mean(kb.astype(F32), axis=0, keepdims=True)
                vb = v_ref[rows, kh * MOBA_DH:(kh + 1) * MOBA_DH]
                vt_ref[kh * NP1 + j + 1, 0:MOBA_DH, :] = vb.astype(F32).T.astype(BF16)
                vt_ref[kh * NP1 + j + 1, MOBA_DH:, :] = ones_rows
            return carry

        lax.fori_loop(0, MOBA_NBLK, prep, 0)
        for kh in range(KVH):
            kaug_ref[kh * NP1, :, 0:LANES] = jnp.zeros((BLK, LANES), BF16)
            kaug_ref[kh * NP1, :, LANES:2 * LANES] = jnp.where(lane == MOBA_NBLK, 1.0, 0.0).astype(BF16)
            vt_ref[kh * NP1] = jnp.zeros((MOBA_VT_ROWS, BLK), BF16)
            qat_ref[kh, MOBA_DH + MOBA_NBLK:, :] = jnp.full((2 * LANES - MOBA_DH - MOBA_NBLK, R), NEG_BIG, BF16)

    @pl.when(jnp.logical_and(pl.program_id(0) == 0, i == 0))
    def _():
        tk = lax.broadcasted_iota(jnp.int32, (BLK, BLK), 0)
        tq = lax.broadcasted_iota(jnp.int32, (BLK, BLK), 1)
        d_own = tq - tk
        for head in range(MOBA_HEADS):
            cols = slice(head * BLK, (head + 1) * BLK)
            bias_ref[0:BLK, cols] = _t5_bias(d_own + BLK, rb_ref, head) * LOG2E
            bias_ref[BLK:2 * BLK, cols] = jnp.where(d_own >= 0, _t5_bias(d_own, rb_ref, head) * LOG2E, -jnp.inf)

    blk = lax.broadcasted_iota(jnp.int32, (MOBA_NBLK, R), 0)
    blkf = blk.astype(F32)
    past = blk < i
    colh = lax.broadcasted_iota(jnp.int32, (1, R), 1) // BLK
    far_row = REL_BUCKETS - 1

    own_max = []
    for kh in range(KVH):
        q = jnp.concatenate([q_ref[:, (kh * G + g) * MOBA_DH:(kh * G + g + 1) * MOBA_DH] for g in range(G)],
                            axis=0)
        q_t = q.astype(F32).T.astype(BF16)
        qat_ref[kh, 0:MOBA_DH, :] = q_t
        s_own = (_dot(kaug_ref[kh * NP1 + i + 1, :, 0:MOBA_DH], q_t)
                 + bias_ref[BLK:2 * BLK, kh * R:(kh + 1) * R])
        s_ref[kh, BLK:2 * BLK, :] = s_own
        own_max.append(jnp.max(s_own, axis=0, keepdims=True))

        km = km_ref[kh * MOBA_NBLK:(kh + 1) * MOBA_NBLK, :]
        km_hi = km.astype(BF16)
        km_lo = (km - km_hi.astype(F32)).astype(BF16)
        gate = _dot(km_hi, q_t) + _dot(km_lo, q_t)
        g_ = jnp.where(past, gate, -jnp.inf)
        selected = jnp.zeros((MOBA_NBLK, R), dtype=jnp.bool_)
        for _ in range(MOBA_TOPK):
            mx = jnp.max(g_, axis=0, keepdims=True)
            first = jnp.min(jnp.where(g_ == mx, blkf, 1e9), axis=0, keepdims=True)
            pick = jnp.logical_and(blkf == first, past)
            selected = jnp.logical_or(selected, pick)
            g_ = jnp.where(pick, -jnp.inf, g_)
        cfar = jnp.where(colh == 0, rb_ref[far_row, kh * G],
                         jnp.where(colh == 1, rb_ref[far_row, kh * G + 1],
                                   jnp.where(colh == 2, rb_ref[far_row, kh * G + 2],
                                             rb_ref[far_row, kh * G + 3])))
        sel_bias = jnp.where(selected, jnp.where(blk < i - 1, cfar * LOG2E, 0.0), NEG_BIG)
        qat_ref[kh, MOBA_DH:MOBA_DH + MOBA_NBLK, :] = sel_bias.astype(BF16)

    for kh in range(KVH):
        s_near = _dot(kaug_ref[kh * NP1 + i], qat_ref[kh]) + bias_ref[0:BLK, kh * R:(kh + 1) * R]
        s_ref[kh, 0:BLK, :] = s_near
        smax_ref[kh] = jnp.maximum(own_max[kh], jnp.max(s_near, axis=0, keepdims=True))

    def produce(slot, kh, start):
        k2 = kaug_ref[pl.ds(kh * NP1 + start, 2)].reshape(2 * BLK, 2 * LANES)
        s = _dot(k2, qat_ref[kh])
        s_ref[slot * KVH + kh] = s
        smax_ref[slot * KVH + kh] = jnp.max(s, axis=0, keepdims=True)

    def consume(slot, kh, start, first=False):
        m_new = smax_ref[slot * KVH + kh]
        if not first:
            m_old = m_ref[kh]
            m_new = jnp.maximum(m_old, m_new)
        p = jnp.exp2(s_ref[slot * KVH + kh] - m_new).astype(BF16)
        pv = (_dot(vt_ref[kh * NP1 + start], p[0:BLK])
              + _dot(vt_ref[kh * NP1 + start + 1], p[BLK:2 * BLK]))
        acc_ref[kh] = pv if first else jnp.exp2(m_old - m_new) * acc_ref[kh] + pv
        m_ref[kh] = m_new

    n_far = jnp.maximum(i - 1, 0)
    n_pairs = (n_far + 1) // 2

    def far_start(t):
        return jnp.maximum(n_far - 2 * t - 1, 0)

    def far_step(t, slot):
        for kh in range(KVH):
            produce(1 - slot, kh, far_start(t + 1))
            consume(slot, kh, far_start(t))

    for kh in range(KVH):
        produce(1, kh, far_start(0))
        consume(0, kh, i, first=True)

    def far_four(u, carry):
        for d in range(4):
            far_step(4 * u + d, (d + 1) % 2)
        return carry

    n_full = jnp.maximum(n_pairs - 1, 0)
    lax.fori_loop(0, n_full // 4, far_four, 0)
    done = (n_full // 4) * 4

    @pl.when(n_full - done >= 2)
    def _():
        far_step(done, 1)
        far_step(done + 1, 0)

    @pl.when(n_full % 2 == 1)
    def _():
        far_step(n_full - 1, 1)

    def finalize(kh):
        acc = acc_ref[kh]
        o = (acc[0:MOBA_DH] * (1.0 / acc[MOBA_DH:MOBA_DH + 1])).T
        for g in range(G):
            cols = slice((kh * G + g) * MOBA_DH, (kh * G + g + 1) * MOBA_DH)
            gt = gate_ref[:, cols].astype(F32)
            o_ref[:, cols] = (o[g * BLK:(g + 1) * BLK] * _silu(gt)).astype(o_ref.dtype)

    for slot in range(2):
        @pl.when(jnp.logical_and(n_pairs >= 1, n_pairs % 2 == slot))
        def _():
            for kh in range(KVH):
                consume(slot, kh, far_start(n_pairs - 1))
                finalize(kh)

    @pl.when(n_pairs == 0)
    def _():
        for kh in range(KVH):
            finalize(kh)


def _moba(cq, ck, cv, cgate, rel_bias):
    R = MOBA_GROUP * MOBA_BLOCK
    KVH = MOBA_KV_HEADS
    NP1 = MOBA_NBLK + 1
    qspec = pl.BlockSpec((MOBA_BLOCK, C_WIDTH), lambda b, i: (b * MOBA_NBLK + i, 0))
    kvspec = pl.BlockSpec((SEQ, KVH * MOBA_DH), lambda b, i: (b, 0), pipeline_mode=pl.Buffered(1))
    return pl.pallas_call(
        _moba_kernel, grid=(BATCH, MOBA_NBLK),
        in_specs=[pl.BlockSpec(memory_space=pltpu.SMEM), qspec, kvspec, kvspec, qspec],
        out_specs=qspec,
        out_shape=jax.ShapeDtypeStruct((TOKENS, C_WIDTH), BF16),
        scratch_shapes=[pltpu.VMEM((KVH * NP1, MOBA_BLOCK, 2 * LANES), BF16),
                        pltpu.VMEM((KVH * NP1, MOBA_VT_ROWS, MOBA_BLOCK), BF16),
                        pltpu.VMEM((KVH * MOBA_NBLK, MOBA_DH), F32),
                        pltpu.VMEM((2 * MOBA_BLOCK, KVH * R), F32),
                        pltpu.VMEM((KVH, 2 * LANES, R), BF16),
                        pltpu.VMEM((2 * KVH, 2 * MOBA_BLOCK, R), F32),
                        pltpu.VMEM((2 * KVH, 1, R), F32),
                        pltpu.VMEM((KVH, 1, R), F32),
                        pltpu.VMEM((KVH, MOBA_VT_ROWS, R), F32)],
        compiler_params=pltpu.CompilerParams(
            dimension_semantics=("arbitrary", "arbitrary"), vmem_limit_bytes=MOBA_VMEM_LIMIT),
        name="moba",
    )(rel_bias, cq, ck, cv, cgate)


def kernel(x, norm_gain, final_gain, rel_bias, w_in_even, gla_w_up, gla_b_up, gla_norm_gain, swa_sinks,
           w_out_even, w_in_odd, w_out_odd):
    assert x.shape == (BATCH, SEQ, D_MODEL)
    h0 = x.reshape(TOKENS, D_MODEL)

    we = w_in_even[0]
    splits = np.cumsum([0, GLA_QK, GLA_QK, A_WIDTH, GLA_RANK, A_WIDTH, B_WIDTH, SWA_KV_HEADS * SWA_DH,
                        SWA_KV_HEADS * SWA_DH, B_WIDTH])
    w_aq, w_ak, w_av, w_down, w_agate, w_bq = [we[:, splits[k]:splits[k + 1]] for k in range(6)]
    w_bkv = we[:, splits[6]:splits[8]]
    w_bgate = we[:, splits[8]:splits[9]]
    w_down = jnp.pad(jnp.concatenate([w_down] * 3, axis=1), ((0, 0), (0, LANES - 3 * GLA_RANK)))
    proj_ws = [w.astype(BF16) for w in (w_aq, w_ak, w_av, w_down, w_agate, w_bq, w_bkv, w_bgate)]
    proj_dt = [BF16, BF16, BF16, F32, BF16, BF16, BF16, BF16]
    scales = [GLA_DK ** -0.5, 1.0, 1.0, 1.0, 1.0, SWA_DH ** -0.5 * LOG2E, 1.0, 1.0]
    aq, ak, av, adown, agate, bq, bkv, bgate = _layer_io(
        h0, [], norm_gain[0:1], proj_ws, proj_dt, scales, write_h=False)

    w_up_hi = gla_w_up[0].astype(BF16)
    w_up_lo = (gla_w_up[0] - w_up_hi.astype(F32)).astype(BF16)
    w_up = jnp.pad(jnp.concatenate([w_up_hi, w_up_hi, w_up_lo], axis=0), ((0, LANES - 3 * GLA_RANK), (0, 0)))
    oa, ob = _gla_swa(aq, ak, av, agate, adown, w_up, gla_b_up[0:1], gla_norm_gain[0:1],
                      bq, bkv, bgate, rel_bias, swa_sinks[0])

    wo = w_out_even[0].astype(BF16)
    wi = w_in_odd[0]
    osp = np.cumsum([0, C_WIDTH, MOBA_KV_HEADS * MOBA_DH, MOBA_KV_HEADS * MOBA_DH, C_WIDTH])
    odd_ws = [wi[:, osp[k]:osp[k + 1]].astype(BF16) for k in range(4)]
    h1, cq, ck, cv, cgate = _layer_io(
        h0, [(oa, wo[:A_WIDTH]), (ob, wo[A_WIDTH:])], norm_gain[1:2], odd_ws, [BF16] * 4,
        [MOBA_DH ** -0.5 * LOG2E, 1.0, 1.0, 1.0], write_h=True, row_sub=ROW_TILE)

    oc = _moba(cq, ck, cv, cgate, rel_bias)

    (out,) = _layer_io(h1, [(oc, w_out_odd[0].astype(BF16))], final_gain.reshape(1, D_MODEL), [], [], [],
                       write_h=False)
    return out.reshape(BATCH, SEQ, D_MODEL)
```

```python
import functools
import math

import numpy as np
import jax
import jax.numpy as jnp
from jax import lax
from jax.experimental import pallas as pl
from jax.experimental.pallas import tpu as pltpu

D_MODEL = 1024
BATCH = 2
SEQ = 8192
TOKENS = BATCH * SEQ

REL_BUCKETS = 32
REL_MAX_EXACT = 16
REL_MAX_DIST = 128
ATTN_HEADS = 8

GLA_HEADS = 4
GLA_DK = 64
GLA_DV = 128
GLA_RANK = 16
GLA_TAU = 16.0
GLA_CHUNK = 64

SWA_HEADS = 8
SWA_KV_HEADS = 2
SWA_DH = 64
SWA_WINDOW = 128

MOBA_HEADS = 8
MOBA_KV_HEADS = 2
MOBA_DH = 128
MOBA_BLOCK = 256
MOBA_TOPK = 3
MOBA_GROUP = MOBA_HEADS // MOBA_KV_HEADS
MOBA_NBLK = SEQ // MOBA_BLOCK
MOBA_VT_ROWS = MOBA_DH + 16
LOG2E = math.log2(math.e)
SWA_VT_ROWS = SWA_KV_HEADS * SWA_DH + 16

EPS = 1e-6
LANES = 128
NEG_BIG = -1e30
VMEM_LIMIT = 48 * 1024 * 1024
MOBA_VMEM_LIMIT = 56 * 1024 * 1024

A_WIDTH = GLA_HEADS * GLA_DV
B_WIDTH = SWA_HEADS * SWA_DH
C_WIDTH = MOBA_HEADS * MOBA_DH
GLA_QK = GLA_HEADS * GLA_DK

ROW_TILE = 1024
ROW_SUB = 256
GLA_TILE = 1024
SWA_TILE = 1024
SWA_SLOTS = 2

F32 = jnp.float32
BF16 = jnp.bfloat16


def _dot(a, b):
    return jnp.dot(a, b, preferred_element_type=F32)


def _dot_nt(a, b):
    return lax.dot_general(a, b, (((1,), (1,)), ((), ())), preferred_element_type=F32)


def _dot_tn(a, b):
    return lax.dot_general(a, b, (((0,), (0,)), ((), ())), preferred_element_type=F32)


def _silu(x):
    h = 0.5 * x
    return h + h * jnp.tanh(h)


def _t5_thresholds():
    d = np.arange(REL_MAX_DIST + 1)
    nf = np.maximum(d, 1).astype(np.float32)
    large = REL_MAX_EXACT + (np.log(nf / np.float32(REL_MAX_EXACT))
                             / np.float32(math.log(REL_MAX_DIST / REL_MAX_EXACT))
                             * np.float32(REL_BUCKETS - REL_MAX_EXACT)).astype(np.int32)
    bucket = np.where(d < REL_MAX_EXACT, d, np.minimum(large, REL_BUCKETS - 1))
    assert np.all(np.diff(bucket) >= 0) and bucket[-1] == REL_BUCKETS - 1
    return [int(np.argmax(bucket >= b)) for b in range(REL_BUCKETS)]


_T5_THRESH = _t5_thresholds()


def _t5_bias(dist, rb_ref, head):
    out = jnp.full(dist.shape, rb_ref[0, head], F32)
    for b in range(1, REL_BUCKETS):
        out = jnp.where(dist >= _T5_THRESH[b], rb_ref[b, head], out)
    return out


def _layer_io_kernel(*refs, n_acc, n_proj, write_h, scales, row_sub, proj_t):
    h_ref = refs[0]
    pos = 1
    acc_pairs = []
    for _ in range(n_acc):
        acc_pairs.append((refs[pos], refs[pos + 1]))
        pos += 2
    gain_ref = refs[pos]
    pos += 1
    wp_refs = refs[pos:pos + n_proj]
    pos += n_proj
    out_refs = refs[pos:]

    o = 1 if write_h else 0
    n_sub = ROW_TILE // row_sub
    pending = {}

    def norm_stage(s):
        rows = slice(s * row_sub, (s + 1) * row_sub)
        h = h_ref[rows, :]
        for a_ref, w_ref in acc_pairs:
            h = h + _dot(a_ref[rows, :], w_ref[...])
        if write_h:
            out_refs[0][rows, :] = h
        y = h * lax.rsqrt(jnp.mean(h * h, axis=-1, keepdims=True) + EPS) * gain_ref[...]
        if n_proj == 0:
            out_refs[o][rows, :] = y
        else:
            pending[s] = y.astype(BF16)

    def proj_stage(s):
        rows = slice(s * row_sub, (s + 1) * row_sub)
        yb = pending.pop(s)
        for k in range(n_proj):
            r = _dot_nt(yb, wp_refs[k][...]) if proj_t else _dot(yb, wp_refs[k][...])
            if scales[k] != 1.0:
                r = r * scales[k]
            out_refs[o + k][rows, :] = r.astype(out_refs[o + k].dtype)

    norm_stage(0)
    for s in range(n_sub):
        if s + 1 < n_sub:
            norm_stage(s + 1)
        if n_proj:
            proj_stage(s)


def _layer_io(h, acc_pairs, gain, proj_ws, proj_dtypes, scales, write_h, row_sub=ROW_SUB, proj_t=False):
    n_rows = h.shape[0]
    grid = (n_rows // ROW_TILE,)
    row_spec = lambda n: pl.BlockSpec((ROW_TILE, n), lambda i: (i, 0))
    full_spec = lambda a: pl.BlockSpec(a.shape, lambda i: (0,) * a.ndim, pipeline_mode=pl.Buffered(1))

    args = [h]
    in_specs = [row_spec(D_MODEL)]
    for a, w in acc_pairs:
        args += [a, w]
        in_specs += [row_spec(a.shape[1]), full_spec(w)]
    args.append(gain)
    in_specs.append(full_spec(gain))
    for w in proj_ws:
        args.append(w)
        in_specs.append(full_spec(w))

    out_shape, out_specs = [], []
    if write_h:
        out_shape.append(jax.ShapeDtypeStruct((n_rows, D_MODEL), F32))
        out_specs.append(row_spec(D_MODEL))
    if proj_ws:
        for w, dt in zip(proj_ws, proj_dtypes):
            n_out = w.shape[0] if proj_t else w.shape[1]
            out_shape.append(jax.ShapeDtypeStruct((n_rows, n_out), dt))
            out_specs.append(row_spec(n_out))
    else:
        out_shape.append(jax.ShapeDtypeStruct((n_rows, D_MODEL), F32))
        out_specs.append(row_spec(D_MODEL))

    kern = functools.partial(_layer_io_kernel, n_acc=len(acc_pairs), n_proj=len(proj_ws),
                             write_h=write_h, scales=tuple(scales), row_sub=row_sub, proj_t=proj_t)
    return pl.pallas_call(
        kern, grid=grid, in_specs=in_specs, out_specs=out_specs, out_shape=out_shape,
        compiler_params=pltpu.CompilerParams(dimension_semantics=("arbitrary",),
                                             vmem_limit_bytes=VMEM_LIMIT),
        name="layer_io",
    )(*args)


def _gla_init(st_ref):
    @pl.when(pl.program_id(1) == 0)
    def _():
        st_ref[...] = jnp.zeros_like(st_ref)


def _gla_steps(q_ref, k_ref, v_ref, gate_ref, down_ref, wup_ref, bup_ref, gain_ref, o_ref, st_ref):
    C = GLA_CHUNK

    a = down_ref[...]
    a_hi = a.astype(BF16)
    a_lo = (a - a_hi.astype(F32)).astype(BF16)
    lane_a = lax.broadcasted_iota(jnp.int32, a.shape, 1) // GLA_RANK
    z = _dot(jnp.where(lane_a == 1, a_lo, a_hi), wup_ref[...]) + bup_ref[...]
    log_a = (jnp.minimum(z, 0.0) - jnp.log(1.0 + jnp.exp(-jnp.abs(z)))) * (1.0 / GLA_TAU)

    r_i = lax.broadcasted_iota(jnp.int32, (C, C), 0)
    c_i = lax.broadcasted_iota(jnp.int32, (C, C), 1)
    tri = (c_i <= r_i).astype(BF16)
    lane_qk = lax.broadcasted_iota(jnp.int32, (C, GLA_QK), 1)
    head_masks = [(lane_qk // GLA_DK) == h for h in range(GLA_HEADS)]
    rs = lax.broadcasted_iota(jnp.int32, (GLA_HEADS * C, C), 0)
    cs = lax.broadcasted_iota(jnp.int32, (GLA_HEADS * C, C), 1)
    causal = (rs % C) >= cs
    st_r = lax.broadcasted_iota(jnp.int32, (A_WIDTH, GLA_QK), 0)
    st_c = lax.broadcasted_iota(jnp.int32, (A_WIDTH, GLA_QK), 1)
    same_head = (st_r // GLA_DV) == (st_c // GLA_DK)
    gain = gain_ref[...]

    pending = {}

    def cumsum(c):
        g = log_a[c * C:(c + 1) * C]
        g_hi = g.astype(BF16)
        g_lo = (g - g_hi.astype(F32)).astype(BF16)
        pending[c] = _dot(tri, g_hi) + _dot(tri, g_lo)

    def prep(c):
        rows = slice(c * C, (c + 1) * C)
        b = pending.pop(c)
        b_last = b[C - 1:C]
        q = q_ref[rows, :].astype(F32)
        k = k_ref[rows, :].astype(F32)
        q_e = q * jnp.exp(b)
        k_e = (k * jnp.exp(-b)).astype(BF16)
        k_l = (k * jnp.exp(b_last - b)).astype(BF16)
        decay = jnp.exp(b_last)
        q_stack = jnp.concatenate([jnp.where(m, q_e, 0.0) for m in head_masks], axis=0).astype(BF16)
        pending[c] = (q_stack, q_e.astype(BF16), k_e, k_l, decay)

    def scores(c):
        q_stack, q_eb, k_e, k_l, decay = pending.pop(c)
        att = jnp.where(causal, _dot_nt(q_stack, k_e), 0.0).astype(BF16)
        pending[c] = (att, q_eb, k_l, decay)

    def main(c):
        rows = slice(c * C, (c + 1) * C)
        att, q_eb, k_l, decay = pending.pop(c)
        v = v_ref[rows, :]
        o_full = _dot(att, v)
        o_intra = jnp.concatenate(
            [o_full[h * C:(h + 1) * C, h * GLA_DV:(h + 1) * GLA_DV] for h in range(GLA_HEADS)], axis=1)
        st = st_ref[...]
        o = o_intra + _dot_nt(q_eb, st.astype(BF16))
        kv_t = _dot_tn(v, k_l)
        st_ref[...] = st * decay + jnp.where(same_head, kv_t, 0.0)

        outs = []
        for h in range(GLA_HEADS):
            oh = o[:, h * GLA_DV:(h + 1) * GLA_DV]
            outs.append(oh * lax.rsqrt(jnp.mean(oh * oh, axis=-1, keepdims=True) + EPS) * gain)
        on = jnp.concatenate(outs, axis=1)
        o_ref[rows, :] = (on * _silu(gate_ref[rows, :].astype(F32))).astype(o_ref.dtype)

    n_chunks = GLA_TILE // C
    steps = [functools.partial(cumsum, 0), functools.partial(cumsum, 1), functools.partial(cumsum, 2),
             functools.partial(prep, 0), functools.partial(prep, 1), functools.partial(scores, 0)]
    for c in range(n_chunks):
        if c + 3 < n_chunks:
            steps.append(functools.partial(cumsum, c + 3))
        if c + 2 < n_chunks:
            steps.append(functools.partial(prep, c + 2))
        if c + 1 < n_chunks:
            steps.append(functools.partial(scores, c + 1))
        steps.append(functools.partial(main, c))
    return steps


def _swa_init(rb_ref, bias_ref):
    W = SWA_WINDOW

    @pl.when(jnp.logical_and(pl.program_id(0) == 0, pl.program_id(1) == 0))
    def _():
        kj = lax.broadcasted_iota(jnp.int32, (2 * W, W), 0)
        qi = lax.broadcasted_iota(jnp.int32, (2 * W, W), 1)
        dist = qi + W - kj
        in_win = jnp.logical_and(dist >= 0, dist < W)
        for h in range(SWA_HEADS):
            bias_ref[:, h * W:(h + 1) * W] = jnp.where(in_win, _t5_bias(dist, rb_ref, h) * LOG2E, -jnp.inf)


def _swa_steps(sink_ref, q_ref, kp_ref, k_ref, vp_ref, v_ref, gate_ref, o_ref, bias_ref, s_ref, smax_ref):
    W = SWA_WINDOW
    H = SWA_HEADS
    DH = SWA_DH
    G = SWA_HEADS // SWA_KV_HEADS
    t = pl.program_id(1)

    kcat = jnp.concatenate([kp_ref[...], k_ref[...]], axis=0)
    vcat = jnp.concatenate([vp_ref[...], v_ref[...]], axis=0)
    vt = vcat.astype(F32).T.astype(BF16)
    ones_rows = jnp.ones((SWA_VT_ROWS - LANES, 2 * W), BF16)
    q_t = q_ref[...].astype(F32).T.astype(BF16)
    zeros = jnp.zeros((DH, W), BF16)
    colh = lax.broadcasted_iota(jnp.int32, (1, H * W), 1) // W
    sink = jnp.full((1, H * W), sink_ref[H - 1], F32)
    for h in range(H - 2, -1, -1):
        sink = jnp.where(colh == h, sink_ref[h], sink)
    sink = sink * LOG2E
    key_prev = lax.broadcasted_iota(jnp.int32, (2 * W, H * W), 0) < W

    def produce(blk):
        tok = slice(blk * W, (blk + 1) * W)
        win = slice(blk * W, blk * W + 2 * W)
        pieces = []
        for h in range(H):
            piece = q_t[h * DH:(h + 1) * DH, tok]
            pieces.append(jnp.concatenate([piece, zeros] if h < G else [zeros, piece], axis=0))
        q_pad = jnp.concatenate(pieces, axis=1)
        s = _dot(kcat[win], q_pad) + bias_ref[...]
        if blk == 0:
            s = jnp.where(jnp.logical_and(t == 0, key_prev), -jnp.inf, s)
        s_ref[blk % SWA_SLOTS] = s
        smax_ref[blk % SWA_SLOTS] = jnp.maximum(jnp.max(s, axis=0, keepdims=True), sink)

    pending = {}

    def attend(blk):
        win = slice(blk * W, blk * W + 2 * W)
        m = smax_ref[blk % SWA_SLOTS]
        p = jnp.exp2(s_ref[blk % SWA_SLOTS] - m).astype(BF16)
        pv = _dot(jnp.concatenate([vt[:, win], ones_rows], axis=0), p)
        pending[blk] = (pv, m)

    def finish(blk):
        tok = slice(blk * W, (blk + 1) * W)
        pv, m = pending.pop(blk)
        inv = 1.0 / (pv[LANES:LANES + 1] + jnp.exp2(sink - m))
        o_t = jnp.concatenate(
            [pv[(h // G) * DH:(h // G + 1) * DH, h * W:(h + 1) * W] * inv[:, h * W:(h + 1) * W]
             for h in range(H)], axis=0)
        gate = gate_ref[tok, :].astype(F32)
        o_ref[tok, :] = (o_t.T * _silu(gate)).astype(o_ref.dtype)

    n_blk = SWA_TILE // W
    ahead = SWA_SLOTS - 1
    steps = [functools.partial(produce, blk) for blk in range(ahead)]
    for blk in range(n_blk):
        if blk + ahead < n_blk:
            steps.append(functools.partial(produce, blk + ahead))
        steps.append(functools.partial(attend, blk))
        if blk >= 1:
            steps.append(functools.partial(finish, blk - 1))
    steps.append(functools.partial(finish, n_blk - 1))
    return steps


def _gla_swa_kernel(aq_ref, ak_ref, av_ref, agate_ref, adown_ref, wup_ref, bup_ref, gain_ref,
                    sink_ref, rb_ref, bq_ref, kp_ref, k_ref, vp_ref, v_ref, bgate_ref,
                    oa_ref, ob_ref, st_ref, bias_ref, s_ref, smax_ref):
    _gla_init(st_ref)
    _swa_init(rb_ref, bias_ref)
    gla = _gla_steps(aq_ref, ak_ref, av_ref, agate_ref, adown_ref, wup_ref, bup_ref, gain_ref, oa_ref, st_ref)
    swa = _swa_steps(sink_ref, bq_ref, kp_ref, k_ref, vp_ref, v_ref, bgate_ref, ob_ref, bias_ref, s_ref, smax_ref)
    for step in gla + swa:
        step()


def _gla_swa(aq, ak, av, agate, adown, w_up, b_up, gain, bq, bkv, bgate, rel_bias, sinks):
    assert GLA_TILE == SWA_TILE
    nt = SEQ // SWA_TILE
    per = SWA_TILE // SWA_WINDOW
    row = lambda n: pl.BlockSpec((SWA_TILE, n), lambda b, t: (b * nt + t, 0))
    full = lambda a: pl.BlockSpec(a.shape, lambda b, t: (0,) * a.ndim)
    own = lambda c: pl.BlockSpec((SWA_TILE, LANES), lambda b, t: (b * nt + t, c))
    prev = lambda c: pl.BlockSpec((SWA_WINDOW, LANES),
                                  lambda b, t: (jnp.maximum((b * nt + t) * per - 1, 0), c))
    smem = pl.BlockSpec(memory_space=pltpu.SMEM)
    return pl.pallas_call(
        _gla_swa_kernel, grid=(BATCH, nt),
        in_specs=[row(GLA_QK), row(GLA_QK), row(A_WIDTH), row(A_WIDTH), row(LANES),
                  full(w_up), full(b_up), full(gain),
                  smem, smem, row(B_WIDTH), prev(0), own(0), prev(1), own(1), row(B_WIDTH)],
        out_specs=[row(A_WIDTH), row(B_WIDTH)],
        out_shape=[jax.ShapeDtypeStruct((TOKENS, A_WIDTH), BF16),
                   jax.ShapeDtypeStruct((TOKENS, B_WIDTH), BF16)],
        scratch_shapes=[pltpu.VMEM((A_WIDTH, GLA_QK), F32),
                        pltpu.VMEM((2 * SWA_WINDOW, SWA_HEADS * SWA_WINDOW), F32),
                        pltpu.VMEM((SWA_SLOTS, 2 * SWA_WINDOW, SWA_HEADS * SWA_WINDOW), F32),
                        pltpu.VMEM((SWA_SLOTS, 1, SWA_HEADS * SWA_WINDOW), F32)],
        compiler_params=pltpu.CompilerParams(dimension_semantics=("arbitrary", "arbitrary"),
                                             vmem_limit_bytes=VMEM_LIMIT),
        name="gla_swa",
    )(aq, ak, av, agate, adown, w_up, b_up, gain, sinks, rel_bias, bq, bkv, bkv, bkv, bkv, bgate)


def _moba_kernel(rb_ref, q_ref, k_ref, v_ref, gate_ref, o_ref,
                 kaug_ref, vt_ref, km_ref, bias_ref, qat_ref, s_ref, smax_ref, m_ref, acc_ref):
    BLK = MOBA_BLOCK
    G = MOBA_GROUP
    KVH = MOBA_KV_HEADS
    R = G * BLK
    NP1 = MOBA_NBLK + 1
    i = pl.program_id(1)

    @pl.when(i == 0)
    def _():
        lane = lax.broadcasted_iota(jnp.int32, (BLK, LANES), 1)
        ones_rows = jnp.ones((MOBA_VT_ROWS - MOBA_DH, BLK), BF16)

        def prep(j, carry):
            rows = pl.ds(pl.multiple_of(j * BLK, BLK), BLK)
            onehot = jnp.where(lane == j, 1.0, 0.0).astype(BF16)
            for kh in range(KVH):
                kb = k_ref[rows, kh * MOBA_DH:(kh + 1) * MOBA_DH]
                kaug_ref[kh * NP1 + j + 1, :, 0:LANES] = kb
                kaug_ref[kh * NP1 + j + 1, :, LANES:2 * LANES] = onehot
                km_ref[pl.ds(kh * MOBA_NBLK + j, 1), :] = jnp.mean(kb.astype(F32), axis=0, keepdims=True)
                vb = v_ref[rows, kh * MOBA_DH:(kh + 1) * MOBA_DH]
                vt_ref[kh * NP1 + j + 1, 0:MOBA_DH, :] = vb.astype(F32).T.astype(BF16)
                vt_ref[kh * NP1 + j + 1, MOBA_DH:, :] = ones_rows
            return carry

        lax.fori_loop(0, MOBA_NBLK, prep, 0)
        for kh in range(KVH):
            kaug_ref[kh * NP1, :, 0:LANES] = jnp.zeros((BLK, LANES), BF16)
            kaug_ref[kh * NP1, :, LANES:2 * LANES] = jnp.where(lane == MOBA_NBLK, 1.0, 0.0).astype(BF16)
            vt_ref[kh * NP1] = jnp.zeros((MOBA_VT_ROWS, BLK), BF16)
            qat_ref[kh, MOBA_DH + MOBA_NBLK:, :] = jnp.full((2 * LANES - MOBA_DH - MOBA_NBLK, R), NEG_BIG, BF16)

    @pl.when(jnp.logical_and(pl.program_id(0) == 0, i == 0))
    def _():
        tk = lax.broadcasted_iota(jnp.int32, (BLK, BLK), 0)
        tq = lax.broadcasted_iota(jnp.int32, (BLK, BLK), 1)
        d_own = tq - tk
        for head in range(MOBA_HEADS):
            cols = slice(head * BLK, (head + 1) * BLK)
            bias_ref[0:BLK, cols] = _t5_bias(d_own + BLK, rb_ref, head) * LOG2E
            bias_ref[BLK:2 * BLK, cols] = jnp.where(d_own >= 0, _t5_bias(d_own, rb_ref, head) * LOG2E, -jnp.inf)

    blk = lax.broadcasted_iota(jnp.int32, (MOBA_NBLK, R), 0)
    blkf = blk.astype(F32)
    past = blk < i
    colh = lax.broadcasted_iota(jnp.int32, (1, R), 1) // BLK
    far_row = REL_BUCKETS - 1

    own_max = []
    for kh in range(KVH):
        q = jnp.concatenate([q_ref[:, (kh * G + g) * MOBA_DH:(kh * G + g + 1) * MOBA_DH] for g in range(G)],
                            axis=0)
        q_t = q.astype(F32).T.astype(BF16)
        qat_ref[kh, 0:MOBA_DH, :] = q_t
        s_own = (_dot(kaug_ref[kh * NP1 + i + 1, :, 0:MOBA_DH], q_t)
                 + bias_ref[BLK:2 * BLK, kh * R:(kh + 1) * R])
        s_ref[kh, BLK:2 * BLK, :] = s_own
        own_max.append(jnp.max(s_own, axis=0, keepdims=True))

        km = km_ref[kh * MOBA_NBLK:(kh + 1) * MOBA_NBLK, :]
        km_hi = km.astype(BF16)
        km_lo = (km - km_hi.astype(F32)).astype(BF16)
        gate = _dot(km_hi, q_t) + _dot(km_lo, q_t)
        g_ = jnp.where(past, gate, -jnp.inf)
        selected = jnp.zeros((MOBA_NBLK, R), dtype=jnp.bool_)
        for _ in range(MOBA_TOPK):
            mx = jnp.max(g_, axis=0, keepdims=True)
            first = jnp.min(jnp.where(g_ == mx, blkf, 1e9), axis=0, keepdims=True)
            pick = jnp.logical_and(blkf == first, past)
            selected = jnp.logical_or(selected, pick)
            g_ = jnp.where(pick, -jnp.inf, g_)
        cfar = jnp.where(colh == 0, rb_ref[far_row, kh * G],
                         jnp.where(colh == 1, rb_ref[far_row, kh * G + 1],
                                   jnp.where(colh == 2, rb_ref[far_row, kh * G + 2],
                                             rb_ref[far_row, kh * G + 3])))
        sel_bias = jnp.where(selected, jnp.where(blk < i - 1, cfar * LOG2E, 0.0), NEG_BIG)
        qat_ref[kh, MOBA_DH:MOBA_DH + MOBA_NBLK, :] = sel_bias.astype(BF16)

    for kh in range(KVH):
        s_near = _dot(kaug_ref[kh * NP1 + i], qat_ref[kh]) + bias_ref[0:BLK, kh * R:(kh + 1) * R]
        s_ref[kh, 0:BLK, :] = s_near
        smax_ref[kh] = jnp.maximum(own_max[kh], jnp.max(s_near, axis=0, keepdims=True))

    def produce(slot, kh, start):
        k2 = kaug_ref[pl.ds(kh * NP1 + start, 2)].reshape(2 * BLK, 2 * LANES)
        s = _dot(k2, qat_ref[kh])
        s_ref[slot * KVH + kh] = s
        smax_ref[slot * KVH + kh] = jnp.max(s, axis=0, keepdims=True)

    def consume(slot, kh, start, first=False):
        m_new = smax_ref[slot * KVH + kh]
        if not first:
            m_old = m_ref[kh]
            m_new = jnp.maximum(m_old, m_new)
        p = jnp.exp2(s_ref[slot * KVH + kh] - m_new).astype(BF16)
        pv = (_dot(vt_ref[kh * NP1 + start], p[0:BLK])
              + _dot(vt_ref[kh * NP1 + start + 1], p[BLK:2 * BLK]))
        acc_ref[kh] = pv if first else jnp.exp2(m_old - m_new) * acc_ref[kh] + pv
        m_ref[kh] = m_new

    n_far = jnp.maximum(i - 1, 0)
    n_pairs = (n_far + 1) // 2

    def far_start(t):
        return jnp.maximum(n_far - 2 * t - 1, 0)

    def far_step(t, slot):
        for kh in range(KVH):
            produce(1 - slot, kh, far_start(t + 1))
            consume(slot, kh, far_start(t))

    for kh in range(KVH):
        produce(1, kh, far_start(0))
        consume(0, kh, i, first=True)

    def far_four(u, carry):
        for d in range(4):
            far_step(4 * u + d, (d + 1) % 2)
        return carry

    n_full = jnp.maximum(n_pairs - 1, 0)
    lax.fori_loop(0, n_full // 4, far_four, 0)
    done = (n_full // 4) * 4

    @pl.when(n_full - done >= 2)
    def _():
        far_step(done, 1)
        far_step(done + 1, 0)

    @pl.when(n_full % 2 == 1)
    def _():
        far_step(n_full - 1, 1)

    def finalize(kh):
        acc = acc_ref[kh]
        o = (acc[0:MOBA_DH] * (1.0 / acc[MOBA_DH:MOBA_DH + 1])).T
        for g in range(G):
            cols = slice((kh * G + g) * MOBA_DH, (kh * G + g + 1) * MOBA_DH)
            gt = gate_ref[:, cols].astype(F32)
            o_ref[:, cols] = (o[g * BLK:(g + 1) * BLK] * _silu(gt)).astype(o_ref.dtype)

    for slot in range(2):
        @pl.when(jnp.logical_and(n_pairs >= 1, n_pairs % 2 == slot))
        def _():
            for kh in range(KVH):
                consume(slot, kh, far_start(n_pairs - 1))
                finalize(kh)

    @pl.when(n_pairs == 0)
    def _():
        for kh in range(KVH):
            finalize(kh)


def _moba(cq, ck, cv, cgate, rel_bias):
    R = MOBA_GROUP * MOBA_BLOCK
    KVH = MOBA_KV_HEADS
    NP1 = MOBA_NBLK + 1
    qspec = pl.BlockSpec((MOBA_BLOCK, C_WIDTH), lambda b, i: (b * MOBA_NBLK + i, 0))
    kvspec = pl.BlockSpec((SEQ, KVH * MOBA_DH), lambda b, i: (b, 0), pipeline_mode=pl.Buffered(1))
    return pl.pallas_call(
        _moba_kernel, grid=(BATCH, MOBA_NBLK),
        in_specs=[pl.BlockSpec(memory_space=pltpu.SMEM), qspec, kvspec, kvspec, qspec],
        out_specs=qspec,
        out_shape=jax.ShapeDtypeStruct((TOKENS, C_WIDTH), BF16),
        scratch_shapes=[pltpu.VMEM((KVH * NP1, MOBA_BLOCK, 2 * LANES), BF16),
                        pltpu.VMEM((KVH * NP1, MOBA_VT_ROWS, MOBA_BLOCK), BF16),
                        pltpu.VMEM((KVH * MOBA_NBLK, MOBA_DH), F32),
                        pltpu.VMEM((2 * MOBA_BLOCK, KVH * R), F32),
                        pltpu.VMEM((KVH, 2 * LANES, R), BF16),
                        pltpu.VMEM((2 * KVH, 2 * MOBA_BLOCK, R), F32),
                        pltpu.VMEM((2 * KVH, 1, R), F32),
                        pltpu.VMEM((KVH, 1, R), F32),
                        pltpu.VMEM((KVH, MOBA_VT_ROWS, R), F32)],
        compiler_params=pltpu.CompilerParams(
            dimension_semantics=("arbitrary", "arbitrary"), vmem_limit_bytes=MOBA_VMEM_LIMIT),
        name="moba",
    )(rel_bias, cq, ck, cv, cgate)


def kernel(x, norm_gain, final_gain, rel_bias, w_in_even, gla_w_up, gla_b_up, gla_norm_gain, swa_sinks,
           w_out_even, w_in_odd, w_out_odd):
    assert x.shape == (BATCH, SEQ, D_MODEL)
    h0 = x.reshape(TOKENS, D_MODEL)

    we = jnp.swapaxes(w_in_even[0], 0, 1)
    splits = np.cumsum([0, GLA_QK, GLA_QK, A_WIDTH, GLA_RANK, A_WIDTH, B_WIDTH, SWA_KV_HEADS * SWA_DH,
                        SWA_KV_HEADS * SWA_DH, B_WIDTH])
    w_aq, w_ak, w_av, w_down, w_agate, w_bq = [we[splits[k]:splits[k + 1]] for k in range(6)]
    w_bkv = we[splits[6]:splits[8]]
    w_bgate = we[splits[8]:splits[9]]
    w_down = jnp.pad(jnp.concatenate([w_down] * 3, axis=0), ((0, LANES - 3 * GLA_RANK), (0, 0)))
    proj_ws = [w.astype(BF16) for w in (w_aq, w_ak, w_av, w_down, w_agate, w_bq, w_bkv, w_bgate)]
    proj_dt = [BF16, BF16, BF16, F32, BF16, BF16, BF16, BF16]
    scales = [GLA_DK ** -0.5, 1.0, 1.0, 1.0, 1.0, SWA_DH ** -0.5 * LOG2E, 1.0, 1.0]
    aq, ak, av, adown, agate, bq, bkv, bgate = _layer_io(
        h0, [], norm_gain[0:1], proj_ws, proj_dt, scales, write_h=False, proj_t=True)

    w_up_hi = gla_w_up[0].astype(BF16)
    w_up_lo = (gla_w_up[0] - w_up_hi.astype(F32)).astype(BF16)
    w_up = jnp.pad(jnp.concatenate([w_up_hi, w_up_hi, w_up_lo], axis=0), ((0, LANES - 3 * GLA_RANK), (0, 0)))
    oa, ob = _gla_swa(aq, ak, av, agate, adown, w_up, gla_b_up[0:1], gla_norm_gain[0:1],
                      bq, bkv, bgate, rel_bias, swa_sinks[0])

    wo = w_out_even[0].astype(BF16)
    wi = w_in_odd[0]
    osp = np.cumsum([0, C_WIDTH, MOBA_KV_HEADS * MOBA_DH, MOBA_KV_HEADS * MOBA_DH, C_WIDTH])
    odd_ws = [wi[:, osp[k]:osp[k + 1]].astype(BF16) for k in range(4)]
    h1, cq, ck, cv, cgate = _layer_io(
        h0, [(oa, wo[:A_WIDTH]), (ob, wo[A_WIDTH:])], norm_gain[1:2], odd_ws, [BF16] * 4,
        [MOBA_DH ** -0.5 * LOG2E, 1.0, 1.0, 1.0], write_h=True, row_sub=ROW_TILE)

    oc = _moba(cq, ck, cv, cgate, rel_bias)

    (out,) = _layer_io(h1, [(oc, w_out_odd[0].astype(BF16))], final_gain.reshape(1, D_MODEL), [], [], [],
                       write_h=False)
    return out.reshape(BATCH, SEQ, D_MODEL)
```

```python
import functools
import math

import numpy as np
import jax
import jax.numpy as jnp
from jax import lax
from jax.experimental import pallas as pl
from jax.experimental.pallas import tpu as pltpu

D_MODEL = 1024
BATCH = 2
SEQ = 8192
TOKENS = BATCH * SEQ

REL_BUCKETS = 32
REL_MAX_EXACT = 16
REL_MAX_DIST = 128
ATTN_HEADS = 8

GLA_HEADS = 4
GLA_DK = 64
GLA_DV = 128
GLA_RANK = 16
GLA_TAU = 16.0
GLA_CHUNK = 64

SWA_HEADS = 8
SWA_KV_HEADS = 2
SWA_DH = 64
SWA_WINDOW = 128

MOBA_HEADS = 8
MOBA_KV_HEADS = 2
MOBA_DH = 128
MOBA_BLOCK = 256
MOBA_TOPK = 3
MOBA_GROUP = MOBA_HEADS // MOBA_KV_HEADS
MOBA_NBLK = SEQ // MOBA_BLOCK
MOBA_VT_ROWS = MOBA_DH + 16
LOG2E = math.log2(math.e)
SWA_VT_ROWS = SWA_KV_HEADS * SWA_DH + 16

EPS = 1e-6
LANES = 128
NEG_BIG = -1e30
VMEM_LIMIT = 48 * 1024 * 1024
MOBA_VMEM_LIMIT = 56 * 1024 * 1024

A_WIDTH = GLA_HEADS * GLA_DV
B_WIDTH = SWA_HEADS * SWA_DH
C_WIDTH = MOBA_HEADS * MOBA_DH
GLA_QK = GLA_HEADS * GLA_DK

ROW_TILE = 1024
ROW_SUB = 256
GLA_TILE = 1024
SWA_TILE = 1024
SWA_SLOTS = 2

F32 = jnp.float32
BF16 = jnp.bfloat16


def _dot(a, b):
    return jnp.dot(a, b, preferred_element_type=F32)


def _dot_nt(a, b):
    return lax.dot_general(a, b, (((1,), (1,)), ((), ())), preferred_element_type=F32)


def _dot_tn(a, b):
    return lax.dot_general(a, b, (((0,), (0,)), ((), ())), preferred_element_type=F32)


def _silu(x):
    h = 0.5 * x
    return h + h * jnp.tanh(h)


def _t5_thresholds():
    d = np.arange(REL_MAX_DIST + 1)
    nf = np.maximum(d, 1).astype(np.float32)
    large = REL_MAX_EXACT + (np.log(nf / np.float32(REL_MAX_EXACT))
                             / np.float32(math.log(REL_MAX_DIST / REL_MAX_EXACT))
                             * np.float32(REL_BUCKETS - REL_MAX_EXACT)).astype(np.int32)
    bucket = np.where(d < REL_MAX_EXACT, d, np.minimum(large, REL_BUCKETS - 1))
    assert np.all(np.diff(bucket) >= 0) and bucket[-1] == REL_BUCKETS - 1
    return [int(np.argmax(bucket >= b)) for b in range(REL_BUCKETS)]


_T5_THRESH = _t5_thresholds()


def _t5_bias(dist, rb_ref, head):
    out = jnp.full(dist.shape, rb_ref[0, head], F32)
    for b in range(1, REL_BUCKETS):
        out = jnp.where(dist >= _T5_THRESH[b], rb_ref[b, head], out)
    return out


def _layer_io_kernel(*refs, n_acc, n_proj, write_h, scales, row_sub, proj_t):
    h_ref = refs[0]
    pos = 1
    acc_pairs = []
    for _ in range(n_acc):
        acc_pairs.append((refs[pos], refs[pos + 1]))
        pos += 2
    gain_ref = refs[pos]
    pos += 1
    wp_refs = refs[pos:pos + n_proj]
    pos += n_proj
    out_refs = refs[pos:]

    o = 1 if write_h else 0
    n_sub = ROW_TILE // row_sub
    pending = {}

    def norm_stage(s):
        rows = slice(s * row_sub, (s + 1) * row_sub)
        h = h_ref[rows, :]
        for a_ref, w_ref in acc_pairs:
            h = h + _dot(a_ref[rows, :], w_ref[...])
        if write_h:
            out_refs[0][rows, :] = h
        y = h * lax.rsqrt(jnp.mean(h * h, axis=-1, keepdims=True) + EPS) * gain_ref[...]
        if n_proj == 0:
            out_refs[o][rows, :] = y
        else:
            pending[s] = y.astype(BF16)

    def proj_stage(s):
        rows = slice(s * row_sub, (s + 1) * row_sub)
        yb = pending.pop(s)
        for k in range(n_proj):
            r = _dot_nt(yb, wp_refs[k][...]) if proj_t else _dot(yb, wp_refs[k][...])
            if scales[k] != 1.0:
                r = r * scales[k]
            out_refs[o + k][rows, :] = r.astype(out_refs[o + k].dtype)

    norm_stage(0)
    for s in range(n_sub):
        if s + 1 < n_sub:
            norm_stage(s + 1)
        if n_proj:
            proj_stage(s)


def _layer_io(h, acc_pairs, gain, proj_ws, proj_dtypes, scales, write_h, row_sub=ROW_SUB, proj_t=False):
    n_rows = h.shape[0]
    grid = (n_rows // ROW_TILE,)
    row_spec = lambda n: pl.BlockSpec((ROW_TILE, n), lambda i: (i, 0))
    full_spec = lambda a: pl.BlockSpec(a.shape, lambda i: (0,) * a.ndim, pipeline_mode=pl.Buffered(1))

    args = [h]
    in_specs = [row_spec(D_MODEL)]
    for a, w in acc_pairs:
        args += [a, w]
        in_specs += [row_spec(a.shape[1]), full_spec(w)]
    args.append(gain)
    in_specs.append(full_spec(gain))
    for w in proj_ws:
        args.append(w)
        in_specs.append(full_spec(w))

    out_shape, out_specs = [], []
    if write_h:
        out_shape.append(jax.ShapeDtypeStruct((n_rows, D_MODEL), F32))
        out_specs.append(row_spec(D_MODEL))
    if proj_ws:
        for w, dt in zip(proj_ws, proj_dtypes):
            n_out = w.shape[0] if proj_t else w.shape[1]
            out_shape.append(jax.ShapeDtypeStruct((n_rows, n_out), dt))
            out_specs.append(row_spec(n_out))
    else:
        out_shape.append(jax.ShapeDtypeStruct((n_rows, D_MODEL), F32))
        out_specs.append(row_spec(D_MODEL))

    kern = functools.partial(_layer_io_kernel, n_acc=len(acc_pairs), n_proj=len(proj_ws),
                             write_h=write_h, scales=tuple(scales), row_sub=row_sub, proj_t=proj_t)
    return pl.pallas_call(
        kern, grid=grid, in_specs=in_specs, out_specs=out_specs, out_shape=out_shape,
        compiler_params=pltpu.CompilerParams(dimension_semantics=("arbitrary",),
                                             vmem_limit_bytes=VMEM_LIMIT),
        name="layer_io",
    )(*args)


def _gla_init(st_ref):
    @pl.when(pl.program_id(1) == 0)
    def _():
        st_ref[...] = jnp.zeros_like(st_ref)


def _gla_steps(q_ref, k_ref, v_ref, gate_ref, down_ref, wup_ref, bup_ref, gain_ref, o_ref, st_ref):
    C = GLA_CHUNK

    a = down_ref[...]
    a_hi = a.astype(BF16)
    a_lo = (a - a_hi.astype(F32)).astype(BF16)
    lane_a = lax.broadcasted_iota(jnp.int32, a.shape, 1) // GLA_RANK
    z = _dot(jnp.where(lane_a == 1, a_lo, a_hi), wup_ref[...]) + bup_ref[...]
    log_a = (jnp.minimum(z, 0.0) - jnp.log(1.0 + jnp.exp(-jnp.abs(z)))) * (1.0 / GLA_TAU)

    r_i = lax.broadcasted_iota(jnp.int32, (C, C), 0)
    c_i = lax.broadcasted_iota(jnp.int32, (C, C), 1)
    tri = (c_i <= r_i).astype(BF16)
    lane_qk = lax.broadcasted_iota(jnp.int32, (C, GLA_QK), 1)
    head_masks = [(lane_qk // GLA_DK) == h for h in range(GLA_HEADS)]
    rs = lax.broadcasted_iota(jnp.int32, (GLA_HEADS * C, C), 0)
    cs = lax.broadcasted_iota(jnp.int32, (GLA_HEADS * C, C), 1)
    causal = (rs % C) >= cs
    st_r = lax.broadcasted_iota(jnp.int32, (A_WIDTH, GLA_QK), 0)
    st_c = lax.broadcasted_iota(jnp.int32, (A_WIDTH, GLA_QK), 1)
    same_head = (st_r // GLA_DV) == (st_c // GLA_DK)
    gain = gain_ref[...]

    pending = {}

    def cumsum(c):
        g = log_a[c * C:(c + 1) * C]
        g_hi = g.astype(BF16)
        g_lo = (g - g_hi.astype(F32)).astype(BF16)
        pending[c] = _dot(tri, g_hi) + _dot(tri, g_lo)

    def prep(c):
        rows = slice(c * C, (c + 1) * C)
        b = pending.pop(c)
        b_last = b[C - 1:C]
        q = q_ref[rows, :].astype(F32)
        k = k_ref[rows, :].astype(F32)
        q_e = q * jnp.exp(b)
        k_e = (k * jnp.exp(-b)).astype(BF16)
        k_l = (k * jnp.exp(b_last - b)).astype(BF16)
        decay = jnp.exp(b_last)
        q_stack = jnp.concatenate([jnp.where(m, q_e, 0.0) for m in head_masks], axis=0).astype(BF16)
        pending[c] = (q_stack, q_e.astype(BF16), k_e, k_l, decay)

    def scores(c):
        q_stack, q_eb, k_e, k_l, decay = pending.pop(c)
        att = jnp.where(causal, _dot_nt(q_stack, k_e), 0.0).astype(BF16)
        pending[c] = (att, q_eb, k_l, decay)

    def main(c):
        rows = slice(c * C, (c + 1) * C)
        att, q_eb, k_l, decay = pending.pop(c)
        v = v_ref[rows, :]
        o_full = _dot(att, v)
        o_intra = jnp.concatenate(
            [o_full[h * C:(h + 1) * C, h * GLA_DV:(h + 1) * GLA_DV] for h in range(GLA_HEADS)], axis=1)
        st = st_ref[...]
        o = o_intra + _dot_nt(q_eb, st.astype(BF16))
        kv_t = _dot_tn(v, k_l)
        st_ref[...] = st * decay + jnp.where(same_head, kv_t, 0.0)

        outs = []
        for h in range(GLA_HEADS):
            oh = o[:, h * GLA_DV:(h + 1) * GLA_DV]
            outs.append(oh * lax.rsqrt(jnp.mean(oh * oh, axis=-1, keepdims=True) + EPS) * gain)
        on = jnp.concatenate(outs, axis=1)
        o_ref[rows, :] = (on * _silu(gate_ref[rows, :].astype(F32))).astype(o_ref.dtype)

    n_chunks = GLA_TILE // C
    steps = [functools.partial(cumsum, 0), functools.partial(cumsum, 1), functools.partial(cumsum, 2),
             functools.partial(prep, 0), functools.partial(prep, 1), functools.partial(scores, 0)]
    for c in range(n_chunks):
        if c + 3 < n_chunks:
            steps.append(functools.partial(cumsum, c + 3))
        if c + 2 < n_chunks:
            steps.append(functools.partial(prep, c + 2))
        if c + 1 < n_chunks:
            steps.append(functools.partial(scores, c + 1))
        steps.append(functools.partial(main, c))
    return steps


def _swa_init(rb_ref, bias_ref):
    W = SWA_WINDOW

    @pl.when(jnp.logical_and(pl.program_id(0) == 0, pl.program_id(1) == 0))
    def _():
        kj = lax.broadcasted_iota(jnp.int32, (2 * W, W), 0)
        qi = lax.broadcasted_iota(jnp.int32, (2 * W, W), 1)
        dist = qi + W - kj
        in_win = jnp.logical_and(dist >= 0, dist < W)
        for h in range(SWA_HEADS):
            bias_ref[:, h * W:(h + 1) * W] = jnp.where(in_win, _t5_bias(dist, rb_ref, h) * LOG2E, -jnp.inf)


def _swa_steps(sink_ref, q_ref, kp_ref, k_ref, vp_ref, v_ref, gate_ref, o_ref, bias_ref, s_ref, smax_ref):
    W = SWA_WINDOW
    H = SWA_HEADS
    DH = SWA_DH
    G = SWA_HEADS // SWA_KV_HEADS
    t = pl.program_id(1)

    kcat = jnp.concatenate([kp_ref[...], k_ref[...]], axis=0)
    vcat = jnp.concatenate([vp_ref[...], v_ref[...]], axis=0)
    vt = vcat.astype(F32).T.astype(BF16)
    ones_rows = jnp.ones((SWA_VT_ROWS - LANES, 2 * W), BF16)
    q_t = q_ref[...].astype(F32).T.astype(BF16)
    zeros = jnp.zeros((DH, W), BF16)
    colh = lax.broadcasted_iota(jnp.int32, (1, H * W), 1) // W
    sink = jnp.full((1, H * W), sink_ref[H - 1], F32)
    for h in range(H - 2, -1, -1):
        sink = jnp.where(colh == h, sink_ref[h], sink)
    sink = sink * LOG2E
    key_prev = lax.broadcasted_iota(jnp.int32, (2 * W, H * W), 0) < W

    def produce(blk):
        tok = slice(blk * W, (blk + 1) * W)
        win = slice(blk * W, blk * W + 2 * W)
        pieces = []
        for h in range(H):
            piece = q_t[h * DH:(h + 1) * DH, tok]
            pieces.append(jnp.concatenate([piece, zeros] if h < G else [zeros, piece], axis=0))
        q_pad = jnp.concatenate(pieces, axis=1)
        s = _dot(kcat[win], q_pad) + bias_ref[...]
        if blk == 0:
            s = jnp.where(jnp.logical_and(t == 0, key_prev), -jnp.inf, s)
        s_ref[blk % SWA_SLOTS] = s
        smax_ref[blk % SWA_SLOTS] = jnp.maximum(jnp.max(s, axis=0, keepdims=True), sink)

    pending = {}

    def attend(blk):
        win = slice(blk * W, blk * W + 2 * W)
        m = smax_ref[blk % SWA_SLOTS]
        p = jnp.exp2(s_ref[blk % SWA_SLOTS] - m).astype(BF16)
        pv = _dot(jnp.concatenate([vt[:, win], ones_rows], axis=0), p)
        pending[blk] = (pv, m)

    def finish(blk):
        tok = slice(blk * W, (blk + 1) * W)
        pv, m = pending.pop(blk)
        inv = 1.0 / (pv[LANES:LANES + 1] + jnp.exp2(sink - m))
        o_t = jnp.concatenate(
            [pv[(h // G) * DH:(h // G + 1) * DH, h * W:(h + 1) * W] * inv[:, h * W:(h + 1) * W]
             for h in range(H)], axis=0)
        gate = gate_ref[tok, :].astype(F32)
        o_ref[tok, :] = (o_t.T * _silu(gate)).astype(o_ref.dtype)

    n_blk = SWA_TILE // W
    ahead = SWA_SLOTS - 1
    steps = [functools.partial(produce, blk) for blk in range(ahead)]
    for blk in range(n_blk):
        if blk + ahead < n_blk:
            steps.append(functools.partial(produce, blk + ahead))
        steps.append(functools.partial(attend, blk))
        if blk >= 1:
            steps.append(functools.partial(finish, blk - 1))
    steps.append(functools.partial(finish, n_blk - 1))
    return steps


def _gla_swa_kernel(aq_ref, ak_ref, av_ref, agate_ref, adown_ref, wup_ref, bup_ref, gain_ref,
                    sink_ref, rb_ref, bq_ref, kp_ref, k_ref, vp_ref, v_ref, bgate_ref,
                    oa_ref, ob_ref, st_ref, bias_ref, s_ref, smax_ref):
    _gla_init(st_ref)
    _swa_init(rb_ref, bias_ref)
    gla = _gla_steps(aq_ref, ak_ref, av_ref, agate_ref, adown_ref, wup_ref, bup_ref, gain_ref, oa_ref, st_ref)
    swa = _swa_steps(sink_ref, bq_ref, kp_ref, k_ref, vp_ref, v_ref, bgate_ref, ob_ref, bias_ref, s_ref, smax_ref)
    for step in gla + swa:
        step()


def _gla_swa(aq, ak, av, agate, adown, w_up, b_up, gain, bq, bkv, bgate, rel_bias, sinks):
    assert GLA_TILE == SWA_TILE
    nt = SEQ // SWA_TILE
    per = SWA_TILE // SWA_WINDOW
    row = lambda n: pl.BlockSpec((SWA_TILE, n), lambda b, t: (b * nt + t, 0))
    full = lambda a: pl.BlockSpec(a.shape, lambda b, t: (0,) * a.ndim)
    own = lambda c: pl.BlockSpec((SWA_TILE, LANES), lambda b, t: (b * nt + t, c))
    prev = lambda c: pl.BlockSpec((SWA_WINDOW, LANES),
                                  lambda b, t: (jnp.maximum((b * nt + t) * per - 1, 0), c))
    smem = pl.BlockSpec(memory_space=pltpu.SMEM)
    return pl.pallas_call(
        _gla_swa_kernel, grid=(BATCH, nt),
        in_specs=[row(GLA_QK), row(GLA_QK), row(A_WIDTH), row(A_WIDTH), row(LANES),
                  full(w_up), full(b_up), full(gain),
                  smem, smem, row(B_WIDTH), prev(0), own(0), prev(1), own(1), row(B_WIDTH)],
        out_specs=[row(A_WIDTH), row(B_WIDTH)],
        out_shape=[jax.ShapeDtypeStruct((TOKENS, A_WIDTH), BF16),
                   jax.ShapeDtypeStruct((TOKENS, B_WIDTH), BF16)],
        scratch_shapes=[pltpu.VMEM((A_WIDTH, GLA_QK), F32),
                        pltpu.VMEM((2 * SWA_WINDOW, SWA_HEADS * SWA_WINDOW), F32),
                        pltpu.VMEM((SWA_SLOTS, 2 * SWA_WINDOW, SWA_HEADS * SWA_WINDOW), F32),
                        pltpu.VMEM((SWA_SLOTS, 1, SWA_HEADS * SWA_WINDOW), F32)],
        compiler_params=pltpu.CompilerParams(dimension_semantics=("arbitrary", "arbitrary"),
                                             vmem_limit_bytes=VMEM_LIMIT),
        name="gla_swa",
    )(aq, ak, av, agate, adown, w_up, b_up, gain, sinks, rel_bias, bq, bkv, bkv, bkv, bkv, bgate)


def _moba_kernel(rb_ref, q_ref, k_ref, v_ref, gate_ref, o_ref,
                 kaug_ref, vt_ref, km_ref, bias_ref, qat_ref, s_ref, smax_ref, m_ref, acc_ref):
    BLK = MOBA_BLOCK
    G = MOBA_GROUP
    KVH = MOBA_KV_HEADS
    R = G * BLK
    NP1 = MOBA_NBLK + 1
    i = pl.program_id(1)

    @pl.when(i == 0)
    def _():
        lane = lax.broadcasted_iota(jnp.int32, (BLK, LANES), 1)
        ones_rows = jnp.ones((MOBA_VT_ROWS - MOBA_DH, BLK), BF16)

        def prep(j, carry):
            rows = pl.ds(pl.multiple_of(j * BLK, BLK), BLK)
            onehot = jnp.where(lane == j, 1.0, 0.0).astype(BF16)
            for kh in range(KVH):
                kb = k_ref[rows, kh * MOBA_DH:(kh + 1) * MOBA_DH]
                kaug_ref[kh * NP1 + j + 1, :, 0:LANES] = kb
                kaug_ref[kh * NP1 + j + 1, :, LANES:2 * LANES] = onehot
                km_ref[pl.ds(kh * MOBA_NBLK + j, 1), :] = jnp.mean(kb.astype(F32), axis=0, keepdims=True)
                vb = v_ref[rows, kh * MOBA_DH:(kh + 1) * MOBA_DH]
                vt_ref[kh * NP1 + j + 1, 0:MOBA_DH, :] = vb.astype(F32).T.astype(BF16)
                vt_ref[kh * NP1 + j + 1, MOBA_DH:, :] = ones_rows
            return carry

        lax.fori_loop(0, MOBA_NBLK, prep, 0)
        for kh in range(KVH):
            kaug_ref[kh * NP1, :, 0:LANES] = jnp.zeros((BLK, LANES), BF16)
            kaug_ref[kh * NP1, :, LANES:2 * LANES] = jnp.where(lane == MOBA_NBLK, 1.0, 0.0).astype(BF16)
            vt_ref[kh * NP1] = jnp.zeros((MOBA_VT_ROWS, BLK), BF16)
            qat_ref[kh, MOBA_DH + MOBA_NBLK:, :] = jnp.full((2 * LANES - MOBA_DH - MOBA_NBLK, R), NEG_BIG, BF16)

    @pl.when(jnp.logical_and(pl.program_id(0) == 0, i == 0))
    def _():
        tk = lax.broadcasted_iota(jnp.int32, (BLK, BLK), 0)
        tq = lax.broadcasted_iota(jnp.int32, (BLK, BLK), 1)
        d_own = tq - tk
        for head in range(MOBA_HEADS):
            cols = slice(head * BLK, (head + 1) * BLK)
            bias_ref[0:BLK, cols] = _t5_bias(d_own + BLK, rb_ref, head) * LOG2E
            bias_ref[BLK:2 * BLK, cols] = jnp.where(d_own >= 0, _t5_bias(d_own, rb_ref, head) * LOG2E, -jnp.inf)

    blk = lax.broadcasted_iota(jnp.int32, (MOBA_NBLK, R), 0)
    blkf = blk.astype(F32)
    past = blk < i
    colh = lax.broadcasted_iota(jnp.int32, (1, R), 1) // BLK
    far_row = REL_BUCKETS - 1

    own_max = []
    for kh in range(KVH):
        q = jnp.concatenate([q_ref[:, (kh * G + g) * MOBA_DH:(kh * G + g + 1) * MOBA_DH] for g in range(G)],
                            axis=0)
        q_t = q.astype(F32).T.astype(BF16)
        qat_ref[kh, 0:MOBA_DH, :] = q_t
        s_own = (_dot(kaug_ref[kh * NP1 + i + 1, :, 0:MOBA_DH], q_t)
                 + bias_ref[BLK:2 * BLK, kh * R:(kh + 1) * R])
        s_ref[kh, BLK:2 * BLK, :] = s_own
        own_max.append(jnp.max(s_own, axis=0, keepdims=True))

        km = km_ref[kh * MOBA_NBLK:(kh + 1) * MOBA_NBLK, :]
        km_hi = km.astype(BF16)
        km_lo = (km - km_hi.astype(F32)).astype(BF16)
        gate = _dot(km_hi, q_t) + _dot(km_lo, q_t)
        g_ = jnp.where(past, gate, -jnp.inf)
        selected = jnp.zeros((MOBA_NBLK, R), dtype=jnp.bool_)
        for _ in range(MOBA_TOPK):
            mx = jnp.max(g_, axis=0, keepdims=True)
            first = jnp.min(jnp.where(g_ == mx, blkf, 1e9), axis=0, keepdims=True)
            pick = jnp.logical_and(blkf == first, past)
            selected = jnp.logical_or(selected, pick)
            g_ = jnp.where(pick, -jnp.inf, g_)
        cfar = jnp.where(colh == 0, rb_ref[far_row, kh * G],
                         jnp.where(colh == 1, rb_ref[far_row, kh * G + 1],
                                   jnp.where(colh == 2, rb_ref[far_row, kh * G + 2],
                                             rb_ref[far_row, kh * G + 3])))
        sel_bias = jnp.where(selected, jnp.where(blk < i - 1, cfar * LOG2E, 0.0), NEG_BIG)
        qat_ref[kh, MOBA_DH:MOBA_DH + MOBA_NBLK, :] = sel_bias.astype(BF16)

    for kh in range(KVH):
        s_near = _dot(kaug_ref[kh * NP1 + i], qat_ref[kh]) + bias_ref[0:BLK, kh * R:(kh + 1) * R]
        s_ref[kh, 0:BLK, :] = s_near
        smax_ref[kh] = jnp.maximum(own_max[kh], jnp.max(s_near, axis=0, keepdims=True))

    def produce(slot, kh, start, dummy_first=False):
        if dummy_first:
            s = _dot(kaug_ref[kh * NP1 + 1], qat_ref[kh])
            s_ref[slot * KVH + kh, BLK:2 * BLK, :] = s
        else:
            k2 = kaug_ref[pl.ds(kh * NP1 + start, 2)].reshape(2 * BLK, 2 * LANES)
            s = _dot(k2, qat_ref[kh])
            s_ref[slot * KVH + kh] = s
        smax_ref[slot * KVH + kh] = jnp.max(s, axis=0, keepdims=True)

    def consume(slot, kh, start, first=False, dummy_first=False):
        m_new = smax_ref[slot * KVH + kh]
        if not first:
            m_old = m_ref[kh]
            m_new = jnp.maximum(m_old, m_new)
        if dummy_first:
            p = jnp.exp2(s_ref[slot * KVH + kh, BLK:2 * BLK, :] - m_new).astype(BF16)
            pv = _dot(vt_ref[kh * NP1 + 1], p)
        else:
            p = jnp.exp2(s_ref[slot * KVH + kh] - m_new).astype(BF16)
            pv = (_dot(vt_ref[kh * NP1 + start], p[0:BLK])
                  + _dot(vt_ref[kh * NP1 + start + 1], p[BLK:2 * BLK]))
        acc_ref[kh] = pv if first else jnp.exp2(m_old - m_new) * acc_ref[kh] + pv
        m_ref[kh] = m_new

    n_far = jnp.maximum(i - 1, 0)
    n_pairs = (n_far + 1) // 2

    def far_start(t):
        return jnp.maximum(n_far - 2 * t - 1, 0)

    def far_step(t, slot, dummy_next=False):
        for kh in range(KVH):
            produce(1 - slot, kh, far_start(t + 1), dummy_first=dummy_next)
            consume(slot, kh, far_start(t))

    for kh in range(KVH):
        produce(1, kh, far_start(0))
        consume(0, kh, i, first=True)

    def far_four(u, carry):
        for d in range(4):
            far_step(4 * u + d, (d + 1) % 2)
        return carry

    n_full = jnp.maximum(n_pairs - 1, 0)
    dummy_last = jnp.logical_and(n_far % 2 == 1, n_full >= 1)
    n_plain = n_full - dummy_last.astype(jnp.int32)
    lax.fori_loop(0, n_plain // 4, far_four, 0)
    done = (n_plain // 4) * 4

    @pl.when(n_plain - done >= 2)
    def _():
        far_step(done, 1)
        far_step(done + 1, 0)

    @pl.when(n_plain % 2 == 1)
    def _():
        far_step(n_plain - 1, 1)

    for slot in range(2):
        @pl.when(jnp.logical_and(dummy_last, (n_plain + 1) % 2 == slot))
        def _():
            far_step(n_plain, slot, dummy_next=True)

    def finalize(kh):
        acc = acc_ref[kh]
        o = (acc[0:MOBA_DH] * (1.0 / acc[MOBA_DH:MOBA_DH + 1])).T
        for g in range(G):
            cols = slice((kh * G + g) * MOBA_DH, (kh * G + g + 1) * MOBA_DH)
            gt = gate_ref[:, cols].astype(F32)
            o_ref[:, cols] = (o[g * BLK:(g + 1) * BLK] * _silu(gt)).astype(o_ref.dtype)

    for slot in range(2):
        for odd in range(2):
            @pl.when(jnp.logical_and(n_pairs >= 1, jnp.logical_and(n_pairs % 2 == slot, n_far % 2 == odd)))
            def _():
                for kh in range(KVH):
                    consume(slot, kh, far_start(n_pairs - 1), dummy_first=bool(odd))
                    finalize(kh)

    @pl.when(n_pairs == 0)
    def _():
        for kh in range(KVH):
            finalize(kh)


def _moba(cq, ck, cv, cgate, rel_bias):
    R = MOBA_GROUP * MOBA_BLOCK
    KVH = MOBA_KV_HEADS
    NP1 = MOBA_NBLK + 1
    qspec = pl.BlockSpec((MOBA_BLOCK, C_WIDTH), lambda b, i: (b * MOBA_NBLK + i, 0))
    kvspec = pl.BlockSpec((SEQ, KVH * MOBA_DH), lambda b, i: (b, 0), pipeline_mode=pl.Buffered(1))
    return pl.pallas_call(
        _moba_kernel, grid=(BATCH, MOBA_NBLK),
        in_specs=[pl.BlockSpec(memory_space=pltpu.SMEM), qspec, kvspec, kvspec, qspec],
        out_specs=qspec,
        out_shape=jax.ShapeDtypeStruct((TOKENS, C_WIDTH), BF16),
        scratch_shapes=[pltpu.VMEM((KVH * NP1, MOBA_BLOCK, 2 * LANES), BF16),
                        pltpu.VMEM((KVH * NP1, MOBA_VT_ROWS, MOBA_BLOCK), BF16),
                        pltpu.VMEM((KVH * MOBA_NBLK, MOBA_DH), F32),
                        pltpu.VMEM((2 * MOBA_BLOCK, KVH * R), F32),
                        pltpu.VMEM((KVH, 2 * LANES, R), BF16),
                        pltpu.VMEM((2 * KVH, 2 * MOBA_BLOCK, R), F32),
                        pltpu.VMEM((2 * KVH, 1, R), F32),
                        pltpu.VMEM((KVH, 1, R), F32),
                        pltpu.VMEM((KVH, MOBA_VT_ROWS, R), F32)],
        compiler_params=pltpu.CompilerParams(
            dimension_semantics=("arbitrary", "arbitrary"), vmem_limit_bytes=MOBA_VMEM_LIMIT),
        name="moba",
    )(rel_bias, cq, ck, cv, cgate)


def kernel(x, norm_gain, final_gain, rel_bias, w_in_even, gla_w_up, gla_b_up, gla_norm_gain, swa_sinks,
           w_out_even, w_in_odd, w_out_odd):
    assert x.shape == (BATCH, SEQ, D_MODEL)
    h0 = x.reshape(TOKENS, D_MODEL)

    we = jnp.swapaxes(w_in_even[0], 0, 1)
    splits = np.cumsum([0, GLA_QK, GLA_QK, A_WIDTH, GLA_RANK, A_WIDTH, B_WIDTH, SWA_KV_HEADS * SWA_DH,
                        SWA_KV_HEADS * SWA_DH, B_WIDTH])
    w_aq, w_ak, w_av, w_down, w_agate, w_bq = [we[splits[k]:splits[k + 1]] for k in range(6)]
    w_bkv = we[splits[6]:splits[8]]
    w_bgate = we[splits[8]:splits[9]]
    w_down = jnp.pad(jnp.concatenate([w_down] * 3, axis=0), ((0, LANES - 3 * GLA_RANK), (0, 0)))
    proj_ws = [w.astype(BF16) for w in (w_aq, w_ak, w_av, w_down, w_agate, w_bq, w_bkv, w_bgate)]
    proj_dt = [BF16, BF16, BF16, F32, BF16, BF16, BF16, BF16]
    scales = [GLA_DK ** -0.5, 1.0, 1.0, 1.0, 1.0, SWA_DH ** -0.5 * LOG2E, 1.0, 1.0]
    aq, ak, av, adown, agate, bq, bkv, bgate = _layer_io(
        h0, [], norm_gain[0:1], proj_ws, proj_dt, scales, write_h=False, proj_t=True)

    w_up_hi = gla_w_up[0].astype(BF16)
    w_up_lo = (gla_w_up[0] - w_up_hi.astype(F32)).astype(BF16)
    w_up = jnp.pad(jnp.concatenate([w_up_hi, w_up_hi, w_up_lo], axis=0), ((0, LANES - 3 * GLA_RANK), (0, 0)))
    oa, ob = _gla_swa(aq, ak, av, agate, adown, w_up, gla_b_up[0:1], gla_norm_gain[0:1],
                      bq, bkv, bgate, rel_bias, swa_sinks[0])

    wo = w_out_even[0].astype(BF16)
    wi = w_in_odd[0]
    osp = np.cumsum([0, C_WIDTH, MOBA_KV_HEADS * MOBA_DH, MOBA_KV_HEADS * MOBA_DH, C_WIDTH])
    odd_ws = [wi[:, osp[k]:osp[k + 1]].astype(BF16) for k in range(4)]
    h1, cq, ck, cv, cgate = _layer_io(
        h0, [(oa, wo[:A_WIDTH]), (ob, wo[A_WIDTH:])], norm_gain[1:2], odd_ws, [BF16] * 4,
        [MOBA_DH ** -0.5 * LOG2E, 1.0, 1.0, 1.0], write_h=True, row_sub=ROW_TILE)

    oc = _moba(cq, ck, cv, cgate, rel_bias)

    (out,) = _layer_io(h1, [(oc, w_out_odd[0].astype(BF16))], final_gain.reshape(1, D_MODEL), [], [], [],
                       write_h=False)
    return out.reshape(BATCH, SEQ, D_MODEL)
```

```python
import functools
import math

import numpy as np
import jax
import jax.numpy as jnp
from jax import lax
from jax.experimental import pallas as pl
from jax.experimental.pallas import tpu as pltpu

D_MODEL = 1024
BATCH = 2
SEQ = 8192
TOKENS = BATCH * SEQ

REL_BUCKETS = 32
REL_MAX_EXACT = 16
REL_MAX_DIST = 128
ATTN_HEADS = 8

GLA_HEADS = 4
GLA_DK = 64
GLA_DV = 128
GLA_RANK = 16
GLA_TAU = 16.0
GLA_CHUNK = 64

SWA_HEADS = 8
SWA_KV_HEADS = 2
SWA_DH = 64
SWA_WINDOW = 128

MOBA_HEADS = 8
MOBA_KV_HEADS = 2
MOBA_DH = 128
MOBA_BLOCK = 256
MOBA_TOPK = 3
MOBA_GROUP = MOBA_HEADS // MOBA_KV_HEADS
MOBA_NBLK = SEQ // MOBA_BLOCK
MOBA_VT_ROWS = MOBA_DH + 16
LOG2E = math.log2(math.e)
SWA_VT_ROWS = SWA_KV_HEADS * SWA_DH + 16

EPS = 1e-6
LANES = 128
NEG_BIG = -1e30
VMEM_LIMIT = 48 * 1024 * 1024
MOBA_VMEM_LIMIT = 56 * 1024 * 1024

A_WIDTH = GLA_HEADS * GLA_DV
B_WIDTH = SWA_HEADS * SWA_DH
C_WIDTH = MOBA_HEADS * MOBA_DH
GLA_QK = GLA_HEADS * GLA_DK

ROW_TILE = 1024
ROW_SUB = 256
GLA_TILE = 1024
SWA_TILE = 1024
SWA_SLOTS = 2

F32 = jnp.float32
BF16 = jnp.bfloat16


def _dot(a, b):
    return jnp.dot(a, b, preferred_element_type=F32)


def _dot_nt(a, b):
    return lax.dot_general(a, b, (((1,), (1,)), ((), ())), preferred_element_type=F32)


def _dot_tn(a, b):
    return lax.dot_general(a, b, (((0,), (0,)), ((), ())), preferred_element_type=F32)


def _silu(x):
    h = 0.5 * x
    return h + h * jnp.tanh(h)


def _t5_thresholds():
    d = np.arange(REL_MAX_DIST + 1)
    nf = np.maximum(d, 1).astype(np.float32)
    large = REL_MAX_EXACT + (np.log(nf / np.float32(REL_MAX_EXACT))
                             / np.float32(math.log(REL_MAX_DIST / REL_MAX_EXACT))
                             * np.float32(REL_BUCKETS - REL_MAX_EXACT)).astype(np.int32)
    bucket = np.where(d < REL_MAX_EXACT, d, np.minimum(large, REL_BUCKETS - 1))
    assert np.all(np.diff(bucket) >= 0) and bucket[-1] == REL_BUCKETS - 1
    return [int(np.argmax(bucket >= b)) for b in range(REL_BUCKETS)]


_T5_THRESH = _t5_thresholds()


def _t5_bias(dist, rb_ref, head):
    out = jnp.full(dist.shape, rb_ref[0, head], F32)
    for b in range(1, REL_BUCKETS):
        out = jnp.where(dist >= _T5_THRESH[b], rb_ref[b, head], out)
    return out


def _layer_io_kernel(*refs, n_acc, n_proj, write_h, scales, row_sub, proj_t, out_t):
    h_ref = refs[0]
    pos = 1
    acc_pairs = []
    for _ in range(n_acc):
        acc_pairs.append((refs[pos], refs[pos + 1]))
        pos += 2
    gain_ref = refs[pos]
    pos += 1
    wp_refs = refs[pos:pos + n_proj]
    pos += n_proj
    out_refs = refs[pos:]

    o = 1 if write_h else 0
    n_sub = ROW_TILE // row_sub
    pending = {}

    def norm_stage(s):
        rows = slice(s * row_sub, (s + 1) * row_sub)
        h = h_ref[rows, :]
        for a_ref, w_ref in acc_pairs:
            h = h + _dot(a_ref[rows, :], w_ref[...])
        if write_h:
            out_refs[0][rows, :] = h
        y = h * lax.rsqrt(jnp.mean(h * h, axis=-1, keepdims=True) + EPS) * gain_ref[...]
        if n_proj == 0:
            out_refs[o][rows, :] = y
        else:
            pending[s] = y.astype(BF16)

    def proj_stage(s):
        rows = slice(s * row_sub, (s + 1) * row_sub)
        yb = pending.pop(s)
        for k in range(n_proj):
            if out_t[k]:
                r = _dot_nt(wp_refs[k][...], yb)
            else:
                r = _dot_nt(yb, wp_refs[k][...]) if proj_t else _dot(yb, wp_refs[k][...])
            if scales[k] != 1.0:
                r = r * scales[k]
            if out_t[k]:
                out_refs[o + k][:, rows] = r.astype(out_refs[o + k].dtype)
            else:
                out_refs[o + k][rows, :] = r.astype(out_refs[o + k].dtype)

    norm_stage(0)
    for s in range(n_sub):
        if s + 1 < n_sub:
            norm_stage(s + 1)
        if n_proj:
            proj_stage(s)


def _layer_io(h, acc_pairs, gain, proj_ws, proj_dtypes, scales, write_h, row_sub=ROW_SUB, proj_t=False,
              out_t=None):
    n_rows = h.shape[0]
    out_t = tuple(out_t) if out_t else (False,) * len(proj_ws)
    grid = (n_rows // ROW_TILE,)
    row_spec = lambda n: pl.BlockSpec((ROW_TILE, n), lambda i: (i, 0))
    full_spec = lambda a: pl.BlockSpec(a.shape, lambda i: (0,) * a.ndim, pipeline_mode=pl.Buffered(1))

    args = [h]
    in_specs = [row_spec(D_MODEL)]
    for a, w in acc_pairs:
        args += [a, w]
        in_specs += [row_spec(a.shape[1]), full_spec(w)]
    args.append(gain)
    in_specs.append(full_spec(gain))
    for w in proj_ws:
        args.append(w)
        in_specs.append(full_spec(w))

    out_shape, out_specs = [], []
    if write_h:
        out_shape.append(jax.ShapeDtypeStruct((n_rows, D_MODEL), F32))
        out_specs.append(row_spec(D_MODEL))
    if proj_ws:
        for w, dt, t in zip(proj_ws, proj_dtypes, out_t):
            n_out = w.shape[0] if (proj_t or t) else w.shape[1]
            if t:
                out_shape.append(jax.ShapeDtypeStruct((n_out, n_rows), dt))
                out_specs.append(pl.BlockSpec((n_out, ROW_TILE), lambda i: (0, i)))
            else:
                out_shape.append(jax.ShapeDtypeStruct((n_rows, n_out), dt))
                out_specs.append(row_spec(n_out))
    else:
        out_shape.append(jax.ShapeDtypeStruct((n_rows, D_MODEL), F32))
        out_specs.append(row_spec(D_MODEL))

    kern = functools.partial(_layer_io_kernel, n_acc=len(acc_pairs), n_proj=len(proj_ws),
                             write_h=write_h, scales=tuple(scales), row_sub=row_sub, proj_t=proj_t, out_t=out_t)
    return pl.pallas_call(
        kern, grid=grid, in_specs=in_specs, out_specs=out_specs, out_shape=out_shape,
        compiler_params=pltpu.CompilerParams(dimension_semantics=("arbitrary",),
                                             vmem_limit_bytes=VMEM_LIMIT),
        name="layer_io",
    )(*args)


def _gla_init(st_ref):
    @pl.when(pl.program_id(1) == 0)
    def _():
        st_ref[...] = jnp.zeros_like(st_ref)


def _gla_steps(q_ref, k_ref, v_ref, gate_ref, down_ref, wup_ref, bup_ref, gain_ref, o_ref, st_ref):
    C = GLA_CHUNK

    a = down_ref[...]
    a_hi = a.astype(BF16)
    a_lo = (a - a_hi.astype(F32)).astype(BF16)
    lane_a = lax.broadcasted_iota(jnp.int32, a.shape, 1) // GLA_RANK
    z = _dot(jnp.where(lane_a == 1, a_lo, a_hi), wup_ref[...]) + bup_ref[...]
    log_a = (jnp.minimum(z, 0.0) - jnp.log(1.0 + jnp.exp(-jnp.abs(z)))) * (1.0 / GLA_TAU)

    r_i = lax.broadcasted_iota(jnp.int32, (C, C), 0)
    c_i = lax.broadcasted_iota(jnp.int32, (C, C), 1)
    tri = (c_i <= r_i).astype(BF16)
    lane_qk = lax.broadcasted_iota(jnp.int32, (C, GLA_QK), 1)
    head_masks = [(lane_qk // GLA_DK) == h for h in range(GLA_HEADS)]
    rs = lax.broadcasted_iota(jnp.int32, (GLA_HEADS * C, C), 0)
    cs = lax.broadcasted_iota(jnp.int32, (GLA_HEADS * C, C), 1)
    causal = (rs % C) >= cs
    st_r = lax.broadcasted_iota(jnp.int32, (A_WIDTH, GLA_QK), 0)
    st_c = lax.broadcasted_iota(jnp.int32, (A_WIDTH, GLA_QK), 1)
    same_head = (st_r // GLA_DV) == (st_c // GLA_DK)
    gain = gain_ref[...]

    pending = {}

    def cumsum(c):
        g = log_a[c * C:(c + 1) * C]
        g_hi = g.astype(BF16)
        g_lo = (g - g_hi.astype(F32)).astype(BF16)
        pending[c] = _dot(tri, g_hi) + _dot(tri, g_lo)

    def prep(c):
        rows = slice(c * C, (c + 1) * C)
        b = pending.pop(c)
        b_last = b[C - 1:C]
        q = q_ref[rows, :].astype(F32)
        k = k_ref[rows, :].astype(F32)
        q_e = q * jnp.exp(b)
        k_e = (k * jnp.exp(-b)).astype(BF16)
        k_l = (k * jnp.exp(b_last - b)).astype(BF16)
        decay = jnp.exp(b_last)
        q_stack = jnp.concatenate([jnp.where(m, q_e, 0.0) for m in head_masks], axis=0).astype(BF16)
        pending[c] = (q_stack, q_e.astype(BF16), k_e, k_l, decay)

    def scores(c):
        q_stack, q_eb, k_e, k_l, decay = pending.pop(c)
        att = jnp.where(causal, _dot_nt(q_stack, k_e), 0.0).astype(BF16)
        pending[c] = (att, q_eb, k_l, decay)

    def main(c):
        rows = slice(c * C, (c + 1) * C)
        att, q_eb, k_l, decay = pending.pop(c)
        v = v_ref[rows, :]
        o_full = _dot(att, v)
        o_intra = jnp.concatenate(
            [o_full[h * C:(h + 1) * C, h * GLA_DV:(h + 1) * GLA_DV] for h in range(GLA_HEADS)], axis=1)
        st = st_ref[...]
        o = o_intra + _dot_nt(q_eb, st.astype(BF16))
        kv_t = _dot_tn(v, k_l)
        st_ref[...] = st * decay + jnp.where(same_head, kv_t, 0.0)

        outs = []
        for h in range(GLA_HEADS):
            oh = o[:, h * GLA_DV:(h + 1) * GLA_DV]
            outs.append(oh * lax.rsqrt(jnp.mean(oh * oh, axis=-1, keepdims=True) + EPS) * gain)
        on = jnp.concatenate(outs, axis=1)
        o_ref[rows, :] = (on * _silu(gate_ref[rows, :].astype(F32))).astype(o_ref.dtype)

    n_chunks = GLA_TILE // C
    steps = [functools.partial(cumsum, 0), functools.partial(cumsum, 1), functools.partial(cumsum, 2),
             functools.partial(prep, 0), functools.partial(prep, 1), functools.partial(scores, 0)]
    for c in range(n_chunks):
        if c + 3 < n_chunks:
            steps.append(functools.partial(cumsum, c + 3))
        if c + 2 < n_chunks:
            steps.append(functools.partial(prep, c + 2))
        if c + 1 < n_chunks:
            steps.append(functools.partial(scores, c + 1))
        steps.append(functools.partial(main, c))
    return steps


def _swa_init(rb_ref, bias_ref):
    W = SWA_WINDOW

    @pl.when(jnp.logical_and(pl.program_id(0) == 0, pl.program_id(1) == 0))
    def _():
        kj = lax.broadcasted_iota(jnp.int32, (2 * W, W), 0)
        qi = lax.broadcasted_iota(jnp.int32, (2 * W, W), 1)
        dist = qi + W - kj
        in_win = jnp.logical_and(dist >= 0, dist < W)
        for h in range(SWA_HEADS):
            bias_ref[:, h * W:(h + 1) * W] = jnp.where(in_win, _t5_bias(dist, rb_ref, h) * LOG2E, -jnp.inf)


def _swa_steps(sink_ref, q_ref, kp_ref, k_ref, vp_ref, v_ref, gate_ref, o_ref, bias_ref, s_ref, smax_ref):
    W = SWA_WINDOW
    H = SWA_HEADS
    DH = SWA_DH
    G = SWA_HEADS // SWA_KV_HEADS
    t = pl.program_id(1)

    kcat = jnp.concatenate([kp_ref[...], k_ref[...]], axis=0)
    vcat = jnp.concatenate([vp_ref[...], v_ref[...]], axis=0)
    vt = vcat.astype(F32).T.astype(BF16)
    ones_rows = jnp.ones((SWA_VT_ROWS - LANES, 2 * W), BF16)
    q_t = q_ref[...].astype(F32).T.astype(BF16)
    zeros = jnp.zeros((DH, W), BF16)
    colh = lax.broadcasted_iota(jnp.int32, (1, H * W), 1) // W
    sink = jnp.full((1, H * W), sink_ref[H - 1], F32)
    for h in range(H - 2, -1, -1):
        sink = jnp.where(colh == h, sink_ref[h], sink)
    sink = sink * LOG2E
    key_prev = lax.broadcasted_iota(jnp.int32, (2 * W, H * W), 0) < W

    def produce(blk):
        tok = slice(blk * W, (blk + 1) * W)
        win = slice(blk * W, blk * W + 2 * W)
        pieces = []
        for h in range(H):
            piece = q_t[h * DH:(h + 1) * DH, tok]
            pieces.append(jnp.concatenate([piece, zeros] if h < G else [zeros, piece], axis=0))
        q_pad = jnp.concatenate(pieces, axis=1)
        s = _dot(kcat[win], q_pad) + bias_ref[...]
        if blk == 0:
            s = jnp.where(jnp.logical_and(t == 0, key_prev), -jnp.inf, s)
        s_ref[blk % SWA_SLOTS] = s
        smax_ref[blk % SWA_SLOTS] = jnp.maximum(jnp.max(s, axis=0, keepdims=True), sink)

    pending = {}

    def attend(blk):
        win = slice(blk * W, blk * W + 2 * W)
        m = smax_ref[blk % SWA_SLOTS]
        p = jnp.exp2(s_ref[blk % SWA_SLOTS] - m).astype(BF16)
        pv = _dot(jnp.concatenate([vt[:, win], ones_rows], axis=0), p)
        pending[blk] = (pv, m)

    def finish(blk):
        tok = slice(blk * W, (blk + 1) * W)
        pv, m = pending.pop(blk)
        inv = 1.0 / (pv[LANES:LANES + 1] + jnp.exp2(sink - m))
        o_t = jnp.concatenate(
            [pv[(h // G) * DH:(h // G + 1) * DH, h * W:(h + 1) * W] * inv[:, h * W:(h + 1) * W]
             for h in range(H)], axis=0)
        gate = gate_ref[tok, :].astype(F32)
        o_ref[tok, :] = (o_t.T * _silu(gate)).astype(o_ref.dtype)

    n_blk = SWA_TILE // W
    ahead = SWA_SLOTS - 1
    steps = [functools.partial(produce, blk) for blk in range(ahead)]
    for blk in range(n_blk):
        if blk + ahead < n_blk:
            steps.append(functools.partial(produce, blk + ahead))
        steps.append(functools.partial(attend, blk))
        if blk >= 1:
            steps.append(functools.partial(finish, blk - 1))
    steps.append(functools.partial(finish, n_blk - 1))
    return steps


def _gla_swa_kernel(aq_ref, ak_ref, av_ref, agate_ref, adown_ref, wup_ref, bup_ref, gain_ref,
                    sink_ref, rb_ref, bq_ref, kp_ref, k_ref, vp_ref, v_ref, bgate_ref,
                    oa_ref, ob_ref, st_ref, bias_ref, s_ref, smax_ref):
    _gla_init(st_ref)
    _swa_init(rb_ref, bias_ref)
    gla = _gla_steps(aq_ref, ak_ref, av_ref, agate_ref, adown_ref, wup_ref, bup_ref, gain_ref, oa_ref, st_ref)
    swa = _swa_steps(sink_ref, bq_ref, kp_ref, k_ref, vp_ref, v_ref, bgate_ref, ob_ref, bias_ref, s_ref, smax_ref)
    for step in gla + swa:
        step()


def _gla_swa(aq, ak, av, agate, adown, w_up, b_up, gain, bq, bkv, bgate, rel_bias, sinks):
    assert GLA_TILE == SWA_TILE
    nt = SEQ // SWA_TILE
    per = SWA_TILE // SWA_WINDOW
    row = lambda n: pl.BlockSpec((SWA_TILE, n), lambda b, t: (b * nt + t, 0))
    full = lambda a: pl.BlockSpec(a.shape, lambda b, t: (0,) * a.ndim)
    own = lambda c: pl.BlockSpec((SWA_TILE, LANES), lambda b, t: (b * nt + t, c))
    prev = lambda c: pl.BlockSpec((SWA_WINDOW, LANES),
                                  lambda b, t: (jnp.maximum((b * nt + t) * per - 1, 0), c))
    smem = pl.BlockSpec(memory_space=pltpu.SMEM)
    return pl.pallas_call(
        _gla_swa_kernel, grid=(BATCH, nt),
        in_specs=[row(GLA_QK), row(GLA_QK), row(A_WIDTH), row(A_WIDTH), row(LANES),
                  full(w_up), full(b_up), full(gain),
                  smem, smem, row(B_WIDTH), prev(0), own(0), prev(1), own(1), row(B_WIDTH)],
        out_specs=[row(A_WIDTH), row(B_WIDTH)],
        out_shape=[jax.ShapeDtypeStruct((TOKENS, A_WIDTH), BF16),
                   jax.ShapeDtypeStruct((TOKENS, B_WIDTH), BF16)],
        scratch_shapes=[pltpu.VMEM((A_WIDTH, GLA_QK), F32),
                        pltpu.VMEM((2 * SWA_WINDOW, SWA_HEADS * SWA_WINDOW), F32),
                        pltpu.VMEM((SWA_SLOTS, 2 * SWA_WINDOW, SWA_HEADS * SWA_WINDOW), F32),
                        pltpu.VMEM((SWA_SLOTS, 1, SWA_HEADS * SWA_WINDOW), F32)],
        compiler_params=pltpu.CompilerParams(dimension_semantics=("arbitrary", "arbitrary"),
                                             vmem_limit_bytes=VMEM_LIMIT),
        name="gla_swa",
    )(aq, ak, av, agate, adown, w_up, b_up, gain, sinks, rel_bias, bq, bkv, bkv, bkv, bkv, bgate)


def _moba_kernel(rb_ref, qt_ref, k_ref, v_ref, gate_ref, o_ref,
                 kaug_ref, vt_ref, km_ref, bias_ref, qat_ref, s_ref, smax_ref, m_ref, acc_ref):
    BLK = MOBA_BLOCK
    G = MOBA_GROUP
    KVH = MOBA_KV_HEADS
    R = G * BLK
    NP1 = MOBA_NBLK + 1
    i = pl.program_id(1)

    @pl.when(i == 0)
    def _():
        lane = lax.broadcasted_iota(jnp.int32, (BLK, LANES), 1)
        ones_rows = jnp.ones((MOBA_VT_ROWS - MOBA_DH, BLK), BF16)

        def prep(j, carry):
            rows = pl.ds(pl.multiple_of(j * BLK, BLK), BLK)
            onehot = jnp.where(lane == j, 1.0, 0.0).astype(BF16)
            for kh in range(KVH):
                kb = k_ref[rows, kh * MOBA_DH:(kh + 1) * MOBA_DH]
                kaug_ref[kh * NP1 + j + 1, :, 0:LANES] = kb
                kaug_ref[kh * NP1 + j + 1, :, LANES:2 * LANES] = onehot
                km_ref[pl.ds(kh * MOBA_NBLK + j, 1), :] = jnp.mean(kb.astype(F32), axis=0, keepdims=True)
                vb = v_ref[rows, kh * MOBA_DH:(kh + 1) * MOBA_DH]
                vt_ref[kh * NP1 + j + 1, 0:MOBA_DH, :] = vb.astype(F32).T.astype(BF16)
                vt_ref[kh * NP1 + j + 1, MOBA_DH:, :] = ones_rows
            return carry

        lax.fori_loop(0, MOBA_NBLK, prep, 0)
        for kh in range(KVH):
            kaug_ref[kh * NP1, :, 0:LANES] = jnp.zeros((BLK, LANES), BF16)
            kaug_ref[kh * NP1, :, LANES:2 * LANES] = jnp.where(lane == MOBA_NBLK, 1.0, 0.0).astype(BF16)
            vt_ref[kh * NP1] = jnp.zeros((MOBA_VT_ROWS, BLK), BF16)
            qat_ref[kh, MOBA_DH + MOBA_NBLK:, :] = jnp.full((2 * LANES - MOBA_DH - MOBA_NBLK, R), NEG_BIG, BF16)

    @pl.when(jnp.logical_and(pl.program_id(0) == 0, i == 0))
    def _():
        tk = lax.broadcasted_iota(jnp.int32, (BLK, BLK), 0)
        tq = lax.broadcasted_iota(jnp.int32, (BLK, BLK), 1)
        d_own = tq - tk
        for head in range(MOBA_HEADS):
            cols = slice(head * BLK, (head + 1) * BLK)
            bias_ref[0:BLK, cols] = _t5_bias(d_own + BLK, rb_ref, head) * LOG2E
            bias_ref[BLK:2 * BLK, cols] = jnp.where(d_own >= 0, _t5_bias(d_own, rb_ref, head) * LOG2E, -jnp.inf)

    blk = lax.broadcasted_iota(jnp.int32, (MOBA_NBLK, R), 0)
    blkf = blk.astype(F32)
    past = blk < i
    colh = lax.broadcasted_iota(jnp.int32, (1, R), 1) // BLK
    far_row = REL_BUCKETS - 1

    own_max = []
    for kh in range(KVH):
        q_t = jnp.concatenate([qt_ref[(kh * G + g) * MOBA_DH:(kh * G + g + 1) * MOBA_DH, :] for g in range(G)],
                              axis=1)
        qat_ref[kh, 0:MOBA_DH, :] = q_t
        s_own = (_dot(kaug_ref[kh * NP1 + i + 1, :, 0:MOBA_DH], q_t)
                 + bias_ref[BLK:2 * BLK, kh * R:(kh + 1) * R])
        s_ref[kh, BLK:2 * BLK, :] = s_own
        own_max.append(jnp.max(s_own, axis=0, keepdims=True))

        km = km_ref[kh * MOBA_NBLK:(kh + 1) * MOBA_NBLK, :]
        km_hi = km.astype(BF16)
        km_lo = (km - km_hi.astype(F32)).astype(BF16)
        gate = _dot(km_hi, q_t) + _dot(km_lo, q_t)
        g_ = jnp.where(past, gate, -jnp.inf)
        selected = jnp.zeros((MOBA_NBLK, R), dtype=jnp.bool_)
        for _ in range(MOBA_TOPK):
            mx = jnp.max(g_, axis=0, keepdims=True)
            first = jnp.min(jnp.where(g_ == mx, blkf, 1e9), axis=0, keepdims=True)
            pick = jnp.logical_and(blkf == first, past)
            selected = jnp.logical_or(selected, pick)
            g_ = jnp.where(pick, -jnp.inf, g_)
        cfar = jnp.where(colh == 0, rb_ref[far_row, kh * G],
                         jnp.where(colh == 1, rb_ref[far_row, kh * G + 1],
                                   jnp.where(colh == 2, rb_ref[far_row, kh * G + 2],
                                             rb_ref[far_row, kh * G + 3])))
        sel_bias = jnp.where(selected, jnp.where(blk < i - 1, cfar * LOG2E, 0.0), NEG_BIG)
        qat_ref[kh, MOBA_DH:MOBA_DH + MOBA_NBLK, :] = sel_bias.astype(BF16)

    for kh in range(KVH):
        s_near = _dot(kaug_ref[kh * NP1 + i], qat_ref[kh]) + bias_ref[0:BLK, kh * R:(kh + 1) * R]
        s_ref[kh, 0:BLK, :] = s_near
        smax_ref[kh] = jnp.maximum(own_max[kh], jnp.max(s_near, axis=0, keepdims=True))

    def produce(slot, kh, start):
        k2 = kaug_ref[pl.ds(kh * NP1 + start, 2)].reshape(2 * BLK, 2 * LANES)
        s = _dot(k2, qat_ref[kh])
        s_ref[slot * KVH + kh] = s
        smax_ref[slot * KVH + kh] = jnp.max(s, axis=0, keepdims=True)

    def consume(slot, kh, start, first=False, dummy_first=False):
        m_new = smax_ref[slot * KVH + kh]
        if not first:
            m_old = m_ref[kh]
            m_new = jnp.maximum(m_old, m_new)
        if dummy_first:
            p = jnp.exp2(s_ref[slot * KVH + kh, BLK:2 * BLK, :] - m_new).astype(BF16)
            pv = _dot(vt_ref[kh * NP1 + 1], p)
        else:
            p = jnp.exp2(s_ref[slot * KVH + kh] - m_new).astype(BF16)
            pv = (_dot(vt_ref[kh * NP1 + start], p[0:BLK])
                  + _dot(vt_ref[kh * NP1 + start + 1], p[BLK:2 * BLK]))
        acc_ref[kh] = pv if first else jnp.exp2(m_old - m_new) * acc_ref[kh] + pv
        m_ref[kh] = m_new

    n_far = jnp.maximum(i - 1, 0)
    n_pairs = (n_far + 1) // 2

    def far_start(t):
        return jnp.maximum(n_far - 2 * t - 1, 0)

    def far_step(t, slot):
        for kh in range(KVH):
            produce(1 - slot, kh, far_start(t + 1))
            consume(slot, kh, far_start(t))

    for kh in range(KVH):
        produce(1, kh, far_start(0))
        consume(0, kh, i, first=True)

    def far_four(u, carry):
        for d in range(4):
            far_step(4 * u + d, (d + 1) % 2)
        return carry

    n_full = jnp.maximum(n_pairs - 1, 0)
    lax.fori_loop(0, n_full // 4, far_four, 0)
    done = (n_full // 4) * 4

    @pl.when(n_full - done >= 2)
    def _():
        far_step(done, 1)
        far_step(done + 1, 0)

    @pl.when(n_full % 2 == 1)
    def _():
        far_step(n_full - 1, 1)

    def finalize(kh):
        acc = acc_ref[kh]
        o = (acc[0:MOBA_DH] * (1.0 / acc[MOBA_DH:MOBA_DH + 1])).T
        for g in range(G):
            cols = slice((kh * G + g) * MOBA_DH, (kh * G + g + 1) * MOBA_DH)
            gt = gate_ref[:, cols].astype(F32)
            o_ref[:, cols] = (o[g * BLK:(g + 1) * BLK] * _silu(gt)).astype(o_ref.dtype)

    for slot in range(2):
        for odd in range(2):
            @pl.when(jnp.logical_and(n_pairs >= 1, jnp.logical_and(n_pairs % 2 == slot, n_far % 2 == odd)))
            def _():
                for kh in range(KVH):
                    consume(slot, kh, far_start(n_pairs - 1), dummy_first=bool(odd))
                    finalize(kh)

    @pl.when(n_pairs == 0)
    def _():
        for kh in range(KVH):
            finalize(kh)


def _moba(cqt, ck, cv, cgate, rel_bias):
    R = MOBA_GROUP * MOBA_BLOCK
    KVH = MOBA_KV_HEADS
    NP1 = MOBA_NBLK + 1
    qspec = pl.BlockSpec((MOBA_BLOCK, C_WIDTH), lambda b, i: (b * MOBA_NBLK + i, 0))
    qtspec = pl.BlockSpec((C_WIDTH, MOBA_BLOCK), lambda b, i: (0, b * MOBA_NBLK + i))
    kvspec = pl.BlockSpec((SEQ, KVH * MOBA_DH), lambda b, i: (b, 0), pipeline_mode=pl.Buffered(1))
    return pl.pallas_call(
        _moba_kernel, grid=(BATCH, MOBA_NBLK),
        in_specs=[pl.BlockSpec(memory_space=pltpu.SMEM), qtspec, kvspec, kvspec, qspec],
        out_specs=qspec,
        out_shape=jax.ShapeDtypeStruct((TOKENS, C_WIDTH), BF16),
        scratch_shapes=[pltpu.VMEM((KVH * NP1, MOBA_BLOCK, 2 * LANES), BF16),
                        pltpu.VMEM((KVH * NP1, MOBA_VT_ROWS, MOBA_BLOCK), BF16),
                        pltpu.VMEM((KVH * MOBA_NBLK, MOBA_DH), F32),
                        pltpu.VMEM((2 * MOBA_BLOCK, KVH * R), F32),
                        pltpu.VMEM((KVH, 2 * LANES, R), BF16),
                        pltpu.VMEM((2 * KVH, 2 * MOBA_BLOCK, R), F32),
                        pltpu.VMEM((2 * KVH, 1, R), F32),
                        pltpu.VMEM((KVH, 1, R), F32),
                        pltpu.VMEM((KVH, MOBA_VT_ROWS, R), F32)],
        compiler_params=pltpu.CompilerParams(
            dimension_semantics=("arbitrary", "arbitrary"), vmem_limit_bytes=MOBA_VMEM_LIMIT),
        name="moba",
    )(rel_bias, cqt, ck, cv, cgate)


def kernel(x, norm_gain, final_gain, rel_bias, w_in_even, gla_w_up, gla_b_up, gla_norm_gain, swa_sinks,
           w_out_even, w_in_odd, w_out_odd):
    assert x.shape == (BATCH, SEQ, D_MODEL)
    h0 = x.reshape(TOKENS, D_MODEL)

    we = jnp.swapaxes(w_in_even[0], 0, 1)
    splits = np.cumsum([0, GLA_QK, GLA_QK, A_WIDTH, GLA_RANK, A_WIDTH, B_WIDTH, SWA_KV_HEADS * SWA_DH,
                        SWA_KV_HEADS * SWA_DH, B_WIDTH])
    w_aq, w_ak, w_av, w_down, w_agate, w_bq = [we[splits[k]:splits[k + 1]] for k in range(6)]
    w_bkv = we[splits[6]:splits[8]]
    w_bgate = we[splits[8]:splits[9]]
    w_down = jnp.pad(jnp.concatenate([w_down] * 3, axis=0), ((0, LANES - 3 * GLA_RANK), (0, 0)))
    proj_ws = [w.astype(BF16) for w in (w_aq, w_ak, w_av, w_down, w_agate, w_bq, w_bkv, w_bgate)]
    proj_dt = [BF16, BF16, BF16, F32, BF16, BF16, BF16, BF16]
    scales = [GLA_DK ** -0.5, 1.0, 1.0, 1.0, 1.0, SWA_DH ** -0.5 * LOG2E, 1.0, 1.0]
    aq, ak, av, adown, agate, bq, bkv, bgate = _layer_io(
        h0, [], norm_gain[0:1], proj_ws, proj_dt, scales, write_h=False, proj_t=True)

    w_up_hi = gla_w_up[0].astype(BF16)
    w_up_lo = (gla_w_up[0] - w_up_hi.astype(F32)).astype(BF16)
    w_up = jnp.pad(jnp.concatenate([w_up_hi, w_up_hi, w_up_lo], axis=0), ((0, LANES - 3 * GLA_RANK), (0, 0)))
    oa, ob = _gla_swa(aq, ak, av, agate, adown, w_up, gla_b_up[0:1], gla_norm_gain[0:1],
                      bq, bkv, bgate, rel_bias, swa_sinks[0])

    wo = w_out_even[0].astype(BF16)
    wi = w_in_odd[0]
    osp = np.cumsum([0, C_WIDTH, MOBA_KV_HEADS * MOBA_DH, MOBA_KV_HEADS * MOBA_DH, C_WIDTH])
    odd_ws = [wi[:, osp[k]:osp[k + 1]].astype(BF16) for k in range(4)]
    odd_ws[0] = jnp.swapaxes(odd_ws[0], 0, 1)
    h1, cqt, ck, cv, cgate = _layer_io(
        h0, [(oa, wo[:A_WIDTH]), (ob, wo[A_WIDTH:])], norm_gain[1:2], odd_ws, [BF16] * 4,
        [MOBA_DH ** -0.5 * LOG2E, 1.0, 1.0, 1.0], write_h=True, row_sub=ROW_TILE,
        out_t=[True, False, False, False])

    oc = _moba(cqt, ck, cv, cgate, rel_bias)

    (out,) = _layer_io(h1, [(oc, w_out_odd[0].astype(BF16))], final_gain.reshape(1, D_MODEL), [], [], [],
                       write_h=False)
    return out.reshape(BATCH, SEQ, D_MODEL)
```

```python
import functools
import math

import numpy as np
import jax
import jax.numpy as jnp
from jax import lax
from jax.experimental import pallas as pl
from jax.experimental.pallas import tpu as pltpu

D_MODEL = 1024
BATCH = 2
SEQ = 8192
TOKENS = BATCH * SEQ

REL_BUCKETS = 32
REL_MAX_EXACT = 16
REL_MAX_DIST = 128
ATTN_HEADS = 8

GLA_HEADS = 4
GLA_DK = 64
GLA_DV = 128
GLA_RANK = 16
GLA_TAU = 16.0
GLA_CHUNK = 64

SWA_HEADS = 8
SWA_KV_HEADS = 2
SWA_DH = 64
SWA_WINDOW = 128

MOBA_HEADS = 8
MOBA_KV_HEADS = 2
MOBA_DH = 128
MOBA_BLOCK = 256
MOBA_TOPK = 3
MOBA_GROUP = MOBA_HEADS // MOBA_KV_HEADS
MOBA_NBLK = SEQ // MOBA_BLOCK
MOBA_VT_ROWS = MOBA_DH + 16
LOG2E = math.log2(math.e)
SWA_VT_ROWS = SWA_KV_HEADS * SWA_DH + 16

EPS = 1e-6
LANES = 128
NEG_BIG = -1e30
VMEM_LIMIT = 48 * 1024 * 1024
MOBA_VMEM_LIMIT = 56 * 1024 * 1024

A_WIDTH = GLA_HEADS * GLA_DV
B_WIDTH = SWA_HEADS * SWA_DH
C_WIDTH = MOBA_HEADS * MOBA_DH
GLA_QK = GLA_HEADS * GLA_DK

ROW_TILE = 1024
ROW_SUB = 256
GLA_TILE = 1024
SWA_TILE = 1024
SWA_SLOTS = 2

F32 = jnp.float32
BF16 = jnp.bfloat16


def _dot(a, b):
    return jnp.dot(a, b, preferred_element_type=F32)


def _dot_nt(a, b):
    return lax.dot_general(a, b, (((1,), (1,)), ((), ())), preferred_element_type=F32)


def _dot_tn(a, b):
    return lax.dot_general(a, b, (((0,), (0,)), ((), ())), preferred_element_type=F32)


def _silu(x):
    h = 0.5 * x
    return h + h * jnp.tanh(h)


def _t5_thresholds():
    d = np.arange(REL_MAX_DIST + 1)
    nf = np.maximum(d, 1).astype(np.float32)
    large = REL_MAX_EXACT + (np.log(nf / np.float32(REL_MAX_EXACT))
                             / np.float32(math.log(REL_MAX_DIST / REL_MAX_EXACT))
                             * np.float32(REL_BUCKETS - REL_MAX_EXACT)).astype(np.int32)
    bucket = np.where(d < REL_MAX_EXACT, d, np.minimum(large, REL_BUCKETS - 1))
    assert np.all(np.diff(bucket) >= 0) and bucket[-1] == REL_BUCKETS - 1
    return [int(np.argmax(bucket >= b)) for b in range(REL_BUCKETS)]


_T5_THRESH = _t5_thresholds()


def _t5_bias(dist, rb_ref, head):
    out = jnp.full(dist.shape, rb_ref[0, head], F32)
    for b in range(1, REL_BUCKETS):
        out = jnp.where(dist >= _T5_THRESH[b], rb_ref[b, head], out)
    return out


def _layer_io_kernel(*refs, n_acc, n_proj, write_h, scales, row_sub, proj_t, out_t):
    h_ref = refs[0]
    pos = 1
    acc_pairs = []
    for _ in range(n_acc):
        acc_pairs.append((refs[pos], refs[pos + 1]))
        pos += 2
    gain_ref = refs[pos]
    pos += 1
    wp_refs = refs[pos:pos + n_proj]
    pos += n_proj
    out_refs = refs[pos:]

    o = 1 if write_h else 0
    n_sub = ROW_TILE // row_sub
    pending = {}

    def norm_stage(s):
        rows = slice(s * row_sub, (s + 1) * row_sub)
        h = h_ref[rows, :]
        for a_ref, w_ref in acc_pairs:
            h = h + _dot(a_ref[rows, :], w_ref[...])
        if write_h:
            out_refs[0][rows, :] = h
        y = h * lax.rsqrt(jnp.mean(h * h, axis=-1, keepdims=True) + EPS) * gain_ref[...]
        if n_proj == 0:
            out_refs[o][rows, :] = y
        else:
            pending[s] = y.astype(BF16)

    def proj_stage(s):
        rows = slice(s * row_sub, (s + 1) * row_sub)
        yb = pending.pop(s)
        for k in range(n_proj):
            r = _dot_nt(yb, wp_refs[k][...]) if proj_t else _dot(yb, wp_refs[k][...])
            if scales[k] != 1.0:
                r = r * scales[k]
            if out_t[k]:
                out_refs[o + k][:, rows] = r.T.astype(out_refs[o + k].dtype)
            else:
                out_refs[o + k][rows, :] = r.astype(out_refs[o + k].dtype)

    norm_stage(0)
    for s in range(n_sub):
        if s + 1 < n_sub:
            norm_stage(s + 1)
        if n_proj:
            proj_stage(s)


def _layer_io(h, acc_pairs, gain, proj_ws, proj_dtypes, scales, write_h, row_sub=ROW_SUB, proj_t=False,
              out_t=None):
    n_rows = h.shape[0]
    out_t = tuple(out_t) if out_t else (False,) * len(proj_ws)
    grid = (n_rows // ROW_TILE,)
    row_spec = lambda n: pl.BlockSpec((ROW_TILE, n), lambda i: (i, 0))
    full_spec = lambda a: pl.BlockSpec(a.shape, lambda i: (0,) * a.ndim, pipeline_mode=pl.Buffered(1))

    args = [h]
    in_specs = [row_spec(D_MODEL)]
    for a, w in acc_pairs:
        args += [a, w]
        in_specs += [row_spec(a.shape[1]), full_spec(w)]
    args.append(gain)
    in_specs.append(full_spec(gain))
    for w in proj_ws:
        args.append(w)
        in_specs.append(full_spec(w))

    out_shape, out_specs = [], []
    if write_h:
        out_shape.append(jax.ShapeDtypeStruct((n_rows, D_MODEL), F32))
        out_specs.append(row_spec(D_MODEL))
    if proj_ws:
        for w, dt, t in zip(proj_ws, proj_dtypes, out_t):
            n_out = w.shape[0] if proj_t else w.shape[1]
            if t:
                out_shape.append(jax.ShapeDtypeStruct((n_rows // ROW_TILE, n_out, ROW_TILE), dt))
                out_specs.append(pl.BlockSpec((None, n_out, ROW_TILE), lambda i: (i, 0, 0)))
            else:
                out_shape.append(jax.ShapeDtypeStruct((n_rows, n_out), dt))
                out_specs.append(row_spec(n_out))
    else:
        out_shape.append(jax.ShapeDtypeStruct((n_rows, D_MODEL), F32))
        out_specs.append(row_spec(D_MODEL))

    kern = functools.partial(_layer_io_kernel, n_acc=len(acc_pairs), n_proj=len(proj_ws),
                             write_h=write_h, scales=tuple(scales), row_sub=row_sub, proj_t=proj_t, out_t=out_t)
    return pl.pallas_call(
        kern, grid=grid, in_specs=in_specs, out_specs=out_specs, out_shape=out_shape,
        compiler_params=pltpu.CompilerParams(dimension_semantics=("arbitrary",),
                                             vmem_limit_bytes=VMEM_LIMIT),
        name="layer_io",
    )(*args)


def _gla_init(st_ref):
    @pl.when(pl.program_id(1) == 0)
    def _():
        st_ref[...] = jnp.zeros_like(st_ref)


def _gla_steps(q_ref, k_ref, v_ref, gate_ref, down_ref, wup_ref, bup_ref, gain_ref, o_ref, st_ref):
    C = GLA_CHUNK

    a = down_ref[...]
    a_hi = a.astype(BF16)
    a_lo = (a - a_hi.astype(F32)).astype(BF16)
    lane_a = lax.broadcasted_iota(jnp.int32, a.shape, 1) // GLA_RANK
    z = _dot(jnp.where(lane_a == 1, a_lo, a_hi), wup_ref[...]) + bup_ref[...]
    log_a = (jnp.minimum(z, 0.0) - jnp.log(1.0 + jnp.exp(-jnp.abs(z)))) * (1.0 / GLA_TAU)

    r_i = lax.broadcasted_iota(jnp.int32, (C, C), 0)
    c_i = lax.broadcasted_iota(jnp.int32, (C, C), 1)
    tri = (c_i <= r_i).astype(BF16)
    lane_qk = lax.broadcasted_iota(jnp.int32, (C, GLA_QK), 1)
    head_masks = [(lane_qk // GLA_DK) == h for h in range(GLA_HEADS)]
    rs = lax.broadcasted_iota(jnp.int32, (GLA_HEADS * C, C), 0)
    cs = lax.broadcasted_iota(jnp.int32, (GLA_HEADS * C, C), 1)
    causal = (rs % C) >= cs
    st_r = lax.broadcasted_iota(jnp.int32, (A_WIDTH, GLA_QK), 0)
    st_c = lax.broadcasted_iota(jnp.int32, (A_WIDTH, GLA_QK), 1)
    same_head = (st_r // GLA_DV) == (st_c // GLA_DK)
    gain = gain_ref[...]

    pending = {}

    def cumsum(c):
        g = log_a[c * C:(c + 1) * C]
        g_hi = g.astype(BF16)
        g_lo = (g - g_hi.astype(F32)).astype(BF16)
        pending[c] = _dot(tri, g_hi) + _dot(tri, g_lo)

    def prep(c):
        rows = slice(c * C, (c + 1) * C)
        b = pending.pop(c)
        b_last = b[C - 1:C]
        q = q_ref[rows, :].astype(F32)
        k = k_ref[rows, :].astype(F32)
        q_e = q * jnp.exp(b)
        k_e = (k * jnp.exp(-b)).astype(BF16)
        k_l = (k * jnp.exp(b_last - b)).astype(BF16)
        decay = jnp.exp(b_last)
        q_stack = jnp.concatenate([jnp.where(m, q_e, 0.0) for m in head_masks], axis=0).astype(BF16)
        pending[c] = (q_stack, q_e.astype(BF16), k_e, k_l, decay)

    def scores(c):
        q_stack, q_eb, k_e, k_l, decay = pending.pop(c)
        att = jnp.where(causal, _dot_nt(q_stack, k_e), 0.0).astype(BF16)
        pending[c] = (att, q_eb, k_l, decay)

    def main(c):
        rows = slice(c * C, (c + 1) * C)
        att, q_eb, k_l, decay = pending.pop(c)
        v = v_ref[rows, :]
        o_full = _dot(att, v)
        o_intra = jnp.concatenate(
            [o_full[h * C:(h + 1) * C, h * GLA_DV:(h + 1) * GLA_DV] for h in range(GLA_HEADS)], axis=1)
        st = st_ref[...]
        o = o_intra + _dot_nt(q_eb, st.astype(BF16))
        kv_t = _dot_tn(v, k_l)
        st_ref[...] = st * decay + jnp.where(same_head, kv_t, 0.0)

        outs = []
        for h in range(GLA_HEADS):
            oh = o[:, h * GLA_DV:(h + 1) * GLA_DV]
            outs.append(oh * lax.rsqrt(jnp.mean(oh * oh, axis=-1, keepdims=True) + EPS) * gain)
        on = jnp.concatenate(outs, axis=1)
        o_ref[rows, :] = (on * _silu(gate_ref[rows, :].astype(F32))).astype(o_ref.dtype)

    n_chunks = GLA_TILE // C
    steps = [functools.partial(cumsum, 0), functools.partial(cumsum, 1), functools.partial(cumsum, 2),
             functools.partial(prep, 0), functools.partial(prep, 1), functools.partial(scores, 0)]
    for c in range(n_chunks):
        if c + 3 < n_chunks:
            steps.append(functools.partial(cumsum, c + 3))
        if c + 2 < n_chunks:
            steps.append(functools.partial(prep, c + 2))
        if c + 1 < n_chunks:
            steps.append(functools.partial(scores, c + 1))
        steps.append(functools.partial(main, c))
    return steps


def _swa_init(rb_ref, bias_ref):
    W = SWA_WINDOW

    @pl.when(jnp.logical_and(pl.program_id(0) == 0, pl.program_id(1) == 0))
    def _():
        kj = lax.broadcasted_iota(jnp.int32, (2 * W, W), 0)
        qi = lax.broadcasted_iota(jnp.int32, (2 * W, W), 1)
        dist = qi + W - kj
        in_win = jnp.logical_and(dist >= 0, dist < W)
        for h in range(SWA_HEADS):
            bias_ref[:, h * W:(h + 1) * W] = jnp.where(in_win, _t5_bias(dist, rb_ref, h) * LOG2E, -jnp.inf)


def _swa_steps(sink_ref, q_ref, kp_ref, k_ref, vp_ref, v_ref, gate_ref, o_ref, bias_ref, s_ref, smax_ref):
    W = SWA_WINDOW
    H = SWA_HEADS
    DH = SWA_DH
    G = SWA_HEADS // SWA_KV_HEADS
    t = pl.program_id(1)

    kcat = jnp.concatenate([kp_ref[...], k_ref[...]], axis=0)
    vcat = jnp.concatenate([vp_ref[...], v_ref[...]], axis=0)
    vt = vcat.astype(F32).T.astype(BF16)
    ones_rows = jnp.ones((SWA_VT_ROWS - LANES, 2 * W), BF16)
    q_t = q_ref[...].astype(F32).T.astype(BF16)
    zeros = jnp.zeros((DH, W), BF16)
    colh = lax.broadcasted_iota(jnp.int32, (1, H * W), 1) // W
    sink = jnp.full((1, H * W), sink_ref[H - 1], F32)
    for h in range(H - 2, -1, -1):
        sink = jnp.where(colh == h, sink_ref[h], sink)
    sink = sink * LOG2E
    key_prev = lax.broadcasted_iota(jnp.int32, (2 * W, H * W), 0) < W

    def produce(blk):
        tok = slice(blk * W, (blk + 1) * W)
        win = slice(blk * W, blk * W + 2 * W)
        pieces = []
        for h in range(H):
            piece = q_t[h * DH:(h + 1) * DH, tok]
            pieces.append(jnp.concatenate([piece, zeros] if h < G else [zeros, piece], axis=0))
        q_pad = jnp.concatenate(pieces, axis=1)
        s = _dot(kcat[win], q_pad) + bias_ref[...]
        if blk == 0:
            s = jnp.where(jnp.logical_and(t == 0, key_prev), -jnp.inf, s)
        s_ref[blk % SWA_SLOTS] = s
        smax_ref[blk % SWA_SLOTS] = jnp.maximum(jnp.max(s, axis=0, keepdims=True), sink)

    pending = {}

    def attend(blk):
        win = slice(blk * W, blk * W + 2 * W)
        m = smax_ref[blk % SWA_SLOTS]
        p = jnp.exp2(s_ref[blk % SWA_SLOTS] - m).astype(BF16)
        pv = _dot(jnp.concatenate([vt[:, win], ones_rows], axis=0), p)
        pending[blk] = (pv, m)

    def finish(blk):
        tok = slice(blk * W, (blk + 1) * W)
        pv, m = pending.pop(blk)
        inv = 1.0 / (pv[LANES:LANES + 1] + jnp.exp2(sink - m))
        o_t = jnp.concatenate(
            [pv[(h // G) * DH:(h // G + 1) * DH, h * W:(h + 1) * W] * inv[:, h * W:(h + 1) * W]
             for h in range(H)], axis=0)
        gate = gate_ref[tok, :].astype(F32)
        o_ref[tok, :] = (o_t.T * _silu(gate)).astype(o_ref.dtype)

    n_blk = SWA_TILE // W
    ahead = SWA_SLOTS - 1
    steps = [functools.partial(produce, blk) for blk in range(ahead)]
    for blk in range(n_blk):
        if blk + ahead < n_blk:
            steps.append(functools.partial(produce, blk + ahead))
        steps.append(functools.partial(attend, blk))
        if blk >= 1:
            steps.append(functools.partial(finish, blk - 1))
    steps.append(functools.partial(finish, n_blk - 1))
    return steps


def _gla_swa_kernel(aq_ref, ak_ref, av_ref, agate_ref, adown_ref, wup_ref, bup_ref, gain_ref,
                    sink_ref, rb_ref, bq_ref, kp_ref, k_ref, vp_ref, v_ref, bgate_ref,
                    oa_ref, ob_ref, st_ref, bias_ref, s_ref, smax_ref):
    _gla_init(st_ref)
    _swa_init(rb_ref, bias_ref)
    gla = _gla_steps(aq_ref, ak_ref, av_ref, agate_ref, adown_ref, wup_ref, bup_ref, gain_ref, oa_ref, st_ref)
    swa = _swa_steps(sink_ref, bq_ref, kp_ref, k_ref, vp_ref, v_ref, bgate_ref, ob_ref, bias_ref, s_ref, smax_ref)
    for step in gla + swa:
        step()


def _gla_swa(aq, ak, av, agate, adown, w_up, b_up, gain, bq, bkv, bgate, rel_bias, sinks):
    assert GLA_TILE == SWA_TILE
    nt = SEQ // SWA_TILE
    per = SWA_TILE // SWA_WINDOW
    row = lambda n: pl.BlockSpec((SWA_TILE, n), lambda b, t: (b * nt + t, 0))
    full = lambda a: pl.BlockSpec(a.shape, lambda b, t: (0,) * a.ndim)
    own = lambda c: pl.BlockSpec((SWA_TILE, LANES), lambda b, t: (b * nt + t, c))
    prev = lambda c: pl.BlockSpec((SWA_WINDOW, LANES),
                                  lambda b, t: (jnp.maximum((b * nt + t) * per - 1, 0), c))
    smem = pl.BlockSpec(memory_space=pltpu.SMEM)
    return pl.pallas_call(
        _gla_swa_kernel, grid=(BATCH, nt),
        in_specs=[row(GLA_QK), row(GLA_QK), row(A_WIDTH), row(A_WIDTH), row(LANES),
                  full(w_up), full(b_up), full(gain),
                  smem, smem, row(B_WIDTH), prev(0), own(0), prev(1), own(1), row(B_WIDTH)],
        out_specs=[row(A_WIDTH), row(B_WIDTH)],
        out_shape=[jax.ShapeDtypeStruct((TOKENS, A_WIDTH), BF16),
                   jax.ShapeDtypeStruct((TOKENS, B_WIDTH), BF16)],
        scratch_shapes=[pltpu.VMEM((A_WIDTH, GLA_QK), F32),
                        pltpu.VMEM((2 * SWA_WINDOW, SWA_HEADS * SWA_WINDOW), F32),
                        pltpu.VMEM((SWA_SLOTS, 2 * SWA_WINDOW, SWA_HEADS * SWA_WINDOW), F32),
                        pltpu.VMEM((SWA_SLOTS, 1, SWA_HEADS * SWA_WINDOW), F32)],
        compiler_params=pltpu.CompilerParams(dimension_semantics=("arbitrary", "arbitrary"),
                                             vmem_limit_bytes=VMEM_LIMIT),
        name="gla_swa",
    )(aq, ak, av, agate, adown, w_up, b_up, gain, sinks, rel_bias, bq, bkv, bkv, bkv, bkv, bgate)


def _moba_kernel(rb_ref, qt_ref, k_ref, v_ref, gate_ref, o_ref,
                 kaug_ref, vt_ref, km_ref, bias_ref, qat_ref, s_ref, smax_ref, m_ref, acc_ref):
    BLK = MOBA_BLOCK
    G = MOBA_GROUP
    KVH = MOBA_KV_HEADS
    R = G * BLK
    NP1 = MOBA_NBLK + 1
    i = pl.program_id(1)

    @pl.when(i == 0)
    def _():
        lane = lax.broadcasted_iota(jnp.int32, (BLK, LANES), 1)
        ones_rows = jnp.ones((MOBA_VT_ROWS - MOBA_DH, BLK), BF16)

        def prep(j, carry):
            rows = pl.ds(pl.multiple_of(j * BLK, BLK), BLK)
            onehot = jnp.where(lane == j, 1.0, 0.0).astype(BF16)
            for kh in range(KVH):
                kb = k_ref[rows, kh * MOBA_DH:(kh + 1) * MOBA_DH]
                kaug_ref[kh * NP1 + j + 1, :, 0:LANES] = kb
                kaug_ref[kh * NP1 + j + 1, :, LANES:2 * LANES] = onehot
                km_ref[pl.ds(kh * MOBA_NBLK + j, 1), :] = jnp.mean(kb.astype(F32), axis=0, keepdims=True)
                vb = v_ref[rows, kh * MOBA_DH:(kh + 1) * MOBA_DH]
                vt_ref[kh * NP1 + j + 1, 0:MOBA_DH, :] = vb.astype(F32).T.astype(BF16)
                vt_ref[kh * NP1 + j + 1, MOBA_DH:, :] = ones_rows
            return carry

        lax.fori_loop(0, MOBA_NBLK, prep, 0)
        for kh in range(KVH):
            kaug_ref[kh * NP1, :, 0:LANES] = jnp.zeros((BLK, LANES), BF16)
            kaug_ref[kh * NP1, :, LANES:2 * LANES] = jnp.where(lane == MOBA_NBLK, 1.0, 0.0).astype(BF16)
            vt_ref[kh * NP1] = jnp.zeros((MOBA_VT_ROWS, BLK), BF16)
            qat_ref[kh, MOBA_DH + MOBA_NBLK:, :] = jnp.full((2 * LANES - MOBA_DH - MOBA_NBLK, R), NEG_BIG, BF16)

    @pl.when(jnp.logical_and(pl.program_id(0) == 0, i == 0))
    def _():
        tk = lax.broadcasted_iota(jnp.int32, (BLK, BLK), 0)
        tq = lax.broadcasted_iota(jnp.int32, (BLK, BLK), 1)
        d_own = tq - tk
        for head in range(MOBA_HEADS):
            cols = slice(head * BLK, (head + 1) * BLK)
            bias_ref[0:BLK, cols] = _t5_bias(d_own + BLK, rb_ref, head) * LOG2E
            bias_ref[BLK:2 * BLK, cols] = jnp.where(d_own >= 0, _t5_bias(d_own, rb_ref, head) * LOG2E, -jnp.inf)

    blk = lax.broadcasted_iota(jnp.int32, (MOBA_NBLK, R), 0)
    blkf = blk.astype(F32)
    past = blk < i
    colh = lax.broadcasted_iota(jnp.int32, (1, R), 1) // BLK
    far_row = REL_BUCKETS - 1

    own_max = []
    for kh in range(KVH):
        q_t = jnp.concatenate([qt_ref[(kh * G + g) * MOBA_DH:(kh * G + g + 1) * MOBA_DH, :] for g in range(G)],
                              axis=1)
        qat_ref[kh, 0:MOBA_DH, :] = q_t
        s_own = (_dot(kaug_ref[kh * NP1 + i + 1, :, 0:MOBA_DH], q_t)
                 + bias_ref[BLK:2 * BLK, kh * R:(kh + 1) * R])
        s_ref[kh, BLK:2 * BLK, :] = s_own
        own_max.append(jnp.max(s_own, axis=0, keepdims=True))

        km = km_ref[kh * MOBA_NBLK:(kh + 1) * MOBA_NBLK, :]
        km_hi = km.astype(BF16)
        km_lo = (km - km_hi.astype(F32)).astype(BF16)
        gate = _dot(km_hi, q_t) + _dot(km_lo, q_t)
        g_ = jnp.where(past, gate, -jnp.inf)
        selected = jnp.zeros((MOBA_NBLK, R), dtype=jnp.bool_)
        for _ in range(MOBA_TOPK):
            mx = jnp.max(g_, axis=0, keepdims=True)
            first = jnp.min(jnp.where(g_ == mx, blkf, 1e9), axis=0, keepdims=True)
            pick = jnp.logical_and(blkf == first, past)
            selected = jnp.logical_or(selected, pick)
            g_ = jnp.where(pick, -jnp.inf, g_)
        cfar = jnp.where(colh == 0, rb_ref[far_row, kh * G],
                         jnp.where(colh == 1, rb_ref[far_row, kh * G + 1],
                                   jnp.where(colh == 2, rb_ref[far_row, kh * G + 2],
                                             rb_ref[far_row, kh * G + 3])))
        sel_bias = jnp.where(selected, jnp.where(blk < i - 1, cfar * LOG2E, 0.0), NEG_BIG)
        qat_ref[kh, MOBA_DH:MOBA_DH + MOBA_NBLK, :] = sel_bias.astype(BF16)

    for kh in range(KVH):
        s_near = _dot(kaug_ref[kh * NP1 + i], qat_ref[kh]) + bias_ref[0:BLK, kh * R:(kh + 1) * R]
        s_ref[kh, 0:BLK, :] = s_near
        smax_ref[kh] = jnp.maximum(own_max[kh], jnp.max(s_near, axis=0, keepdims=True))

    def produce(slot, kh, start):
        k2 = kaug_ref[pl.ds(kh * NP1 + start, 2)].reshape(2 * BLK, 2 * LANES)
        s = _dot(k2, qat_ref[kh])
        s_ref[slot * KVH + kh] = s
        smax_ref[slot * KVH + kh] = jnp.max(s, axis=0, keepdims=True)

    def consume(slot, kh, start, first=False, dummy_first=False):
        m_new = smax_ref[slot * KVH + kh]
        if not first:
            m_old = m_ref[kh]
            m_new = jnp.maximum(m_old, m_new)
        if dummy_first:
            p = jnp.exp2(s_ref[slot * KVH + kh, BLK:2 * BLK, :] - m_new).astype(BF16)
            pv = _dot(vt_ref[kh * NP1 + 1], p)
        else:
            p = jnp.exp2(s_ref[slot * KVH + kh] - m_new).astype(BF16)
            pv = (_dot(vt_ref[kh * NP1 + start], p[0:BLK])
                  + _dot(vt_ref[kh * NP1 + start + 1], p[BLK:2 * BLK]))
        acc_ref[kh] = pv if first else jnp.exp2(m_old - m_new) * acc_ref[kh] + pv
        m_ref[kh] = m_new

    n_far = jnp.maximum(i - 1, 0)
    n_pairs = (n_far + 1) // 2

    def far_start(t):
        return jnp.maximum(n_far - 2 * t - 1, 0)

    def far_step(t, slot):
        for kh in range(KVH):
            produce(1 - slot, kh, far_start(t + 1))
            consume(slot, kh, far_start(t))

    for kh in range(KVH):
        produce(1, kh, far_start(0))
        consume(0, kh, i, first=True)

    def far_four(u, carry):
        for d in range(4):
            far_step(4 * u + d, (d + 1) % 2)
        return carry

    n_full = jnp.maximum(n_pairs - 1, 0)
    lax.fori_loop(0, n_full // 4, far_four, 0)
    done = (n_full // 4) * 4

    @pl.when(n_full - done >= 2)
    def _():
        far_step(done, 1)
        far_step(done + 1, 0)

    @pl.when(n_full % 2 == 1)
    def _():
        far_step(n_full - 1, 1)

    def finalize(kh):
        acc = acc_ref[kh]
        o = (acc[0:MOBA_DH] * (1.0 / acc[MOBA_DH:MOBA_DH + 1])).T
        for g in range(G):
            cols = slice((kh * G + g) * MOBA_DH, (kh * G + g + 1) * MOBA_DH)
            gt = gate_ref[:, cols].astype(F32)
            o_ref[:, cols] = (o[g * BLK:(g + 1) * BLK] * _silu(gt)).astype(o_ref.dtype)

    for slot in range(2):
        for odd in range(2):
            @pl.when(jnp.logical_and(n_pairs >= 1, jnp.logical_and(n_pairs % 2 == slot, n_far % 2 == odd)))
            def _():
                for kh in range(KVH):
                    consume(slot, kh, far_start(n_pairs - 1), dummy_first=bool(odd))
                    finalize(kh)

    @pl.when(n_pairs == 0)
    def _():
        for kh in range(KVH):
            finalize(kh)


def _moba(cqt, ck, cv, cgate, rel_bias):
    R = MOBA_GROUP * MOBA_BLOCK
    KVH = MOBA_KV_HEADS
    NP1 = MOBA_NBLK + 1
    qspec = pl.BlockSpec((MOBA_BLOCK, C_WIDTH), lambda b, i: (b * MOBA_NBLK + i, 0))
    per_tile = ROW_TILE // MOBA_BLOCK
    qtspec = pl.BlockSpec((None, C_WIDTH, MOBA_BLOCK),
                          lambda b, i: ((b * MOBA_NBLK + i) // per_tile, 0, (b * MOBA_NBLK + i) % per_tile))
    kvspec = pl.BlockSpec((SEQ, KVH * MOBA_DH), lambda b, i: (b, 0), pipeline_mode=pl.Buffered(1))
    return pl.pallas_call(
        _moba_kernel, grid=(BATCH, MOBA_NBLK),
        in_specs=[pl.BlockSpec(memory_space=pltpu.SMEM), qtspec, kvspec, kvspec, qspec],
        out_specs=qspec,
        out_shape=jax.ShapeDtypeStruct((TOKENS, C_WIDTH), BF16),
        scratch_shapes=[pltpu.VMEM((KVH * NP1, MOBA_BLOCK, 2 * LANES), BF16),
                        pltpu.VMEM((KVH * NP1, MOBA_VT_ROWS, MOBA_BLOCK), BF16),
                        pltpu.VMEM((KVH * MOBA_NBLK, MOBA_DH), F32),
                        pltpu.VMEM((2 * MOBA_BLOCK, KVH * R), F32),
                        pltpu.VMEM((KVH, 2 * LANES, R), BF16),
                        pltpu.VMEM((2 * KVH, 2 * MOBA_BLOCK, R), F32),
                        pltpu.VMEM((2 * KVH, 1, R), F32),
                        pltpu.VMEM((KVH, 1, R), F32),
                        pltpu.VMEM((KVH, MOBA_VT_ROWS, R), F32)],
        compiler_params=pltpu.CompilerParams(
            dimension_semantics=("arbitrary", "arbitrary"), vmem_limit_bytes=MOBA_VMEM_LIMIT),
        name="moba",
    )(rel_bias, cqt, ck, cv, cgate)


def kernel(x, norm_gain, final_gain, rel_bias, w_in_even, gla_w_up, gla_b_up, gla_norm_gain, swa_sinks,
           w_out_even, w_in_odd, w_out_odd):
    assert x.shape == (BATCH, SEQ, D_MODEL)
    h0 = x.reshape(TOKENS, D_MODEL)

    we = jnp.swapaxes(w_in_even[0], 0, 1)
    splits = np.cumsum([0, GLA_QK, GLA_QK, A_WIDTH, GLA_RANK, A_WIDTH, B_WIDTH, SWA_KV_HEADS * SWA_DH,
                        SWA_KV_HEADS * SWA_DH, B_WIDTH])
    w_aq, w_ak, w_av, w_down, w_agate, w_bq = [we[splits[k]:splits[k + 1]] for k in range(6)]
    w_bkv = we[splits[6]:splits[8]]
    w_bgate = we[splits[8]:splits[9]]
    w_down = jnp.pad(jnp.concatenate([w_down] * 3, axis=0), ((0, LANES - 3 * GLA_RANK), (0, 0)))
    proj_ws = [w.astype(BF16) for w in (w_aq, w_ak, w_av, w_down, w_agate, w_bq, w_bkv, w_bgate)]
    proj_dt = [BF16, BF16, BF16, F32, BF16, BF16, BF16, BF16]
    scales = [GLA_DK ** -0.5, 1.0, 1.0, 1.0, 1.0, SWA_DH ** -0.5 * LOG2E, 1.0, 1.0]
    aq, ak, av, adown, agate, bq, bkv, bgate = _layer_io(
        h0, [], norm_gain[0:1], proj_ws, proj_dt, scales, write_h=False, proj_t=True)

    w_up_hi = gla_w_up[0].astype(BF16)
    w_up_lo = (gla_w_up[0] - w_up_hi.astype(F32)).astype(BF16)
    w_up = jnp.pad(jnp.concatenate([w_up_hi, w_up_hi, w_up_lo], axis=0), ((0, LANES - 3 * GLA_RANK), (0, 0)))
    oa, ob = _gla_swa(aq, ak, av, agate, adown, w_up, gla_b_up[0:1], gla_norm_gain[0:1],
                      bq, bkv, bgate, rel_bias, swa_sinks[0])

    wo = w_out_even[0].astype(BF16)
    wi = w_in_odd[0]
    osp = np.cumsum([0, C_WIDTH, MOBA_KV_HEADS * MOBA_DH, MOBA_KV_HEADS * MOBA_DH, C_WIDTH])
    odd_ws = [wi[:, osp[k]:osp[k + 1]].astype(BF16) for k in range(4)]
    h1, cqt, ck, cv, cgate = _layer_io(
        h0, [(oa, wo[:A_WIDTH]), (ob, wo[A_WIDTH:])], norm_gain[1:2], odd_ws, [BF16] * 4,
        [MOBA_DH ** -0.5 * LOG2E, 1.0, 1.0, 1.0], write_h=True, row_sub=ROW_TILE,
        out_t=[True, False, False, False])

    oc = _moba(cqt, ck, cv, cgate, rel_bias)

    (out,) = _layer_io(h1, [(oc, w_out_odd[0].astype(BF16))], final_gain.reshape(1, D_MODEL), [], [], [],
                       write_h=False)
    return out.reshape(BATCH, SEQ, D_MODEL)
```

```python
import functools
import math

import numpy as np
import jax
import jax.numpy as jnp
from jax import lax
from jax.experimental import pallas as pl
from jax.experimental.pallas import tpu as pltpu

D_MODEL = 1024
BATCH = 2
SEQ = 8192
TOKENS = BATCH * SEQ

REL_BUCKETS = 32
REL_MAX_EXACT = 16
REL_MAX_DIST = 128
ATTN_HEADS = 8

GLA_HEADS = 4
GLA_DK = 64
GLA_DV = 128
GLA_RANK = 16
GLA_TAU = 16.0
GLA_CHUNK = 64

SWA_HEADS = 8
SWA_KV_HEADS = 2
SWA_DH = 64
SWA_WINDOW = 128

MOBA_HEADS = 8
MOBA_KV_HEADS = 2
MOBA_DH = 128
MOBA_BLOCK = 256
MOBA_TOPK = 3
MOBA_GROUP = MOBA_HEADS // MOBA_KV_HEADS
MOBA_NBLK = SEQ // MOBA_BLOCK
MOBA_VT_ROWS = MOBA_DH + 16
LOG2E = math.log2(math.e)
SWA_VT_ROWS = SWA_KV_HEADS * SWA_DH + 16

EPS = 1e-6
LANES = 128
NEG_BIG = -1e30
VMEM_LIMIT = 48 * 1024 * 1024
MOBA_VMEM_LIMIT = 56 * 1024 * 1024

A_WIDTH = GLA_HEADS * GLA_DV
B_WIDTH = SWA_HEADS * SWA_DH
C_WIDTH = MOBA_HEADS * MOBA_DH
GLA_QK = GLA_HEADS * GLA_DK

ROW_TILE = 1024
ROW_SUB = 256
GLA_TILE = 1024
SWA_TILE = 1024
SWA_SLOTS = 2

F32 = jnp.float32
BF16 = jnp.bfloat16


def _dot(a, b):
    return jnp.dot(a, b, preferred_element_type=F32)


def _dot_nt(a, b):
    return lax.dot_general(a, b, (((1,), (1,)), ((), ())), preferred_element_type=F32)


def _dot_tn(a, b):
    return lax.dot_general(a, b, (((0,), (0,)), ((), ())), preferred_element_type=F32)


def _silu(x):
    h = 0.5 * x
    return h + h * jnp.tanh(h)


def _t5_thresholds():
    d = np.arange(REL_MAX_DIST + 1)
    nf = np.maximum(d, 1).astype(np.float32)
    large = REL_MAX_EXACT + (np.log(nf / np.float32(REL_MAX_EXACT))
                             / np.float32(math.log(REL_MAX_DIST / REL_MAX_EXACT))
                             * np.float32(REL_BUCKETS - REL_MAX_EXACT)).astype(np.int32)
    bucket = np.where(d < REL_MAX_EXACT, d, np.minimum(large, REL_BUCKETS - 1))
    assert np.all(np.diff(bucket) >= 0) and bucket[-1] == REL_BUCKETS - 1
    return [int(np.argmax(bucket >= b)) for b in range(REL_BUCKETS)]


_T5_THRESH = _t5_thresholds()


def _t5_bias(dist, rb_ref, head):
    out = jnp.full(dist.shape, rb_ref[0, head], F32)
    for b in range(1, REL_BUCKETS):
        out = jnp.where(dist >= _T5_THRESH[b], rb_ref[b, head], out)
    return out


def _layer_io_kernel(*refs, n_acc, n_proj, write_h, scales, row_sub, proj_t):
    h_ref = refs[0]
    pos = 1
    acc_pairs = []
    for _ in range(n_acc):
        acc_pairs.append((refs[pos], refs[pos + 1]))
        pos += 2
    gain_ref = refs[pos]
    pos += 1
    wp_refs = refs[pos:pos + n_proj]
    pos += n_proj
    out_refs = refs[pos:]

    o = 1 if write_h else 0
    n_sub = ROW_TILE // row_sub
    pending = {}

    def norm_stage(s):
        rows = slice(s * row_sub, (s + 1) * row_sub)
        h = h_ref[rows, :]
        for a_ref, w_ref in acc_pairs:
            h = h + _dot(a_ref[rows, :], w_ref[...])
        if write_h:
            out_refs[0][rows, :] = h
        y = h * lax.rsqrt(jnp.mean(h * h, axis=-1, keepdims=True) + EPS) * gain_ref[...]
        if n_proj == 0:
            out_refs[o][rows, :] = y
        else:
            pending[s] = y.astype(BF16)

    def proj_stage(s):
        rows = slice(s * row_sub, (s + 1) * row_sub)
        yb = pending.pop(s)
        for k in range(n_proj):
            r = _dot_nt(yb, wp_refs[k][...]) if proj_t else _dot(yb, wp_refs[k][...])
            if scales[k] != 1.0:
                r = r * scales[k]
            out_refs[o + k][rows, :] = r.astype(out_refs[o + k].dtype)

    norm_stage(0)
    for s in range(n_sub):
        if s + 1 < n_sub:
            norm_stage(s + 1)
        if n_proj:
            proj_stage(s)


def _layer_io(h, acc_pairs, gain, proj_ws, proj_dtypes, scales, write_h, row_sub=ROW_SUB, proj_t=False):
    n_rows = h.shape[0]
    grid = (n_rows // ROW_TILE,)
    row_spec = lambda n: pl.BlockSpec((ROW_TILE, n), lambda i: (i, 0))
    full_spec = lambda a: pl.BlockSpec(a.shape, lambda i: (0,) * a.ndim, pipeline_mode=pl.Buffered(1))

    args = [h]
    in_specs = [row_spec(D_MODEL)]
    for a, w in acc_pairs:
        args += [a, w]
        in_specs += [row_spec(a.shape[1]), full_spec(w)]
    args.append(gain)
    in_specs.append(full_spec(gain))
    for w in proj_ws:
        args.append(w)
        in_specs.append(full_spec(w))

    out_shape, out_specs = [], []
    if write_h:
        out_shape.append(jax.ShapeDtypeStruct((n_rows, D_MODEL), F32))
        out_specs.append(row_spec(D_MODEL))
    if proj_ws:
        for w, dt in zip(proj_ws, proj_dtypes):
            n_out = w.shape[0] if proj_t else w.shape[1]
            out_shape.append(jax.ShapeDtypeStruct((n_rows, n_out), dt))
            out_specs.append(row_spec(n_out))
    else:
        out_shape.append(jax.ShapeDtypeStruct((n_rows, D_MODEL), F32))
        out_specs.append(row_spec(D_MODEL))

    kern = functools.partial(_layer_io_kernel, n_acc=len(acc_pairs), n_proj=len(proj_ws),
                             write_h=write_h, scales=tuple(scales), row_sub=row_sub, proj_t=proj_t)
    return pl.pallas_call(
        kern, grid=grid, in_specs=in_specs, out_specs=out_specs, out_shape=out_shape,
        compiler_params=pltpu.CompilerParams(dimension_semantics=("arbitrary",),
                                             vmem_limit_bytes=VMEM_LIMIT),
        name="layer_io",
    )(*args)


IN_RING = 3


def _final_io_kernel(h_hbm, a_hbm, gain_ref, w_ref, o_ref, hbuf, abuf, sem):
    s = pl.program_id(0)
    n = pl.num_programs(0)

    def copies(step):
        start = step * ROW_TILE
        rows = pl.ds(start if isinstance(step, int) else pl.multiple_of(start, ROW_TILE), ROW_TILE)
        slot = step % IN_RING
        return (pltpu.make_async_copy(h_hbm.at[rows], hbuf.at[slot], sem.at[0, slot]),
                pltpu.make_async_copy(a_hbm.at[rows], abuf.at[slot], sem.at[1, slot]))

    @pl.when(s == 0)
    def _():
        for step in range(IN_RING - 1):
            for c in copies(step):
                c.start()

    @pl.when(s + IN_RING - 1 < n)
    def _():
        for c in copies(s + IN_RING - 1):
            c.start()

    for c in copies(s):
        c.wait()

    slot = s % IN_RING
    for t in range(ROW_TILE // ROW_SUB):
        rows = slice(t * ROW_SUB, (t + 1) * ROW_SUB)
        h = hbuf[slot, rows, :] + _dot(abuf[slot, rows, :], w_ref[...])
        o_ref[rows, :] = h * lax.rsqrt(jnp.mean(h * h, axis=-1, keepdims=True) + EPS) * gain_ref[...]


def _final_io(h, act, w, gain):
    n_rows = h.shape[0]
    n_steps = n_rows // ROW_TILE
    assert n_steps >= IN_RING - 1
    full_spec = lambda a: pl.BlockSpec(a.shape, lambda i: (0,) * a.ndim, pipeline_mode=pl.Buffered(1))
    return pl.pallas_call(
        _final_io_kernel, grid=(n_steps,),
        in_specs=[pl.BlockSpec(memory_space=pl.ANY), pl.BlockSpec(memory_space=pl.ANY),
                  full_spec(gain), full_spec(w)],
        out_specs=pl.BlockSpec((ROW_TILE, D_MODEL), lambda i: (i, 0)),
        out_shape=jax.ShapeDtypeStruct((n_rows, D_MODEL), F32),
        scratch_shapes=[pltpu.VMEM((IN_RING, ROW_TILE, D_MODEL), h.dtype),
                        pltpu.VMEM((IN_RING, ROW_TILE, act.shape[1]), act.dtype),
                        pltpu.SemaphoreType.DMA((2, IN_RING))],
        compiler_params=pltpu.CompilerParams(dimension_semantics=("arbitrary",),
                                             vmem_limit_bytes=VMEM_LIMIT),
        name="final_io",
    )(h, act, gain, w)


def _gla_init(st_ref):
    @pl.when(pl.program_id(1) == 0)
    def _():
        st_ref[...] = jnp.zeros_like(st_ref)


def _gla_steps(q_ref, k_ref, v_ref, gate_ref, down_ref, wup_ref, bup_ref, gain_ref, o_ref, st_ref):
    C = GLA_CHUNK

    a = down_ref[...]
    a_hi = a.astype(BF16)
    a_lo = (a - a_hi.astype(F32)).astype(BF16)
    lane_a = lax.broadcasted_iota(jnp.int32, a.shape, 1) // GLA_RANK
    z = _dot(jnp.where(lane_a == 1, a_lo, a_hi), wup_ref[...]) + bup_ref[...]
    log_a = (jnp.minimum(z, 0.0) - jnp.log(1.0 + jnp.exp(-jnp.abs(z)))) * (1.0 / GLA_TAU)

    r_i = lax.broadcasted_iota(jnp.int32, (C, C), 0)
    c_i = lax.broadcasted_iota(jnp.int32, (C, C), 1)
    tri = (c_i <= r_i).astype(BF16)
    lane_qk = lax.broadcasted_iota(jnp.int32, (C, GLA_QK), 1)
    head_masks = [(lane_qk // GLA_DK) == h for h in range(GLA_HEADS)]
    rs = lax.broadcasted_iota(jnp.int32, (GLA_HEADS * C, C), 0)
    cs = lax.broadcasted_iota(jnp.int32, (GLA_HEADS * C, C), 1)
    causal = (rs % C) >= cs
    st_r = lax.broadcasted_iota(jnp.int32, (A_WIDTH, GLA_QK), 0)
    st_c = lax.broadcasted_iota(jnp.int32, (A_WIDTH, GLA_QK), 1)
    same_head = (st_r // GLA_DV) == (st_c // GLA_DK)
    gain = gain_ref[...]

    pending = {}

    def cumsum(c):
        g = log_a[c * C:(c + 1) * C]
        g_hi = g.astype(BF16)
        g_lo = (g - g_hi.astype(F32)).astype(BF16)
        pending[c] = _dot(tri, g_hi) + _dot(tri, g_lo)

    def prep(c):
        rows = slice(c * C, (c + 1) * C)
        b = pending.pop(c)
        b_last = b[C - 1:C]
        q = q_ref[rows, :].astype(F32)
        k = k_ref[rows, :].astype(F32)
        q_e = q * jnp.exp(b)
        k_e = (k * jnp.exp(-b)).astype(BF16)
        k_l = (k * jnp.exp(b_last - b)).astype(BF16)
        decay = jnp.exp(b_last)
        q_stack = jnp.concatenate([jnp.where(m, q_e, 0.0) for m in head_masks], axis=0).astype(BF16)
        pending[c] = (q_stack, q_e.astype(BF16), k_e, k_l, decay)

    def scores(c):
        q_stack, q_eb, k_e, k_l, decay = pending.pop(c)
        att = jnp.where(causal, _dot_nt(q_stack, k_e), 0.0).astype(BF16)
        pending[c] = (att, q_eb, k_l, decay)

    def main(c):
        rows = slice(c * C, (c + 1) * C)
        att, q_eb, k_l, decay = pending.pop(c)
        v = v_ref[rows, :]
        o_full = _dot(att, v)
        o_intra = jnp.concatenate(
            [o_full[h * C:(h + 1) * C, h * GLA_DV:(h + 1) * GLA_DV] for h in range(GLA_HEADS)], axis=1)
        st = st_ref[...]
        o = o_intra + _dot_nt(q_eb, st.astype(BF16))
        kv_t = _dot_tn(v, k_l)
        st_ref[...] = st * decay + jnp.where(same_head, kv_t, 0.0)

        outs = []
        for h in range(GLA_HEADS):
            oh = o[:, h * GLA_DV:(h + 1) * GLA_DV]
            outs.append(oh * lax.rsqrt(jnp.mean(oh * oh, axis=-1, keepdims=True) + EPS) * gain)
        on = jnp.concatenate(outs, axis=1)
        o_ref[rows, :] = (on * _silu(gate_ref[rows, :].astype(F32))).astype(o_ref.dtype)

    n_chunks = GLA_TILE // C
    steps = [functools.partial(cumsum, 0), functools.partial(cumsum, 1), functools.partial(cumsum, 2),
             functools.partial(prep, 0), functools.partial(prep, 1), functools.partial(scores, 0)]
    for c in range(n_chunks):
        if c + 3 < n_chunks:
            steps.append(functools.partial(cumsum, c + 3))
        if c + 2 < n_chunks:
            steps.append(functools.partial(prep, c + 2))
        if c + 1 < n_chunks:
            steps.append(functools.partial(scores, c + 1))
        steps.append(functools.partial(main, c))
    return steps


def _swa_init(rb_ref, bias_ref):
    W = SWA_WINDOW

    @pl.when(jnp.logical_and(pl.program_id(0) == 0, pl.program_id(1) == 0))
    def _():
        kj = lax.broadcasted_iota(jnp.int32, (2 * W, W), 0)
        qi = lax.broadcasted_iota(jnp.int32, (2 * W, W), 1)
        dist = qi + W - kj
        in_win = jnp.logical_and(dist >= 0, dist < W)
        for h in range(SWA_HEADS):
            bias_ref[:, h * W:(h + 1) * W] = jnp.where(in_win, _t5_bias(dist, rb_ref, h) * LOG2E, -jnp.inf)


def _swa_steps(sink_ref, q_ref, kp_ref, k_ref, vp_ref, v_ref, gate_ref, o_ref, bias_ref, s_ref, smax_ref):
    W = SWA_WINDOW
    H = SWA_HEADS
    DH = SWA_DH
    G = SWA_HEADS // SWA_KV_HEADS
    t = pl.program_id(1)

    kcat = jnp.concatenate([kp_ref[...], k_ref[...]], axis=0)
    vcat = jnp.concatenate([vp_ref[...], v_ref[...]], axis=0)
    vt = vcat.astype(F32).T.astype(BF16)
    ones_rows = jnp.ones((SWA_VT_ROWS - LANES, 2 * W), BF16)
    q_t = q_ref[...].astype(F32).T.astype(BF16)
    zeros = jnp.zeros((DH, W), BF16)
    colh = lax.broadcasted_iota(jnp.int32, (1, H * W), 1) // W
    sink = jnp.full((1, H * W), sink_ref[H - 1], F32)
    for h in range(H - 2, -1, -1):
        sink = jnp.where(colh == h, sink_ref[h], sink)
    sink = sink * LOG2E
    key_prev = lax.broadcasted_iota(jnp.int32, (2 * W, H * W), 0) < W

    def produce(blk):
        tok = slice(blk * W, (blk + 1) * W)
        win = slice(blk * W, blk * W + 2 * W)
        pieces = []
        for h in range(H):
            piece = q_t[h * DH:(h + 1) * DH, tok]
            pieces.append(jnp.concatenate([piece, zeros] if h < G else [zeros, piece], axis=0))
        q_pad = jnp.concatenate(pieces, axis=1)
        s = _dot(kcat[win], q_pad) + bias_ref[...]
        if blk == 0:
            s = jnp.where(jnp.logical_and(t == 0, key_prev), -jnp.inf, s)
        s_ref[blk % SWA_SLOTS] = s
        smax_ref[blk % SWA_SLOTS] = jnp.maximum(jnp.max(s, axis=0, keepdims=True), sink)

    pending = {}

    def attend(blk):
        win = slice(blk * W, blk * W + 2 * W)
        m = smax_ref[blk % SWA_SLOTS]
        p = jnp.exp2(s_ref[blk % SWA_SLOTS] - m).astype(BF16)
        pv = _dot(jnp.concatenate([vt[:, win], ones_rows], axis=0), p)
        pending[blk] = (pv, m)

    def finish(blk):
        tok = slice(blk * W, (blk + 1) * W)
        pv, m = pending.pop(blk)
        inv = 1.0 / (pv[LANES:LANES + 1] + jnp.exp2(sink - m))
        o_t = jnp.concatenate(
            [pv[(h // G) * DH:(h // G + 1) * DH, h * W:(h + 1) * W] * inv[:, h * W:(h + 1) * W]
             for h in range(H)], axis=0)
        gate = gate_ref[tok, :].astype(F32)
        o_ref[tok, :] = (o_t.T * _silu(gate)).astype(o_ref.dtype)

    n_blk = SWA_TILE // W
    ahead = SWA_SLOTS - 1
    steps = [functools.partial(produce, blk) for blk in range(ahead)]
    for blk in range(n_blk):
        if blk + ahead < n_blk:
            steps.append(functools.partial(produce, blk + ahead))
        steps.append(functools.partial(attend, blk))
        if blk >= 1:
            steps.append(functools.partial(finish, blk - 1))
    steps.append(functools.partial(finish, n_blk - 1))
    return steps


def _gla_swa_kernel(aq_ref, ak_ref, av_ref, agate_ref, adown_ref, wup_ref, bup_ref, gain_ref,
                    sink_ref, rb_ref, bq_ref, kp_ref, k_ref, vp_ref, v_ref, bgate_ref,
                    oa_ref, ob_ref, st_ref, bias_ref, s_ref, smax_ref):
    _gla_init(st_ref)
    _swa_init(rb_ref, bias_ref)
    gla = _gla_steps(aq_ref, ak_ref, av_ref, agate_ref, adown_ref, wup_ref, bup_ref, gain_ref, oa_ref, st_ref)
    swa = _swa_steps(sink_ref, bq_ref, kp_ref, k_ref, vp_ref, v_ref, bgate_ref, ob_ref, bias_ref, s_ref, smax_ref)
    for step in gla + swa:
        step()


def _gla_swa(aq, ak, av, agate, adown, w_up, b_up, gain, bq, bkv, bgate, rel_bias, sinks):
    assert GLA_TILE == SWA_TILE
    nt = SEQ // SWA_TILE
    per = SWA_TILE // SWA_WINDOW
    row = lambda n: pl.BlockSpec((SWA_TILE, n), lambda b, t: (b * nt + t, 0))
    full = lambda a: pl.BlockSpec(a.shape, lambda b, t: (0,) * a.ndim)
    own = lambda c: pl.BlockSpec((SWA_TILE, LANES), lambda b, t: (b * nt + t, c))
    prev = lambda c: pl.BlockSpec((SWA_WINDOW, LANES),
                                  lambda b, t: (jnp.maximum((b * nt + t) * per - 1, 0), c))
    smem = pl.BlockSpec(memory_space=pltpu.SMEM)
    return pl.pallas_call(
        _gla_swa_kernel, grid=(BATCH, nt),
        in_specs=[row(GLA_QK), row(GLA_QK), row(A_WIDTH), row(A_WIDTH), row(LANES),
                  full(w_up), full(b_up), full(gain),
                  smem, smem, row(B_WIDTH), prev(0), own(0), prev(1), own(1), row(B_WIDTH)],
        out_specs=[row(A_WIDTH), row(B_WIDTH)],
        out_shape=[jax.ShapeDtypeStruct((TOKENS, A_WIDTH), BF16),
                   jax.ShapeDtypeStruct((TOKENS, B_WIDTH), BF16)],
        scratch_shapes=[pltpu.VMEM((A_WIDTH, GLA_QK), F32),
                        pltpu.VMEM((2 * SWA_WINDOW, SWA_HEADS * SWA_WINDOW), F32),
                        pltpu.VMEM((SWA_SLOTS, 2 * SWA_WINDOW, SWA_HEADS * SWA_WINDOW), F32),
                        pltpu.VMEM((SWA_SLOTS, 1, SWA_HEADS * SWA_WINDOW), F32)],
        compiler_params=pltpu.CompilerParams(dimension_semantics=("arbitrary", "arbitrary"),
                                             vmem_limit_bytes=VMEM_LIMIT),
        name="gla_swa",
    )(aq, ak, av, agate, adown, w_up, b_up, gain, sinks, rel_bias, bq, bkv, bkv, bkv, bkv, bgate)


def _moba_kernel(rb_ref, q_ref, k_ref, v_ref, gate_ref, o_ref,
                 kaug_ref, vt_ref, km_ref, bias_ref, qat_ref, s_ref, smax_ref, m_ref, acc_ref):
    BLK = MOBA_BLOCK
    G = MOBA_GROUP
    KVH = MOBA_KV_HEADS
    R = G * BLK
    NP1 = MOBA_NBLK + 1
    i = pl.program_id(1)

    @pl.when(i == 0)
    def _():
        lane = lax.broadcasted_iota(jnp.int32, (BLK, LANES), 1)
        ones_rows = jnp.ones((MOBA_VT_ROWS - MOBA_DH, BLK), BF16)

        def prep(j, carry):
            rows = pl.ds(pl.multiple_of(j * BLK, BLK), BLK)
            onehot = jnp.where(lane == j, 1.0, 0.0).astype(BF16)
            for kh in range(KVH):
                kb = k_ref[rows, kh * MOBA_DH:(kh + 1) * MOBA_DH]
                kaug_ref[kh * NP1 + j + 1, :, 0:LANES] = kb
                kaug_ref[kh * NP1 + j + 1, :, LANES:2 * LANES] = onehot
                km_ref[pl.ds(kh * MOBA_NBLK + j, 1), :] = jnp.mean(kb.astype(F32), axis=0, keepdims=True)
                vb = v_ref[rows, kh * MOBA_DH:(kh + 1) * MOBA_DH]
                vt_ref[kh * NP1 + j + 1, 0:MOBA_DH, :] = vb.astype(F32).T.astype(BF16)
                vt_ref[kh * NP1 + j + 1, MOBA_DH:, :] = ones_rows
            return carry

        lax.fori_loop(0, MOBA_NBLK, prep, 0)
        for kh in range(KVH):
            kaug_ref[kh * NP1, :, 0:LANES] = jnp.zeros((BLK, LANES), BF16)
            kaug_ref[kh * NP1, :, LANES:2 * LANES] = jnp.where(lane == MOBA_NBLK, 1.0, 0.0).astype(BF16)
            vt_ref[kh * NP1] = jnp.zeros((MOBA_VT_ROWS, BLK), BF16)
            qat_ref[kh, MOBA_DH + MOBA_NBLK:, :] = jnp.full((2 * LANES - MOBA_DH - MOBA_NBLK, R), NEG_BIG, BF16)

    @pl.when(jnp.logical_and(pl.program_id(0) == 0, i == 0))
    def _():
        tk = lax.broadcasted_iota(jnp.int32, (BLK, BLK), 0)
        tq = lax.broadcasted_iota(jnp.int32, (BLK, BLK), 1)
        d_own = tq - tk
        for head in range(MOBA_HEADS):
            cols = slice(head * BLK, (head + 1) * BLK)
            bias_ref[0:BLK, cols] = _t5_bias(d_own + BLK, rb_ref, head) * LOG2E
            bias_ref[BLK:2 * BLK, cols] = jnp.where(d_own >= 0, _t5_bias(d_own, rb_ref, head) * LOG2E, -jnp.inf)

    blk = lax.broadcasted_iota(jnp.int32, (MOBA_NBLK, R), 0)
    blkf = blk.astype(F32)
    past = blk < i
    colh = lax.broadcasted_iota(jnp.int32, (1, R), 1) // BLK
    far_row = REL_BUCKETS - 1

    own_max = []
    for kh in range(KVH):
        q = jnp.concatenate([q_ref[:, (kh * G + g) * MOBA_DH:(kh * G + g + 1) * MOBA_DH] for g in range(G)],
                            axis=0)
        q_t = q.astype(F32).T.astype(BF16)
        qat_ref[kh, 0:MOBA_DH, :] = q_t
        s_own = (_dot(kaug_ref[kh * NP1 + i + 1, :, 0:MOBA_DH], q_t)
                 + bias_ref[BLK:2 * BLK, kh * R:(kh + 1) * R])
        s_ref[kh, BLK:2 * BLK, :] = s_own
        own_max.append(jnp.max(s_own, axis=0, keepdims=True))

        km = km_ref[kh * MOBA_NBLK:(kh + 1) * MOBA_NBLK, :]
        km_hi = km.astype(BF16)
        km_lo = (km - km_hi.astype(F32)).astype(BF16)
        gate = _dot(km_hi, q_t) + _dot(km_lo, q_t)
        g_ = jnp.where(past, gate, -jnp.inf)
        selected = jnp.zeros((MOBA_NBLK, R), dtype=jnp.bool_)
        for _ in range(MOBA_TOPK):
            mx = jnp.max(g_, axis=0, keepdims=True)
            first = jnp.min(jnp.where(g_ == mx, blkf, 1e9), axis=0, keepdims=True)
            pick = jnp.logical_and(blkf == first, past)
            selected = jnp.logical_or(selected, pick)
            g_ = jnp.where(pick, -jnp.inf, g_)
        cfar = jnp.where(colh == 0, rb_ref[far_row, kh * G],
                         jnp.where(colh == 1, rb_ref[far_row, kh * G + 1],
                                   jnp.where(colh == 2, rb_ref[far_row, kh * G + 2],
                                             rb_ref[far_row, kh * G + 3])))
        sel_bias = jnp.where(selected, jnp.where(blk < i - 1, cfar * LOG2E, 0.0), NEG_BIG)
        qat_ref[kh, MOBA_DH:MOBA_DH + MOBA_NBLK, :] = sel_bias.astype(BF16)

    for kh in range(KVH):
        s_near = _dot(kaug_ref[kh * NP1 + i], qat_ref[kh]) + bias_ref[0:BLK, kh * R:(kh + 1) * R]
        s_ref[kh, 0:BLK, :] = s_near
        smax_ref[kh] = jnp.maximum(own_max[kh], jnp.max(s_near, axis=0, keepdims=True))

    def produce(slot, kh, start):
        k2 = kaug_ref[pl.ds(kh * NP1 + start, 2)].reshape(2 * BLK, 2 * LANES)
        s = _dot(k2, qat_ref[kh])
        s_ref[slot * KVH + kh] = s
        smax_ref[slot * KVH + kh] = jnp.max(s, axis=0, keepdims=True)

    def consume(slot, kh, start, first=False, dummy_first=False):
        m_new = smax_ref[slot * KVH + kh]
        if not first:
            m_old = m_ref[kh]
            m_new = jnp.maximum(m_old, m_new)
        if dummy_first:
            p = jnp.exp2(s_ref[slot * KVH + kh, BLK:2 * BLK, :] - m_new).astype(BF16)
            pv = _dot(vt_ref[kh * NP1 + 1], p)
        else:
            p = jnp.exp2(s_ref[slot * KVH + kh] - m_new).astype(BF16)
            pv = (_dot(vt_ref[kh * NP1 + start], p[0:BLK])
                  + _dot(vt_ref[kh * NP1 + start + 1], p[BLK:2 * BLK]))
        acc_ref[kh] = pv if first else jnp.exp2(m_old - m_new) * acc_ref[kh] + pv
        m_ref[kh] = m_new

    n_far = jnp.maximum(i - 1, 0)
    n_pairs = (n_far + 1) // 2

    def far_start(t):
        return jnp.maximum(n_far - 2 * t - 1, 0)

    def far_step(t, slot):
        for kh in range(KVH):
            produce(1 - slot, kh, far_start(t + 1))
            consume(slot, kh, far_start(t))

    for kh in range(KVH):
        produce(1, kh, far_start(0))
        consume(0, kh, i, first=True)

    def far_four(u, carry):
        for d in range(4):
            far_step(4 * u + d, (d + 1) % 2)
        return carry

    n_full = jnp.maximum(n_pairs - 1, 0)
    lax.fori_loop(0, n_full // 4, far_four, 0)
    done = (n_full // 4) * 4

    @pl.when(n_full - done >= 2)
    def _():
        far_step(done, 1)
        far_step(done + 1, 0)

    @pl.when(n_full % 2 == 1)
    def _():
        far_step(n_full - 1, 1)

    def finalize(kh):
        acc = acc_ref[kh]
        o = (acc[0:MOBA_DH] * (1.0 / acc[MOBA_DH:MOBA_DH + 1])).T
        for g in range(G):
            cols = slice((kh * G + g) * MOBA_DH, (kh * G + g + 1) * MOBA_DH)
            gt = gate_ref[:, cols].astype(F32)
            o_ref[:, cols] = (o[g * BLK:(g + 1) * BLK] * _silu(gt)).astype(o_ref.dtype)

    for slot in range(2):
        for odd in range(2):
            @pl.when(jnp.logical_and(n_pairs >= 1, jnp.logical_and(n_pairs % 2 == slot, n_far % 2 == odd)))
            def _():
                for kh in range(KVH):
                    consume(slot, kh, far_start(n_pairs - 1), dummy_first=bool(odd))
                    finalize(kh)

    @pl.when(n_pairs == 0)
    def _():
        for kh in range(KVH):
            finalize(kh)


def _moba(cq, ck, cv, cgate, rel_bias):
    R = MOBA_GROUP * MOBA_BLOCK
    KVH = MOBA_KV_HEADS
    NP1 = MOBA_NBLK + 1
    qspec = pl.BlockSpec((MOBA_BLOCK, C_WIDTH), lambda b, i: (b * MOBA_NBLK + i, 0))
    kvspec = pl.BlockSpec((SEQ, KVH * MOBA_DH), lambda b, i: (b, 0), pipeline_mode=pl.Buffered(1))
    return pl.pallas_call(
        _moba_kernel, grid=(BATCH, MOBA_NBLK),
        in_specs=[pl.BlockSpec(memory_space=pltpu.SMEM), qspec, kvspec, kvspec, qspec],
        out_specs=qspec,
        out_shape=jax.ShapeDtypeStruct((TOKENS, C_WIDTH), BF16),
        scratch_shapes=[pltpu.VMEM((KVH * NP1, MOBA_BLOCK, 2 * LANES), BF16),
                        pltpu.VMEM((KVH * NP1, MOBA_VT_ROWS, MOBA_BLOCK), BF16),
                        pltpu.VMEM((KVH * MOBA_NBLK, MOBA_DH), F32),
                        pltpu.VMEM((2 * MOBA_BLOCK, KVH * R), F32),
                        pltpu.VMEM((KVH, 2 * LANES, R), BF16),
                        pltpu.VMEM((2 * KVH, 2 * MOBA_BLOCK, R), F32),
                        pltpu.VMEM((2 * KVH, 1, R), F32),
                        pltpu.VMEM((KVH, 1, R), F32),
                        pltpu.VMEM((KVH, MOBA_VT_ROWS, R), F32)],
        compiler_params=pltpu.CompilerParams(
            dimension_semantics=("arbitrary", "arbitrary"), vmem_limit_bytes=MOBA_VMEM_LIMIT),
        name="moba",
    )(rel_bias, cq, ck, cv, cgate)


def kernel(x, norm_gain, final_gain, rel_bias, w_in_even, gla_w_up, gla_b_up, gla_norm_gain, swa_sinks,
           w_out_even, w_in_odd, w_out_odd):
    assert x.shape == (BATCH, SEQ, D_MODEL)
    h0 = x.reshape(TOKENS, D_MODEL)

    we = jnp.swapaxes(w_in_even[0], 0, 1)
    splits = np.cumsum([0, GLA_QK, GLA_QK, A_WIDTH, GLA_RANK, A_WIDTH, B_WIDTH, SWA_KV_HEADS * SWA_DH,
                        SWA_KV_HEADS * SWA_DH, B_WIDTH])
    w_aq, w_ak, w_av, w_down, w_agate, w_bq = [we[splits[k]:splits[k + 1]] for k in range(6)]
    w_bkv = we[splits[6]:splits[8]]
    w_bgate = we[splits[8]:splits[9]]
    w_down = jnp.pad(jnp.concatenate([w_down] * 3, axis=0), ((0, LANES - 3 * GLA_RANK), (0, 0)))
    proj_ws = [w.astype(BF16) for w in (w_aq, w_ak, w_av, w_down, w_agate, w_bq, w_bkv, w_bgate)]
    proj_dt = [BF16, BF16, BF16, F32, BF16, BF16, BF16, BF16]
    scales = [GLA_DK ** -0.5, 1.0, 1.0, 1.0, 1.0, SWA_DH ** -0.5 * LOG2E, 1.0, 1.0]
    aq, ak, av, adown, agate, bq, bkv, bgate = _layer_io(
        h0, [], norm_gain[0:1], proj_ws, proj_dt, scales, write_h=False, proj_t=True)

    w_up_hi = gla_w_up[0].astype(BF16)
    w_up_lo = (gla_w_up[0] - w_up_hi.astype(F32)).astype(BF16)
    w_up = jnp.pad(jnp.concatenate([w_up_hi, w_up_hi, w_up_lo], axis=0), ((0, LANES - 3 * GLA_RANK), (0, 0)))
    oa, ob = _gla_swa(aq, ak, av, agate, adown, w_up, gla_b_up[0:1], gla_norm_gain[0:1],
                      bq, bkv, bgate, rel_bias, swa_sinks[0])

    wo = w_out_even[0].astype(BF16)
    wi = w_in_odd[0]
    osp = np.cumsum([0, C_WIDTH, MOBA_KV_HEADS * MOBA_DH, MOBA_KV_HEADS * MOBA_DH, C_WIDTH])
    odd_ws = [wi[:, osp[k]:osp[k + 1]].astype(BF16) for k in range(4)]
    h1, cq, ck, cv, cgate = _layer_io(
        h0, [(oa, wo[:A_WIDTH]), (ob, wo[A_WIDTH:])], norm_gain[1:2], odd_ws, [BF16] * 4,
        [MOBA_DH ** -0.5 * LOG2E, 1.0, 1.0, 1.0], write_h=True, row_sub=ROW_TILE)

    oc = _moba(cq, ck, cv, cgate, rel_bias)

    out = _final_io(h1, oc, w_out_odd[0].astype(BF16), final_gain.reshape(1, D_MODEL))
    return out.reshape(BATCH, SEQ, D_MODEL)
```

```python
import functools
import math

import numpy as np
import jax
import jax.numpy as jnp
from jax import lax
from jax.experimental import pallas as pl
from jax.experimental.pallas import tpu as pltpu

D_MODEL = 1024
BATCH = 2
SEQ = 8192
TOKENS = BATCH * SEQ

REL_BUCKETS = 32
REL_MAX_EXACT = 16
REL_MAX_DIST = 128
ATTN_HEADS = 8

GLA_HEADS = 4
GLA_DK = 64
GLA_DV = 128
GLA_RANK = 16
GLA_TAU = 16.0
GLA_CHUNK = 64

SWA_HEADS = 8
SWA_KV_HEADS = 2
SWA_DH = 64
SWA_WINDOW = 128

MOBA_HEADS = 8
MOBA_KV_HEADS = 2
MOBA_DH = 128
MOBA_BLOCK = 256
MOBA_TOPK = 3
MOBA_GROUP = MOBA_HEADS // MOBA_KV_HEADS
MOBA_NBLK = SEQ // MOBA_BLOCK
MOBA_VT_ROWS = MOBA_DH + 16
LOG2E = math.log2(math.e)
SWA_VT_ROWS = SWA_KV_HEADS * SWA_DH + 16

EPS = 1e-6
LANES = 128
NEG_BIG = -1e30
VMEM_LIMIT = 48 * 1024 * 1024
MOBA_VMEM_LIMIT = 56 * 1024 * 1024

A_WIDTH = GLA_HEADS * GLA_DV
B_WIDTH = SWA_HEADS * SWA_DH
C_WIDTH = MOBA_HEADS * MOBA_DH
GLA_QK = GLA_HEADS * GLA_DK

ROW_TILE = 1024
ROW_SUB = 256
GLA_TILE = 1024
SWA_TILE = 1024
SWA_SLOTS = 2

F32 = jnp.float32
BF16 = jnp.bfloat16


def _dot(a, b):
    return jnp.dot(a, b, preferred_element_type=F32)


def _dot_nt(a, b):
    return lax.dot_general(a, b, (((1,), (1,)), ((), ())), preferred_element_type=F32)


def _dot_tn(a, b):
    return lax.dot_general(a, b, (((0,), (0,)), ((), ())), preferred_element_type=F32)


def _silu(x):
    h = 0.5 * x
    return h + h * jnp.tanh(h)


def _t5_thresholds():
    d = np.arange(REL_MAX_DIST + 1)
    nf = np.maximum(d, 1).astype(np.float32)
    large = REL_MAX_EXACT + (np.log(nf / np.float32(REL_MAX_EXACT))
                             / np.float32(math.log(REL_MAX_DIST / REL_MAX_EXACT))
                             * np.float32(REL_BUCKETS - REL_MAX_EXACT)).astype(np.int32)
    bucket = np.where(d < REL_MAX_EXACT, d, np.minimum(large, REL_BUCKETS - 1))
    assert np.all(np.diff(bucket) >= 0) and bucket[-1] == REL_BUCKETS - 1
    return [int(np.argmax(bucket >= b)) for b in range(REL_BUCKETS)]


_T5_THRESH = _t5_thresholds()


def _t5_bias(dist, rb_ref, head):
    out = jnp.full(dist.shape, rb_ref[0, head], F32)
    for b in range(1, REL_BUCKETS):
        out = jnp.where(dist >= _T5_THRESH[b], rb_ref[b, head], out)
    return out


def _layer_io_kernel(*refs, n_acc, n_proj, write_h, scales, row_sub, proj_t):
    h_ref = refs[0]
    pos = 1
    acc_pairs = []
    for _ in range(n_acc):
        acc_pairs.append((refs[pos], refs[pos + 1]))
        pos += 2
    gain_ref = refs[pos]
    pos += 1
    wp_refs = refs[pos:pos + n_proj]
    pos += n_proj
    out_refs = refs[pos:]

    o = 1 if write_h else 0
    n_sub = ROW_TILE // row_sub
    pending = {}

    def norm_stage(s):
        rows = slice(s * row_sub, (s + 1) * row_sub)
        h = h_ref[rows, :]
        for a_ref, w_ref in acc_pairs:
            h = h + _dot(a_ref[rows, :], w_ref[...])
        if write_h:
            out_refs[0][rows, :] = h
        y = h * lax.rsqrt(jnp.mean(h * h, axis=-1, keepdims=True) + EPS) * gain_ref[...]
        if n_proj == 0:
            out_refs[o][rows, :] = y
        else:
            pending[s] = y.astype(BF16)

    def proj_stage(s):
        rows = slice(s * row_sub, (s + 1) * row_sub)
        yb = pending.pop(s)
        for k in range(n_proj):
            r = _dot_nt(yb, wp_refs[k][...]) if proj_t else _dot(yb, wp_refs[k][...])
            if scales[k] != 1.0:
                r = r * scales[k]
            out_refs[o + k][rows, :] = r.astype(out_refs[o + k].dtype)

    norm_stage(0)
    for s in range(n_sub):
        if s + 1 < n_sub:
            norm_stage(s + 1)
        if n_proj:
            proj_stage(s)


def _layer_io(h, acc_pairs, gain, proj_ws, proj_dtypes, scales, write_h, row_sub=ROW_SUB, proj_t=False):
    n_rows = h.shape[0]
    grid = (n_rows // ROW_TILE,)
    row_spec = lambda n: pl.BlockSpec((ROW_TILE, n), lambda i: (i, 0))
    full_spec = lambda a: pl.BlockSpec(a.shape, lambda i: (0,) * a.ndim, pipeline_mode=pl.Buffered(1))

    args = [h]
    in_specs = [row_spec(D_MODEL)]
    for a, w in acc_pairs:
        args += [a, w]
        in_specs += [row_spec(a.shape[1]), full_spec(w)]
    args.append(gain)
    in_specs.append(full_spec(gain))
    for w in proj_ws:
        args.append(w)
        in_specs.append(full_spec(w))

    out_shape, out_specs = [], []
    if write_h:
        out_shape.append(jax.ShapeDtypeStruct((n_rows, D_MODEL), F32))
        out_specs.append(row_spec(D_MODEL))
    if proj_ws:
        for w, dt in zip(proj_ws, proj_dtypes):
            n_out = w.shape[0] if proj_t else w.shape[1]
            out_shape.append(jax.ShapeDtypeStruct((n_rows, n_out), dt))
            out_specs.append(row_spec(n_out))
    else:
        out_shape.append(jax.ShapeDtypeStruct((n_rows, D_MODEL), F32))
        out_specs.append(row_spec(D_MODEL))

    kern = functools.partial(_layer_io_kernel, n_acc=len(acc_pairs), n_proj=len(proj_ws),
                             write_h=write_h, scales=tuple(scales), row_sub=row_sub, proj_t=proj_t)
    return pl.pallas_call(
        kern, grid=grid, in_specs=in_specs, out_specs=out_specs, out_shape=out_shape,
        compiler_params=pltpu.CompilerParams(dimension_semantics=("arbitrary",),
                                             vmem_limit_bytes=VMEM_LIMIT),
        name="layer_io",
    )(*args)


IN_RING = 3


def _final_io_kernel(h_hbm, a_hbm, gain_ref, w_ref, o_ref, hbuf, abuf, sem):
    s = pl.program_id(0)
    n = pl.num_programs(0)

    def copies(step):
        start = step * ROW_TILE
        rows = pl.ds(start if isinstance(step, int) else pl.multiple_of(start, ROW_TILE), ROW_TILE)
        slot = step % IN_RING
        return (pltpu.make_async_copy(h_hbm.at[rows], hbuf.at[slot], sem.at[0, slot]),
                pltpu.make_async_copy(a_hbm.at[rows], abuf.at[slot], sem.at[1, slot]))

    @pl.when(s == 0)
    def _():
        for step in range(IN_RING - 1):
            for c in copies(step):
                c.start()

    @pl.when(s + IN_RING - 1 < n)
    def _():
        for c in copies(s + IN_RING - 1):
            c.start()

    for c in copies(s):
        c.wait()

    slot = s % IN_RING
    for t in range(ROW_TILE // ROW_SUB):
        rows = slice(t * ROW_SUB, (t + 1) * ROW_SUB)
        h = hbuf[slot, rows, :] + _dot(abuf[slot, rows, :], w_ref[...])
        o_ref[rows, :] = h * lax.rsqrt(jnp.mean(h * h, axis=-1, keepdims=True) + EPS) * gain_ref[...]


def _final_io(h, act, w, gain):
    n_rows = h.shape[0]
    n_steps = n_rows // ROW_TILE
    assert n_steps >= IN_RING - 1
    full_spec = lambda a: pl.BlockSpec(a.shape, lambda i: (0,) * a.ndim, pipeline_mode=pl.Buffered(1))
    return pl.pallas_call(
        _final_io_kernel, grid=(n_steps,),
        in_specs=[pl.BlockSpec(memory_space=pl.ANY), pl.BlockSpec(memory_space=pl.ANY),
                  full_spec(gain), full_spec(w)],
        out_specs=pl.BlockSpec((ROW_TILE, D_MODEL), lambda i: (i, 0)),
        out_shape=jax.ShapeDtypeStruct((n_rows, D_MODEL), F32),
        scratch_shapes=[pltpu.VMEM((IN_RING, ROW_TILE, D_MODEL), h.dtype),
                        pltpu.VMEM((IN_RING, ROW_TILE, act.shape[1]), act.dtype),
                        pltpu.SemaphoreType.DMA((2, IN_RING))],
        compiler_params=pltpu.CompilerParams(dimension_semantics=("arbitrary",),
                                             vmem_limit_bytes=VMEM_LIMIT),
        name="final_io",
    )(h, act, gain, w)


def _gla_init(st_ref):
    @pl.when(pl.program_id(1) == 0)
    def _():
        st_ref[...] = jnp.zeros_like(st_ref)


def _gla_steps(q_ref, k_ref, v_ref, gate_ref, down_ref, wup_ref, bup_ref, gain_ref, o_ref, st_ref):
    C = GLA_CHUNK

    a = down_ref[...]
    a_hi = a.astype(BF16)
    a_lo = (a - a_hi.astype(F32)).astype(BF16)
    lane_a = lax.broadcasted_iota(jnp.int32, a.shape, 1) // GLA_RANK
    z = _dot(jnp.where(lane_a == 1, a_lo, a_hi), wup_ref[...]) + bup_ref[...]
    log_a = (jnp.minimum(z, 0.0) - jnp.log(1.0 + jnp.exp(-jnp.abs(z)))) * (1.0 / GLA_TAU)

    r_i = lax.broadcasted_iota(jnp.int32, (C, C), 0)
    c_i = lax.broadcasted_iota(jnp.int32, (C, C), 1)
    tri = (c_i <= r_i).astype(BF16)
    lane_qk = lax.broadcasted_iota(jnp.int32, (C, GLA_QK), 1)
    head_masks = [(lane_qk // GLA_DK) == h for h in range(GLA_HEADS)]
    rs = lax.broadcasted_iota(jnp.int32, (GLA_HEADS * C, C), 0)
    cs = lax.broadcasted_iota(jnp.int32, (GLA_HEADS * C, C), 1)
    causal = (rs % C) >= cs
    st_r = lax.broadcasted_iota(jnp.int32, (A_WIDTH, GLA_QK), 0)
    st_c = lax.broadcasted_iota(jnp.int32, (A_WIDTH, GLA_QK), 1)
    same_head = (st_r // GLA_DV) == (st_c // GLA_DK)
    gain = gain_ref[...]

    pending = {}

    def cumsum(c):
        g = log_a[c * C:(c + 1) * C]
        g_hi = g.astype(BF16)
        g_lo = (g - g_hi.astype(F32)).astype(BF16)
        pending[c] = _dot(tri, g_hi) + _dot(tri, g_lo)

    def prep(c):
        rows = slice(c * C, (c + 1) * C)
        b = pending.pop(c)
        b_last = b[C - 1:C]
        q = q_ref[rows, :].astype(F32)
        k = k_ref[rows, :].astype(F32)
        q_e = q * jnp.exp(b)
        k_e = (k * jnp.exp(-b)).astype(BF16)
        k_l = (k * jnp.exp(b_last - b)).astype(BF16)
        decay = jnp.exp(b_last)
        q_stack = jnp.concatenate([jnp.where(m, q_e, 0.0) for m in head_masks], axis=0).astype(BF16)
        pending[c] = (q_stack, q_e.astype(BF16), k_e, k_l, decay)

    def scores(c):
        q_stack, q_eb, k_e, k_l, decay = pending.pop(c)
        att = jnp.where(causal, _dot_nt(q_stack, k_e), 0.0).astype(BF16)
        pending[c] = (att, q_eb, k_l, decay)

    def main(c):
        rows = slice(c * C, (c + 1) * C)
        att, q_eb, k_l, decay = pending.pop(c)
        v = v_ref[rows, :]
        o_full = _dot(att, v)
        o_intra = jnp.concatenate(
            [o_full[h * C:(h + 1) * C, h * GLA_DV:(h + 1) * GLA_DV] for h in range(GLA_HEADS)], axis=1)
        st = st_ref[...]
        o = o_intra + _dot_nt(q_eb, st.astype(BF16))
        kv_t = _dot_tn(v, k_l)
        st_ref[...] = st * decay + jnp.where(same_head, kv_t, 0.0)

        outs = []
        for h in range(GLA_HEADS):
            oh = o[:, h * GLA_DV:(h + 1) * GLA_DV]
            outs.append(oh * lax.rsqrt(jnp.mean(oh * oh, axis=-1, keepdims=True) + EPS) * gain)
        on = jnp.concatenate(outs, axis=1)
        o_ref[rows, :] = (on * _silu(gate_ref[rows, :].astype(F32))).astype(o_ref.dtype)

    n_chunks = GLA_TILE // C
    steps = [functools.partial(cumsum, 0), functools.partial(cumsum, 1), functools.partial(cumsum, 2),
             functools.partial(prep, 0), functools.partial(prep, 1), functools.partial(scores, 0)]
    for c in range(n_chunks):
        if c + 3 < n_chunks:
            steps.append(functools.partial(cumsum, c + 3))
        if c + 2 < n_chunks:
            steps.append(functools.partial(prep, c + 2))
        if c + 1 < n_chunks:
            steps.append(functools.partial(scores, c + 1))
        steps.append(functools.partial(main, c))
    return steps


def _swa_init(rb_ref, bias_ref):
    W = SWA_WINDOW

    @pl.when(jnp.logical_and(pl.program_id(0) == 0, pl.program_id(1) == 0))
    def _():
        kj = lax.broadcasted_iota(jnp.int32, (2 * W, W), 0)
        qi = lax.broadcasted_iota(jnp.int32, (2 * W, W), 1)
        dist = qi + W - kj
        in_win = jnp.logical_and(dist >= 0, dist < W)
        for h in range(SWA_HEADS):
            bias_ref[:, h * W:(h + 1) * W] = jnp.where(in_win, _t5_bias(dist, rb_ref, h) * LOG2E, -jnp.inf)


def _swa_steps(sink_ref, q_ref, kp_ref, k_ref, vp_ref, v_ref, gate_ref, o_ref, bias_ref, s_ref, smax_ref):
    W = SWA_WINDOW
    H = SWA_HEADS
    DH = SWA_DH
    G = SWA_HEADS // SWA_KV_HEADS
    t = pl.program_id(1)

    kcat = jnp.concatenate([kp_ref[...], k_ref[...]], axis=0)
    vcat = jnp.concatenate([vp_ref[...], v_ref[...]], axis=0)
    vt = vcat.astype(F32).T.astype(BF16)
    ones_rows = jnp.ones((SWA_VT_ROWS - LANES, 2 * W), BF16)
    q_t = q_ref[...].astype(F32).T.astype(BF16)
    zeros = jnp.zeros((DH, W), BF16)
    colh = lax.broadcasted_iota(jnp.int32, (1, H * W), 1) // W
    sink = jnp.full((1, H * W), sink_ref[H - 1], F32)
    for h in range(H - 2, -1, -1):
        sink = jnp.where(colh == h, sink_ref[h], sink)
    sink = sink * LOG2E
    key_prev = lax.broadcasted_iota(jnp.int32, (2 * W, H * W), 0) < W

    def produce(blk):
        tok = slice(blk * W, (blk + 1) * W)
        win = slice(blk * W, blk * W + 2 * W)
        pieces = []
        for h in range(H):
            piece = q_t[h * DH:(h + 1) * DH, tok]
            pieces.append(jnp.concatenate([piece, zeros] if h < G else [zeros, piece], axis=0))
        q_pad = jnp.concatenate(pieces, axis=1)
        s = _dot(kcat[win], q_pad) + bias_ref[...]
        if blk == 0:
            s = jnp.where(jnp.logical_and(t == 0, key_prev), -jnp.inf, s)
        s_ref[blk % SWA_SLOTS] = s
        smax_ref[blk % SWA_SLOTS] = jnp.maximum(jnp.max(s, axis=0, keepdims=True), sink)

    pending = {}

    def attend(blk):
        win = slice(blk * W, blk * W + 2 * W)
        m = smax_ref[blk % SWA_SLOTS]
        p = jnp.exp2(s_ref[blk % SWA_SLOTS] - m).astype(BF16)
        pv = _dot(jnp.concatenate([vt[:, win], ones_rows], axis=0), p)
        pending[blk] = (pv, m)

    def finish(blk):
        tok = slice(blk * W, (blk + 1) * W)
        pv, m = pending.pop(blk)
        inv = 1.0 / (pv[LANES:LANES + 1] + jnp.exp2(sink - m))
        o_t = jnp.concatenate(
            [pv[(h // G) * DH:(h // G + 1) * DH, h * W:(h + 1) * W] * inv[:, h * W:(h + 1) * W]
             for h in range(H)], axis=0)
        gate = gate_ref[tok, :].astype(F32)
        o_ref[tok, :] = (o_t.T * _silu(gate)).astype(o_ref.dtype)

    n_blk = SWA_TILE // W
    ahead = SWA_SLOTS - 1
    steps = [functools.partial(produce, blk) for blk in range(ahead)]
    for blk in range(n_blk):
        if blk + ahead < n_blk:
            steps.append(functools.partial(produce, blk + ahead))
        steps.append(functools.partial(attend, blk))
        if blk >= 1:
            steps.append(functools.partial(finish, blk - 1))
    steps.append(functools.partial(finish, n_blk - 1))
    return steps


def _gla_swa_kernel(aq_ref, ak_ref, av_ref, agate_ref, adown_ref, wup_ref, bup_ref, gain_ref,
                    sink_ref, rb_ref, bq_ref, kp_ref, k_ref, vp_ref, v_ref, bgate_ref,
                    oa_ref, ob_ref, st_ref, bias_ref, s_ref, smax_ref):
    _gla_init(st_ref)
    _swa_init(rb_ref, bias_ref)
    gla = _gla_steps(aq_ref, ak_ref, av_ref, agate_ref, adown_ref, wup_ref, bup_ref, gain_ref, oa_ref, st_ref)
    swa = _swa_steps(sink_ref, bq_ref, kp_ref, k_ref, vp_ref, v_ref, bgate_ref, ob_ref, bias_ref, s_ref, smax_ref)
    for step in gla + swa:
        step()


def _gla_swa(aq, ak, av, agate, adown, w_up, b_up, gain, bq, bkv, bgate, rel_bias, sinks):
    assert GLA_TILE == SWA_TILE
    nt = SEQ // SWA_TILE
    per = SWA_TILE // SWA_WINDOW
    row = lambda n: pl.BlockSpec((SWA_TILE, n), lambda b, t: (b * nt + t, 0))
    full = lambda a: pl.BlockSpec(a.shape, lambda b, t: (0,) * a.ndim)
    own = lambda c: pl.BlockSpec((SWA_TILE, LANES), lambda b, t: (b * nt + t, c))
    prev = lambda c: pl.BlockSpec((SWA_WINDOW, LANES),
                                  lambda b, t: (jnp.maximum((b * nt + t) * per - 1, 0), c))
    smem = pl.BlockSpec(memory_space=pltpu.SMEM)
    return pl.pallas_call(
        _gla_swa_kernel, grid=(BATCH, nt),
        in_specs=[row(GLA_QK), row(GLA_QK), row(A_WIDTH), row(A_WIDTH), row(LANES),
                  full(w_up), full(b_up), full(gain),
                  smem, smem, row(B_WIDTH), prev(0), own(0), prev(1), own(1), row(B_WIDTH)],
        out_specs=[row(A_WIDTH), row(B_WIDTH)],
        out_shape=[jax.ShapeDtypeStruct((TOKENS, A_WIDTH), BF16),
                   jax.ShapeDtypeStruct((TOKENS, B_WIDTH), BF16)],
        scratch_shapes=[pltpu.VMEM((A_WIDTH, GLA_QK), F32),
                        pltpu.VMEM((2 * SWA_WINDOW, SWA_HEADS * SWA_WINDOW), F32),
                        pltpu.VMEM((SWA_SLOTS, 2 * SWA_WINDOW, SWA_HEADS * SWA_WINDOW), F32),
                        pltpu.VMEM((SWA_SLOTS, 1, SWA_HEADS * SWA_WINDOW), F32)],
        compiler_params=pltpu.CompilerParams(dimension_semantics=("arbitrary", "arbitrary"),
                                             vmem_limit_bytes=VMEM_LIMIT),
        name="gla_swa",
    )(aq, ak, av, agate, adown, w_up, b_up, gain, sinks, rel_bias, bq, bkv, bkv, bkv, bkv, bgate)


def _moba_kernel(rb_ref, q_ref, k_ref, v_ref, gate_ref, o_ref,
                 kaug_ref, vt_ref, km_ref, bias_ref, qat_ref, s_ref, smax_ref, m_ref, acc_ref):
    BLK = MOBA_BLOCK
    G = MOBA_GROUP
    KVH = MOBA_KV_HEADS
    R = G * BLK
    NP1 = MOBA_NBLK + 1
    i = pl.program_id(1)

    @pl.when(i == 0)
    def _():
        lane = lax.broadcasted_iota(jnp.int32, (BLK, LANES), 1)
        ones_rows = jnp.ones((MOBA_VT_ROWS - MOBA_DH, BLK), BF16)

        def prep(j, carry):
            rows = pl.ds(pl.multiple_of(j * BLK, BLK), BLK)
            onehot = jnp.where(lane == j, 1.0, 0.0).astype(BF16)
            for kh in range(KVH):
                kb = k_ref[rows, kh * MOBA_DH:(kh + 1) * MOBA_DH]
                kaug_ref[kh * NP1 + j + 1, :, 0:LANES] = kb
                kaug_ref[kh * NP1 + j + 1, :, LANES:2 * LANES] = onehot
                km_ref[pl.ds(kh * MOBA_NBLK + j, 1), :] = jnp.mean(kb.astype(F32), axis=0, keepdims=True)
                vb = v_ref[rows, kh * MOBA_DH:(kh + 1) * MOBA_DH]
                vt_ref[kh * NP1 + j + 1, 0:MOBA_DH, :] = vb.astype(F32).T.astype(BF16)
                vt_ref[kh * NP1 + j + 1, MOBA_DH:, :] = ones_rows
            return carry

        lax.fori_loop(0, MOBA_NBLK, prep, 0)
        for kh in range(KVH):
            kaug_ref[kh * NP1, :, 0:LANES] = jnp.zeros((BLK, LANES), BF16)
            kaug_ref[kh * NP1, :, LANES:2 * LANES] = jnp.where(lane == MOBA_NBLK, 1.0, 0.0).astype(BF16)
            vt_ref[kh * NP1] = jnp.zeros((MOBA_VT_ROWS, BLK), BF16)
            qat_ref[kh, MOBA_DH + MOBA_NBLK:, :] = jnp.full((2 * LANES - MOBA_DH - MOBA_NBLK, R), NEG_BIG, BF16)

    @pl.when(jnp.logical_and(pl.program_id(0) == 0, i == 0))
    def _():
        tk = lax.broadcasted_iota(jnp.int32, (BLK, BLK), 0)
        tq = lax.broadcasted_iota(jnp.int32, (BLK, BLK), 1)
        d_own = tq - tk
        for head in range(MOBA_HEADS):
            cols = slice(head * BLK, (head + 1) * BLK)
            bias_ref[0:BLK, cols] = _t5_bias(d_own + BLK, rb_ref, head) * LOG2E
            bias_ref[BLK:2 * BLK, cols] = jnp.where(d_own >= 0, _t5_bias(d_own, rb_ref, head) * LOG2E, -jnp.inf)

    blk = lax.broadcasted_iota(jnp.int32, (MOBA_NBLK, R), 0)
    blkf = blk.astype(F32)
    past = blk < i
    colh = lax.broadcasted_iota(jnp.int32, (1, R), 1) // BLK
    far_row = REL_BUCKETS - 1

    own_max = []
    for kh in range(KVH):
        q = jnp.concatenate([q_ref[:, (kh * G + g) * MOBA_DH:(kh * G + g + 1) * MOBA_DH] for g in range(G)],
                            axis=0)
        q_t = q.astype(F32).T.astype(BF16)
        qat_ref[kh, 0:MOBA_DH, :] = q_t
        s_own = (_dot_nt(kaug_ref[kh * NP1 + i + 1, :, 0:MOBA_DH], q)
                 + bias_ref[BLK:2 * BLK, kh * R:(kh + 1) * R])
        s_ref[kh, BLK:2 * BLK, :] = s_own
        own_max.append(jnp.max(s_own, axis=0, keepdims=True))

        km = km_ref[kh * MOBA_NBLK:(kh + 1) * MOBA_NBLK, :]
        km_hi = km.astype(BF16)
        km_lo = (km - km_hi.astype(F32)).astype(BF16)
        gate = _dot(km_hi, q_t) + _dot(km_lo, q_t)
        g_ = jnp.where(past, gate, -jnp.inf)
        selected = jnp.zeros((MOBA_NBLK, R), dtype=jnp.bool_)
        for _ in range(MOBA_TOPK):
            mx = jnp.max(g_, axis=0, keepdims=True)
            first = jnp.min(jnp.where(g_ == mx, blkf, 1e9), axis=0, keepdims=True)
            pick = jnp.logical_and(blkf == first, past)
            selected = jnp.logical_or(selected, pick)
            g_ = jnp.where(pick, -jnp.inf, g_)
        cfar = jnp.where(colh == 0, rb_ref[far_row, kh * G],
                         jnp.where(colh == 1, rb_ref[far_row, kh * G + 1],
                                   jnp.where(colh == 2, rb_ref[far_row, kh * G + 2],
                                             rb_ref[far_row, kh * G + 3])))
        sel_bias = jnp.where(selected, jnp.where(blk < i - 1, cfar * LOG2E, 0.0), NEG_BIG)
        qat_ref[kh, MOBA_DH:MOBA_DH + MOBA_NBLK, :] = sel_bias.astype(BF16)

    for kh in range(KVH):
        s_near = _dot(kaug_ref[kh * NP1 + i], qat_ref[kh]) + bias_ref[0:BLK, kh * R:(kh + 1) * R]
        s_ref[kh, 0:BLK, :] = s_near
        smax_ref[kh] = jnp.maximum(own_max[kh], jnp.max(s_near, axis=0, keepdims=True))

    def produce(slot, kh, start, dummy_first=False):
        if dummy_first:
            s = _dot(kaug_ref[kh * NP1 + 1], qat_ref[kh])
            s_ref[slot * KVH + kh, BLK:2 * BLK, :] = s
        else:
            k2 = kaug_ref[pl.ds(kh * NP1 + start, 2)].reshape(2 * BLK, 2 * LANES)
            s = _dot(k2, qat_ref[kh])
            s_ref[slot * KVH + kh] = s
        smax_ref[slot * KVH + kh] = jnp.max(s, axis=0, keepdims=True)

    def consume(slot, kh, start, first=False, dummy_first=False):
        m_new = smax_ref[slot * KVH + kh]
        if not first:
            m_old = m_ref[kh]
            m_new = jnp.maximum(m_old, m_new)
        if dummy_first:
            p = jnp.exp2(s_ref[slot * KVH + kh, BLK:2 * BLK, :] - m_new).astype(BF16)
            pv = _dot(vt_ref[kh * NP1 + 1], p)
        else:
            p = jnp.exp2(s_ref[slot * KVH + kh] - m_new).astype(BF16)
            pv = (_dot(vt_ref[kh * NP1 + start], p[0:BLK])
                  + _dot(vt_ref[kh * NP1 + start + 1], p[BLK:2 * BLK]))
        acc_ref[kh] = pv if first else jnp.exp2(m_old - m_new) * acc_ref[kh] + pv
        m_ref[kh] = m_new

    n_far = jnp.maximum(i - 1, 0)
    n_pairs = (n_far + 1) // 2

    def far_start(t):
        return jnp.maximum(n_far - 2 * t - 1, 0)

    def far_step(t, slot, dummy_next=False):
        for kh in range(KVH):
            produce(1 - slot, kh, far_start(t + 1), dummy_first=dummy_next)
            consume(slot, kh, far_start(t))

    for kh in range(KVH):
        produce(1, kh, far_start(0))
        consume(0, kh, i, first=True)

    def far_four(u, carry):
        for d in range(4):
            far_step(4 * u + d, (d + 1) % 2)
        return carry

    n_full = jnp.maximum(n_pairs - 1, 0)
    dummy_last = jnp.logical_and(n_far % 2 == 1, n_full >= 1)
    n_plain = n_full - dummy_last.astype(jnp.int32)
    lax.fori_loop(0, n_plain // 4, far_four, 0)
    done = (n_plain // 4) * 4

    @pl.when(n_plain - done >= 2)
    def _():
        far_step(done, 1)
        far_step(done + 1, 0)

    @pl.when(n_plain % 2 == 1)
    def _():
        far_step(n_plain - 1, 1)

    for slot in range(2):
        @pl.when(jnp.logical_and(dummy_last, (n_plain + 1) % 2 == slot))
        def _():
            far_step(n_plain, slot, dummy_next=True)

    def finalize(kh):
        acc = acc_ref[kh]
        o = (acc[0:MOBA_DH] * (1.0 / acc[MOBA_DH:MOBA_DH + 1])).T
        for g in range(G):
            cols = slice((kh * G + g) * MOBA_DH, (kh * G + g + 1) * MOBA_DH)
            gt = gate_ref[:, cols].astype(F32)
            o_ref[:, cols] = (o[g * BLK:(g + 1) * BLK] * _silu(gt)).astype(o_ref.dtype)

    for slot in range(2):
        for odd in range(2):
            @pl.when(jnp.logical_and(n_pairs >= 1, jnp.logical_and(n_pairs % 2 == slot, n_far % 2 == odd)))
            def _():
                for kh in range(KVH):
                    consume(slot, kh, far_start(n_pairs - 1), dummy_first=bool(odd))
                    finalize(kh)

    @pl.when(n_pairs == 0)
    def _():
        for kh in range(KVH):
            finalize(kh)


def _moba(cq, ck, cv, cgate, rel_bias):
    R = MOBA_GROUP * MOBA_BLOCK
    KVH = MOBA_KV_HEADS
    NP1 = MOBA_NBLK + 1
    qspec = pl.BlockSpec((MOBA_BLOCK, C_WIDTH), lambda b, i: (b * MOBA_NBLK + i, 0))
    kvspec = pl.BlockSpec((SEQ, KVH * MOBA_DH), lambda b, i: (b, 0), pipeline_mode=pl.Buffered(1))
    return pl.pallas_call(
        _moba_kernel, grid=(BATCH, MOBA_NBLK),
        in_specs=[pl.BlockSpec(memory_space=pltpu.SMEM), qspec, kvspec, kvspec, qspec],
        out_specs=qspec,
        out_shape=jax.ShapeDtypeStruct((TOKENS, C_WIDTH), BF16),
        scratch_shapes=[pltpu.VMEM((KVH * NP1, MOBA_BLOCK, 2 * LANES), BF16),
                        pltpu.VMEM((KVH * NP1, MOBA_VT_ROWS, MOBA_BLOCK), BF16),
                        pltpu.VMEM((KVH * MOBA_NBLK, MOBA_DH), F32),
                        pltpu.VMEM((2 * MOBA_BLOCK, KVH * R), F32),
                        pltpu.VMEM((KVH, 2 * LANES, R), BF16),
                        pltpu.VMEM((2 * KVH, 2 * MOBA_BLOCK, R), F32),
                        pltpu.VMEM((2 * KVH, 1, R), F32),
                        pltpu.VMEM((KVH, 1, R), F32),
                        pltpu.VMEM((KVH, MOBA_VT_ROWS, R), F32)],
        compiler_params=pltpu.CompilerParams(
            dimension_semantics=("arbitrary", "arbitrary"), vmem_limit_bytes=MOBA_VMEM_LIMIT),
        name="moba",
    )(rel_bias, cq, ck, cv, cgate)


def kernel(x, norm_gain, final_gain, rel_bias, w_in_even, gla_w_up, gla_b_up, gla_norm_gain, swa_sinks,
           w_out_even, w_in_odd, w_out_odd):
    assert x.shape == (BATCH, SEQ, D_MODEL)
    h0 = x.reshape(TOKENS, D_MODEL)

    we = jnp.swapaxes(w_in_even[0], 0, 1)
    splits = np.cumsum([0, GLA_QK, GLA_QK, A_WIDTH, GLA_RANK, A_WIDTH, B_WIDTH, SWA_KV_HEADS * SWA_DH,
                        SWA_KV_HEADS * SWA_DH, B_WIDTH])
    w_aq, w_ak, w_av, w_down, w_agate, w_bq = [we[splits[k]:splits[k + 1]] for k in range(6)]
    w_bkv = we[splits[6]:splits[8]]
    w_bgate = we[splits[8]:splits[9]]
    w_down = jnp.pad(jnp.concatenate([w_down] * 3, axis=0), ((0, LANES - 3 * GLA_RANK), (0, 0)))
    proj_ws = [w.astype(BF16) for w in (w_aq, w_ak, w_av, w_down, w_agate, w_bq, w_bkv, w_bgate)]
    proj_dt = [BF16, BF16, BF16, F32, BF16, BF16, BF16, BF16]
    scales = [GLA_DK ** -0.5, 1.0, 1.0, 1.0, 1.0, SWA_DH ** -0.5 * LOG2E, 1.0, 1.0]
    aq, ak, av, adown, agate, bq, bkv, bgate = _layer_io(
        h0, [], norm_gain[0:1], proj_ws, proj_dt, scales, write_h=False, proj_t=True)

    w_up_hi = gla_w_up[0].astype(BF16)
    w_up_lo = (gla_w_up[0] - w_up_hi.astype(F32)).astype(BF16)
    w_up = jnp.pad(jnp.concatenate([w_up_hi, w_up_hi, w_up_lo], axis=0), ((0, LANES - 3 * GLA_RANK), (0, 0)))
    oa, ob = _gla_swa(aq, ak, av, agate, adown, w_up, gla_b_up[0:1], gla_norm_gain[0:1],
                      bq, bkv, bgate, rel_bias, swa_sinks[0])

    wo = w_out_even[0].astype(BF16)
    wi = w_in_odd[0]
    osp = np.cumsum([0, C_WIDTH, MOBA_KV_HEADS * MOBA_DH, MOBA_KV_HEADS * MOBA_DH, C_WIDTH])
    odd_ws = [wi[:, osp[k]:osp[k + 1]].astype(BF16) for k in range(4)]
    h1, cq, ck, cv, cgate = _layer_io(
        h0, [(oa, wo[:A_WIDTH]), (ob, wo[A_WIDTH:])], norm_gain[1:2], odd_ws, [BF16] * 4,
        [MOBA_DH ** -0.5 * LOG2E, 1.0, 1.0, 1.0], write_h=True, row_sub=ROW_TILE)

    oc = _moba(cq, ck, cv, cgate, rel_bias)

    out = _final_io(h1, oc, w_out_odd[0].astype(BF16), final_gain.reshape(1, D_MODEL))
    return out.reshape(BATCH, SEQ, D_MODEL)
```
